```python
import math
import jax, jax.numpy as jnp
from jax import lax
import numpy as np

D_MODEL = 1024
BATCH = 1
SEQ = 16384
DEPTH = 1
DEC_BATCH = 4
DEC_SEQ = 8192
PAST_LEN = 128

N_META = 16
BLOCK = 128
WINDOW = 128
N_Q_HEADS = 8
N_KV_HEADS = 2
HEAD_DIM = 64
ATT_WIDTH = N_Q_HEADS * HEAD_DIM
KV_WIDTH = N_KV_HEADS * HEAD_DIM
CONV_WIDTH = D_MODEL // 2
CONV_K = 31
N_BUCKETS = 32
MAX_DISTANCE = 128
N_GROUPS = 4
EXPERTS_PER_GROUP = 8
N_EXPERTS = N_GROUPS * EXPERTS_PER_GROUP
TOP_K = 2
D_EXPERT = 256
LN_EPS = 1e-5
ALPHA = (2 * DEPTH) ** 0.25
BETA = (8 * DEPTH) ** -0.25
NEG = -1e30
Q_END = ATT_WIDTH
K_END = Q_END + KV_WIDTH
V_END = K_END + KV_WIDTH
GLU_END = V_END + 2 * CONV_WIDTH
GA_END = GLU_END + D_MODEL
PROJ_WIDTH = GA_END + D_MODEL

kernel_name = 'hybrid_swa_conformer_hmoe_encoder'


def layer_norm(x, g, b):
    x32 = x.astype(jnp.float32)
    mu = jnp.mean(x32, axis=-1, keepdims=True)
    var = jnp.mean(jnp.square(x32 - mu), axis=-1, keepdims=True)
    return ((x32 - mu) * lax.rsqrt(var + LN_EPS)).astype(x.dtype) * g + b


def t5_bucket(rel):
    half = N_BUCKETS // 2
    max_exact = half // 2
    ret = jnp.where(rel > 0, half, 0)
    n = jnp.abs(rel)
    nf = jnp.maximum(n, 1).astype(jnp.float32)
    large = max_exact + (jnp.log(nf / max_exact) / math.log(MAX_DISTANCE / max_exact)
                         * (half - max_exact)).astype(jnp.int32)
    large = jnp.minimum(large, half - 1)
    return ret + jnp.where(n < max_exact, n, large)


def windowed_attention(q, k, v, sink, rel_bias):
    B, L = q.shape[0], q.shape[1]
    nb = L // BLOCK
    grp = N_Q_HEADS // N_KV_HEADS
    scale = HEAD_DIM ** -0.5
    off = BLOCK - N_META
    qb = q.reshape(B, nb, BLOCK, N_KV_HEADS, grp, HEAD_DIM)

    def band(t):
        tp = jnp.pad(t, ((0, 0), (BLOCK, BLOCK), (0, 0), (0, 0)))
        tp = tp.reshape(B, nb + 2, BLOCK, N_KV_HEADS, HEAD_DIM)
        return jnp.concatenate([tp[:, :-2], tp[:, 1:-1], tp[:, 2:]], axis=2)

    kb, vb = band(k), band(v)
    km, vm = k[:, off:BLOCK], v[:, off:BLOCK]

    qi = jnp.arange(BLOCK)
    kj = jnp.arange(3 * BLOCK) - BLOCK
    rel = kj[None, :] - qi[:, None]
    blk = jnp.arange(nb)
    kp = blk[:, None] * BLOCK + kj[None, :]
    key_ok = (kp >= BLOCK) & (kp < L)
    band_ok = (jnp.abs(rel) <= WINDOW)[None] & key_ok[:, None, :]
    band_bias = rel_bias[t5_bucket(rel)].astype(jnp.float32)
    band_bias = band_bias.transpose(2, 0, 1).reshape(N_KV_HEADS, grp, BLOCK, 3 * BLOCK)

    qpos = blk[:, None] * BLOCK + qi[None, :] - off
    meta_rel = jnp.arange(N_META)[None, None, :] - qpos[:, :, None]
    meta_bias = rel_bias[t5_bucket(meta_rel)].astype(jnp.float32)
    meta_bias = meta_bias.transpose(0, 3, 1, 2).reshape(nb, N_KV_HEADS, grp, BLOCK, N_META)

    s_band = jnp.einsum('bnqhgd,bnkhd->bnhgqk', qb, kb, preferred_element_type=jnp.float32) * scale + band_bias
    s_band = jnp.where(band_ok[None, :, None, None], s_band, NEG)
    s_meta = jnp.einsum('bnqhgd,bmhd->bnhgqm', qb, km, preferred_element_type=jnp.float32) * scale + meta_bias
    s_sink = jnp.broadcast_to(sink.astype(jnp.float32).reshape(1, 1, N_KV_HEADS, grp, 1, 1),
                              s_meta.shape[:-1] + (1,))
    p = jax.nn.softmax(jnp.concatenate([s_band, s_meta, s_sink], axis=-1), axis=-1)
    p_band = p[..., :3 * BLOCK].astype(v.dtype)
    p_meta = p[..., 3 * BLOCK:3 * BLOCK + N_META].astype(v.dtype)
    o = (jnp.einsum('bnhgqk,bnkhd->bnqhgd', p_band, vb)
         + jnp.einsum('bnhgqm,bmhd->bnqhgd', p_meta, vm))
    return o.reshape(B, L, ATT_WIDTH)


def conformer_conv(u, pad_mask, conv_w, conv_b, ln_g, ln_b, w_pw2):
    a, gt = jnp.split(u, 2, axis=-1)
    z = a * jax.nn.sigmoid(gt)
    z = jnp.where(pad_mask[None, :, None], 0.0, z)
    z = lax.conv_general_dilated(z, conv_w[:, None, :], window_strides=(1,),
                                 padding=((CONV_K // 2, CONV_K // 2),),
                                 dimension_numbers=('NWC', 'WIO', 'NWC'),
                                 feature_group_count=CONV_WIDTH) + conv_b
    z = jax.nn.silu(layer_norm(z, ln_g, ln_b))
    return z @ w_pw2


def mixer(h, pad_mask, rel_bias, w_in, w_att_branch, sink, conv_w, conv_b,
          conv_ln_g, conv_ln_b, w_conv_out, w_out):
    B, L, _ = h.shape
    z = h @ w_in
    q = z[..., :Q_END].reshape(B, L, N_Q_HEADS, HEAD_DIM)
    k = z[..., Q_END:K_END].reshape(B, L, N_KV_HEADS, HEAD_DIM)
    v = z[..., K_END:V_END].reshape(B, L, N_KV_HEADS, HEAD_DIM)
    att = windowed_attention(q, k, v, sink, rel_bias) @ w_att_branch
    conv = conformer_conv(z[..., V_END:GLU_END], pad_mask, conv_w, conv_b,
                          conv_ln_g, conv_ln_b, w_conv_out)
    g_att = jax.nn.sigmoid(z[..., GLU_END:GA_END])
    g_conv = jax.nn.sigmoid(z[..., GA_END:])
    return (g_att * att + g_conv * conv) @ w_out


def hier_moe(x, w_group, b_group, w_router, b_router, w_gate, w_up, w_down):
    B, L, D = x.shape
    t = x.reshape(B * L, D)
    pg = jax.nn.softmax((t @ w_group).astype(jnp.float32) + b_group.astype(jnp.float32), axis=-1)
    g_w, g_idx = lax.top_k(pg, 1)
    le = ((t @ w_router).astype(jnp.float32) + b_router.astype(jnp.float32))
    le = le.reshape(-1, N_GROUPS, EXPERTS_PER_GROUP)
    sel = jnp.broadcast_to(g_idx[:, :, None], (le.shape[0], 1, EXPERTS_PER_GROUP))
    le = jnp.take_along_axis(le, sel, axis=1)[:, 0]
    e_w, e_idx = lax.top_k(jax.nn.softmax(le, axis=-1), TOP_K)
    e_w = e_w / jnp.sum(e_w, axis=-1, keepdims=True)
    w = g_w * e_w
    flat = g_idx * EXPERTS_PER_GROUP + e_idx
    combine = jnp.einsum('tk,tke->te', w, jax.nn.one_hot(flat, N_EXPERTS, dtype=jnp.float32))
    combine = combine.astype(x.dtype).reshape(-1, N_GROUPS, EXPERTS_PER_GROUP)
    out = jnp.zeros_like(t)
    for gi in range(N_GROUPS):
        sl = slice(gi * EXPERTS_PER_GROUP, (gi + 1) * EXPERTS_PER_GROUP)
        hid = jax.nn.silu(jnp.einsum('td,edf->tef', t, w_gate[sl])) * jnp.einsum('td,edf->tef', t, w_up[sl])
        hid = hid * combine[:, gi, :, None]
        out = out + jnp.einsum('tef,efd->td', hid, w_down[sl])
    return out.reshape(B, L, D)


def encode(x, meta, ln_in_g, ln_in_b, rel_bias, w_in, w_att_branch, sink, conv_w, conv_b,
           conv_ln_g, conv_ln_b, w_conv_out, w_out, ln1_g, ln1_b, w_group, b_group,
           w_router, b_router, w_gate, w_up, w_down, ln2_g, ln2_b):
    B, S, D = x.shape
    off = BLOCK - N_META
    h = jnp.concatenate([jnp.zeros((B, off, D), x.dtype),
                         jnp.broadcast_to(meta[None].astype(x.dtype), (B, N_META, D)), x], axis=1)
    L = S + BLOCK
    pad_mask = jnp.arange(L) < off
    h = layer_norm(h, ln_in_g, ln_in_b)
    for l in range(DEPTH):
        m = mixer(h, pad_mask, rel_bias, w_in[l], w_att_branch[l], sink[l], conv_w[l], conv_b[l],
                  conv_ln_g[l], conv_ln_b[l], w_conv_out[l], w_out[l])
        h = layer_norm(ALPHA * h + m, ln1_g[l], ln1_b[l])
        f = hier_moe(h, w_group[l], b_group[l], w_router[l], b_router[l], w_gate[l], w_up[l], w_down[l])
        h = layer_norm(ALPHA * h + f, ln2_g[l], ln2_b[l])
    return h[:, BLOCK:]


def setup_inputs(seed: int = 0) -> dict:
    key = jax.random.key(seed)
    ks = jax.random.split(key, 32)
    f32 = jnp.float32

    def nrm(k, shape, s):
        return jax.random.normal(k, shape, f32) * s

    col_scale = jnp.ones((PROJ_WIDTH,), f32).at[K_END:V_END].set(BETA)
    return {
        'x_prompt': nrm(ks[0], (BATCH, SEQ, D_MODEL), 1.0),
        'x_sample': nrm(ks[1], (DEC_BATCH, DEC_SEQ, D_MODEL), 1.0),
        'meta': nrm(ks[2], (N_META, D_MODEL), 1.0),
        'ln_in_g': 1.0 + nrm(ks[3], (D_MODEL,), 0.02),
        'ln_in_b': nrm(ks[4], (D_MODEL,), 0.02),
        'rel_bias': nrm(ks[5], (N_BUCKETS, N_Q_HEADS), 0.2),
        'w_in': nrm(ks[6], (DEPTH, D_MODEL, PROJ_WIDTH), D_MODEL ** -0.5) * col_scale,
        'w_att_branch': nrm(ks[7], (DEPTH, ATT_WIDTH, D_MODEL), ATT_WIDTH ** -0.5),
        'sink': nrm(ks[8], (DEPTH, N_Q_HEADS), 0.5),
        'conv_w': nrm(ks[9], (DEPTH, CONV_K, CONV_WIDTH), CONV_K ** -0.5),
        'conv_b': nrm(ks[10], (DEPTH, CONV_WIDTH), 0.02),
        'conv_ln_g': 1.0 + nrm(ks[11], (DEPTH, CONV_WIDTH), 0.02),
        'conv_ln_b': nrm(ks[12], (DEPTH, CONV_WIDTH), 0.02),
        'w_conv_out': nrm(ks[13], (DEPTH, CONV_WIDTH, D_MODEL), CONV_WIDTH ** -0.5),
        'w_out': nrm(ks[14], (DEPTH, D_MODEL, D_MODEL), D_MODEL ** -0.5 * BETA),
        'ln1_g': 1.0 + nrm(ks[15], (DEPTH, D_MODEL), 0.02),
        'ln1_b': nrm(ks[16], (DEPTH, D_MODEL), 0.02),
        'w_group': nrm(ks[17], (DEPTH, D_MODEL, N_GROUPS), D_MODEL ** -0.5),
        'b_group': nrm(ks[18], (DEPTH, N_GROUPS), 0.01),
        'w_router': nrm(ks[19], (DEPTH, D_MODEL, N_EXPERTS), D_MODEL ** -0.5),
        'b_router': nrm(ks[20], (DEPTH, N_EXPERTS), 0.01),
        'w_gate': nrm(ks[21], (DEPTH, N_EXPERTS, D_MODEL, D_EXPERT), D_MODEL ** -0.5),
        'w_up': nrm(ks[22], (DEPTH, N_EXPERTS, D_MODEL, D_EXPERT), D_MODEL ** -0.5),
        'w_down': nrm(ks[23], (DEPTH, N_EXPERTS, D_EXPERT, D_MODEL), D_EXPERT ** -0.5 * BETA),
        'ln2_g': 1.0 + nrm(ks[24], (DEPTH, D_MODEL), 0.02),
        'ln2_b': nrm(ks[25], (DEPTH, D_MODEL), 0.02),
    }


def reference(x_prompt, x_sample, meta, ln_in_g, ln_in_b, rel_bias, w_in, w_att_branch, sink,
              conv_w, conv_b, conv_ln_g, conv_ln_b, w_conv_out, w_out, ln1_g, ln1_b, w_group,
              b_group, w_router, b_router, w_gate, w_up, w_down, ln2_g, ln2_b):
    y_prompt = encode(x_prompt, meta, ln_in_g, ln_in_b, rel_bias, w_in, w_att_branch, sink, conv_w,
                      conv_b, conv_ln_g, conv_ln_b, w_conv_out, w_out, ln1_g, ln1_b, w_group, b_group,
                      w_router, b_router, w_gate, w_up, w_down, ln2_g, ln2_b)
    y_sample = encode(x_sample, meta, ln_in_g, ln_in_b, rel_bias, w_in, w_att_branch, sink, conv_w,
                      conv_b, conv_ln_g, conv_ln_b, w_conv_out, w_out, ln1_g, ln1_b, w_group, b_group,
                      w_router, b_router, w_gate, w_up, w_down, ln2_g, ln2_b)
    return (y_prompt, y_sample)
```

```python
import functools
import math

import jax
import jax.numpy as jnp
from jax import lax
from jax.experimental import pallas as pl
from jax.experimental.pallas import tpu as pltpu

D_MODEL = 1024
N_META = 16
BLOCK = 128
WINDOW = 128
N_Q_HEADS = 8
N_KV_HEADS = 2
HEAD_DIM = 64
ATT_WIDTH = N_Q_HEADS * HEAD_DIM
KV_WIDTH = N_KV_HEADS * HEAD_DIM
CONV_WIDTH = D_MODEL // 2
CONV_K = 31
N_BUCKETS = 32
MAX_DISTANCE = 128
N_GROUPS = 4
EXPERTS_PER_GROUP = 8
N_EXPERTS = N_GROUPS * EXPERTS_PER_GROUP
D_EXPERT = 256
LN_EPS = 1e-5
DEPTH = 1
ALPHA = (2 * DEPTH) ** 0.25
NEG = -1e30
Q_END = ATT_WIDTH
K_END = Q_END + KV_WIDTH
V_END = K_END + KV_WIDTH
GLU_END = V_END + 2 * CONV_WIDTH
GA_END = GLU_END + D_MODEL

N_PAIRS = N_Q_HEADS // 2
LANES = 128
CONV_HALO = 16
ROUTER_LANES = 128

TILE_PROJ = 512
TILE_ATTN = 512
TILE_CONV = 512
CONV_ROWS = 64
TILE_OUT = 512
TILE_MOE = 1024
VMEM_LIMIT = 56 * 1024 * 1024

BF16 = jnp.bfloat16
F32 = jnp.float32


def _layer_norm(x, g, b):
    mu = jnp.mean(x, axis=-1, keepdims=True)
    xc = x - mu
    var = jnp.mean(xc * xc, axis=-1, keepdims=True)
    return xc * lax.rsqrt(var + LN_EPS) * g + b


def _dot(a, b):
    return jnp.dot(a, b, preferred_element_type=F32)


def _dot_nt(a, b):
    return lax.dot_general(a, b, (((1,), (1,)), ((), ())), preferred_element_type=F32)


def _proj_kernel(x_ref, g_ref, b_ref, wq_ref, wkv_ref, wglu_ref, wg_ref,
                 q_ref, kv_ref, zc_ref, ga_ref, gc_ref):
    h = _layer_norm(x_ref[0], g_ref[...], b_ref[...]).astype(BF16)
    q_ref[0] = _dot(h, wq_ref[...]).astype(BF16)
    kv_ref[0] = _dot(h, wkv_ref[...]).astype(BF16)
    u = _dot(h, wglu_ref[...])
    zc_ref[0] = u[:, :CONV_WIDTH] * jax.nn.sigmoid(u[:, CONV_WIDTH:])
    gates = _dot(h, wg_ref[...])
    ga_ref[0] = jax.nn.sigmoid(gates[:, :D_MODEL]).astype(BF16)
    gc_ref[0] = jax.nn.sigmoid(gates[:, D_MODEL:]).astype(BF16)


def _proj(x, ln_g, ln_b, wq, wkv, wglu, wg, tile):
    B, S, D = x.shape
    grid = (B, S // tile)
    row = lambda w: pl.BlockSpec((1, tile, w), lambda b, i: (b, i, 0))
    full = lambda a: pl.BlockSpec(a.shape, lambda b, i: (0,) * a.ndim)
    return pl.pallas_call(
        _proj_kernel,
        grid=grid,
        in_specs=[row(D), full(ln_g), full(ln_b), full(wq), full(wkv), full(wglu), full(wg)],
        out_specs=[row(ATT_WIDTH), row(2 * KV_WIDTH), row(CONV_WIDTH), row(D_MODEL), row(D_MODEL)],
        out_shape=[
            jax.ShapeDtypeStruct((B, S, ATT_WIDTH), BF16),
            jax.ShapeDtypeStruct((B, S, 2 * KV_WIDTH), BF16),
            jax.ShapeDtypeStruct((B, S, CONV_WIDTH), F32),
            jax.ShapeDtypeStruct((B, S, D_MODEL), BF16),
            jax.ShapeDtypeStruct((B, S, D_MODEL), BF16),
        ],
        compiler_params=pltpu.CompilerParams(
            dimension_semantics=("parallel", "parallel"), vmem_limit_bytes=VMEM_LIMIT),
        name="proj",
    )(x, ln_g, ln_b, wq, wkv, wglu, wg)


def _attn_kernel(q_ref, kvp_ref, kvc_ref, kvn_ref, kvm_ref, bias_ref, mbias_ref, sink_ref, o_ref, *, tile):
    i = pl.program_id(1)
    n_tiles = pl.num_programs(1)
    blocks = tile // BLOCK
    scale = HEAD_DIM ** -0.5

    lane = lax.broadcasted_iota(jnp.int32, (1, LANES), 1)
    lo = lane < HEAD_DIM

    def split_heads(t):
        z = jnp.zeros_like(t)
        return jnp.where(lo, t, z), jnp.where(lo, z, t)

    kv_ext = jnp.concatenate([kvp_ref[0], kvc_ref[0], kvn_ref[0]], axis=0)
    k_ext = kv_ext[:, :KV_WIDTH] * jnp.asarray(scale, BF16)
    v_ext = kv_ext[:, KV_WIDTH:]
    ka, kb = split_heads(k_ext)
    va, vb = split_heads(v_ext)
    kma, kmb = split_heads(kvm_ref[:, :KV_WIDTH] * jnp.asarray(scale, BF16))
    vma, vmb = split_heads(kvm_ref[:, KV_WIDTH:])
    km_cat = jnp.concatenate([kma, kmb], axis=0)
    vm_cat = jnp.concatenate([vma, vmb], axis=0)

    mlane = lax.broadcasted_iota(jnp.int32, (1, 2 * N_META), 1)
    m_first = mlane < N_META
    sink = sink_ref[...]
    lane_o = lax.broadcasted_iota(jnp.int32, (1, LANES), 1) < HEAD_DIM

    for j in range(blocks):
        first = jnp.logical_and(i == 0, j == 0)
        last = jnp.logical_and(i == n_tiles - 1, j == blocks - 1)
        variant = jnp.where(first, 0, jnp.where(last, 2, 1))
        mvariant = jnp.where(first, 0, 1)

        r0 = j * BLOCK
        qb = q_ref[0, r0:r0 + BLOCK, :]
        q4 = jnp.concatenate([qb[:, p * LANES:(p + 1) * LANES] for p in range(N_PAIRS)], axis=0)
        k_cat = jnp.concatenate([ka[r0:r0 + 3 * BLOCK], kb[r0:r0 + 3 * BLOCK]], axis=0)
        v_cat = jnp.concatenate([va[r0:r0 + 3 * BLOCK], vb[r0:r0 + 3 * BLOCK]], axis=0)

        s = _dot_nt(q4, k_cat) + bias_ref[variant]
        sm = _dot_nt(q4, km_cat) + mbias_ref[mvariant]

        s_a, s_b = s[:, :3 * BLOCK], s[:, 3 * BLOCK:]
        sm_a = jnp.where(m_first, sm, NEG)
        sm_b = jnp.where(m_first, NEG, sm)
        m_a = jnp.maximum(jnp.maximum(jnp.max(s_a, axis=1, keepdims=True),
                                      jnp.max(sm_a, axis=1, keepdims=True)), sink[:, 0:1])
        m_b = jnp.maximum(jnp.maximum(jnp.max(s_b, axis=1, keepdims=True),
                                      jnp.max(sm_b, axis=1, keepdims=True)), sink[:, 1:2])
        p_a = jnp.exp(s_a - m_a)
        p_b = jnp.exp(s_b - m_b)
        pm = jnp.exp(jnp.where(m_first, sm - m_a, sm - m_b))
        l_a = (jnp.sum(p_a, axis=1, keepdims=True) + jnp.sum(jnp.where(m_first, pm, 0.0), axis=1, keepdims=True)
               + jnp.exp(sink[:, 0:1] - m_a))
        l_b = (jnp.sum(p_b, axis=1, keepdims=True) + jnp.sum(jnp.where(m_first, 0.0, pm), axis=1, keepdims=True)
               + jnp.exp(sink[:, 1:2] - m_b))
        p = jnp.concatenate([p_a, p_b], axis=1).astype(BF16)
        o = _dot(p, v_cat) + _dot(pm.astype(BF16), vm_cat)
        o = o * jnp.where(lane_o, 1.0 / l_a, 1.0 / l_b)
        for pr in range(N_PAIRS):
            o_ref[0, r0:r0 + BLOCK, pr * LANES:(pr + 1) * LANES] = o[pr * BLOCK:(pr + 1) * BLOCK].astype(BF16)


def _attn(q, kv, kv_meta, bias, mbias, sink_tab, tile):
    B, S, _ = q.shape
    n_tiles = S // tile
    bpt = tile // BLOCK
    n_blocks = S // BLOCK
    grid = (B, n_tiles)
    full = lambda a: pl.BlockSpec(a.shape, lambda b, i: (0,) * a.ndim)
    return pl.pallas_call(
        functools.partial(_attn_kernel, tile=tile),
        grid=grid,
        in_specs=[
            pl.BlockSpec((1, tile, ATT_WIDTH), lambda b, i: (b, i, 0)),
            pl.BlockSpec((1, BLOCK, 2 * KV_WIDTH), lambda b, i: (b, jnp.maximum(i * bpt - 1, 0), 0)),
            pl.BlockSpec((1, tile, 2 * KV_WIDTH), lambda b, i: (b, i, 0)),
            pl.BlockSpec((1, BLOCK, 2 * KV_WIDTH), lambda b, i: (b, jnp.minimum((i + 1) * bpt, n_blocks - 1), 0)),
            full(kv_meta), full(bias), full(mbias), full(sink_tab),
        ],
        out_specs=pl.BlockSpec((1, tile, ATT_WIDTH), lambda b, i: (b, i, 0)),
        out_shape=jax.ShapeDtypeStruct((B, S, ATT_WIDTH), BF16),
        compiler_params=pltpu.CompilerParams(
            dimension_semantics=("parallel", "parallel"), vmem_limit_bytes=VMEM_LIMIT),
        name="attn",
    )(q, kv, kv, kv, kv_meta, bias, mbias, sink_tab)


def _conv_kernel(zp_ref, zc_ref, zn_ref, zm_ref, w_ref, cb_ref, g_ref, b_ref, o_ref, ext_ref, *, tile):
    i = pl.program_id(1)
    n_tiles = pl.num_programs(1)
    ext_ref[0:CONV_HALO, :] = jnp.where(i == 0, zm_ref[...], zp_ref[0])
    ext_ref[CONV_HALO:CONV_HALO + tile, :] = zc_ref[0]
    ext_ref[CONV_HALO + tile:, :] = jnp.where(i == n_tiles - 1, 0.0, zn_ref[0])
    cb, g, b = cb_ref[...], g_ref[...], b_ref[...]
    off = CONV_HALO - CONV_K // 2
    for r in range(tile // CONV_ROWS):
        r0 = r * CONV_ROWS
        acc = jnp.zeros((CONV_ROWS, CONV_WIDTH), F32)
        for k in range(CONV_K):
            acc = acc + ext_ref[r0 + off + k:r0 + off + k + CONV_ROWS, :] * w_ref[k:k + 1, :]
        y = _layer_norm(acc + cb, g, b)
        o_ref[0, r0:r0 + CONV_ROWS, :] = (y * jax.nn.sigmoid(y)).astype(BF16)


def _conv(zc, z_meta, conv_w, conv_b, ln_g, ln_b, tile):
    B, S, C = zc.shape
    n_tiles = S // tile
    hpt = tile // CONV_HALO
    n_halo = S // CONV_HALO
    full = lambda a: pl.BlockSpec(a.shape, lambda b, i: (0,) * a.ndim)
    return pl.pallas_call(
        functools.partial(_conv_kernel, tile=tile),
        grid=(B, n_tiles),
        in_specs=[
            pl.BlockSpec((1, CONV_HALO, C), lambda b, i: (b, jnp.maximum(i * hpt - 1, 0), 0)),
            pl.BlockSpec((1, tile, C), lambda b, i: (b, i, 0)),
            pl.BlockSpec((1, CONV_HALO, C), lambda b, i: (b, jnp.minimum((i + 1) * hpt, n_halo - 1), 0)),
            full(z_meta), full(conv_w), full(conv_b), full(ln_g), full(ln_b),
        ],
        out_specs=pl.BlockSpec((1, tile, C), lambda b, i: (b, i, 0)),
        out_shape=jax.ShapeDtypeStruct((B, S, C), BF16),
        scratch_shapes=[pltpu.VMEM((tile + 2 * CONV_HALO, C), F32)],
        compiler_params=pltpu.CompilerParams(
            dimension_semantics=("parallel", "parallel"), vmem_limit_bytes=VMEM_LIMIT),
        name="conv",
    )(zc, zc, zc, z_meta, conv_w, conv_b, ln_g, ln_b)


def _route(r):
    lane = lax.broadcasted_iota(jnp.int32, r.shape, 1)
    big = jnp.int32(1 << 20)
    is_g = jnp.logical_and(lane >= N_EXPERTS, lane < N_EXPERTS + N_GROUPS)
    lg = jnp.where(is_g, r, -jnp.inf)
    mg = jnp.max(lg, axis=1, keepdims=True)
    g_w = 1.0 / jnp.sum(jnp.exp(lg - mg), axis=1, keepdims=True)
    g_idx = jnp.min(jnp.where(lg == mg, lane - N_EXPERTS, big), axis=1, keepdims=True)
    in_group = jnp.logical_and(lane < N_EXPERTS, jnp.right_shift(lane, 3) == g_idx)
    le = jnp.where(in_group, r, -jnp.inf)
    m1 = jnp.max(le, axis=1, keepdims=True)
    den = jnp.sum(jnp.exp(le - m1), axis=1, keepdims=True)
    i1 = jnp.min(jnp.where(le == m1, lane, big), axis=1, keepdims=True)
    le2 = jnp.where(lane == i1, -jnp.inf, le)
    m2 = jnp.max(le2, axis=1, keepdims=True)
    i2 = jnp.min(jnp.where(le2 == m2, lane, big), axis=1, keepdims=True)
    p1 = 1.0 / den
    p2 = jnp.exp(m2 - m1) / den
    tot = p1 + p2
    w1 = g_w * (p1 / tot)
    w2 = g_w * (p2 / tot)
    return jnp.where(lane == i1, w1, 0.0) + jnp.where(lane == i2, w2, 0.0)


def _out_kernel(x_ref, att_ref, cz_ref, ga_ref, gc_ref, lng_ref, lnb_ref, watt_ref, wco_ref, wout_ref,
                l1g_ref, l1b_ref, wr_ref, br_ref, h1_ref, comb_ref):
    h0 = _layer_norm(x_ref[0], lng_ref[...], lnb_ref[...])
    a = _dot(att_ref[0], watt_ref[...])
    c = _dot(cz_ref[0], wco_ref[...])
    mix = ga_ref[0].astype(F32) * a + gc_ref[0].astype(F32) * c
    m = _dot(mix.astype(BF16), wout_ref[...])
    h1 = _layer_norm(ALPHA * h0 + m, l1g_ref[...], l1b_ref[...])
    h1_ref[0] = h1
    r = _dot(h1.astype(BF16), wr_ref[...]) + br_ref[...]
    comb_ref[0] = _route(r)


def _out(x, att, cz, ga, gc, ln_g, ln_b, watt, wco, wout, l1g, l1b, wr, br, tile):
    B, S, D = x.shape
    row = lambda w: pl.BlockSpec((1, tile, w), lambda b, i: (b, i, 0))
    full = lambda a: pl.BlockSpec(a.shape, lambda b, i: (0,) * a.ndim)
    return pl.pallas_call(
        _out_kernel,
        grid=(B, S // tile),
        in_specs=[row(D), row(ATT_WIDTH), row(CONV_WIDTH), row(D), row(D), full(ln_g), full(ln_b),
                  full(watt), full(wco), full(wout), full(l1g), full(l1b), full(wr), full(br)],
        out_specs=[row(D), row(ROUTER_LANES)],
        out_shape=[jax.ShapeDtypeStruct((B, S, D), F32), jax.ShapeDtypeStruct((B, S, ROUTER_LANES), F32)],
        compiler_params=pltpu.CompilerParams(
            dimension_semantics=("parallel", "parallel"), vmem_limit_bytes=VMEM_LIMIT),
        name="out",
    )(x, att, cz, ga, gc, ln_g, ln_b, watt, wco, wout, l1g, l1b, wr, br)


def _moe_kernel(h_ref, comb_ref, wg_ref, wu_ref, wd_ref, l2g_ref, l2b_ref, o_ref, acc_ref):
    e = pl.program_id(1)

    @pl.when(e == 0)
    def _():
        acc_ref[...] = jnp.zeros_like(acc_ref)

    h = h_ref[...]
    hb = h.astype(BF16)
    comb = comb_ref[...]
    lane = lax.broadcasted_iota(jnp.int32, comb.shape, 1)
    c = jnp.sum(jnp.where(lane == e, comb, 0.0), axis=1, keepdims=True)
    gt = _dot(hb, wg_ref[...])
    up = _dot(hb, wu_ref[...])
    hid = (gt * jax.nn.sigmoid(gt)) * up * c
    acc_ref[...] += _dot(hid.astype(BF16), wd_ref[...])

    @pl.when(e == pl.num_programs(1) - 1)
    def _():
        o_ref[...] = _layer_norm(ALPHA * h + acc_ref[...], l2g_ref[...], l2b_ref[...])


def _moe(h1, comb, w_gate, w_up, w_down, l2g, l2b, tile):
    T, D = h1.shape
    full = lambda a: pl.BlockSpec(a.shape, lambda t, e: (0,) * a.ndim)
    return pl.pallas_call(
        _moe_kernel,
        grid=(T // tile, N_EXPERTS),
        in_specs=[
            pl.BlockSpec((tile, D), lambda t, e: (t, 0)),
            pl.BlockSpec((tile, ROUTER_LANES), lambda t, e: (t, 0)),
            pl.BlockSpec((None, D, D_EXPERT), lambda t, e: (e, 0, 0)),
            pl.BlockSpec((None, D, D_EXPERT), lambda t, e: (e, 0, 0)),
            pl.BlockSpec((None, D_EXPERT, D), lambda t, e: (e, 0, 0)),
            full(l2g), full(l2b),
        ],
        out_specs=pl.BlockSpec((tile, D), lambda t, e: (t, 0)),
        out_shape=jax.ShapeDtypeStruct((T, D), F32),
        scratch_shapes=[pltpu.VMEM((tile, D), F32)],
        compiler_params=pltpu.CompilerParams(
            dimension_semantics=("parallel", "arbitrary"), vmem_limit_bytes=VMEM_LIMIT),
        name="moe",
    )(h1, comb, w_gate, w_up, w_down, l2g, l2b)


def _t5_bucket(rel):
    half = N_BUCKETS // 2
    max_exact = half // 2
    ret = jnp.where(rel > 0, half, 0)
    n = jnp.abs(rel)
    nf = jnp.maximum(n, 1).astype(F32)
    large = max_exact + (jnp.log(nf / max_exact) / math.log(MAX_DISTANCE / max_exact)
                         * (half - max_exact)).astype(jnp.int32)
    large = jnp.minimum(large, half - 1)
    return ret + jnp.where(n < max_exact, n, large)


def _pair_rows(t):
    return jnp.concatenate([t[:N_PAIRS], t[N_PAIRS:]], axis=-1).reshape(N_PAIRS * BLOCK, -1)


def _bias_tables(rel_bias, sink):
    qi = jnp.arange(BLOCK)
    kj = jnp.arange(3 * BLOCK) - BLOCK
    rel = kj[None, :] - qi[:, None]
    band = rel_bias[_t5_bucket(rel)].astype(F32).transpose(2, 0, 1)
    in_win = (jnp.abs(rel) <= WINDOW)[None]
    not_prev = (kj >= 0)[None, None, :]
    not_next = (kj < BLOCK)[None, None, :]
    variants = [jnp.where(in_win & not_prev, band, NEG),
                jnp.where(in_win, band, NEG),
                jnp.where(in_win & not_next, band, NEG)]
    bias = jnp.stack([_pair_rows(v) for v in variants])

    off = BLOCK - N_META
    mvars = []
    for blk in (1, 2):
        qpos = blk * BLOCK + qi - off
        meta_rel = jnp.arange(N_META)[None, :] - qpos[:, None]
        mvars.append(_pair_rows(rel_bias[_t5_bucket(meta_rel)].astype(F32).transpose(2, 0, 1)))
    mbias = jnp.stack(mvars)
    s = sink.astype(F32)
    sink_tab = jnp.repeat(jnp.stack([s[:N_PAIRS], s[N_PAIRS:]], axis=-1), BLOCK, axis=0)
    return bias, mbias, sink_tab


def kernel(x_prompt, x_sample, meta, ln_in_g, ln_in_b, rel_bias, w_in, w_att_branch, sink, conv_w, conv_b,
           conv_ln_g, conv_ln_b, w_conv_out, w_out, ln1_g, ln1_b, w_group, b_group, w_router, b_router,
           w_gate, w_up, w_down, ln2_g, ln2_b):
    row = lambda v: v.reshape(1, -1).astype(F32)
    perm = jnp.array([h * HEAD_DIM + d for p in range(N_PAIRS) for h in (p, p + N_PAIRS) for d in range(HEAD_DIM)])
    w = w_in[0]
    wq = w[:, :Q_END][:, perm].astype(BF16)
    wkv = w[:, Q_END:V_END].astype(BF16)
    wglu = w[:, V_END:GLU_END].astype(BF16)
    wg = w[:, GLU_END:].astype(BF16)
    watt = w_att_branch[0][perm, :].astype(BF16)
    wco = w_conv_out[0].astype(BF16)
    wout = w_out[0].astype(BF16)
    wr = jnp.zeros((D_MODEL, ROUTER_LANES), F32)
    wr = wr.at[:, :N_EXPERTS].set(w_router[0]).at[:, N_EXPERTS:N_EXPERTS + N_GROUPS].set(w_group[0]).astype(BF16)
    br = jnp.zeros((1, ROUTER_LANES), F32)
    br = br.at[0, :N_EXPERTS].set(b_router[0]).at[0, N_EXPERTS:N_EXPERTS + N_GROUPS].set(b_group[0])
    wge, wue, wde = w_gate[0].astype(BF16), w_up[0].astype(BF16), w_down[0].astype(BF16)
    ln_g, ln_b = row(ln_in_g), row(ln_in_b)
    bias, mbias, sink_tab = _bias_tables(rel_bias, sink[0])

    xm = jnp.concatenate([jnp.zeros((BLOCK - N_META, D_MODEL), F32), meta.astype(F32)], axis=0)[None]
    _, kv_m, zc_m, _, _ = _proj(xm, ln_g, ln_b, wq, wkv, wglu, wg, BLOCK)
    kv_meta = kv_m[0, BLOCK - N_META:]
    z_meta = zc_m[0, BLOCK - N_META:]

    def encode(x):
        B, S, D = x.shape
        q, kv, zc, ga, gc = _proj(x, ln_g, ln_b, wq, wkv, wglu, wg, TILE_PROJ)
        att = _attn(q, kv, kv_meta, bias, mbias, sink_tab, TILE_ATTN)
        cz = _conv(zc, z_meta, conv_w[0], row(conv_b[0]), row(conv_ln_g[0]), row(conv_ln_b[0]), TILE_CONV)
        h1, comb = _out(x, att, cz, ga, gc, ln_g, ln_b, watt, wco, wout, row(ln1_g[0]), row(ln1_b[0]), wr, br,
                        TILE_OUT)
        y = _moe(h1.reshape(B * S, D), comb.reshape(B * S, ROUTER_LANES), wge, wue, wde,
                 row(ln2_g[0]), row(ln2_b[0]), TILE_MOE)
        return y.reshape(B, S, D)

    return encode(x_prompt), encode(x_sample)
```

```python
import functools
import math

import jax
import jax.numpy as jnp
from jax import lax
from jax.experimental import pallas as pl
from jax.experimental.pallas import tpu as pltpu
from jax.experimental.pallas import tpu_sc as plsc

D_MODEL = 1024
N_META = 16
BLOCK = 128
WINDOW = 128
N_Q_HEADS = 8
N_KV_HEADS = 2
HEAD_DIM = 64
ATT_WIDTH = N_Q_HEADS * HEAD_DIM
KV_WIDTH = N_KV_HEADS * HEAD_DIM
CONV_WIDTH = D_MODEL // 2
CONV_K = 31
N_BUCKETS = 32
MAX_DISTANCE = 128
N_GROUPS = 4
EXPERTS_PER_GROUP = 8
N_EXPERTS = N_GROUPS * EXPERTS_PER_GROUP
D_EXPERT = 256
LN_EPS = 1e-5
DEPTH = 1
ALPHA = (2 * DEPTH) ** 0.25
NEG = -1e30
Q_END = ATT_WIDTH
K_END = Q_END + KV_WIDTH
V_END = K_END + KV_WIDTH
GLU_END = V_END + 2 * CONV_WIDTH
GA_END = GLU_END + D_MODEL

N_PAIRS = N_Q_HEADS // 2
LANES = 128
CONV_HALO = 16
ROUTER_LANES = 128

TILE_PROJ = 512
TILE_ATTN = 512
TILE_CONV = 512
CONV_ROWS = 64
TILE_OUT = 512
TILE_EXPERT = 512
TILE_FINAL = 512
PACK_WORDS = 4
SC_WINDOW = 128
VMEM_LIMIT = 56 * 1024 * 1024

BF16 = jnp.bfloat16
F32 = jnp.float32


def _layer_norm(x, g, b):
    mu = jnp.mean(x, axis=-1, keepdims=True)
    xc = x - mu
    var = jnp.mean(xc * xc, axis=-1, keepdims=True)
    return xc * lax.rsqrt(var + LN_EPS) * g + b


def _dot(a, b):
    return jnp.dot(a, b, preferred_element_type=F32)


def _dot_nt(a, b):
    return lax.dot_general(a, b, (((1,), (1,)), ((), ())), preferred_element_type=F32)


def _proj_kernel(x_ref, g_ref, b_ref, wq_ref, wkv_ref, wglu_ref, wg_ref,
                 q_ref, kv_ref, zc_ref, ga_ref, gc_ref):
    h = _layer_norm(x_ref[0], g_ref[...], b_ref[...]).astype(BF16)
    q_ref[0] = _dot(h, wq_ref[...]).astype(BF16)
    kv_ref[0] = _dot(h, wkv_ref[...]).astype(BF16)
    u = _dot(h, wglu_ref[...])
    zc_ref[0] = u[:, :CONV_WIDTH] * jax.nn.sigmoid(u[:, CONV_WIDTH:])
    gates = _dot(h, wg_ref[...])
    ga_ref[0] = jax.nn.sigmoid(gates[:, :D_MODEL]).astype(BF16)
    gc_ref[0] = jax.nn.sigmoid(gates[:, D_MODEL:]).astype(BF16)


def _proj(x, ln_g, ln_b, wq, wkv, wglu, wg, tile):
    B, S, D = x.shape
    grid = (B, S // tile)
    row = lambda w: pl.BlockSpec((1, tile, w), lambda b, i: (b, i, 0))
    full = lambda a: pl.BlockSpec(a.shape, lambda b, i: (0,) * a.ndim)
    return pl.pallas_call(
        _proj_kernel,
        grid=grid,
        in_specs=[row(D), full(ln_g), full(ln_b), full(wq), full(wkv), full(wglu), full(wg)],
        out_specs=[row(ATT_WIDTH), row(2 * KV_WIDTH), row(CONV_WIDTH), row(D_MODEL), row(D_MODEL)],
        out_shape=[
            jax.ShapeDtypeStruct((B, S, ATT_WIDTH), BF16),
            jax.ShapeDtypeStruct((B, S, 2 * KV_WIDTH), BF16),
            jax.ShapeDtypeStruct((B, S, CONV_WIDTH), F32),
            jax.ShapeDtypeStruct((B, S, D_MODEL), BF16),
            jax.ShapeDtypeStruct((B, S, D_MODEL), BF16),
        ],
        compiler_params=pltpu.CompilerParams(
            dimension_semantics=("parallel", "parallel"), vmem_limit_bytes=VMEM_LIMIT),
        name="proj",
    )(x, ln_g, ln_b, wq, wkv, wglu, wg)


def _attn_kernel(q_ref, kvp_ref, kvc_ref, kvn_ref, kvm_ref, bias_ref, mbias_ref, sink_ref, o_ref, *, tile):
    i = pl.program_id(1)
    n_tiles = pl.num_programs(1)
    blocks = tile // BLOCK
    scale = HEAD_DIM ** -0.5

    lane = lax.broadcasted_iota(jnp.int32, (1, LANES), 1)
    lo = lane < HEAD_DIM

    def split_heads(t):
        z = jnp.zeros_like(t)
        return jnp.where(lo, t, z), jnp.where(lo, z, t)

    kv_ext = jnp.concatenate([kvp_ref[0], kvc_ref[0], kvn_ref[0]], axis=0)
    k_ext = kv_ext[:, :KV_WIDTH] * jnp.asarray(scale, BF16)
    v_ext = kv_ext[:, KV_WIDTH:]
    ka, kb = split_heads(k_ext)
    va, vb = split_heads(v_ext)
    kma, kmb = split_heads(kvm_ref[:, :KV_WIDTH] * jnp.asarray(scale, BF16))
    vma, vmb = split_heads(kvm_ref[:, KV_WIDTH:])
    km_cat = jnp.concatenate([kma, kmb], axis=0)
    vm_cat = jnp.concatenate([vma, vmb], axis=0)

    mlane = lax.broadcasted_iota(jnp.int32, (1, 2 * N_META), 1)
    m_first = mlane < N_META
    sink = sink_ref[...]
    lane_o = lax.broadcasted_iota(jnp.int32, (1, LANES), 1) < HEAD_DIM

    for j in range(blocks):
        first = jnp.logical_and(i == 0, j == 0)
        last = jnp.logical_and(i == n_tiles - 1, j == blocks - 1)
        variant = jnp.where(first, 0, jnp.where(last, 2, 1))
        mvariant = jnp.where(first, 0, 1)

        r0 = j * BLOCK
        qb = q_ref[0, r0:r0 + BLOCK, :]
        q4 = jnp.concatenate([qb[:, p * LANES:(p + 1) * LANES] for p in range(N_PAIRS)], axis=0)
        k_cat = jnp.concatenate([ka[r0:r0 + 3 * BLOCK], kb[r0:r0 + 3 * BLOCK]], axis=0)
        v_cat = jnp.concatenate([va[r0:r0 + 3 * BLOCK], vb[r0:r0 + 3 * BLOCK]], axis=0)

        s = _dot_nt(q4, k_cat) + bias_ref[variant]
        sm = _dot_nt(q4, km_cat) + mbias_ref[mvariant]

        s_a, s_b = s[:, :3 * BLOCK], s[:, 3 * BLOCK:]
        sm_a = jnp.where(m_first, sm, NEG)
        sm_b = jnp.where(m_first, NEG, sm)
        m_a = jnp.maximum(jnp.maximum(jnp.max(s_a, axis=1, keepdims=True),
                                      jnp.max(sm_a, axis=1, keepdims=True)), sink[:, 0:1])
        m_b = jnp.maximum(jnp.maximum(jnp.max(s_b, axis=1, keepdims=True),
                                      jnp.max(sm_b, axis=1, keepdims=True)), sink[:, 1:2])
        p_a = jnp.exp(s_a - m_a)
        p_b = jnp.exp(s_b - m_b)
        pm = jnp.exp(jnp.where(m_first, sm - m_a, sm - m_b))
        l_a = (jnp.sum(p_a, axis=1, keepdims=True) + jnp.sum(jnp.where(m_first, pm, 0.0), axis=1, keepdims=True)
               + jnp.exp(sink[:, 0:1] - m_a))
        l_b = (jnp.sum(p_b, axis=1, keepdims=True) + jnp.sum(jnp.where(m_first, 0.0, pm), axis=1, keepdims=True)
               + jnp.exp(sink[:, 1:2] - m_b))
        p = jnp.concatenate([p_a, p_b], axis=1).astype(BF16)
        o = _dot(p, v_cat) + _dot(pm.astype(BF16), vm_cat)
        o = o * jnp.where(lane_o, 1.0 / l_a, 1.0 / l_b)
        for pr in range(N_PAIRS):
            o_ref[0, r0:r0 + BLOCK, pr * LANES:(pr + 1) * LANES] = o[pr * BLOCK:(pr + 1) * BLOCK].astype(BF16)


def _attn(q, kv, kv_meta, bias, mbias, sink_tab, tile):
    B, S, _ = q.shape
    n_tiles = S // tile
    bpt = tile // BLOCK
    n_blocks = S // BLOCK
    grid = (B, n_tiles)
    full = lambda a: pl.BlockSpec(a.shape, lambda b, i: (0,) * a.ndim)
    return pl.pallas_call(
        functools.partial(_attn_kernel, tile=tile),
        grid=grid,
        in_specs=[
            pl.BlockSpec((1, tile, ATT_WIDTH), lambda b, i: (b, i, 0)),
            pl.BlockSpec((1, BLOCK, 2 * KV_WIDTH), lambda b, i: (b, jnp.maximum(i * bpt - 1, 0), 0)),
            pl.BlockSpec((1, tile, 2 * KV_WIDTH), lambda b, i: (b, i, 0)),
            pl.BlockSpec((1, BLOCK, 2 * KV_WIDTH), lambda b, i: (b, jnp.minimum((i + 1) * bpt, n_blocks - 1), 0)),
            full(kv_meta), full(bias), full(mbias), full(sink_tab),
        ],
        out_specs=pl.BlockSpec((1, tile, ATT_WIDTH), lambda b, i: (b, i, 0)),
        out_shape=jax.ShapeDtypeStruct((B, S, ATT_WIDTH), BF16),
        compiler_params=pltpu.CompilerParams(
            dimension_semantics=("parallel", "parallel"), vmem_limit_bytes=VMEM_LIMIT),
        name="attn",
    )(q, kv, kv, kv, kv_meta, bias, mbias, sink_tab)


def _conv_kernel(zp_ref, zc_ref, zn_ref, zm_ref, w_ref, cb_ref, g_ref, b_ref, o_ref, ext_ref, *, tile):
    i = pl.program_id(1)
    n_tiles = pl.num_programs(1)
    ext_ref[0:CONV_HALO, :] = jnp.where(i == 0, zm_ref[...], zp_ref[0])
    ext_ref[CONV_HALO:CONV_HALO + tile, :] = zc_ref[0]
    ext_ref[CONV_HALO + tile:, :] = jnp.where(i == n_tiles - 1, 0.0, zn_ref[0])
    cb, g, b = cb_ref[...], g_ref[...], b_ref[...]
    off = CONV_HALO - CONV_K // 2
    for r in range(tile // CONV_ROWS):
        r0 = r * CONV_ROWS
        acc = jnp.zeros((CONV_ROWS, CONV_WIDTH), F32)
        for k in range(CONV_K):
            acc = acc + ext_ref[r0 + off + k:r0 + off + k + CONV_ROWS, :] * w_ref[k:k + 1, :]
        y = _layer_norm(acc + cb, g, b)
        o_ref[0, r0:r0 + CONV_ROWS, :] = (y * jax.nn.sigmoid(y)).astype(BF16)


def _conv(zc, z_meta, conv_w, conv_b, ln_g, ln_b, tile):
    B, S, C = zc.shape
    n_tiles = S // tile
    hpt = tile // CONV_HALO
    n_halo = S // CONV_HALO
    full = lambda a: pl.BlockSpec(a.shape, lambda b, i: (0,) * a.ndim)
    return pl.pallas_call(
        functools.partial(_conv_kernel, tile=tile),
        grid=(B, n_tiles),
        in_specs=[
            pl.BlockSpec((1, CONV_HALO, C), lambda b, i: (b, jnp.maximum(i * hpt - 1, 0), 0)),
            pl.BlockSpec((1, tile, C), lambda b, i: (b, i, 0)),
            pl.BlockSpec((1, CONV_HALO, C), lambda b, i: (b, jnp.minimum((i + 1) * hpt, n_halo - 1), 0)),
            full(z_meta), full(conv_w), full(conv_b), full(ln_g), full(ln_b),
        ],
        out_specs=pl.BlockSpec((1, tile, C), lambda b, i: (b, i, 0)),
        out_shape=jax.ShapeDtypeStruct((B, S, C), BF16),
        scratch_shapes=[pltpu.VMEM((tile + 2 * CONV_HALO, C), F32)],
        compiler_params=pltpu.CompilerParams(
            dimension_semantics=("parallel", "parallel"), vmem_limit_bytes=VMEM_LIMIT),
        name="conv",
    )(zc, zc, zc, z_meta, conv_w, conv_b, ln_g, ln_b)


def _route(r):
    lane = lax.broadcasted_iota(jnp.int32, r.shape, 1)
    big = jnp.int32(1 << 20)
    is_g = jnp.logical_and(lane >= N_EXPERTS, lane < N_EXPERTS + N_GROUPS)
    lg = jnp.where(is_g, r, -jnp.inf)
    mg = jnp.max(lg, axis=1, keepdims=True)
    g_w = 1.0 / jnp.sum(jnp.exp(lg - mg), axis=1, keepdims=True)
    g_idx = jnp.min(jnp.where(lg == mg, lane - N_EXPERTS, big), axis=1, keepdims=True)
    in_group = jnp.logical_and(lane < N_EXPERTS, jnp.right_shift(lane, 3) == g_idx)
    le = jnp.where(in_group, r, -jnp.inf)
    m1 = jnp.max(le, axis=1, keepdims=True)
    den = jnp.sum(jnp.exp(le - m1), axis=1, keepdims=True)
    i1 = jnp.min(jnp.where(le == m1, lane, big), axis=1, keepdims=True)
    le2 = jnp.where(lane == i1, -jnp.inf, le)
    m2 = jnp.max(le2, axis=1, keepdims=True)
    i2 = jnp.min(jnp.where(le2 == m2, lane, big), axis=1, keepdims=True)
    p1 = 1.0 / den
    p2 = jnp.exp(m2 - m1) / den
    tot = p1 + p2
    return i1, i2, g_w * (p1 / tot), g_w * (p2 / tot)


def _pack_bf16_pairs(x):
    half = x.shape[1] // 2
    words = []
    for j in range(half // LANES):
        lo = pltpu.bitcast(x[:, j * LANES:(j + 1) * LANES].astype(BF16).astype(F32), jnp.uint32)
        hi = pltpu.bitcast(x[:, half + j * LANES:half + (j + 1) * LANES].astype(BF16).astype(F32), jnp.uint32)
        words.append(hi | (lo >> 16))
    return words


def _unpack_bf16_pairs(words):
    lo = [pltpu.bitcast(w << 16, F32) for w in words]
    hi = [pltpu.bitcast(w & jnp.uint32(0xFFFF0000), F32) for w in words]
    return jnp.concatenate(lo + hi, axis=1)


def _out_kernel(x_ref, att_ref, cz_ref, ga_ref, gc_ref, lng_ref, lnb_ref, watt_ref, wco_ref, wout_ref,
                l1g_ref, l1b_ref, wr_ref, br_ref, h1_ref, h1p_ref, rt_ref, cnt_ref, *, tile):
    first = jnp.logical_and(pl.program_id(0) == 0, pl.program_id(1) == 0)

    @pl.when(first)
    def _():
        cnt_ref[...] = jnp.zeros_like(cnt_ref)

    h0 = _layer_norm(x_ref[0], lng_ref[...], lnb_ref[...])
    a = _dot(att_ref[0], watt_ref[...])
    c = _dot(cz_ref[0], wco_ref[...])
    mix = ga_ref[0].astype(F32) * a + gc_ref[0].astype(F32) * c
    m = _dot(mix.astype(BF16), wout_ref[...])
    h1 = _layer_norm(ALPHA * h0 + m, l1g_ref[...], l1b_ref[...])
    h1_ref[0] = h1
    for j, w in enumerate(_pack_bf16_pairs(h1)):
        h1p_ref[0, :, j, :, :] = w.reshape(tile // 8, 8, LANES)

    r = _dot(h1.astype(BF16), wr_ref[...]) + br_ref[...]
    i1, i2, w1, w2 = _route(r)
    lane = lax.broadcasted_iota(jnp.int32, (tile, ROUTER_LANES), 1)
    hit1, hit2 = lane == i1, lane == i2
    onehot = jnp.where(jnp.logical_or(hit1, hit2), 1.0, 0.0)
    row_i = lax.broadcasted_iota(jnp.int32, (tile, tile), 0)
    col_i = lax.broadcasted_iota(jnp.int32, (tile, tile), 1)
    before = jnp.where(col_i < row_i, 1.0, 0.0).astype(BF16)
    seen = _dot(before, onehot.astype(BF16)) + cnt_ref[0:1, :]
    rank1 = jnp.sum(jnp.where(hit1, seen, 0.0), axis=1, keepdims=True)
    rank2 = jnp.sum(jnp.where(hit2, seen, 0.0), axis=1, keepdims=True)
    cnt_ref[...] = cnt_ref[...] + jnp.sum(onehot, axis=0, keepdims=True)
    fields = (i1.astype(F32), i2.astype(F32), w1, w2, rank1, rank2)
    rt = jnp.zeros((tile, ROUTER_LANES), F32)
    for k, v in enumerate(fields):
        rt = jnp.where(lane == k, v, rt)
    rt_ref[0] = rt


def _out(x, att, cz, ga, gc, ln_g, ln_b, watt, wco, wout, l1g, l1b, wr, br, tile):
    B, S, D = x.shape
    row = lambda w: pl.BlockSpec((1, tile, w), lambda b, i: (b, i, 0))
    full = lambda a: pl.BlockSpec(a.shape, lambda b, i: (0,) * a.ndim)
    return pl.pallas_call(
        functools.partial(_out_kernel, tile=tile),
        grid=(B, S // tile),
        in_specs=[row(D), row(ATT_WIDTH), row(CONV_WIDTH), row(D), row(D), full(ln_g), full(ln_b),
                  full(watt), full(wco), full(wout), full(l1g), full(l1b), full(wr), full(br)],
        out_specs=[row(D),
                   pl.BlockSpec((1, tile // 8, PACK_WORDS, 8, LANES), lambda b, i: (b, i, 0, 0, 0)),
                   row(ROUTER_LANES),
                   pl.BlockSpec((8, ROUTER_LANES), lambda b, i: (0, 0))],
        out_shape=[jax.ShapeDtypeStruct((B, S, D), F32),
                   jax.ShapeDtypeStruct((B, S // 8, PACK_WORDS, 8, LANES), jnp.uint32),
                   jax.ShapeDtypeStruct((B, S, ROUTER_LANES), F32),
                   jax.ShapeDtypeStruct((8, ROUTER_LANES), F32)],
        compiler_params=pltpu.CompilerParams(
            dimension_semantics=("arbitrary", "arbitrary"), vmem_limit_bytes=VMEM_LIMIT),
        name="out",
    )(x, att, cz, ga, gc, ln_g, ln_b, watt, wco, wout, l1g, l1b, wr, br)


def _sc_scatter2(src, idx_a, idx_b, n_out):
    m = src.shape[0]
    mesh = plsc.VectorSubcoreMesh(core_axis_name="c", subcore_axis_name="s")

    @functools.partial(pl.kernel, out_type=jax.ShapeDtypeStruct((n_out, LANES), src.dtype), mesh=mesh)
    def k(x_hbm, ia_hbm, ib_hbm, o_hbm):
        def body(x_vmem, ia_vmem, ib_vmem):
            pltpu.sync_copy(x_vmem, o_hbm.at[ia_vmem.at[0]])
            pltpu.sync_copy(x_vmem, o_hbm.at[ib_vmem.at[0]])

        pltpu.emit_pipeline(
            body, grid=(m // SC_WINDOW,),
            in_specs=[pl.BlockSpec((SC_WINDOW, LANES), index_map=lambda i: (i, 0)),
                      pl.BlockSpec((1, SC_WINDOW), index_map=lambda i: (0, i)),
                      pl.BlockSpec((1, SC_WINDOW), index_map=lambda i: (0, i))],
            out_specs=[],
            core_axis_name=("c", "s"), dimension_semantics=(pltpu.PARALLEL,),
        )(x_hbm, ia_hbm, ib_hbm)

    return k(src, idx_a.reshape(1, m), idx_b.reshape(1, m))


def _sc_gather(table, idx):
    m = idx.shape[0]
    mesh = plsc.VectorSubcoreMesh(core_axis_name="c", subcore_axis_name="s")

    @functools.partial(pl.kernel, out_type=jax.ShapeDtypeStruct((m, LANES), table.dtype), mesh=mesh)
    def k(x_hbm, i_hbm, o_hbm):
        def body(i_vmem, o_vmem):
            pltpu.sync_copy(x_hbm.at[i_vmem.at[0]], o_vmem)

        pltpu.emit_pipeline(
            body, grid=(m // SC_WINDOW,),
            in_specs=[pl.BlockSpec((1, SC_WINDOW), index_map=lambda i: (0, i))],
            out_specs=[pl.BlockSpec((SC_WINDOW, LANES), index_map=lambda i: (i, 0))],
            core_axis_name=("c", "s"), dimension_semantics=(pltpu.PARALLEL,),
        )(i_hbm, o_hbm)

    return k(table, idx.reshape(1, m))


def _expert_kernel(te_ref, tv_ref, xs_ref, wg_ref, wu_ref, wd_ref, ys_ref, *, tile):
    n = pl.program_id(0)
    valid = tv_ref[n]

    @pl.when(valid > 0)
    def _():
        x = _unpack_bf16_pairs([xs_ref[:, j, :, :].reshape(tile, LANES) for j in range(PACK_WORDS)])
        rows = lax.broadcasted_iota(jnp.int32, (tile, 1), 0)
        x = jnp.where(rows < valid, x, 0.0).astype(BF16)
        gt = _dot(x, wg_ref[...])
        up = _dot(x, wu_ref[...])
        hid = (gt * jax.nn.sigmoid(gt)) * up
        y = _dot(hid.astype(BF16), wd_ref[...])
        for j, w in enumerate(_pack_bf16_pairs(y)):
            ys_ref[:, j, :, :] = w.reshape(tile // 8, 8, LANES)

    @pl.when(valid <= 0)
    def _():
        ys_ref[...] = jnp.zeros_like(ys_ref)


def _experts(tile_expert, tile_valid, xs, w_gate, w_up, w_down, tile):
    n_tiles = xs.shape[0] * 8 // tile
    blk = pl.BlockSpec((tile // 8, PACK_WORDS, 8, LANES), lambda n, te, tv: (n, 0, 0, 0))
    return pl.pallas_call(
        functools.partial(_expert_kernel, tile=tile),
        grid_spec=pltpu.PrefetchScalarGridSpec(
            num_scalar_prefetch=2,
            grid=(n_tiles,),
            in_specs=[blk,
                      pl.BlockSpec((None, D_MODEL, D_EXPERT), lambda n, te, tv: (te[n], 0, 0)),
                      pl.BlockSpec((None, D_MODEL, D_EXPERT), lambda n, te, tv: (te[n], 0, 0)),
                      pl.BlockSpec((None, D_EXPERT, D_MODEL), lambda n, te, tv: (te[n], 0, 0))],
            out_specs=blk,
        ),
        out_shape=jax.ShapeDtypeStruct(xs.shape, jnp.uint32),
        compiler_params=pltpu.CompilerParams(
            dimension_semantics=("arbitrary",), vmem_limit_bytes=VMEM_LIMIT),
        name="experts",
    )(tile_expert, tile_valid, xs, w_gate, w_up, w_down)


def _final_kernel(h_ref, g_ref, rt_ref, l2g_ref, l2b_ref, o_ref, *, tile):
    rt = rt_ref[0]
    lane = lax.broadcasted_iota(jnp.int32, rt.shape, 1)
    w1 = jnp.sum(jnp.where(lane == 2, rt, 0.0), axis=1, keepdims=True)
    w2 = jnp.sum(jnp.where(lane == 3, rt, 0.0), axis=1, keepdims=True)
    y1 = _unpack_bf16_pairs([g_ref[0, 0, :, j, :, :].reshape(tile, LANES) for j in range(PACK_WORDS)])
    y2 = _unpack_bf16_pairs([g_ref[1, 0, :, j, :, :].reshape(tile, LANES) for j in range(PACK_WORDS)])
    f = w1 * y1 + w2 * y2
    o_ref[0] = _layer_norm(ALPHA * h_ref[0] + f, l2g_ref[...], l2b_ref[...])


def _final(h1, g, rt, l2g, l2b, tile):
    B, S, D = h1.shape
    row = lambda w: pl.BlockSpec((1, tile, w), lambda b, i: (b, i, 0))
    full = lambda a: pl.BlockSpec(a.shape, lambda b, i: (0,) * a.ndim)
    return pl.pallas_call(
        functools.partial(_final_kernel, tile=tile),
        grid=(B, S // tile),
        in_specs=[row(D),
                  pl.BlockSpec((2, 1, tile // 8, PACK_WORDS, 8, LANES), lambda b, i: (0, b, i, 0, 0, 0)),
                  row(ROUTER_LANES), full(l2g), full(l2b)],
        out_specs=row(D),
        out_shape=jax.ShapeDtypeStruct((B, S, D), F32),
        compiler_params=pltpu.CompilerParams(
            dimension_semantics=("parallel", "parallel"), vmem_limit_bytes=VMEM_LIMIT),
        name="final",
    )(h1, g, rt, l2g, l2b)


def _dispatch_plan(rt, counts, n_tokens, tile):
    cnt = counts[0, :N_EXPERTS].astype(jnp.int32)
    padded = (cnt + tile - 1) // tile * tile
    base = jnp.cumsum(padded) - padded
    rtf = rt.reshape(n_tokens, ROUTER_LANES)
    e_ids = jnp.arange(N_EXPERTS, dtype=jnp.int32)

    def dest(col_e, col_r):
        e = rtf[:, col_e].astype(jnp.int32)
        seg = jnp.sum(jnp.where(e[:, None] == e_ids[None, :], base[None, :], 0), axis=1)
        pos = seg + rtf[:, col_r].astype(jnp.int32)
        p = pos.reshape(n_tokens // 8, 1, 8)
        j = jnp.arange(PACK_WORDS, dtype=jnp.int32).reshape(1, PACK_WORDS, 1)
        return ((p // 8) * (8 * PACK_WORDS) + j * 8 + p % 8).reshape(-1)

    n_tiles = (2 * n_tokens) // tile + N_EXPERTS
    start = jnp.arange(n_tiles, dtype=jnp.int32) * tile
    seg_end = base + padded
    te = jnp.minimum(jnp.sum((start[:, None] >= seg_end[None, :]).astype(jnp.int32), axis=1), N_EXPERTS - 1)
    te_base = jnp.sum(jnp.where(te[:, None] == e_ids[None, :], base[None, :], 0), axis=1)
    te_cnt = jnp.sum(jnp.where(te[:, None] == e_ids[None, :], cnt[None, :], 0), axis=1)
    tv = jnp.clip(te_cnt - (start - te_base), 0, tile)
    return dest(0, 4), dest(1, 5), te, tv, n_tiles


def _t5_bucket(rel):
    half = N_BUCKETS // 2
    max_exact = half // 2
    ret = jnp.where(rel > 0, half, 0)
    n = jnp.abs(rel)
    nf = jnp.maximum(n, 1).astype(F32)
    large = max_exact + (jnp.log(nf / max_exact) / math.log(MAX_DISTANCE / max_exact)
                         * (half - max_exact)).astype(jnp.int32)
    large = jnp.minimum(large, half - 1)
    return ret + jnp.where(n < max_exact, n, large)


def _pair_rows(t):
    return jnp.concatenate([t[:N_PAIRS], t[N_PAIRS:]], axis=-1).reshape(N_PAIRS * BLOCK, -1)


def _bias_tables(rel_bias, sink):
    qi = jnp.arange(BLOCK)
    kj = jnp.arange(3 * BLOCK) - BLOCK
    rel = kj[None, :] - qi[:, None]
    band = rel_bias[_t5_bucket(rel)].astype(F32).transpose(2, 0, 1)
    in_win = (jnp.abs(rel) <= WINDOW)[None]
    not_prev = (kj >= 0)[None, None, :]
    not_next = (kj < BLOCK)[None, None, :]
    variants = [jnp.where(in_win & not_prev, band, NEG),
                jnp.where(in_win, band, NEG),
                jnp.where(in_win & not_next, band, NEG)]
    bias = jnp.stack([_pair_rows(v) for v in variants])

    off = BLOCK - N_META
    mvars = []
    for blk in (1, 2):
        qpos = blk * BLOCK + qi - off
        meta_rel = jnp.arange(N_META)[None, :] - qpos[:, None]
        mvars.append(_pair_rows(rel_bias[_t5_bucket(meta_rel)].astype(F32).transpose(2, 0, 1)))
    mbias = jnp.stack(mvars)
    s = sink.astype(F32)
    sink_tab = jnp.repeat(jnp.stack([s[:N_PAIRS], s[N_PAIRS:]], axis=-1), BLOCK, axis=0)
    return bias, mbias, sink_tab


def kernel(x_prompt, x_sample, meta, ln_in_g, ln_in_b, rel_bias, w_in, w_att_branch, sink, conv_w, conv_b,
           conv_ln_g, conv_ln_b, w_conv_out, w_out, ln1_g, ln1_b, w_group, b_group, w_router, b_router,
           w_gate, w_up, w_down, ln2_g, ln2_b):
    row = lambda v: v.reshape(1, -1).astype(F32)
    w = w_in[0]
    wq = (w[:, :Q_END].reshape(D_MODEL, 2, N_PAIRS, HEAD_DIM).transpose(0, 2, 1, 3)
          .reshape(D_MODEL, ATT_WIDTH).astype(BF16))
    watt = (w_att_branch[0].reshape(2, N_PAIRS, HEAD_DIM, D_MODEL).transpose(1, 0, 2, 3)
            .reshape(ATT_WIDTH, D_MODEL).astype(BF16))
    wkv = w[:, Q_END:V_END].astype(BF16)
    wglu = w[:, V_END:GLU_END].astype(BF16)
    wg = w[:, GLU_END:].astype(BF16)
    wco = w_conv_out[0].astype(BF16)
    wout = w_out[0].astype(BF16)
    wr = jnp.zeros((D_MODEL, ROUTER_LANES), F32)
    wr = wr.at[:, :N_EXPERTS].set(w_router[0]).at[:, N_EXPERTS:N_EXPERTS + N_GROUPS].set(w_group[0]).astype(BF16)
    br = jnp.zeros((1, ROUTER_LANES), F32)
    br = br.at[0, :N_EXPERTS].set(b_router[0]).at[0, N_EXPERTS:N_EXPERTS + N_GROUPS].set(b_group[0])
    wge, wue, wde = w_gate[0].astype(BF16), w_up[0].astype(BF16), w_down[0].astype(BF16)
    ln_g, ln_b = row(ln_in_g), row(ln_in_b)
    bias, mbias, sink_tab = _bias_tables(rel_bias, sink[0])

    xm = jnp.concatenate([jnp.zeros((BLOCK - N_META, D_MODEL), F32), meta.astype(F32)], axis=0)[None]
    _, kv_m, zc_m, _, _ = _proj(xm, ln_g, ln_b, wq, wkv, wglu, wg, BLOCK)
    kv_meta = kv_m[0, BLOCK - N_META:]
    z_meta = zc_m[0, BLOCK - N_META:]

    def encode(x):
        B, S, D = x.shape
        T = B * S
        q, kv, zc, ga, gc = _proj(x, ln_g, ln_b, wq, wkv, wglu, wg, TILE_PROJ)
        att = _attn(q, kv, kv_meta, bias, mbias, sink_tab, TILE_ATTN)
        cz = _conv(zc, z_meta, conv_w[0], row(conv_b[0]), row(conv_ln_g[0]), row(conv_ln_b[0]), TILE_CONV)
        h1, h1p, rt, counts = _out(x, att, cz, ga, gc, ln_g, ln_b, watt, wco, wout, row(ln1_g[0]), row(ln1_b[0]),
                                   wr, br, TILE_OUT)
        idx1, idx2, tile_expert, tile_valid, n_tiles = _dispatch_plan(rt, counts, T, TILE_EXPERT)
        n_rows = n_tiles * TILE_EXPERT
        xs = _sc_scatter2(h1p.reshape(T * PACK_WORDS, LANES), idx1, idx2, n_rows * PACK_WORDS)
        ys = _experts(tile_expert, tile_valid, xs.reshape(n_rows // 8, PACK_WORDS, 8, LANES), wge, wue, wde,
                      TILE_EXPERT)
        g = _sc_gather(ys.reshape(n_rows * PACK_WORDS, LANES), jnp.concatenate([idx1, idx2]))
        return _final(h1, g.reshape(2, B, S // 8, PACK_WORDS, 8, LANES), rt, row(ln2_g[0]), row(ln2_b[0]),
                      TILE_FINAL)

    return encode(x_prompt), encode(x_sample)
```

```python
import functools
import math

import jax
import jax.numpy as jnp
from jax import lax
from jax.experimental import pallas as pl
from jax.experimental.pallas import tpu as pltpu
from jax.experimental.pallas import tpu_sc as plsc

D_MODEL = 1024
N_META = 16
BLOCK = 128
WINDOW = 128
N_Q_HEADS = 8
N_KV_HEADS = 2
HEAD_DIM = 64
ATT_WIDTH = N_Q_HEADS * HEAD_DIM
KV_WIDTH = N_KV_HEADS * HEAD_DIM
CONV_WIDTH = D_MODEL // 2
CONV_K = 31
N_BUCKETS = 32
MAX_DISTANCE = 128
N_GROUPS = 4
EXPERTS_PER_GROUP = 8
N_EXPERTS = N_GROUPS * EXPERTS_PER_GROUP
D_EXPERT = 256
LN_EPS = 1e-5
DEPTH = 1
ALPHA = (2 * DEPTH) ** 0.25
NEG = -1e30
Q_END = ATT_WIDTH
K_END = Q_END + KV_WIDTH
V_END = K_END + KV_WIDTH
GLU_END = V_END + 2 * CONV_WIDTH
GA_END = GLU_END + D_MODEL

N_PAIRS = N_Q_HEADS // 2
LANES = 128
CONV_HALO = 16
ROUTER_LANES = 128

TILE_PROJ = 512
TILE_ATTN = 512
TILE_CONV = 512
CONV_ROWS = 128
LN_ROWS = 64
SHIFT_ROWS = 128
TILE_OUT = 512
TILE_EXPERT = 512
TILE_FINAL = 512
PACK_WORDS = 4
SC_WINDOW = 128
VMEM_LIMIT = 56 * 1024 * 1024

BF16 = jnp.bfloat16
F32 = jnp.float32


def _layer_norm(x, g, b):
    mu = jnp.mean(x, axis=-1, keepdims=True)
    xc = x - mu
    var = jnp.mean(xc * xc, axis=-1, keepdims=True)
    return xc * lax.rsqrt(var + LN_EPS) * g + b


def _dot(a, b):
    return jnp.dot(a, b, preferred_element_type=F32)


def _dot_nt(a, b):
    return lax.dot_general(a, b, (((1,), (1,)), ((), ())), preferred_element_type=F32)


def _proj_kernel(x_ref, g_ref, b_ref, wq_ref, wkv_ref, wglu_ref, wg_ref,
                 q_ref, kv_ref, zc_ref, ga_ref, gc_ref):
    h = _layer_norm(x_ref[0], g_ref[...], b_ref[...]).astype(BF16)
    q_ref[0] = _dot(h, wq_ref[...]).astype(BF16)
    kv_ref[0] = _dot(h, wkv_ref[...]).astype(BF16)
    u = _dot(h, wglu_ref[...])
    zc_ref[0] = u[:, :CONV_WIDTH] * jax.nn.sigmoid(u[:, CONV_WIDTH:])
    gates = _dot(h, wg_ref[...])
    ga_ref[0] = jax.nn.sigmoid(gates[:, :D_MODEL]).astype(BF16)
    gc_ref[0] = jax.nn.sigmoid(gates[:, D_MODEL:]).astype(BF16)


def _proj(x, ln_g, ln_b, wq, wkv, wglu, wg, tile):
    B, S, D = x.shape
    grid = (B, S // tile)
    row = lambda w: pl.BlockSpec((1, tile, w), lambda b, i: (b, i, 0))
    full = lambda a: pl.BlockSpec(a.shape, lambda b, i: (0,) * a.ndim)
    return pl.pallas_call(
        _proj_kernel,
        grid=grid,
        in_specs=[row(D), full(ln_g), full(ln_b), full(wq), full(wkv), full(wglu), full(wg)],
        out_specs=[row(ATT_WIDTH), row(2 * KV_WIDTH), row(CONV_WIDTH), row(D_MODEL), row(D_MODEL)],
        out_shape=[
            jax.ShapeDtypeStruct((B, S, ATT_WIDTH), BF16),
            jax.ShapeDtypeStruct((B, S, 2 * KV_WIDTH), BF16),
            jax.ShapeDtypeStruct((B, S, CONV_WIDTH), F32),
            jax.ShapeDtypeStruct((B, S, D_MODEL), BF16),
            jax.ShapeDtypeStruct((B, S, D_MODEL), BF16),
        ],
        compiler_params=pltpu.CompilerParams(
            dimension_semantics=("parallel", "parallel"), vmem_limit_bytes=VMEM_LIMIT),
        name="proj",
    )(x, ln_g, ln_b, wq, wkv, wglu, wg)


def _attn_kernel(q_ref, kvp_ref, kvc_ref, kvn_ref, kvm_ref, bias_ref, mbias_ref, sink_ref, o_ref, *, tile):
    i = pl.program_id(1)
    n_tiles = pl.num_programs(1)
    blocks = tile // BLOCK
    scale = HEAD_DIM ** -0.5

    lane = lax.broadcasted_iota(jnp.int32, (1, LANES), 1)
    lo = lane < HEAD_DIM

    def split_heads(t):
        z = jnp.zeros_like(t)
        return jnp.where(lo, t, z), jnp.where(lo, z, t)

    kv_ext = jnp.concatenate([kvp_ref[0], kvc_ref[0], kvn_ref[0]], axis=0)
    k_ext = kv_ext[:, :KV_WIDTH] * jnp.asarray(scale, BF16)
    v_ext = kv_ext[:, KV_WIDTH:]
    ka, kb = split_heads(k_ext)
    va, vb = split_heads(v_ext)
    kma, kmb = split_heads(kvm_ref[:, :KV_WIDTH] * jnp.asarray(scale, BF16))
    vma, vmb = split_heads(kvm_ref[:, KV_WIDTH:])
    km_cat = jnp.concatenate([kma, kmb], axis=0)
    vm_cat = jnp.concatenate([vma, vmb], axis=0)

    mlane = lax.broadcasted_iota(jnp.int32, (1, 2 * N_META), 1)
    m_first = mlane < N_META
    sink = sink_ref[...]
    lane_o = lax.broadcasted_iota(jnp.int32, (1, LANES), 1) < HEAD_DIM

    for j in range(blocks):
        first = jnp.logical_and(i == 0, j == 0)
        last = jnp.logical_and(i == n_tiles - 1, j == blocks - 1)
        variant = jnp.where(first, 0, jnp.where(last, 2, 1))
        mvariant = jnp.where(first, 0, 1)

        r0 = j * BLOCK
        qb = q_ref[0, r0:r0 + BLOCK, :]
        q4 = jnp.concatenate([qb[:, p * LANES:(p + 1) * LANES] for p in range(N_PAIRS)], axis=0)
        k_cat = jnp.concatenate([ka[r0:r0 + 3 * BLOCK], kb[r0:r0 + 3 * BLOCK]], axis=0)
        v_cat = jnp.concatenate([va[r0:r0 + 3 * BLOCK], vb[r0:r0 + 3 * BLOCK]], axis=0)

        s = _dot_nt(q4, k_cat) + bias_ref[variant]
        sm = _dot_nt(q4, km_cat) + mbias_ref[mvariant]

        s_a, s_b = s[:, :3 * BLOCK], s[:, 3 * BLOCK:]
        sm_a = jnp.where(m_first, sm, NEG)
        sm_b = jnp.where(m_first, NEG, sm)
        m_a = jnp.maximum(jnp.maximum(jnp.max(s_a, axis=1, keepdims=True),
                                      jnp.max(sm_a, axis=1, keepdims=True)), sink[:, 0:1])
        m_b = jnp.maximum(jnp.maximum(jnp.max(s_b, axis=1, keepdims=True),
                                      jnp.max(sm_b, axis=1, keepdims=True)), sink[:, 1:2])
        p_a = jnp.exp(s_a - m_a)
        p_b = jnp.exp(s_b - m_b)
        pm = jnp.exp(jnp.where(m_first, sm - m_a, sm - m_b))
        l_a = (jnp.sum(p_a, axis=1, keepdims=True) + jnp.sum(jnp.where(m_first, pm, 0.0), axis=1, keepdims=True)
               + jnp.exp(sink[:, 0:1] - m_a))
        l_b = (jnp.sum(p_b, axis=1, keepdims=True) + jnp.sum(jnp.where(m_first, 0.0, pm), axis=1, keepdims=True)
               + jnp.exp(sink[:, 1:2] - m_b))
        p = jnp.concatenate([p_a, p_b], axis=1).astype(BF16)
        o = _dot(p, v_cat) + _dot(pm.astype(BF16), vm_cat)
        o = o * jnp.where(lane_o, 1.0 / l_a, 1.0 / l_b)
        for pr in range(N_PAIRS):
            o_ref[0, r0:r0 + BLOCK, pr * LANES:(pr + 1) * LANES] = o[pr * BLOCK:(pr + 1) * BLOCK].astype(BF16)


def _attn(q, kv, kv_meta, bias, mbias, sink_tab, tile):
    B, S, _ = q.shape
    n_tiles = S // tile
    bpt = tile // BLOCK
    n_blocks = S // BLOCK
    grid = (B, n_tiles)
    full = lambda a: pl.BlockSpec(a.shape, lambda b, i: (0,) * a.ndim)
    return pl.pallas_call(
        functools.partial(_attn_kernel, tile=tile),
        grid=grid,
        in_specs=[
            pl.BlockSpec((1, tile, ATT_WIDTH), lambda b, i: (b, i, 0)),
            pl.BlockSpec((1, BLOCK, 2 * KV_WIDTH), lambda b, i: (b, jnp.maximum(i * bpt - 1, 0), 0)),
            pl.BlockSpec((1, tile, 2 * KV_WIDTH), lambda b, i: (b, i, 0)),
            pl.BlockSpec((1, BLOCK, 2 * KV_WIDTH), lambda b, i: (b, jnp.minimum((i + 1) * bpt, n_blocks - 1), 0)),
            full(kv_meta), full(bias), full(mbias), full(sink_tab),
        ],
        out_specs=pl.BlockSpec((1, tile, ATT_WIDTH), lambda b, i: (b, i, 0)),
        out_shape=jax.ShapeDtypeStruct((B, S, ATT_WIDTH), BF16),
        compiler_params=pltpu.CompilerParams(
            dimension_semantics=("parallel", "parallel"), vmem_limit_bytes=VMEM_LIMIT),
        name="attn",
    )(q, kv, kv, kv, kv_meta, bias, mbias, sink_tab)


def _conv_kernel(zp_ref, zc_ref, zn_ref, zm_ref, w_ref, cb_ref, g_ref, b_ref, o_ref, ext_ref, sh_ref, y_ref, *, tile):
    i = pl.program_id(1)
    n_tiles = pl.num_programs(1)
    ext_ref[0:CONV_HALO, :] = jnp.where(i == 0, zm_ref[...], zp_ref[0])
    ext_ref[CONV_HALO:CONV_HALO + tile, :] = zc_ref[0]
    ext_ref[CONV_HALO + tile:, :] = jnp.where(i == n_tiles - 1, 0.0, zn_ref[0])
    off = CONV_HALO - CONV_K // 2
    reach = (off + CONV_K - 1) // 8 * 8
    for p in range(1, 8):
        for r0 in range(0, tile + reach, SHIFT_ROWS):
            n = min(SHIFT_ROWS, tile + reach - r0)
            sh_ref[p - 1, r0:r0 + n, :] = ext_ref[r0 + p:r0 + p + n, :]

    def taps(r0, cs):
        acc = jnp.zeros((CONV_ROWS, LANES), F32)
        for k in range(CONV_K):
            p, a = (off + k) % 8, (off + k) // 8 * 8
            rows = slice(r0 + a, r0 + a + CONV_ROWS)
            win = ext_ref[rows, cs] if p == 0 else sh_ref[p - 1, rows, cs]
            acc = acc + win * w_ref[k:k + 1, cs]
        y_ref[r0:r0 + CONV_ROWS, cs] = acc

    for c in range(CONV_WIDTH // LANES):
        for r in range(tile // CONV_ROWS):
            pl.when(i >= 0)(functools.partial(taps, r * CONV_ROWS, slice(c * LANES, (c + 1) * LANES)))
    cb, g, b = cb_ref[...], g_ref[...], b_ref[...]
    for r in range(tile // LN_ROWS):
        r0 = r * LN_ROWS
        y = _layer_norm(y_ref[r0:r0 + LN_ROWS, :] + cb, g, b)
        o_ref[0, r0:r0 + LN_ROWS, :] = (y * jax.nn.sigmoid(y)).astype(BF16)


def _conv(zc, z_meta, conv_w, conv_b, ln_g, ln_b, tile):
    B, S, C = zc.shape
    n_tiles = S // tile
    hpt = tile // CONV_HALO
    n_halo = S // CONV_HALO
    full = lambda a: pl.BlockSpec(a.shape, lambda b, i: (0,) * a.ndim)
    return pl.pallas_call(
        functools.partial(_conv_kernel, tile=tile),
        grid=(B, n_tiles),
        in_specs=[
            pl.BlockSpec((1, CONV_HALO, C), lambda b, i: (b, jnp.maximum(i * hpt - 1, 0), 0)),
            pl.BlockSpec((1, tile, C), lambda b, i: (b, i, 0)),
            pl.BlockSpec((1, CONV_HALO, C), lambda b, i: (b, jnp.minimum((i + 1) * hpt, n_halo - 1), 0)),
            full(z_meta), full(conv_w), full(conv_b), full(ln_g), full(ln_b),
        ],
        out_specs=pl.BlockSpec((1, tile, C), lambda b, i: (b, i, 0)),
        out_shape=jax.ShapeDtypeStruct((B, S, C), BF16),
        scratch_shapes=[pltpu.VMEM((tile + 2 * CONV_HALO, C), F32),
                        pltpu.VMEM((7, tile + 2 * CONV_HALO - 8, C), F32),
                        pltpu.VMEM((tile, C), F32)],
        compiler_params=pltpu.CompilerParams(
            dimension_semantics=("parallel", "parallel"), vmem_limit_bytes=VMEM_LIMIT),
        name="conv",
    )(zc, zc, zc, z_meta, conv_w, conv_b, ln_g, ln_b)


def _route(r):
    lane = lax.broadcasted_iota(jnp.int32, r.shape, 1)
    big = jnp.int32(1 << 20)
    is_g = jnp.logical_and(lane >= N_EXPERTS, lane < N_EXPERTS + N_GROUPS)
    lg = jnp.where(is_g, r, -jnp.inf)
    mg = jnp.max(lg, axis=1, keepdims=True)
    g_w = 1.0 / jnp.sum(jnp.exp(lg - mg), axis=1, keepdims=True)
    g_idx = jnp.min(jnp.where(lg == mg, lane - N_EXPERTS, big), axis=1, keepdims=True)
    in_group = jnp.logical_and(lane < N_EXPERTS, jnp.right_shift(lane, 3) == g_idx)
    le = jnp.where(in_group, r, -jnp.inf)
    m1 = jnp.max(le, axis=1, keepdims=True)
    den = jnp.sum(jnp.exp(le - m1), axis=1, keepdims=True)
    i1 = jnp.min(jnp.where(le == m1, lane, big), axis=1, keepdims=True)
    le2 = jnp.where(lane == i1, -jnp.inf, le)
    m2 = jnp.max(le2, axis=1, keepdims=True)
    i2 = jnp.min(jnp.where(le2 == m2, lane, big), axis=1, keepdims=True)
    p1 = 1.0 / den
    p2 = jnp.exp(m2 - m1) / den
    tot = p1 + p2
    return i1, i2, g_w * (p1 / tot), g_w * (p2 / tot)


def _pack_bf16_pairs(x):
    half = x.shape[1] // 2
    words = []
    for j in range(half // LANES):
        lo = pltpu.bitcast(x[:, j * LANES:(j + 1) * LANES].astype(BF16).astype(F32), jnp.uint32)
        hi = pltpu.bitcast(x[:, half + j * LANES:half + (j + 1) * LANES].astype(BF16).astype(F32), jnp.uint32)
        words.append(hi | (lo >> 16))
    return words


def _unpack_bf16_pairs(words):
    lo = [pltpu.bitcast(w << 16, F32) for w in words]
    hi = [pltpu.bitcast(w & jnp.uint32(0xFFFF0000), F32) for w in words]
    return jnp.concatenate(lo + hi, axis=1)


def _out_kernel(x_ref, att_ref, cz_ref, ga_ref, gc_ref, lng_ref, lnb_ref, watt_ref, wco_ref, wout_ref,
                l1g_ref, l1b_ref, wr_ref, br_ref, h1_ref, h1p_ref, rt_ref, cnt_ref, *, tile):
    first = jnp.logical_and(pl.program_id(0) == 0, pl.program_id(1) == 0)

    @pl.when(first)
    def _():
        cnt_ref[...] = jnp.zeros_like(cnt_ref)

    h0 = _layer_norm(x_ref[0], lng_ref[...], lnb_ref[...])
    a = _dot(att_ref[0], watt_ref[...])
    c = _dot(cz_ref[0], wco_ref[...])
    mix = ga_ref[0].astype(F32) * a + gc_ref[0].astype(F32) * c
    m = _dot(mix.astype(BF16), wout_ref[...])
    h1 = _layer_norm(ALPHA * h0 + m, l1g_ref[...], l1b_ref[...])
    h1_ref[0] = h1
    for j, w in enumerate(_pack_bf16_pairs(h1)):
        h1p_ref[0, :, j, :, :] = w.reshape(tile // 8, 8, LANES)

    r = _dot(h1.astype(BF16), wr_ref[...]) + br_ref[...]
    i1, i2, w1, w2 = _route(r)
    lane = lax.broadcasted_iota(jnp.int32, (tile, ROUTER_LANES), 1)
    hit1, hit2 = lane == i1, lane == i2
    onehot = jnp.where(jnp.logical_or(hit1, hit2), 1.0, 0.0)
    row_i = lax.broadcasted_iota(jnp.int32, (tile, tile), 0)
    col_i = lax.broadcasted_iota(jnp.int32, (tile, tile), 1)
    before = jnp.where(col_i < row_i, 1.0, 0.0).astype(BF16)
    seen = _dot(before, onehot.astype(BF16)) + cnt_ref[0:1, :]
    rank1 = jnp.sum(jnp.where(hit1, seen, 0.0), axis=1, keepdims=True)
    rank2 = jnp.sum(jnp.where(hit2, seen, 0.0), axis=1, keepdims=True)
    cnt_ref[...] = cnt_ref[...] + jnp.sum(onehot, axis=0, keepdims=True)
    fields = (i1.astype(F32), i2.astype(F32), w1, w2, rank1, rank2)
    rt = jnp.zeros((tile, ROUTER_LANES), F32)
    for k, v in enumerate(fields):
        rt = jnp.where(lane == k, v, rt)
    rt_ref[0] = rt


def _out(x, att, cz, ga, gc, ln_g, ln_b, watt, wco, wout, l1g, l1b, wr, br, tile):
    B, S, D = x.shape
    row = lambda w: pl.BlockSpec((1, tile, w), lambda b, i: (b, i, 0))
    full = lambda a: pl.BlockSpec(a.shape, lambda b, i: (0,) * a.ndim)
    return pl.pallas_call(
        functools.partial(_out_kernel, tile=tile),
        grid=(B, S // tile),
        in_specs=[row(D), row(ATT_WIDTH), row(CONV_WIDTH), row(D), row(D), full(ln_g), full(ln_b),
                  full(watt), full(wco), full(wout), full(l1g), full(l1b), full(wr), full(br)],
        out_specs=[row(D),
                   pl.BlockSpec((1, tile // 8, PACK_WORDS, 8, LANES), lambda b, i: (b, i, 0, 0, 0)),
                   row(ROUTER_LANES),
                   pl.BlockSpec((8, ROUTER_LANES), lambda b, i: (0, 0))],
        out_shape=[jax.ShapeDtypeStruct((B, S, D), F32),
                   jax.ShapeDtypeStruct((B, S // 8, PACK_WORDS, 8, LANES), jnp.uint32),
                   jax.ShapeDtypeStruct((B, S, ROUTER_LANES), F32),
                   jax.ShapeDtypeStruct((8, ROUTER_LANES), F32)],
        compiler_params=pltpu.CompilerParams(
            dimension_semantics=("arbitrary", "arbitrary"), vmem_limit_bytes=VMEM_LIMIT),
        name="out",
    )(x, att, cz, ga, gc, ln_g, ln_b, watt, wco, wout, l1g, l1b, wr, br)


def _sc_scatter2(src, idx_a, idx_b, n_out):
    m = src.shape[0]
    mesh = plsc.VectorSubcoreMesh(core_axis_name="c", subcore_axis_name="s")

    @functools.partial(pl.kernel, out_type=jax.ShapeDtypeStruct((n_out, LANES), src.dtype), mesh=mesh)
    def k(x_hbm, ia_hbm, ib_hbm, o_hbm):
        def body(x_vmem, ia_vmem, ib_vmem):
            pltpu.sync_copy(x_vmem, o_hbm.at[ia_vmem.at[0]])
            pltpu.sync_copy(x_vmem, o_hbm.at[ib_vmem.at[0]])

        pltpu.emit_pipeline(
            body, grid=(m // SC_WINDOW,),
            in_specs=[pl.BlockSpec((SC_WINDOW, LANES), index_map=lambda i: (i, 0)),
                      pl.BlockSpec((1, SC_WINDOW), index_map=lambda i: (0, i)),
                      pl.BlockSpec((1, SC_WINDOW), index_map=lambda i: (0, i))],
            out_specs=[],
            core_axis_name=("c", "s"), dimension_semantics=(pltpu.PARALLEL,),
        )(x_hbm, ia_hbm, ib_hbm)

    return k(src, idx_a.reshape(1, m), idx_b.reshape(1, m))


def _sc_gather(table, idx):
    m = idx.shape[0]
    mesh = plsc.VectorSubcoreMesh(core_axis_name="c", subcore_axis_name="s")

    @functools.partial(pl.kernel, out_type=jax.ShapeDtypeStruct((m, LANES), table.dtype), mesh=mesh)
    def k(x_hbm, i_hbm, o_hbm):
        def body(i_vmem, o_vmem):
            pltpu.sync_copy(x_hbm.at[i_vmem.at[0]], o_vmem)

        pltpu.emit_pipeline(
            body, grid=(m // SC_WINDOW,),
            in_specs=[pl.BlockSpec((1, SC_WINDOW), index_map=lambda i: (0, i))],
            out_specs=[pl.BlockSpec((SC_WINDOW, LANES), index_map=lambda i: (i, 0))],
            core_axis_name=("c", "s"), dimension_semantics=(pltpu.PARALLEL,),
        )(i_hbm, o_hbm)

    return k(table, idx.reshape(1, m))


def _expert_kernel(te_ref, tv_ref, xs_ref, wg_ref, wu_ref, wd_ref, ys_ref, *, tile):
    n = pl.program_id(0)
    valid = tv_ref[n]

    @pl.when(valid > 0)
    def _():
        x = _unpack_bf16_pairs([xs_ref[:, j, :, :].reshape(tile, LANES) for j in range(PACK_WORDS)])
        rows = lax.broadcasted_iota(jnp.int32, (tile, 1), 0)
        x = jnp.where(rows < valid, x, 0.0).astype(BF16)
        gt = _dot(x, wg_ref[...])
        up = _dot(x, wu_ref[...])
        hid = (gt * jax.nn.sigmoid(gt)) * up
        y = _dot(hid.astype(BF16), wd_ref[...])
        for j, w in enumerate(_pack_bf16_pairs(y)):
            ys_ref[:, j, :, :] = w.reshape(tile // 8, 8, LANES)

    @pl.when(valid <= 0)
    def _():
        ys_ref[...] = jnp.zeros_like(ys_ref)


def _experts(tile_expert, tile_valid, xs, w_gate, w_up, w_down, tile):
    n_tiles = xs.shape[0] * 8 // tile
    blk = pl.BlockSpec((tile // 8, PACK_WORDS, 8, LANES), lambda n, te, tv: (n, 0, 0, 0))
    return pl.pallas_call(
        functools.partial(_expert_kernel, tile=tile),
        grid_spec=pltpu.PrefetchScalarGridSpec(
            num_scalar_prefetch=2,
            grid=(n_tiles,),
            in_specs=[blk,
                      pl.BlockSpec((None, D_MODEL, D_EXPERT), lambda n, te, tv: (te[n], 0, 0)),
                      pl.BlockSpec((None, D_MODEL, D_EXPERT), lambda n, te, tv: (te[n], 0, 0)),
                      pl.BlockSpec((None, D_EXPERT, D_MODEL), lambda n, te, tv: (te[n], 0, 0))],
            out_specs=blk,
        ),
        out_shape=jax.ShapeDtypeStruct(xs.shape, jnp.uint32),
        compiler_params=pltpu.CompilerParams(
            dimension_semantics=("arbitrary",), vmem_limit_bytes=VMEM_LIMIT),
        name="experts",
    )(tile_expert, tile_valid, xs, w_gate, w_up, w_down)


def _final_kernel(h_ref, g_ref, rt_ref, l2g_ref, l2b_ref, o_ref, *, tile):
    rt = rt_ref[0]
    lane = lax.broadcasted_iota(jnp.int32, rt.shape, 1)
    w1 = jnp.sum(jnp.where(lane == 2, rt, 0.0), axis=1, keepdims=True)
    w2 = jnp.sum(jnp.where(lane == 3, rt, 0.0), axis=1, keepdims=True)
    y1 = _unpack_bf16_pairs([g_ref[0, 0, :, j, :, :].reshape(tile, LANES) for j in range(PACK_WORDS)])
    y2 = _unpack_bf16_pairs([g_ref[1, 0, :, j, :, :].reshape(tile, LANES) for j in range(PACK_WORDS)])
    f = w1 * y1 + w2 * y2
    o_ref[0] = _layer_norm(ALPHA * h_ref[0] + f, l2g_ref[...], l2b_ref[...])


def _final(h1, g, rt, l2g, l2b, tile):
    B, S, D = h1.shape
    row = lambda w: pl.BlockSpec((1, tile, w), lambda b, i: (b, i, 0))
    full = lambda a: pl.BlockSpec(a.shape, lambda b, i: (0,) * a.ndim)
    return pl.pallas_call(
        functools.partial(_final_kernel, tile=tile),
        grid=(B, S // tile),
        in_specs=[row(D),
                  pl.BlockSpec((2, 1, tile // 8, PACK_WORDS, 8, LANES), lambda b, i: (0, b, i, 0, 0, 0)),
                  row(ROUTER_LANES), full(l2g), full(l2b)],
        out_specs=row(D),
        out_shape=jax.ShapeDtypeStruct((B, S, D), F32),
        compiler_params=pltpu.CompilerParams(
            dimension_semantics=("parallel", "parallel"), vmem_limit_bytes=VMEM_LIMIT),
        name="final",
    )(h1, g, rt, l2g, l2b)


def _dispatch_plan(rt, counts, n_tokens, tile):
    cnt = counts[0, :N_EXPERTS].astype(jnp.int32)
    padded = (cnt + tile - 1) // tile * tile
    base = jnp.cumsum(padded) - padded
    rtf = rt.reshape(n_tokens, ROUTER_LANES)
    e_ids = jnp.arange(N_EXPERTS, dtype=jnp.int32)

    def dest(col_e, col_r):
        e = rtf[:, col_e].astype(jnp.int32)
        seg = jnp.sum(jnp.where(e[:, None] == e_ids[None, :], base[None, :], 0), axis=1)
        pos = seg + rtf[:, col_r].astype(jnp.int32)
        p = pos.reshape(n_tokens // 8, 1, 8)
        j = jnp.arange(PACK_WORDS, dtype=jnp.int32).reshape(1, PACK_WORDS, 1)
        return ((p // 8) * (8 * PACK_WORDS) + j * 8 + p % 8).reshape(-1)

    n_tiles = (2 * n_tokens) // tile + N_EXPERTS
    start = jnp.arange(n_tiles, dtype=jnp.int32) * tile
    seg_end = base + padded
    te = jnp.minimum(jnp.sum((start[:, None] >= seg_end[None, :]).astype(jnp.int32), axis=1), N_EXPERTS - 1)
    te_base = jnp.sum(jnp.where(te[:, None] == e_ids[None, :], base[None, :], 0), axis=1)
    te_cnt = jnp.sum(jnp.where(te[:, None] == e_ids[None, :], cnt[None, :], 0), axis=1)
    tv = jnp.clip(te_cnt - (start - te_base), 0, tile)
    return dest(0, 4), dest(1, 5), te, tv, n_tiles


def _t5_bucket(rel):
    half = N_BUCKETS // 2
    max_exact = half // 2
    ret = jnp.where(rel > 0, half, 0)
    n = jnp.abs(rel)
    nf = jnp.maximum(n, 1).astype(F32)
    large = max_exact + (jnp.log(nf / max_exact) / math.log(MAX_DISTANCE / max_exact)
                         * (half - max_exact)).astype(jnp.int32)
    large = jnp.minimum(large, half - 1)
    return ret + jnp.where(n < max_exact, n, large)


def _bucket_bias(rel_bias, bucket):
    rb = rel_bias.astype(F32)
    out = jnp.zeros((N_Q_HEADS,) + bucket.shape, F32)
    for b in range(N_BUCKETS):
        out = out + jnp.where(bucket[None] == b, rb[b][:, None, None], 0.0)
    return out


def _pair_rows(t):
    return jnp.concatenate([t[:N_PAIRS], t[N_PAIRS:]], axis=-1).reshape(N_PAIRS * BLOCK, -1)


def _bias_tables(rel_bias, sink):
    qi = jnp.arange(BLOCK)
    kj = jnp.arange(3 * BLOCK) - BLOCK
    rel = kj[None, :] - qi[:, None]
    band = _bucket_bias(rel_bias, _t5_bucket(rel))
    in_win = (jnp.abs(rel) <= WINDOW)[None]
    not_prev = (kj >= 0)[None, None, :]
    not_next = (kj < BLOCK)[None, None, :]
    variants = [jnp.where(in_win & not_prev, band, NEG),
                jnp.where(in_win, band, NEG),
                jnp.where(in_win & not_next, band, NEG)]
    bias = jnp.stack([_pair_rows(v) for v in variants])

    off = BLOCK - N_META
    mvars = []
    for blk in (1, 2):
        qpos = blk * BLOCK + qi - off
        meta_rel = jnp.arange(N_META)[None, :] - qpos[:, None]
        mvars.append(_pair_rows(_bucket_bias(rel_bias, _t5_bucket(meta_rel))))
    mbias = jnp.stack(mvars)
    s = sink.astype(F32)
    sink_tab = jnp.repeat(jnp.stack([s[:N_PAIRS], s[N_PAIRS:]], axis=-1), BLOCK, axis=0)
    return bias, mbias, sink_tab


def kernel(x_prompt, x_sample, meta, ln_in_g, ln_in_b, rel_bias, w_in, w_att_branch, sink, conv_w, conv_b,
           conv_ln_g, conv_ln_b, w_conv_out, w_out, ln1_g, ln1_b, w_group, b_group, w_router, b_router,
           w_gate, w_up, w_down, ln2_g, ln2_b):
    row = lambda v: v.reshape(1, -1).astype(F32)
    w = w_in[0]
    wq = (w[:, :Q_END].reshape(D_MODEL, 2, N_PAIRS, HEAD_DIM).transpose(0, 2, 1, 3)
          .reshape(D_MODEL, ATT_WIDTH).astype(BF16))
    watt = (w_att_branch[0].reshape(2, N_PAIRS, HEAD_DIM, D_MODEL).transpose(1, 0, 2, 3)
            .reshape(ATT_WIDTH, D_MODEL).astype(BF16))
    wkv = w[:, Q_END:V_END].astype(BF16)
    wglu = w[:, V_END:GLU_END].astype(BF16)
    wg = w[:, GLU_END:].astype(BF16)
    wco = w_conv_out[0].astype(BF16)
    wout = w_out[0].astype(BF16)
    wr = jnp.zeros((D_MODEL, ROUTER_LANES), F32)
    wr = wr.at[:, :N_EXPERTS].set(w_router[0]).at[:, N_EXPERTS:N_EXPERTS + N_GROUPS].set(w_group[0]).astype(BF16)
    br = jnp.zeros((1, ROUTER_LANES), F32)
    br = br.at[0, :N_EXPERTS].set(b_router[0]).at[0, N_EXPERTS:N_EXPERTS + N_GROUPS].set(b_group[0])
    wge, wue, wde = w_gate[0].astype(BF16), w_up[0].astype(BF16), w_down[0].astype(BF16)
    ln_g, ln_b = row(ln_in_g), row(ln_in_b)
    bias, mbias, sink_tab = _bias_tables(rel_bias, sink[0])

    xm = jnp.concatenate([jnp.zeros((BLOCK - N_META, D_MODEL), F32), meta.astype(F32)], axis=0)[None]
    _, kv_m, zc_m, _, _ = _proj(xm, ln_g, ln_b, wq, wkv, wglu, wg, BLOCK)
    kv_meta = kv_m[0, BLOCK - N_META:]
    z_meta = zc_m[0, BLOCK - N_META:]

    def encode(x):
        B, S, D = x.shape
        T = B * S
        q, kv, zc, ga, gc = _proj(x, ln_g, ln_b, wq, wkv, wglu, wg, TILE_PROJ)
        att = _attn(q, kv, kv_meta, bias, mbias, sink_tab, TILE_ATTN)
        cz = _conv(zc, z_meta, conv_w[0], row(conv_b[0]), row(conv_ln_g[0]), row(conv_ln_b[0]), TILE_CONV)
        h1, h1p, rt, counts = _out(x, att, cz, ga, gc, ln_g, ln_b, watt, wco, wout, row(ln1_g[0]), row(ln1_b[0]),
                                   wr, br, TILE_OUT)
        idx1, idx2, tile_expert, tile_valid, n_tiles = _dispatch_plan(rt, counts, T, TILE_EXPERT)
        n_rows = n_tiles * TILE_EXPERT
        xs = _sc_scatter2(h1p.reshape(T * PACK_WORDS, LANES), idx1, idx2, n_rows * PACK_WORDS)
        ys = _experts(tile_expert, tile_valid, xs.reshape(n_rows // 8, PACK_WORDS, 8, LANES), wge, wue, wde,
                      TILE_EXPERT)
        g = _sc_gather(ys.reshape(n_rows * PACK_WORDS, LANES), jnp.concatenate([idx1, idx2]))
        return _final(h1, g.reshape(2, B, S // 8, PACK_WORDS, 8, LANES), rt, row(ln2_g[0]), row(ln2_b[0]),
                      TILE_FINAL)

    return encode(x_prompt), encode(x_sample)
```

```python
import functools
import math
from typing import NamedTuple

import jax
import jax.numpy as jnp
from jax import lax
from jax.experimental import pallas as pl
from jax.experimental.pallas import tpu as pltpu
from jax.experimental.pallas import tpu_sc as plsc

D_MODEL = 1024
N_META = 16
BLOCK = 128
WINDOW = 128
N_Q_HEADS = 8
N_KV_HEADS = 2
HEAD_DIM = 64
ATT_WIDTH = N_Q_HEADS * HEAD_DIM
KV_WIDTH = N_KV_HEADS * HEAD_DIM
CONV_WIDTH = D_MODEL // 2
CONV_K = 31
N_BUCKETS = 32
MAX_DISTANCE = 128
N_GROUPS = 4
EXPERTS_PER_GROUP = 8
N_EXPERTS = N_GROUPS * EXPERTS_PER_GROUP
D_EXPERT = 256
LN_EPS = 1e-5
DEPTH = 1
ALPHA = (2 * DEPTH) ** 0.25
NEG = -1e30
Q_END = ATT_WIDTH
K_END = Q_END + KV_WIDTH
V_END = K_END + KV_WIDTH
GLU_END = V_END + 2 * CONV_WIDTH
GA_END = GLU_END + D_MODEL

N_PAIRS = N_Q_HEADS // 2
LANES = 128
CONV_HALO = 16
ROUTER_LANES = 128

TILE = 512
CONV_ROWS = 128
LN_ROWS = 64
SHIFT_ROWS = 128
TILE_EXPERT = 512
PACK_WORDS = 4
SC_WINDOW = 128
ROUTE_FIELDS = 8
VMEM_LIMIT = 56 * 1024 * 1024

BF16 = jnp.bfloat16
F32 = jnp.float32


def _layer_norm(x, g, b):
    mu = jnp.mean(x, axis=-1, keepdims=True)
    xc = x - mu
    var = jnp.mean(xc * xc, axis=-1, keepdims=True)
    return xc * lax.rsqrt(var + LN_EPS) * g + b


def _dot(a, b):
    return jnp.dot(a, b, preferred_element_type=F32)


def _dot_nt(a, b):
    return lax.dot_general(a, b, (((1,), (1,)), ((), ())), preferred_element_type=F32)


class _Geom(NamedTuple):
    n_p: int
    n_s: int
    tp: int
    ts: int


def _geom(x_prompt, x_sample, tile):
    (bp, sp, _), (bs, ss, _) = x_prompt.shape, x_sample.shape
    return _Geom(bp * sp // tile, bs * ss // tile, sp // tile, ss // tile)


def _seq_pos(t, g):
    is_p = t < g.n_p
    local = jnp.where(is_p, lax.rem(t, g.tp), lax.rem(jnp.maximum(t - g.n_p, 0), g.ts))
    return is_p, local == 0, local == jnp.where(is_p, g.tp - 1, g.ts - 1)


def _x_specs(g, tile, width):
    return [pl.BlockSpec((tile, width), lambda t: (jnp.minimum(t, g.n_p - 1), 0)),
            pl.BlockSpec((tile, width), lambda t: (jnp.maximum(t - g.n_p, 0), 0))]


def _proj_kernel(xp_ref, xs_ref, g_ref, b_ref, wq_ref, wkv_ref, wglu_ref, wg_ref,
                 q_ref, kv_ref, zc_ref, ga_ref, gc_ref, *, geom):
    x = jnp.where(pl.program_id(0) < geom.n_p, xp_ref[...], xs_ref[...])
    h = _layer_norm(x, g_ref[...], b_ref[...]).astype(BF16)
    q_ref[...] = _dot(h, wq_ref[...]).astype(BF16)
    kv_ref[...] = _dot(h, wkv_ref[...]).astype(BF16)
    u = _dot(h, wglu_ref[...])
    zc_ref[...] = u[:, :CONV_WIDTH] * jax.nn.sigmoid(u[:, CONV_WIDTH:])
    gates = _dot(h, wg_ref[...])
    ga_ref[...] = jax.nn.sigmoid(gates[:, :D_MODEL]).astype(BF16)
    gc_ref[...] = jax.nn.sigmoid(gates[:, D_MODEL:]).astype(BF16)


def _proj(xp, xs, geom, ln_g, ln_b, wq, wkv, wglu, wg, tile):
    T = (geom.n_p + geom.n_s) * tile
    row = lambda w: pl.BlockSpec((tile, w), lambda t: (t, 0))
    full = lambda a: pl.BlockSpec(a.shape, lambda t: (0,) * a.ndim)
    return pl.pallas_call(
        functools.partial(_proj_kernel, geom=geom),
        grid=(geom.n_p + geom.n_s,),
        in_specs=_x_specs(geom, tile, D_MODEL) + [full(ln_g), full(ln_b), full(wq), full(wkv), full(wglu), full(wg)],
        out_specs=[row(ATT_WIDTH), row(2 * KV_WIDTH), row(CONV_WIDTH), row(D_MODEL), row(D_MODEL)],
        out_shape=[
            jax.ShapeDtypeStruct((T, ATT_WIDTH), BF16),
            jax.ShapeDtypeStruct((T, 2 * KV_WIDTH), BF16),
            jax.ShapeDtypeStruct((T, CONV_WIDTH), F32),
            jax.ShapeDtypeStruct((T, D_MODEL), BF16),
            jax.ShapeDtypeStruct((T, D_MODEL), BF16),
        ],
        compiler_params=pltpu.CompilerParams(dimension_semantics=("parallel",), vmem_limit_bytes=VMEM_LIMIT),
        name="proj",
    )(xp, xs, ln_g, ln_b, wq, wkv, wglu, wg)


def _attn_kernel(q_ref, kvp_ref, kvc_ref, kvn_ref, kvm_ref, bias_ref, mbias_ref, sink_ref, o_ref, *, tile, geom):
    _, seq_first, seq_last = _seq_pos(pl.program_id(0), geom)
    blocks = tile // BLOCK
    scale = HEAD_DIM ** -0.5

    lane = lax.broadcasted_iota(jnp.int32, (1, LANES), 1)
    lo = lane < HEAD_DIM

    def split_heads(t):
        z = jnp.zeros_like(t)
        return jnp.where(lo, t, z), jnp.where(lo, z, t)

    kv_ext = jnp.concatenate([kvp_ref[...], kvc_ref[...], kvn_ref[...]], axis=0)
    k_ext = kv_ext[:, :KV_WIDTH] * jnp.asarray(scale, BF16)
    v_ext = kv_ext[:, KV_WIDTH:]
    ka, kb = split_heads(k_ext)
    va, vb = split_heads(v_ext)
    kma, kmb = split_heads(kvm_ref[:, :KV_WIDTH] * jnp.asarray(scale, BF16))
    vma, vmb = split_heads(kvm_ref[:, KV_WIDTH:])
    km_cat = jnp.concatenate([kma, kmb], axis=0)
    vm_cat = jnp.concatenate([vma, vmb], axis=0)

    mlane = lax.broadcasted_iota(jnp.int32, (1, 2 * N_META), 1)
    m_first = mlane < N_META
    sink = sink_ref[...]
    lane_o = lax.broadcasted_iota(jnp.int32, (1, LANES), 1) < HEAD_DIM

    for j in range(blocks):
        first = jnp.logical_and(seq_first, j == 0)
        last = jnp.logical_and(seq_last, j == blocks - 1)
        variant = jnp.where(first, 0, jnp.where(last, 2, 1))
        mvariant = jnp.where(first, 0, 1)

        r0 = j * BLOCK
        qb = q_ref[r0:r0 + BLOCK, :]
        q4 = jnp.concatenate([qb[:, p * LANES:(p + 1) * LANES] for p in range(N_PAIRS)], axis=0)
        k_cat = jnp.concatenate([ka[r0:r0 + 3 * BLOCK], kb[r0:r0 + 3 * BLOCK]], axis=0)
        v_cat = jnp.concatenate([va[r0:r0 + 3 * BLOCK], vb[r0:r0 + 3 * BLOCK]], axis=0)

        s = _dot_nt(q4, k_cat) + bias_ref[variant]
        sm = _dot_nt(q4, km_cat) + mbias_ref[mvariant]

        s_a, s_b = s[:, :3 * BLOCK], s[:, 3 * BLOCK:]
        sm_a = jnp.where(m_first, sm, NEG)
        sm_b = jnp.where(m_first, NEG, sm)
        m_a = jnp.maximum(jnp.maximum(jnp.max(s_a, axis=1, keepdims=True),
                                      jnp.max(sm_a, axis=1, keepdims=True)), sink[:, 0:1])
        m_b = jnp.maximum(jnp.maximum(jnp.max(s_b, axis=1, keepdims=True),
                                      jnp.max(sm_b, axis=1, keepdims=True)), sink[:, 1:2])
        p_a = jnp.exp(s_a - m_a)
        p_b = jnp.exp(s_b - m_b)
        pm = jnp.exp(jnp.where(m_first, sm - m_a, sm - m_b))
        l_a = (jnp.sum(p_a, axis=1, keepdims=True) + jnp.sum(jnp.where(m_first, pm, 0.0), axis=1, keepdims=True)
               + jnp.exp(sink[:, 0:1] - m_a))
        l_b = (jnp.sum(p_b, axis=1, keepdims=True) + jnp.sum(jnp.where(m_first, 0.0, pm), axis=1, keepdims=True)
               + jnp.exp(sink[:, 1:2] - m_b))
        p = jnp.concatenate([p_a, p_b], axis=1).astype(BF16)
        o = _dot(p, v_cat) + _dot(pm.astype(BF16), vm_cat)
        o = o * jnp.where(lane_o, 1.0 / l_a, 1.0 / l_b)
        for pr in range(N_PAIRS):
            o_ref[r0:r0 + BLOCK, pr * LANES:(pr + 1) * LANES] = o[pr * BLOCK:(pr + 1) * BLOCK].astype(BF16)


def _attn(q, kv, geom, kv_meta, bias, mbias, sink_tab, tile):
    T = q.shape[0]
    bpt = tile // BLOCK
    n_blocks = T // BLOCK
    full = lambda a: pl.BlockSpec(a.shape, lambda t: (0,) * a.ndim)
    return pl.pallas_call(
        functools.partial(_attn_kernel, tile=tile, geom=geom),
        grid=(T // tile,),
        in_specs=[
            pl.BlockSpec((tile, ATT_WIDTH), lambda t: (t, 0)),
            pl.BlockSpec((BLOCK, 2 * KV_WIDTH), lambda t: (jnp.maximum(t * bpt - 1, 0), 0)),
            pl.BlockSpec((tile, 2 * KV_WIDTH), lambda t: (t, 0)),
            pl.BlockSpec((BLOCK, 2 * KV_WIDTH), lambda t: (jnp.minimum((t + 1) * bpt, n_blocks - 1), 0)),
            full(kv_meta), full(bias), full(mbias), full(sink_tab),
        ],
        out_specs=pl.BlockSpec((tile, ATT_WIDTH), lambda t: (t, 0)),
        out_shape=jax.ShapeDtypeStruct((T, ATT_WIDTH), BF16),
        compiler_params=pltpu.CompilerParams(dimension_semantics=("parallel",), vmem_limit_bytes=VMEM_LIMIT),
        name="attn",
    )(q, kv, kv, kv, kv_meta, bias, mbias, sink_tab)


def _conv_kernel(zp_ref, zc_ref, zn_ref, zm_ref, w_ref, cb_ref, g_ref, b_ref, o_ref, ext_ref, sh_ref, y_ref,
                 *, tile, geom):
    i = pl.program_id(0)
    _, seq_first, seq_last = _seq_pos(i, geom)
    ext_ref[0:CONV_HALO, :] = jnp.where(seq_first, zm_ref[...], zp_ref[...])
    ext_ref[CONV_HALO:CONV_HALO + tile, :] = zc_ref[...]
    ext_ref[CONV_HALO + tile:, :] = jnp.where(seq_last, 0.0, zn_ref[...])
    off = CONV_HALO - CONV_K // 2
    reach = (off + CONV_K - 1) // 8 * 8
    for p in range(1, 8):
        for r0 in range(0, tile + reach, SHIFT_ROWS):
            n = min(SHIFT_ROWS, tile + reach - r0)
            sh_ref[p - 1, r0:r0 + n, :] = ext_ref[r0 + p:r0 + p + n, :]

    def taps(r0, cs):
        acc = jnp.zeros((CONV_ROWS, LANES), F32)
        for k in range(CONV_K):
            p, a = (off + k) % 8, (off + k) // 8 * 8
            rows = slice(r0 + a, r0 + a + CONV_ROWS)
            win = ext_ref[rows, cs] if p == 0 else sh_ref[p - 1, rows, cs]
            acc = acc + win * w_ref[k:k + 1, cs]
        y_ref[r0:r0 + CONV_ROWS, cs] = acc

    for c in range(CONV_WIDTH // LANES):
        for r in range(tile // CONV_ROWS):
            pl.when(i >= 0)(functools.partial(taps, r * CONV_ROWS, slice(c * LANES, (c + 1) * LANES)))
    cb, g, b = cb_ref[...], g_ref[...], b_ref[...]
    for r in range(tile // LN_ROWS):
        r0 = r * LN_ROWS
        y = _layer_norm(y_ref[r0:r0 + LN_ROWS, :] + cb, g, b)
        o_ref[r0:r0 + LN_ROWS, :] = (y * jax.nn.sigmoid(y)).astype(BF16)


def _conv(zc, geom, z_meta, conv_w, conv_b, ln_g, ln_b, tile):
    T, C = zc.shape
    hpt = tile // CONV_HALO
    n_halo = T // CONV_HALO
    full = lambda a: pl.BlockSpec(a.shape, lambda t: (0,) * a.ndim)
    return pl.pallas_call(
        functools.partial(_conv_kernel, tile=tile, geom=geom),
        grid=(T // tile,),
        in_specs=[
            pl.BlockSpec((CONV_HALO, C), lambda t: (jnp.maximum(t * hpt - 1, 0), 0)),
            pl.BlockSpec((tile, C), lambda t: (t, 0)),
            pl.BlockSpec((CONV_HALO, C), lambda t: (jnp.minimum((t + 1) * hpt, n_halo - 1), 0)),
            full(z_meta), full(conv_w), full(conv_b), full(ln_g), full(ln_b),
        ],
        out_specs=pl.BlockSpec((tile, C), lambda t: (t, 0)),
        out_shape=jax.ShapeDtypeStruct((T, C), BF16),
        scratch_shapes=[pltpu.VMEM((tile + 2 * CONV_HALO, C), F32),
                        pltpu.VMEM((7, tile + 2 * CONV_HALO - 8, C), F32),
                        pltpu.VMEM((tile, C), F32)],
        compiler_params=pltpu.CompilerParams(dimension_semantics=("parallel",), vmem_limit_bytes=VMEM_LIMIT),
        name="conv",
    )(zc, zc, zc, z_meta, conv_w, conv_b, ln_g, ln_b)


def _route(r):
    lane = lax.broadcasted_iota(jnp.int32, r.shape, 1)
    big = jnp.int32(1 << 20)
    is_g = jnp.logical_and(lane >= N_EXPERTS, lane < N_EXPERTS + N_GROUPS)
    lg = jnp.where(is_g, r, -jnp.inf)
    mg = jnp.max(lg, axis=1, keepdims=True)
    g_w = 1.0 / jnp.sum(jnp.exp(lg - mg), axis=1, keepdims=True)
    g_idx = jnp.min(jnp.where(lg == mg, lane - N_EXPERTS, big), axis=1, keepdims=True)
    in_group = jnp.logical_and(lane < N_EXPERTS, jnp.right_shift(lane, 3) == g_idx)
    le = jnp.where(in_group, r, -jnp.inf)
    m1 = jnp.max(le, axis=1, keepdims=True)
    den = jnp.sum(jnp.exp(le - m1), axis=1, keepdims=True)
    i1 = jnp.min(jnp.where(le == m1, lane, big), axis=1, keepdims=True)
    le2 = jnp.where(lane == i1, -jnp.inf, le)
    m2 = jnp.max(le2, axis=1, keepdims=True)
    i2 = jnp.min(jnp.where(le2 == m2, lane, big), axis=1, keepdims=True)
    p1 = 1.0 / den
    p2 = jnp.exp(m2 - m1) / den
    tot = p1 + p2
    return i1, i2, g_w * (p1 / tot), g_w * (p2 / tot)


def _pack_bf16_pairs(x):
    half = x.shape[1] // 2
    words = []
    for j in range(half // LANES):
        lo = pltpu.bitcast(x[:, j * LANES:(j + 1) * LANES].astype(BF16).astype(F32), jnp.uint32)
        hi = pltpu.bitcast(x[:, half + j * LANES:half + (j + 1) * LANES].astype(BF16).astype(F32), jnp.uint32)
        words.append(hi | (lo >> 16))
    return words


def _unpack_bf16_pairs(words):
    lo = [pltpu.bitcast(w << 16, F32) for w in words]
    hi = [pltpu.bitcast(w & jnp.uint32(0xFFFF0000), F32) for w in words]
    return jnp.concatenate(lo + hi, axis=1)


def _out_kernel(xp_ref, xs_ref, att_ref, cz_ref, ga_ref, gc_ref, lng_ref, lnb_ref, watt_ref, wco_ref, wout_ref,
                l1g_ref, l1b_ref, wr_ref, br_ref, h1_ref, h1p_ref, rt_ref, fld_ref, cnt_ref, *, tile, geom):
    t = pl.program_id(0)

    @pl.when(t == 0)
    def _():
        cnt_ref[...] = jnp.zeros_like(cnt_ref)

    x = jnp.where(t < geom.n_p, xp_ref[...], xs_ref[...])
    h0 = _layer_norm(x, lng_ref[...], lnb_ref[...])
    a = _dot(att_ref[...], watt_ref[...])
    c = _dot(cz_ref[...], wco_ref[...])
    mix = ga_ref[...].astype(F32) * a + gc_ref[...].astype(F32) * c
    m = _dot(mix.astype(BF16), wout_ref[...])
    h1 = _layer_norm(ALPHA * h0 + m, l1g_ref[...], l1b_ref[...])
    h1_ref[...] = h1
    for j, w in enumerate(_pack_bf16_pairs(h1)):
        h1p_ref[:, j, :, :] = w.reshape(tile // 8, 8, LANES)

    r = _dot(h1.astype(BF16), wr_ref[...]) + br_ref[...]
    i1, i2, w1, w2 = _route(r)
    lane = lax.broadcasted_iota(jnp.int32, (tile, ROUTER_LANES), 1)
    hit1, hit2 = lane == i1, lane == i2
    onehot = jnp.where(jnp.logical_or(hit1, hit2), 1.0, 0.0)
    row_i = lax.broadcasted_iota(jnp.int32, (tile, tile), 0)
    col_i = lax.broadcasted_iota(jnp.int32, (tile, tile), 1)
    before = jnp.where(col_i < row_i, 1.0, 0.0).astype(BF16)
    seen = _dot(before, onehot.astype(BF16)) + cnt_ref[0:1, :]
    rank1 = jnp.sum(jnp.where(hit1, seen, 0.0), axis=1, keepdims=True)
    rank2 = jnp.sum(jnp.where(hit2, seen, 0.0), axis=1, keepdims=True)
    cnt_ref[...] = cnt_ref[...] + jnp.sum(onehot, axis=0, keepdims=True)
    fields = (i1.astype(F32), i2.astype(F32), w1, w2, rank1, rank2)
    rt = jnp.zeros((tile, ROUTER_LANES), F32)
    for k, v in enumerate(fields):
        rt = jnp.where(lane == k, v, rt)
    rt_ref[...] = rt
    fld_ref[...] = jnp.transpose(rt)[0:ROUTE_FIELDS, :]


def _out(xp, xs, geom, att, cz, ga, gc, ln_g, ln_b, watt, wco, wout, l1g, l1b, wr, br, tile):
    T, D = att.shape[0], D_MODEL
    row = lambda w: pl.BlockSpec((tile, w), lambda t: (t, 0))
    full = lambda a: pl.BlockSpec(a.shape, lambda t: (0,) * a.ndim)
    return pl.pallas_call(
        functools.partial(_out_kernel, tile=tile, geom=geom),
        grid=(T // tile,),
        in_specs=_x_specs(geom, tile, D) + [row(ATT_WIDTH), row(CONV_WIDTH), row(D), row(D), full(ln_g), full(ln_b),
                                            full(watt), full(wco), full(wout), full(l1g), full(l1b), full(wr), full(br)],
        out_specs=[row(D),
                   pl.BlockSpec((tile // 8, PACK_WORDS, 8, LANES), lambda t: (t, 0, 0, 0)),
                   row(ROUTER_LANES),
                   pl.BlockSpec((ROUTE_FIELDS, tile), lambda t: (0, t)),
                   pl.BlockSpec((8, ROUTER_LANES), lambda t: (0, 0))],
        out_shape=[jax.ShapeDtypeStruct((T, D), F32),
                   jax.ShapeDtypeStruct((T // 8, PACK_WORDS, 8, LANES), jnp.uint32),
                   jax.ShapeDtypeStruct((T, ROUTER_LANES), F32),
                   jax.ShapeDtypeStruct((ROUTE_FIELDS, T), F32),
                   jax.ShapeDtypeStruct((8, ROUTER_LANES), F32)],
        compiler_params=pltpu.CompilerParams(dimension_semantics=("arbitrary",), vmem_limit_bytes=VMEM_LIMIT),
        name="out",
    )(xp, xs, att, cz, ga, gc, ln_g, ln_b, watt, wco, wout, l1g, l1b, wr, br)


def _sc_scatter2(src, idx_a, idx_b, n_out):
    m = src.shape[0]
    mesh = plsc.VectorSubcoreMesh(core_axis_name="c", subcore_axis_name="s")

    @functools.partial(pl.kernel, out_type=jax.ShapeDtypeStruct((n_out, LANES), src.dtype), mesh=mesh)
    def k(x_hbm, ia_hbm, ib_hbm, o_hbm):
        def body(x_vmem, ia_vmem, ib_vmem):
            pltpu.sync_copy(x_vmem, o_hbm.at[ia_vmem.at[0]])
            pltpu.sync_copy(x_vmem, o_hbm.at[ib_vmem.at[0]])

        pltpu.emit_pipeline(
            body, grid=(m // SC_WINDOW,),
            in_specs=[pl.BlockSpec((SC_WINDOW, LANES), index_map=lambda i: (i, 0)),
                      pl.BlockSpec((1, SC_WINDOW), index_map=lambda i: (0, i)),
                      pl.BlockSpec((1, SC_WINDOW), index_map=lambda i: (0, i))],
            out_specs=[],
            core_axis_name=("c", "s"), dimension_semantics=(pltpu.PARALLEL,),
        )(x_hbm, ia_hbm, ib_hbm)

    return k(src, idx_a.reshape(1, m), idx_b.reshape(1, m))


def _sc_gather(table, idx):
    m = idx.shape[0]
    mesh = plsc.VectorSubcoreMesh(core_axis_name="c", subcore_axis_name="s")

    @functools.partial(pl.kernel, out_type=jax.ShapeDtypeStruct((m, LANES), table.dtype), mesh=mesh)
    def k(x_hbm, i_hbm, o_hbm):
        def body(i_vmem, o_vmem):
            pltpu.sync_copy(x_hbm.at[i_vmem.at[0]], o_vmem)

        pltpu.emit_pipeline(
            body, grid=(m // SC_WINDOW,),
            in_specs=[pl.BlockSpec((1, SC_WINDOW), index_map=lambda i: (0, i))],
            out_specs=[pl.BlockSpec((SC_WINDOW, LANES), index_map=lambda i: (i, 0))],
            core_axis_name=("c", "s"), dimension_semantics=(pltpu.PARALLEL,),
        )(i_hbm, o_hbm)

    return k(table, idx.reshape(1, m))


def _expert_kernel(te_ref, tv_ref, xs_ref, wg_ref, wu_ref, wd_ref, ys_ref, wgb_ref, wub_ref, wdb_ref, *, tile):
    n = pl.program_id(0)
    valid = tv_ref[n]

    @pl.when(jnp.logical_or(n == 0, te_ref[n] != te_ref[jnp.maximum(n - 1, 0)]))
    def _():
        wgb_ref[...] = wg_ref[...].astype(BF16)
        wub_ref[...] = wu_ref[...].astype(BF16)
        wdb_ref[...] = wd_ref[...].astype(BF16)

    @pl.when(valid > 0)
    def _():
        x = _unpack_bf16_pairs([xs_ref[:, j, :, :].reshape(tile, LANES) for j in range(PACK_WORDS)])
        rows = lax.broadcasted_iota(jnp.int32, (tile, 1), 0)
        x = jnp.where(rows < valid, x, 0.0).astype(BF16)
        gt = _dot(x, wgb_ref[...])
        up = _dot(x, wub_ref[...])
        hid = (gt * jax.nn.sigmoid(gt)) * up
        y = _dot(hid.astype(BF16), wdb_ref[...])
        for j, w in enumerate(_pack_bf16_pairs(y)):
            ys_ref[:, j, :, :] = w.reshape(tile // 8, 8, LANES)

    @pl.when(valid <= 0)
    def _():
        ys_ref[...] = jnp.zeros_like(ys_ref)


def _experts(tile_expert, tile_valid, xs, w_gate, w_up, w_down, tile):
    n_tiles = xs.shape[0] * 8 // tile
    blk = pl.BlockSpec((tile // 8, PACK_WORDS, 8, LANES), lambda n, te, tv: (n, 0, 0, 0))
    return pl.pallas_call(
        functools.partial(_expert_kernel, tile=tile),
        grid_spec=pltpu.PrefetchScalarGridSpec(
            num_scalar_prefetch=2,
            grid=(n_tiles,),
            in_specs=[blk,
                      pl.BlockSpec((None, D_MODEL, D_EXPERT), lambda n, te, tv: (te[n], 0, 0)),
                      pl.BlockSpec((None, D_MODEL, D_EXPERT), lambda n, te, tv: (te[n], 0, 0)),
                      pl.BlockSpec((None, D_EXPERT, D_MODEL), lambda n, te, tv: (te[n], 0, 0))],
            out_specs=blk,
            scratch_shapes=[pltpu.VMEM((D_MODEL, D_EXPERT), BF16), pltpu.VMEM((D_MODEL, D_EXPERT), BF16),
                            pltpu.VMEM((D_EXPERT, D_MODEL), BF16)],
        ),
        out_shape=jax.ShapeDtypeStruct(xs.shape, jnp.uint32),
        compiler_params=pltpu.CompilerParams(
            dimension_semantics=("arbitrary",), vmem_limit_bytes=VMEM_LIMIT),
        name="experts",
    )(tile_expert, tile_valid, xs, w_gate, w_up, w_down)


def _final_kernel(h_ref, g_ref, rt_ref, l2g_ref, l2b_ref, o_ref, *, tile):
    rt = rt_ref[...]
    lane = lax.broadcasted_iota(jnp.int32, rt.shape, 1)
    w1 = jnp.sum(jnp.where(lane == 2, rt, 0.0), axis=1, keepdims=True)
    w2 = jnp.sum(jnp.where(lane == 3, rt, 0.0), axis=1, keepdims=True)
    y1 = _unpack_bf16_pairs([g_ref[0, :, j, :, :].reshape(tile, LANES) for j in range(PACK_WORDS)])
    y2 = _unpack_bf16_pairs([g_ref[1, :, j, :, :].reshape(tile, LANES) for j in range(PACK_WORDS)])
    f = w1 * y1 + w2 * y2
    o_ref[...] = _layer_norm(ALPHA * h_ref[...] + f, l2g_ref[...], l2b_ref[...])


def _final(h1, g, rt, l2g, l2b, tile, first_tile, n_tiles):
    D = h1.shape[1]
    row = lambda w: pl.BlockSpec((tile, w), lambda i: (first_tile + i, 0))
    full = lambda a: pl.BlockSpec(a.shape, lambda i: (0,) * a.ndim)
    return pl.pallas_call(
        functools.partial(_final_kernel, tile=tile),
        grid=(n_tiles,),
        in_specs=[row(D),
                  pl.BlockSpec((2, tile // 8, PACK_WORDS, 8, LANES), lambda i: (0, first_tile + i, 0, 0, 0)),
                  row(ROUTER_LANES), full(l2g), full(l2b)],
        out_specs=pl.BlockSpec((tile, D), lambda i: (i, 0)),
        out_shape=jax.ShapeDtypeStruct((n_tiles * tile, D), F32),
        compiler_params=pltpu.CompilerParams(dimension_semantics=("parallel",), vmem_limit_bytes=VMEM_LIMIT),
        name="final",
    )(h1, g, rt, l2g, l2b)


def _dispatch_plan(fields, counts, n_tokens, tile):
    cnt = counts[0, :N_EXPERTS].astype(jnp.int32)
    padded = (cnt + tile - 1) // tile * tile
    base = jnp.cumsum(padded) - padded
    e_ids = jnp.arange(N_EXPERTS, dtype=jnp.int32)

    def dest(row_e, row_r):
        e = fields[row_e].astype(jnp.int32)
        seg = jnp.sum(jnp.where(e[None, :] == e_ids[:, None], base[:, None], 0), axis=0)
        pos = seg + fields[row_r].astype(jnp.int32)
        p = pos.reshape(n_tokens // 8, 1, 8)
        j = jnp.arange(PACK_WORDS, dtype=jnp.int32).reshape(1, PACK_WORDS, 1)
        return ((p // 8) * (8 * PACK_WORDS) + j * 8 + p % 8).reshape(-1)

    n_tiles = (2 * n_tokens) // tile + N_EXPERTS
    start = jnp.arange(n_tiles, dtype=jnp.int32) * tile
    seg_end = base + padded
    te = jnp.minimum(jnp.sum((start[:, None] >= seg_end[None, :]).astype(jnp.int32), axis=1), N_EXPERTS - 1)
    te_base = jnp.sum(jnp.where(te[:, None] == e_ids[None, :], base[None, :], 0), axis=1)
    te_cnt = jnp.sum(jnp.where(te[:, None] == e_ids[None, :], cnt[None, :], 0), axis=1)
    tv = jnp.clip(te_cnt - (start - te_base), 0, tile)
    return dest(0, 4), dest(1, 5), te, tv, n_tiles


def _t5_bucket(rel):
    half = N_BUCKETS // 2
    max_exact = half // 2
    ret = jnp.where(rel > 0, half, 0)
    n = jnp.abs(rel)
    nf = jnp.maximum(n, 1).astype(F32)
    large = max_exact + (jnp.log(nf / max_exact) / math.log(MAX_DISTANCE / max_exact)
                         * (half - max_exact)).astype(jnp.int32)
    large = jnp.minimum(large, half - 1)
    return ret + jnp.where(n < max_exact, n, large)


def _bucket_bias(rel_bias, bucket):
    rb = rel_bias.astype(F32)
    out = jnp.zeros((N_Q_HEADS,) + bucket.shape, F32)
    for b in range(N_BUCKETS):
        out = out + jnp.where(bucket[None] == b, rb[b][:, None, None], 0.0)
    return out


def _pair_rows(t):
    return jnp.concatenate([t[:N_PAIRS], t[N_PAIRS:]], axis=-1).reshape(N_PAIRS * BLOCK, -1)


def _bias_tables(rel_bias, sink):
    qi = jnp.arange(BLOCK)
    kj = jnp.arange(3 * BLOCK) - BLOCK
    rel = kj[None, :] - qi[:, None]
    band = _bucket_bias(rel_bias, _t5_bucket(rel))
    in_win = (jnp.abs(rel) <= WINDOW)[None]
    not_prev = (kj >= 0)[None, None, :]
    not_next = (kj < BLOCK)[None, None, :]
    variants = [jnp.where(in_win & not_prev, band, NEG),
                jnp.where(in_win, band, NEG),
                jnp.where(in_win & not_next, band, NEG)]
    bias = jnp.stack([_pair_rows(v) for v in variants])

    off = BLOCK - N_META
    mvars = []
    for blk in (1, 2):
        qpos = blk * BLOCK + qi - off
        meta_rel = jnp.arange(N_META)[None, :] - qpos[:, None]
        mvars.append(_pair_rows(_bucket_bias(rel_bias, _t5_bucket(meta_rel))))
    mbias = jnp.stack(mvars)
    s = sink.astype(F32)
    sink_tab = jnp.repeat(jnp.stack([s[:N_PAIRS], s[N_PAIRS:]], axis=-1), BLOCK, axis=0)
    return bias, mbias, sink_tab


def kernel(x_prompt, x_sample, meta, ln_in_g, ln_in_b, rel_bias, w_in, w_att_branch, sink, conv_w, conv_b,
           conv_ln_g, conv_ln_b, w_conv_out, w_out, ln1_g, ln1_b, w_group, b_group, w_router, b_router,
           w_gate, w_up, w_down, ln2_g, ln2_b):
    row = lambda v: v.reshape(1, -1).astype(F32)
    w = w_in[0]
    wq = (w[:, :Q_END].reshape(D_MODEL, 2, N_PAIRS, HEAD_DIM).transpose(0, 2, 1, 3)
          .reshape(D_MODEL, ATT_WIDTH).astype(BF16))
    watt = (w_att_branch[0].reshape(2, N_PAIRS, HEAD_DIM, D_MODEL).transpose(1, 0, 2, 3)
            .reshape(ATT_WIDTH, D_MODEL).astype(BF16))
    wkv = w[:, Q_END:V_END].astype(BF16)
    wglu = w[:, V_END:GLU_END].astype(BF16)
    wg = w[:, GLU_END:].astype(BF16)
    wco = w_conv_out[0].astype(BF16)
    wout = w_out[0].astype(BF16)
    wr = jnp.zeros((D_MODEL, ROUTER_LANES), F32)
    wr = wr.at[:, :N_EXPERTS].set(w_router[0]).at[:, N_EXPERTS:N_EXPERTS + N_GROUPS].set(w_group[0]).astype(BF16)
    br = jnp.zeros((1, ROUTER_LANES), F32)
    br = br.at[0, :N_EXPERTS].set(b_router[0]).at[0, N_EXPERTS:N_EXPERTS + N_GROUPS].set(b_group[0])
    ln_g, ln_b = row(ln_in_g), row(ln_in_b)
    bias, mbias, sink_tab = _bias_tables(rel_bias, sink[0])

    xm = jnp.concatenate([jnp.zeros((BLOCK - N_META, D_MODEL), F32), meta.astype(F32)], axis=0)
    _, kv_m, zc_m, _, _ = _proj(xm, xm, _Geom(1, 1, 1, 1), ln_g, ln_b, wq, wkv, wglu, wg, BLOCK)
    kv_meta = kv_m[BLOCK - N_META:BLOCK]
    z_meta = zc_m[BLOCK - N_META:BLOCK]

    (bp, sp, _), (bs, ss, _) = x_prompt.shape, x_sample.shape
    xp, xs = x_prompt.reshape(bp * sp, D_MODEL), x_sample.reshape(bs * ss, D_MODEL)
    geom = _geom(x_prompt, x_sample, TILE)
    T = (geom.n_p + geom.n_s) * TILE
    q, kv, zc, ga, gc = _proj(xp, xs, geom, ln_g, ln_b, wq, wkv, wglu, wg, TILE)
    att = _attn(q, kv, geom, kv_meta, bias, mbias, sink_tab, TILE)
    cz = _conv(zc, geom, z_meta, conv_w[0], row(conv_b[0]), row(conv_ln_g[0]), row(conv_ln_b[0]), TILE)
    h1, h1p, rt, fields, counts = _out(xp, xs, geom, att, cz, ga, gc, ln_g, ln_b, watt, wco, wout,
                                       row(ln1_g[0]), row(ln1_b[0]), wr, br, TILE)
    idx1, idx2, tile_expert, tile_valid, n_tiles = _dispatch_plan(fields, counts, T, TILE_EXPERT)
    n_rows = n_tiles * TILE_EXPERT
    xsorted = _sc_scatter2(h1p.reshape(T * PACK_WORDS, LANES), idx1, idx2, n_rows * PACK_WORDS)
    ys = _experts(tile_expert, tile_valid, xsorted.reshape(n_rows // 8, PACK_WORDS, 8, LANES),
                  w_gate[0], w_up[0], w_down[0], TILE_EXPERT)
    g = _sc_gather(ys.reshape(n_rows * PACK_WORDS, LANES), jnp.concatenate([idx1, idx2]))
    g = g.reshape(2, T // 8, PACK_WORDS, 8, LANES)
    l2g, l2b = row(ln2_g[0]), row(ln2_b[0])
    y_p = _final(h1, g, rt, l2g, l2b, TILE, 0, geom.n_p)
    y_s = _final(h1, g, rt, l2g, l2b, TILE, geom.n_p, geom.n_s)
    return y_p.reshape(x_prompt.shape), y_s.reshape(x_sample.shape)
```

```python
import functools
import math
from typing import NamedTuple

import jax
import jax.numpy as jnp
from jax import lax
from jax.experimental import pallas as pl
from jax.experimental.pallas import tpu as pltpu
from jax.experimental.pallas import tpu_sc as plsc

D_MODEL = 1024
N_META = 16
BLOCK = 128
WINDOW = 128
N_Q_HEADS = 8
N_KV_HEADS = 2
HEAD_DIM = 64
ATT_WIDTH = N_Q_HEADS * HEAD_DIM
KV_WIDTH = N_KV_HEADS * HEAD_DIM
CONV_WIDTH = D_MODEL // 2
CONV_K = 31
N_BUCKETS = 32
MAX_DISTANCE = 128
N_GROUPS = 4
EXPERTS_PER_GROUP = 8
N_EXPERTS = N_GROUPS * EXPERTS_PER_GROUP
D_EXPERT = 256
LN_EPS = 1e-5
DEPTH = 1
ALPHA = (2 * DEPTH) ** 0.25
NEG = -1e30
Q_END = ATT_WIDTH
K_END = Q_END + KV_WIDTH
V_END = K_END + KV_WIDTH
GLU_END = V_END + 2 * CONV_WIDTH
GA_END = GLU_END + D_MODEL

N_PAIRS = N_Q_HEADS // 2
LANES = 128
CONV_HALO = 16
ROUTER_LANES = 128

TILE = 512
CONV_ROWS = 128
LN_ROWS = 64
SHIFT_ROWS = 128
TILE_EXPERT = 512
PACK_WORDS = 4
SC_WINDOW = 128
ROUTE_FIELDS = 8
VMEM_LIMIT = 56 * 1024 * 1024

BF16 = jnp.bfloat16
F32 = jnp.float32


def _layer_norm(x, g, b):
    mu = jnp.mean(x, axis=-1, keepdims=True)
    xc = x - mu
    var = jnp.mean(xc * xc, axis=-1, keepdims=True)
    return xc * lax.rsqrt(var + LN_EPS) * g + b


def _dot(a, b):
    return jnp.dot(a, b, preferred_element_type=F32)


def _dot_nt(a, b):
    return lax.dot_general(a, b, (((1,), (1,)), ((), ())), preferred_element_type=F32)


class _Geom(NamedTuple):
    n_p: int
    n_s: int
    tp: int
    ts: int


def _geom(x_prompt, x_sample, tile):
    (bp, sp, _), (bs, ss, _) = x_prompt.shape, x_sample.shape
    return _Geom(bp * sp // tile, bs * ss // tile, sp // tile, ss // tile)


def _seq_pos(t, g):
    is_p = t < g.n_p
    local = jnp.where(is_p, lax.rem(t, g.tp), lax.rem(jnp.maximum(t - g.n_p, 0), g.ts))
    return is_p, local == 0, local == jnp.where(is_p, g.tp - 1, g.ts - 1)


def _x_specs(g, tile, width):
    return [pl.BlockSpec((tile, width), lambda t: (jnp.minimum(t, g.n_p - 1), 0)),
            pl.BlockSpec((tile, width), lambda t: (jnp.maximum(t - g.n_p, 0), 0))]


def _proj_kernel(xp_ref, xs_ref, g_ref, b_ref, wq_ref, wkv_ref, wglu_ref, wg_ref,
                 q_ref, kv_ref, zc_ref, ga_ref, gc_ref, *, geom):
    x = jnp.where(pl.program_id(0) < geom.n_p, xp_ref[...], xs_ref[...])
    h = _layer_norm(x, g_ref[...], b_ref[...]).astype(BF16)
    q_ref[...] = _dot(h, wq_ref[...]).astype(BF16)
    kv_ref[...] = _dot(h, wkv_ref[...]).astype(BF16)
    u = _dot(h, wglu_ref[...])
    zc_ref[...] = u[:, :CONV_WIDTH] * jax.nn.sigmoid(u[:, CONV_WIDTH:])
    gates = _dot(h, wg_ref[...])
    ga_ref[...] = jax.nn.sigmoid(gates[:, :D_MODEL]).astype(BF16)
    gc_ref[...] = jax.nn.sigmoid(gates[:, D_MODEL:]).astype(BF16)


def _proj(xp, xs, geom, ln_g, ln_b, wq, wkv, wglu, wg, tile):
    T = (geom.n_p + geom.n_s) * tile
    row = lambda w: pl.BlockSpec((tile, w), lambda t: (t, 0))
    full = lambda a: pl.BlockSpec(a.shape, lambda t: (0,) * a.ndim)
    return pl.pallas_call(
        functools.partial(_proj_kernel, geom=geom),
        grid=(geom.n_p + geom.n_s,),
        in_specs=_x_specs(geom, tile, D_MODEL) + [full(ln_g), full(ln_b), full(wq), full(wkv), full(wglu), full(wg)],
        out_specs=[row(ATT_WIDTH), row(2 * KV_WIDTH), row(CONV_WIDTH), row(D_MODEL), row(D_MODEL)],
        out_shape=[
            jax.ShapeDtypeStruct((T, ATT_WIDTH), BF16),
            jax.ShapeDtypeStruct((T, 2 * KV_WIDTH), BF16),
            jax.ShapeDtypeStruct((T, CONV_WIDTH), F32),
            jax.ShapeDtypeStruct((T, D_MODEL), BF16),
            jax.ShapeDtypeStruct((T, D_MODEL), BF16),
        ],
        compiler_params=pltpu.CompilerParams(dimension_semantics=("parallel",), vmem_limit_bytes=VMEM_LIMIT),
        name="proj",
    )(xp, xs, ln_g, ln_b, wq, wkv, wglu, wg)


def _attn_kernel(q_ref, kvp_ref, kvc_ref, kvn_ref, kvm_ref, bias_ref, mbias_ref, sink_ref, o_ref, *, tile, geom):
    _, seq_first, seq_last = _seq_pos(pl.program_id(0), geom)
    blocks = tile // BLOCK
    scale = HEAD_DIM ** -0.5

    lane = lax.broadcasted_iota(jnp.int32, (1, LANES), 1)
    lo = lane < HEAD_DIM

    def split_heads(t):
        z = jnp.zeros_like(t)
        return jnp.where(lo, t, z), jnp.where(lo, z, t)

    kv_ext = jnp.concatenate([kvp_ref[...], kvc_ref[...], kvn_ref[...]], axis=0)
    k_ext = kv_ext[:, :KV_WIDTH] * jnp.asarray(scale, BF16)
    v_ext = kv_ext[:, KV_WIDTH:]
    ka, kb = split_heads(k_ext)
    va, vb = split_heads(v_ext)
    kma, kmb = split_heads(kvm_ref[:, :KV_WIDTH] * jnp.asarray(scale, BF16))
    vma, vmb = split_heads(kvm_ref[:, KV_WIDTH:])
    km_cat = jnp.concatenate([kma, kmb], axis=0)
    vm_cat = jnp.concatenate([vma, vmb], axis=0)

    mlane = lax.broadcasted_iota(jnp.int32, (1, 2 * N_META), 1)
    m_first = mlane < N_META
    sink = sink_ref[...]
    lane_o = lax.broadcasted_iota(jnp.int32, (1, LANES), 1) < HEAD_DIM

    for j in range(blocks):
        first = jnp.logical_and(seq_first, j == 0)
        last = jnp.logical_and(seq_last, j == blocks - 1)
        variant = jnp.where(first, 0, jnp.where(last, 2, 1))
        mvariant = jnp.where(first, 0, 1)

        r0 = j * BLOCK
        qb = q_ref[r0:r0 + BLOCK, :]
        q4 = jnp.concatenate([qb[:, p * LANES:(p + 1) * LANES] for p in range(N_PAIRS)], axis=0)
        k_cat = jnp.concatenate([ka[r0:r0 + 3 * BLOCK], kb[r0:r0 + 3 * BLOCK]], axis=0)
        v_cat = jnp.concatenate([va[r0:r0 + 3 * BLOCK], vb[r0:r0 + 3 * BLOCK]], axis=0)

        s = _dot_nt(q4, k_cat) + bias_ref[variant]
        sm = _dot_nt(q4, km_cat) + mbias_ref[mvariant]

        s_a, s_b = s[:, :3 * BLOCK], s[:, 3 * BLOCK:]
        sm_a = jnp.where(m_first, sm, NEG)
        sm_b = jnp.where(m_first, NEG, sm)
        m_a = jnp.maximum(jnp.maximum(jnp.max(s_a, axis=1, keepdims=True),
                                      jnp.max(sm_a, axis=1, keepdims=True)), sink[:, 0:1])
        m_b = jnp.maximum(jnp.maximum(jnp.max(s_b, axis=1, keepdims=True),
                                      jnp.max(sm_b, axis=1, keepdims=True)), sink[:, 1:2])
        p_a = jnp.exp(s_a - m_a)
        p_b = jnp.exp(s_b - m_b)
        pm = jnp.exp(jnp.where(m_first, sm - m_a, sm - m_b))
        l_a = (jnp.sum(p_a, axis=1, keepdims=True) + jnp.sum(jnp.where(m_first, pm, 0.0), axis=1, keepdims=True)
               + jnp.exp(sink[:, 0:1] - m_a))
        l_b = (jnp.sum(p_b, axis=1, keepdims=True) + jnp.sum(jnp.where(m_first, 0.0, pm), axis=1, keepdims=True)
               + jnp.exp(sink[:, 1:2] - m_b))
        p = jnp.concatenate([p_a, p_b], axis=1).astype(BF16)
        o = _dot(p, v_cat) + _dot(pm.astype(BF16), vm_cat)
        o = o * jnp.where(lane_o, 1.0 / l_a, 1.0 / l_b)
        for pr in range(N_PAIRS):
            o_ref[r0:r0 + BLOCK, pr * LANES:(pr + 1) * LANES] = o[pr * BLOCK:(pr + 1) * BLOCK].astype(BF16)


def _attn(q, kv, geom, kv_meta, bias, mbias, sink_tab, tile):
    T = q.shape[0]
    bpt = tile // BLOCK
    n_blocks = T // BLOCK
    full = lambda a: pl.BlockSpec(a.shape, lambda t: (0,) * a.ndim)
    return pl.pallas_call(
        functools.partial(_attn_kernel, tile=tile, geom=geom),
        grid=(T // tile,),
        in_specs=[
            pl.BlockSpec((tile, ATT_WIDTH), lambda t: (t, 0)),
            pl.BlockSpec((BLOCK, 2 * KV_WIDTH), lambda t: (jnp.maximum(t * bpt - 1, 0), 0)),
            pl.BlockSpec((tile, 2 * KV_WIDTH), lambda t: (t, 0)),
            pl.BlockSpec((BLOCK, 2 * KV_WIDTH), lambda t: (jnp.minimum((t + 1) * bpt, n_blocks - 1), 0)),
            full(kv_meta), full(bias), full(mbias), full(sink_tab),
        ],
        out_specs=pl.BlockSpec((tile, ATT_WIDTH), lambda t: (t, 0)),
        out_shape=jax.ShapeDtypeStruct((T, ATT_WIDTH), BF16),
        compiler_params=pltpu.CompilerParams(dimension_semantics=("parallel",), vmem_limit_bytes=VMEM_LIMIT),
        name="attn",
    )(q, kv, kv, kv, kv_meta, bias, mbias, sink_tab)


def _conv_kernel(zp_ref, zc_ref, zn_ref, zm_ref, w_ref, cb_ref, g_ref, b_ref, o_ref, ext_ref, sh_ref, y_ref,
                 *, tile, geom):
    i = pl.program_id(0)
    _, seq_first, seq_last = _seq_pos(i, geom)
    ext_ref[0:CONV_HALO, :] = jnp.where(seq_first, zm_ref[...], zp_ref[...])
    ext_ref[CONV_HALO:CONV_HALO + tile, :] = zc_ref[...]
    ext_ref[CONV_HALO + tile:, :] = jnp.where(seq_last, 0.0, zn_ref[...])
    off = CONV_HALO - CONV_K // 2
    reach = (off + CONV_K - 1) // 8 * 8
    for p in range(1, 8):
        for r0 in range(0, tile + reach, SHIFT_ROWS):
            n = min(SHIFT_ROWS, tile + reach - r0)
            sh_ref[p - 1, r0:r0 + n, :] = ext_ref[r0 + p:r0 + p + n, :]

    def taps(r0, cs):
        acc = jnp.zeros((CONV_ROWS, LANES), F32)
        for k in range(CONV_K):
            p, a = (off + k) % 8, (off + k) // 8 * 8
            rows = slice(r0 + a, r0 + a + CONV_ROWS)
            win = ext_ref[rows, cs] if p == 0 else sh_ref[p - 1, rows, cs]
            acc = acc + win * w_ref[k:k + 1, cs]
        y_ref[r0:r0 + CONV_ROWS, cs] = acc

    for c in range(CONV_WIDTH // LANES):
        for r in range(tile // CONV_ROWS):
            pl.when(i >= 0)(functools.partial(taps, r * CONV_ROWS, slice(c * LANES, (c + 1) * LANES)))
    cb, g, b = cb_ref[...], g_ref[...], b_ref[...]
    for r in range(tile // LN_ROWS):
        r0 = r * LN_ROWS
        y = _layer_norm(y_ref[r0:r0 + LN_ROWS, :] + cb, g, b)
        o_ref[r0:r0 + LN_ROWS, :] = (y * jax.nn.sigmoid(y)).astype(BF16)


def _conv(zc, geom, z_meta, conv_w, conv_b, ln_g, ln_b, tile):
    T, C = zc.shape
    hpt = tile // CONV_HALO
    n_halo = T // CONV_HALO
    full = lambda a: pl.BlockSpec(a.shape, lambda t: (0,) * a.ndim)
    return pl.pallas_call(
        functools.partial(_conv_kernel, tile=tile, geom=geom),
        grid=(T // tile,),
        in_specs=[
            pl.BlockSpec((CONV_HALO, C), lambda t: (jnp.maximum(t * hpt - 1, 0), 0)),
            pl.BlockSpec((tile, C), lambda t: (t, 0)),
            pl.BlockSpec((CONV_HALO, C), lambda t: (jnp.minimum((t + 1) * hpt, n_halo - 1), 0)),
            full(z_meta), full(conv_w), full(conv_b), full(ln_g), full(ln_b),
        ],
        out_specs=pl.BlockSpec((tile, C), lambda t: (t, 0)),
        out_shape=jax.ShapeDtypeStruct((T, C), BF16),
        scratch_shapes=[pltpu.VMEM((tile + 2 * CONV_HALO, C), F32),
                        pltpu.VMEM((7, tile + 2 * CONV_HALO - 8, C), F32),
                        pltpu.VMEM((tile, C), F32)],
        compiler_params=pltpu.CompilerParams(dimension_semantics=("parallel",), vmem_limit_bytes=VMEM_LIMIT),
        name="conv",
    )(zc, zc, zc, z_meta, conv_w, conv_b, ln_g, ln_b)


def _route(r):
    lane = lax.broadcasted_iota(jnp.int32, r.shape, 1)
    big = jnp.int32(1 << 20)
    is_g = jnp.logical_and(lane >= N_EXPERTS, lane < N_EXPERTS + N_GROUPS)
    lg = jnp.where(is_g, r, -jnp.inf)
    mg = jnp.max(lg, axis=1, keepdims=True)
    g_w = 1.0 / jnp.sum(jnp.exp(lg - mg), axis=1, keepdims=True)
    g_idx = jnp.min(jnp.where(lg == mg, lane - N_EXPERTS, big), axis=1, keepdims=True)
    in_group = jnp.logical_and(lane < N_EXPERTS, jnp.right_shift(lane, 3) == g_idx)
    le = jnp.where(in_group, r, -jnp.inf)
    m1 = jnp.max(le, axis=1, keepdims=True)
    den = jnp.sum(jnp.exp(le - m1), axis=1, keepdims=True)
    i1 = jnp.min(jnp.where(le == m1, lane, big), axis=1, keepdims=True)
    le2 = jnp.where(lane == i1, -jnp.inf, le)
    m2 = jnp.max(le2, axis=1, keepdims=True)
    i2 = jnp.min(jnp.where(le2 == m2, lane, big), axis=1, keepdims=True)
    p1 = 1.0 / den
    p2 = jnp.exp(m2 - m1) / den
    tot = p1 + p2
    return i1, i2, g_w * (p1 / tot), g_w * (p2 / tot)


def _pack_bf16_pairs(x):
    half = x.shape[1] // 2
    words = []
    for j in range(half // LANES):
        lo = pltpu.bitcast(x[:, j * LANES:(j + 1) * LANES].astype(BF16).astype(F32), jnp.uint32)
        hi = pltpu.bitcast(x[:, half + j * LANES:half + (j + 1) * LANES].astype(BF16).astype(F32), jnp.uint32)
        words.append(hi | (lo >> 16))
    return words


def _unpack_bf16_pairs(words):
    lo = [pltpu.bitcast(w << 16, F32) for w in words]
    hi = [pltpu.bitcast(w & jnp.uint32(0xFFFF0000), F32) for w in words]
    return jnp.concatenate(lo + hi, axis=1)


def _out_kernel(x_ref, att_ref, cz_ref, ga_ref, gc_ref, lng_ref, lnb_ref, watt_ref, wco_ref, wout_ref,
                l1g_ref, l1b_ref, wr_ref, br_ref, h1_ref, h1p_ref, rt_ref, fld_ref, cnt_ref, *, tile):
    @pl.when(pl.program_id(0) == 0)
    def _():
        cnt_ref[...] = jnp.zeros_like(cnt_ref)

    h0 = _layer_norm(x_ref[...], lng_ref[...], lnb_ref[...])
    a = _dot(att_ref[...], watt_ref[...])
    c = _dot(cz_ref[...], wco_ref[...])
    mix = ga_ref[...].astype(F32) * a + gc_ref[...].astype(F32) * c
    m = _dot(mix.astype(BF16), wout_ref[...])
    h1 = _layer_norm(ALPHA * h0 + m, l1g_ref[...], l1b_ref[...])
    h1_ref[...] = h1
    for j, w in enumerate(_pack_bf16_pairs(h1)):
        h1p_ref[:, j, :, :] = w.reshape(tile // 8, 8, LANES)

    r = _dot(h1.astype(BF16), wr_ref[...]) + br_ref[...]
    i1, i2, w1, w2 = _route(r)
    lane = lax.broadcasted_iota(jnp.int32, (tile, ROUTER_LANES), 1)
    hit1, hit2 = lane == i1, lane == i2
    onehot = jnp.where(jnp.logical_or(hit1, hit2), 1.0, 0.0)
    row_i = lax.broadcasted_iota(jnp.int32, (tile, tile), 0)
    col_i = lax.broadcasted_iota(jnp.int32, (tile, tile), 1)
    before = jnp.where(col_i < row_i, 1.0, 0.0).astype(BF16)
    seen = _dot(before, onehot.astype(BF16)) + cnt_ref[0:1, :]
    rank1 = jnp.sum(jnp.where(hit1, seen, 0.0), axis=1, keepdims=True)
    rank2 = jnp.sum(jnp.where(hit2, seen, 0.0), axis=1, keepdims=True)
    cnt_ref[...] = cnt_ref[...] + jnp.sum(onehot, axis=0, keepdims=True)
    fields = (i1.astype(F32), i2.astype(F32), w1, w2, rank1, rank2)
    rt = jnp.zeros((tile, ROUTER_LANES), F32)
    for k, v in enumerate(fields):
        rt = jnp.where(lane == k, v, rt)
    rt_ref[...] = rt
    fld_ref[...] = jnp.transpose(rt)[0:ROUTE_FIELDS, :]


def _out(x, att, cz, ga, gc, ln_g, ln_b, watt, wco, wout, l1g, l1b, wr, br, tile, first_tile):
    T, D = x.shape
    own = lambda w: pl.BlockSpec((tile, w), lambda i: (i, 0))
    flat = lambda w: pl.BlockSpec((tile, w), lambda i: (first_tile + i, 0))
    full = lambda a: pl.BlockSpec(a.shape, lambda i: (0,) * a.ndim)
    return pl.pallas_call(
        functools.partial(_out_kernel, tile=tile),
        grid=(T // tile,),
        in_specs=[own(D), flat(ATT_WIDTH), flat(CONV_WIDTH), flat(D), flat(D), full(ln_g), full(ln_b),
                  full(watt), full(wco), full(wout), full(l1g), full(l1b), full(wr), full(br)],
        out_specs=[own(D),
                   pl.BlockSpec((tile // 8, PACK_WORDS, 8, LANES), lambda i: (i, 0, 0, 0)),
                   own(ROUTER_LANES),
                   pl.BlockSpec((ROUTE_FIELDS, tile), lambda i: (0, i)),
                   pl.BlockSpec((8, ROUTER_LANES), lambda i: (0, 0))],
        out_shape=[jax.ShapeDtypeStruct((T, D), F32),
                   jax.ShapeDtypeStruct((T // 8, PACK_WORDS, 8, LANES), jnp.uint32),
                   jax.ShapeDtypeStruct((T, ROUTER_LANES), F32),
                   jax.ShapeDtypeStruct((ROUTE_FIELDS, T), F32),
                   jax.ShapeDtypeStruct((8, ROUTER_LANES), F32)],
        compiler_params=pltpu.CompilerParams(dimension_semantics=("arbitrary",), vmem_limit_bytes=VMEM_LIMIT),
        name="out",
    )(x, att, cz, ga, gc, ln_g, ln_b, watt, wco, wout, l1g, l1b, wr, br)


def _sc_scatter2(src, idx_a, idx_b, n_out):
    m = src.shape[0]
    mesh = plsc.VectorSubcoreMesh(core_axis_name="c", subcore_axis_name="s")

    @functools.partial(pl.kernel, out_type=jax.ShapeDtypeStruct((n_out, LANES), src.dtype), mesh=mesh)
    def k(x_hbm, ia_hbm, ib_hbm, o_hbm):
        def body(x_vmem, ia_vmem, ib_vmem):
            pltpu.sync_copy(x_vmem, o_hbm.at[ia_vmem.at[0]])
            pltpu.sync_copy(x_vmem, o_hbm.at[ib_vmem.at[0]])

        pltpu.emit_pipeline(
            body, grid=(m // SC_WINDOW,),
            in_specs=[pl.BlockSpec((SC_WINDOW, LANES), index_map=lambda i: (i, 0)),
                      pl.BlockSpec((1, SC_WINDOW), index_map=lambda i: (0, i)),
                      pl.BlockSpec((1, SC_WINDOW), index_map=lambda i: (0, i))],
            out_specs=[],
            core_axis_name=("c", "s"), dimension_semantics=(pltpu.PARALLEL,),
        )(x_hbm, ia_hbm, ib_hbm)

    return k(src, idx_a.reshape(1, m), idx_b.reshape(1, m))


def _sc_gather(table, idx):
    m = idx.shape[0]
    mesh = plsc.VectorSubcoreMesh(core_axis_name="c", subcore_axis_name="s")

    @functools.partial(pl.kernel, out_type=jax.ShapeDtypeStruct((m, LANES), table.dtype), mesh=mesh)
    def k(x_hbm, i_hbm, o_hbm):
        def body(i_vmem, o_vmem):
            pltpu.sync_copy(x_hbm.at[i_vmem.at[0]], o_vmem)

        pltpu.emit_pipeline(
            body, grid=(m // SC_WINDOW,),
            in_specs=[pl.BlockSpec((1, SC_WINDOW), index_map=lambda i: (0, i))],
            out_specs=[pl.BlockSpec((SC_WINDOW, LANES), index_map=lambda i: (i, 0))],
            core_axis_name=("c", "s"), dimension_semantics=(pltpu.PARALLEL,),
        )(i_hbm, o_hbm)

    return k(table, idx.reshape(1, m))


def _expert_kernel(te_ref, tv_ref, xs_ref, wg_ref, wu_ref, wd_ref, ys_ref, wgb_ref, wub_ref, wdb_ref, *, tile):
    n = pl.program_id(0)
    valid = tv_ref[n]

    @pl.when(jnp.logical_or(n == 0, te_ref[n] != te_ref[jnp.maximum(n - 1, 0)]))
    def _():
        wgb_ref[...] = wg_ref[...].astype(BF16)
        wub_ref[...] = wu_ref[...].astype(BF16)
        wdb_ref[...] = wd_ref[...].astype(BF16)

    @pl.when(valid > 0)
    def _():
        x = _unpack_bf16_pairs([xs_ref[:, j, :, :].reshape(tile, LANES) for j in range(PACK_WORDS)])
        rows = lax.broadcasted_iota(jnp.int32, (tile, 1), 0)
        x = jnp.where(rows < valid, x, 0.0).astype(BF16)
        gt = _dot(x, wgb_ref[...])
        up = _dot(x, wub_ref[...])
        hid = (gt * jax.nn.sigmoid(gt)) * up
        y = _dot(hid.astype(BF16), wdb_ref[...])
        for j, w in enumerate(_pack_bf16_pairs(y)):
            ys_ref[:, j, :, :] = w.reshape(tile // 8, 8, LANES)

    @pl.when(valid <= 0)
    def _():
        ys_ref[...] = jnp.zeros_like(ys_ref)


def _experts(tile_expert, tile_valid, xs, w_gate, w_up, w_down, tile):
    n_tiles = xs.shape[0] * 8 // tile
    blk = pl.BlockSpec((tile // 8, PACK_WORDS, 8, LANES), lambda n, te, tv: (n, 0, 0, 0))
    return pl.pallas_call(
        functools.partial(_expert_kernel, tile=tile),
        grid_spec=pltpu.PrefetchScalarGridSpec(
            num_scalar_prefetch=2,
            grid=(n_tiles,),
            in_specs=[blk,
                      pl.BlockSpec((None, D_MODEL, D_EXPERT), lambda n, te, tv: (te[n], 0, 0)),
                      pl.BlockSpec((None, D_MODEL, D_EXPERT), lambda n, te, tv: (te[n], 0, 0)),
                      pl.BlockSpec((None, D_EXPERT, D_MODEL), lambda n, te, tv: (te[n], 0, 0))],
            out_specs=blk,
            scratch_shapes=[pltpu.VMEM((D_MODEL, D_EXPERT), BF16), pltpu.VMEM((D_MODEL, D_EXPERT), BF16),
                            pltpu.VMEM((D_EXPERT, D_MODEL), BF16)],
        ),
        out_shape=jax.ShapeDtypeStruct(xs.shape, jnp.uint32),
        compiler_params=pltpu.CompilerParams(
            dimension_semantics=("arbitrary",), vmem_limit_bytes=VMEM_LIMIT),
        name="experts",
    )(tile_expert, tile_valid, xs, w_gate, w_up, w_down)


def _final_kernel(h_ref, g_ref, rt_ref, l2g_ref, l2b_ref, o_ref, *, tile):
    rt = rt_ref[...]
    lane = lax.broadcasted_iota(jnp.int32, rt.shape, 1)
    w1 = jnp.sum(jnp.where(lane == 2, rt, 0.0), axis=1, keepdims=True)
    w2 = jnp.sum(jnp.where(lane == 3, rt, 0.0), axis=1, keepdims=True)
    y1 = _unpack_bf16_pairs([g_ref[0, :, j, :, :].reshape(tile, LANES) for j in range(PACK_WORDS)])
    y2 = _unpack_bf16_pairs([g_ref[1, :, j, :, :].reshape(tile, LANES) for j in range(PACK_WORDS)])
    f = w1 * y1 + w2 * y2
    o_ref[...] = _layer_norm(ALPHA * h_ref[...] + f, l2g_ref[...], l2b_ref[...])


def _final(h1, g, rt, l2g, l2b, tile):
    T, D = h1.shape
    row = lambda w: pl.BlockSpec((tile, w), lambda i: (i, 0))
    full = lambda a: pl.BlockSpec(a.shape, lambda i: (0,) * a.ndim)
    return pl.pallas_call(
        functools.partial(_final_kernel, tile=tile),
        grid=(T // tile,),
        in_specs=[row(D),
                  pl.BlockSpec((2, tile // 8, PACK_WORDS, 8, LANES), lambda i: (0, i, 0, 0, 0)),
                  row(ROUTER_LANES), full(l2g), full(l2b)],
        out_specs=row(D),
        out_shape=jax.ShapeDtypeStruct((T, D), F32),
        compiler_params=pltpu.CompilerParams(dimension_semantics=("parallel",), vmem_limit_bytes=VMEM_LIMIT),
        name="final",
    )(h1, g, rt, l2g, l2b)


def _dispatch_plan(fields, counts, n_tokens, tile):
    cnt = counts[0, :N_EXPERTS].astype(jnp.int32)
    padded = (cnt + tile - 1) // tile * tile
    base = jnp.cumsum(padded) - padded
    e_ids = jnp.arange(N_EXPERTS, dtype=jnp.int32)

    def dest(row_e, row_r):
        e = fields[row_e].astype(jnp.int32)
        seg = jnp.sum(jnp.where(e[None, :] == e_ids[:, None], base[:, None], 0), axis=0)
        pos = seg + fields[row_r].astype(jnp.int32)
        p = pos.reshape(n_tokens // 8, 1, 8)
        j = jnp.arange(PACK_WORDS, dtype=jnp.int32).reshape(1, PACK_WORDS, 1)
        return ((p // 8) * (8 * PACK_WORDS) + j * 8 + p % 8).reshape(-1)

    n_tiles = (2 * n_tokens) // tile + N_EXPERTS
    start = jnp.arange(n_tiles, dtype=jnp.int32) * tile
    seg_end = base + padded
    te = jnp.minimum(jnp.sum((start[:, None] >= seg_end[None, :]).astype(jnp.int32), axis=1), N_EXPERTS - 1)
    te_base = jnp.sum(jnp.where(te[:, None] == e_ids[None, :], base[None, :], 0), axis=1)
    te_cnt = jnp.sum(jnp.where(te[:, None] == e_ids[None, :], cnt[None, :], 0), axis=1)
    tv = jnp.clip(te_cnt - (start - te_base), 0, tile)
    return dest(0, 4), dest(1, 5), te, tv, n_tiles


def _t5_bucket(rel):
    half = N_BUCKETS // 2
    max_exact = half // 2
    ret = jnp.where(rel > 0, half, 0)
    n = jnp.abs(rel)
    nf = jnp.maximum(n, 1).astype(F32)
    large = max_exact + (jnp.log(nf / max_exact) / math.log(MAX_DISTANCE / max_exact)
                         * (half - max_exact)).astype(jnp.int32)
    large = jnp.minimum(large, half - 1)
    return ret + jnp.where(n < max_exact, n, large)


def _bucket_bias(rel_bias, bucket):
    rb = rel_bias.astype(F32)
    out = jnp.zeros((N_Q_HEADS,) + bucket.shape, F32)
    for b in range(N_BUCKETS):
        out = out + jnp.where(bucket[None] == b, rb[b][:, None, None], 0.0)
    return out


def _pair_rows(t):
    return jnp.concatenate([t[:N_PAIRS], t[N_PAIRS:]], axis=-1).reshape(N_PAIRS * BLOCK, -1)


def _bias_tables(rel_bias, sink):
    qi = jnp.arange(BLOCK)
    kj = jnp.arange(3 * BLOCK) - BLOCK
    rel = kj[None, :] - qi[:, None]
    band = _bucket_bias(rel_bias, _t5_bucket(rel))
    in_win = (jnp.abs(rel) <= WINDOW)[None]
    not_prev = (kj >= 0)[None, None, :]
    not_next = (kj < BLOCK)[None, None, :]
    variants = [jnp.where(in_win & not_prev, band, NEG),
                jnp.where(in_win, band, NEG),
                jnp.where(in_win & not_next, band, NEG)]
    bias = jnp.stack([_pair_rows(v) for v in variants])

    off = BLOCK - N_META
    mvars = []
    for blk in (1, 2):
        qpos = blk * BLOCK + qi - off
        meta_rel = jnp.arange(N_META)[None, :] - qpos[:, None]
        mvars.append(_pair_rows(_bucket_bias(rel_bias, _t5_bucket(meta_rel))))
    mbias = jnp.stack(mvars)
    s = sink.astype(F32)
    sink_tab = jnp.repeat(jnp.stack([s[:N_PAIRS], s[N_PAIRS:]], axis=-1), BLOCK, axis=0)
    return bias, mbias, sink_tab


def kernel(x_prompt, x_sample, meta, ln_in_g, ln_in_b, rel_bias, w_in, w_att_branch, sink, conv_w, conv_b,
           conv_ln_g, conv_ln_b, w_conv_out, w_out, ln1_g, ln1_b, w_group, b_group, w_router, b_router,
           w_gate, w_up, w_down, ln2_g, ln2_b):
    row = lambda v: v.reshape(1, -1).astype(F32)
    w = w_in[0]
    wq = (w[:, :Q_END].reshape(D_MODEL, 2, N_PAIRS, HEAD_DIM).transpose(0, 2, 1, 3)
          .reshape(D_MODEL, ATT_WIDTH).astype(BF16))
    watt = (w_att_branch[0].reshape(2, N_PAIRS, HEAD_DIM, D_MODEL).transpose(1, 0, 2, 3)
            .reshape(ATT_WIDTH, D_MODEL).astype(BF16))
    wkv = w[:, Q_END:V_END].astype(BF16)
    wglu = w[:, V_END:GLU_END].astype(BF16)
    wg = w[:, GLU_END:].astype(BF16)
    wco = w_conv_out[0].astype(BF16)
    wout = w_out[0].astype(BF16)
    wr = jnp.zeros((D_MODEL, ROUTER_LANES), F32)
    wr = wr.at[:, :N_EXPERTS].set(w_router[0]).at[:, N_EXPERTS:N_EXPERTS + N_GROUPS].set(w_group[0]).astype(BF16)
    br = jnp.zeros((1, ROUTER_LANES), F32)
    br = br.at[0, :N_EXPERTS].set(b_router[0]).at[0, N_EXPERTS:N_EXPERTS + N_GROUPS].set(b_group[0])
    ln_g, ln_b = row(ln_in_g), row(ln_in_b)
    bias, mbias, sink_tab = _bias_tables(rel_bias, sink[0])

    xm = jnp.concatenate([jnp.zeros((BLOCK - N_META, D_MODEL), F32), meta.astype(F32)], axis=0)
    _, kv_m, zc_m, _, _ = _proj(xm, xm, _Geom(1, 1, 1, 1), ln_g, ln_b, wq, wkv, wglu, wg, BLOCK)
    kv_meta = kv_m[BLOCK - N_META:BLOCK]
    z_meta = zc_m[BLOCK - N_META:BLOCK]

    (bp, sp, _), (bs, ss, _) = x_prompt.shape, x_sample.shape
    xp, xs = x_prompt.reshape(bp * sp, D_MODEL), x_sample.reshape(bs * ss, D_MODEL)
    geom = _geom(x_prompt, x_sample, TILE)
    q, kv, zc, ga, gc = _proj(xp, xs, geom, ln_g, ln_b, wq, wkv, wglu, wg, TILE)
    att = _attn(q, kv, geom, kv_meta, bias, mbias, sink_tab, TILE)
    cz = _conv(zc, geom, z_meta, conv_w[0], row(conv_b[0]), row(conv_ln_g[0]), row(conv_ln_b[0]), TILE)
    l1g, l1b, l2g, l2b = row(ln1_g[0]), row(ln1_b[0]), row(ln2_g[0]), row(ln2_b[0])

    def moe(x, first_tile):
        n = x.shape[0]
        h1, h1p, rt, fields, counts = _out(x, att, cz, ga, gc, ln_g, ln_b, watt, wco, wout, l1g, l1b, wr, br,
                                           TILE, first_tile)
        idx1, idx2, tile_expert, tile_valid, n_tiles = _dispatch_plan(fields, counts, n, TILE_EXPERT)
        n_rows = n_tiles * TILE_EXPERT
        xsorted = _sc_scatter2(h1p.reshape(n * PACK_WORDS, LANES), idx1, idx2, n_rows * PACK_WORDS)
        ys = _experts(tile_expert, tile_valid, xsorted.reshape(n_rows // 8, PACK_WORDS, 8, LANES),
                      w_gate[0], w_up[0], w_down[0], TILE_EXPERT)
        g = _sc_gather(ys.reshape(n_rows * PACK_WORDS, LANES), jnp.concatenate([idx1, idx2]))
        return _final(h1, g.reshape(2, n // 8, PACK_WORDS, 8, LANES), rt, l2g, l2b, TILE)

    return moe(xp, 0).reshape(x_prompt.shape), moe(xs, geom.n_p).reshape(x_sample.shape)
```

```python
import functools
import math
from typing import NamedTuple

import jax
import jax.numpy as jnp
from jax import lax
from jax.experimental import pallas as pl
from jax.experimental.pallas import tpu as pltpu
from jax.experimental.pallas import tpu_sc as plsc

D_MODEL = 1024
N_META = 16
BLOCK = 128
WINDOW = 128
N_Q_HEADS = 8
N_KV_HEADS = 2
HEAD_DIM = 64
ATT_WIDTH = N_Q_HEADS * HEAD_DIM
KV_WIDTH = N_KV_HEADS * HEAD_DIM
CONV_WIDTH = D_MODEL // 2
CONV_K = 31
N_BUCKETS = 32
MAX_DISTANCE = 128
N_GROUPS = 4
EXPERTS_PER_GROUP = 8
N_EXPERTS = N_GROUPS * EXPERTS_PER_GROUP
D_EXPERT = 256
LN_EPS = 1e-5
DEPTH = 1
ALPHA = (2 * DEPTH) ** 0.25
NEG = -1e30
Q_END = ATT_WIDTH
K_END = Q_END + KV_WIDTH
V_END = K_END + KV_WIDTH
GLU_END = V_END + 2 * CONV_WIDTH
GA_END = GLU_END + D_MODEL

N_PAIRS = N_Q_HEADS // 2
LANES = 128
CONV_HALO = 16
ROUTER_LANES = 128

TILE = 512
CONV_ROWS = 128
LN_ROWS = 64
SHIFT_ROWS = 128
TILE_EXPERT = 512
PACK_WORDS = 4
SC_WINDOW = 128
ROUTE_FIELDS = 8
VMEM_LIMIT = 56 * 1024 * 1024

BF16 = jnp.bfloat16
F32 = jnp.float32


def _layer_norm(x, g, b):
    mu = jnp.mean(x, axis=-1, keepdims=True)
    xc = x - mu
    var = jnp.mean(xc * xc, axis=-1, keepdims=True)
    return xc * lax.rsqrt(var + LN_EPS) * g + b


def _dot(a, b):
    return jnp.dot(a, b, preferred_element_type=F32)


def _dot_nt(a, b):
    return lax.dot_general(a, b, (((1,), (1,)), ((), ())), preferred_element_type=F32)


class _Geom(NamedTuple):
    n_p: int
    n_s: int
    tp: int
    ts: int


def _geom(x_prompt, x_sample, tile):
    (bp, sp, _), (bs, ss, _) = x_prompt.shape, x_sample.shape
    return _Geom(bp * sp // tile, bs * ss // tile, sp // tile, ss // tile)


def _seq_pos(t, g):
    is_p = t < g.n_p
    local = jnp.where(is_p, lax.rem(t, g.tp), lax.rem(jnp.maximum(t - g.n_p, 0), g.ts))
    return is_p, local == 0, local == jnp.where(is_p, g.tp - 1, g.ts - 1)


def _x_specs(g, tile, width):
    return [pl.BlockSpec((tile, width), lambda t: (jnp.minimum(t, g.n_p - 1), 0)),
            pl.BlockSpec((tile, width), lambda t: (jnp.maximum(t - g.n_p, 0), 0))]


def _proj_kernel(xp_ref, xs_ref, g_ref, b_ref, wq_ref, wkv_ref, wglu_ref, q_ref, kv_ref, zc_ref, *, geom):
    x = jnp.where(pl.program_id(0) < geom.n_p, xp_ref[...], xs_ref[...])
    h = _layer_norm(x, g_ref[...], b_ref[...]).astype(BF16)
    q_ref[...] = _dot(h, wq_ref[...]).astype(BF16)
    kv_ref[...] = _dot(h, wkv_ref[...]).astype(BF16)
    u = _dot(h, wglu_ref[...])
    zc_ref[...] = (u[:, :CONV_WIDTH] * jax.nn.sigmoid(u[:, CONV_WIDTH:])).astype(BF16)


def _proj(xp, xs, geom, ln_g, ln_b, wq, wkv, wglu, tile):
    T = (geom.n_p + geom.n_s) * tile
    row = lambda w: pl.BlockSpec((tile, w), lambda t: (t, 0))
    full = lambda a: pl.BlockSpec(a.shape, lambda t: (0,) * a.ndim)
    return pl.pallas_call(
        functools.partial(_proj_kernel, geom=geom),
        grid=(geom.n_p + geom.n_s,),
        in_specs=_x_specs(geom, tile, D_MODEL) + [full(ln_g), full(ln_b), full(wq), full(wkv), full(wglu)],
        out_specs=[row(ATT_WIDTH), row(2 * KV_WIDTH), row(CONV_WIDTH)],
        out_shape=[
            jax.ShapeDtypeStruct((T, ATT_WIDTH), BF16),
            jax.ShapeDtypeStruct((T, 2 * KV_WIDTH), BF16),
            jax.ShapeDtypeStruct((T, CONV_WIDTH), BF16),
        ],
        compiler_params=pltpu.CompilerParams(dimension_semantics=("parallel",), vmem_limit_bytes=VMEM_LIMIT),
        name="proj",
    )(xp, xs, ln_g, ln_b, wq, wkv, wglu)


def _attn_kernel(q_ref, kvp_ref, kvc_ref, kvn_ref, kvm_ref, bias_ref, mbias_ref, sink_ref, o_ref, *, tile, geom):
    _, seq_first, seq_last = _seq_pos(pl.program_id(0), geom)
    blocks = tile // BLOCK
    scale = HEAD_DIM ** -0.5

    lane = lax.broadcasted_iota(jnp.int32, (1, LANES), 1)
    lo = lane < HEAD_DIM

    def split_heads(t):
        z = jnp.zeros_like(t)
        return jnp.where(lo, t, z), jnp.where(lo, z, t)

    kv_ext = jnp.concatenate([kvp_ref[...], kvc_ref[...], kvn_ref[...]], axis=0)
    k_ext = kv_ext[:, :KV_WIDTH] * jnp.asarray(scale, BF16)
    v_ext = kv_ext[:, KV_WIDTH:]
    ka, kb = split_heads(k_ext)
    va, vb = split_heads(v_ext)
    kma, kmb = split_heads(kvm_ref[:, :KV_WIDTH] * jnp.asarray(scale, BF16))
    vma, vmb = split_heads(kvm_ref[:, KV_WIDTH:])
    km_cat = jnp.concatenate([kma, kmb], axis=0)
    vm_cat = jnp.concatenate([vma, vmb], axis=0)

    mlane = lax.broadcasted_iota(jnp.int32, (1, 2 * N_META), 1)
    m_first = mlane < N_META
    sink = sink_ref[...]
    lane_o = lax.broadcasted_iota(jnp.int32, (1, LANES), 1) < HEAD_DIM

    for j in range(blocks):
        first = jnp.logical_and(seq_first, j == 0)
        last = jnp.logical_and(seq_last, j == blocks - 1)
        variant = jnp.where(first, 0, jnp.where(last, 2, 1))
        mvariant = jnp.where(first, 0, 1)

        r0 = j * BLOCK
        qb = q_ref[r0:r0 + BLOCK, :]
        q4 = jnp.concatenate([qb[:, p * LANES:(p + 1) * LANES] for p in range(N_PAIRS)], axis=0)
        k_cat = jnp.concatenate([ka[r0:r0 + 3 * BLOCK], kb[r0:r0 + 3 * BLOCK]], axis=0)
        v_cat = jnp.concatenate([va[r0:r0 + 3 * BLOCK], vb[r0:r0 + 3 * BLOCK]], axis=0)

        s = _dot_nt(q4, k_cat) + bias_ref[variant]
        sm = _dot_nt(q4, km_cat) + mbias_ref[mvariant]

        s_a, s_b = s[:, :3 * BLOCK], s[:, 3 * BLOCK:]
        sm_a = jnp.where(m_first, sm, NEG)
        sm_b = jnp.where(m_first, NEG, sm)
        m_a = jnp.maximum(jnp.maximum(jnp.max(s_a, axis=1, keepdims=True),
                                      jnp.max(sm_a, axis=1, keepdims=True)), sink[:, 0:1])
        m_b = jnp.maximum(jnp.maximum(jnp.max(s_b, axis=1, keepdims=True),
                                      jnp.max(sm_b, axis=1, keepdims=True)), sink[:, 1:2])
        p_a = jnp.exp(s_a - m_a)
        p_b = jnp.exp(s_b - m_b)
        pm = jnp.exp(jnp.where(m_first, sm - m_a, sm - m_b))
        l_a = (jnp.sum(p_a, axis=1, keepdims=True) + jnp.sum(jnp.where(m_first, pm, 0.0), axis=1, keepdims=True)
               + jnp.exp(sink[:, 0:1] - m_a))
        l_b = (jnp.sum(p_b, axis=1, keepdims=True) + jnp.sum(jnp.where(m_first, 0.0, pm), axis=1, keepdims=True)
               + jnp.exp(sink[:, 1:2] - m_b))
        p = jnp.concatenate([p_a, p_b], axis=1).astype(BF16)
        o = _dot(p, v_cat) + _dot(pm.astype(BF16), vm_cat)
        o = o * jnp.where(lane_o, 1.0 / l_a, 1.0 / l_b)
        for pr in range(N_PAIRS):
            o_ref[r0:r0 + BLOCK, pr * LANES:(pr + 1) * LANES] = o[pr * BLOCK:(pr + 1) * BLOCK].astype(BF16)


def _attn(q, kv, geom, kv_meta, bias, mbias, sink_tab, tile):
    T = q.shape[0]
    bpt = tile // BLOCK
    n_blocks = T // BLOCK
    full = lambda a: pl.BlockSpec(a.shape, lambda t: (0,) * a.ndim)
    return pl.pallas_call(
        functools.partial(_attn_kernel, tile=tile, geom=geom),
        grid=(T // tile,),
        in_specs=[
            pl.BlockSpec((tile, ATT_WIDTH), lambda t: (t, 0)),
            pl.BlockSpec((BLOCK, 2 * KV_WIDTH), lambda t: (jnp.maximum(t * bpt - 1, 0), 0)),
            pl.BlockSpec((tile, 2 * KV_WIDTH), lambda t: (t, 0)),
            pl.BlockSpec((BLOCK, 2 * KV_WIDTH), lambda t: (jnp.minimum((t + 1) * bpt, n_blocks - 1), 0)),
            full(kv_meta), full(bias), full(mbias), full(sink_tab),
        ],
        out_specs=pl.BlockSpec((tile, ATT_WIDTH), lambda t: (t, 0)),
        out_shape=jax.ShapeDtypeStruct((T, ATT_WIDTH), BF16),
        compiler_params=pltpu.CompilerParams(dimension_semantics=("parallel",), vmem_limit_bytes=VMEM_LIMIT),
        name="attn",
    )(q, kv, kv, kv, kv_meta, bias, mbias, sink_tab)


def _conv_kernel(zp_ref, zc_ref, zn_ref, zm_ref, w_ref, cb_ref, g_ref, b_ref, o_ref, ext_ref, sh_ref, y_ref,
                 *, tile, geom):
    i = pl.program_id(0)
    _, seq_first, seq_last = _seq_pos(i, geom)
    ext_ref[0:CONV_HALO, :] = jnp.where(seq_first, zm_ref[...], zp_ref[...]).astype(F32)
    ext_ref[CONV_HALO:CONV_HALO + tile, :] = zc_ref[...].astype(F32)
    ext_ref[CONV_HALO + tile:, :] = jnp.where(seq_last, 0.0, zn_ref[...].astype(F32))
    off = CONV_HALO - CONV_K // 2
    reach = (off + CONV_K - 1) // 8 * 8
    for p in range(1, 8):
        for r0 in range(0, tile + reach, SHIFT_ROWS):
            n = min(SHIFT_ROWS, tile + reach - r0)
            sh_ref[p - 1, r0:r0 + n, :] = ext_ref[r0 + p:r0 + p + n, :]

    def taps(r0, cs):
        acc = jnp.zeros((CONV_ROWS, LANES), F32)
        for k in range(CONV_K):
            p, a = (off + k) % 8, (off + k) // 8 * 8
            rows = slice(r0 + a, r0 + a + CONV_ROWS)
            win = ext_ref[rows, cs] if p == 0 else sh_ref[p - 1, rows, cs]
            acc = acc + win * w_ref[k:k + 1, cs]
        y_ref[r0:r0 + CONV_ROWS, cs] = acc

    for c in range(CONV_WIDTH // LANES):
        for r in range(tile // CONV_ROWS):
            pl.when(i >= 0)(functools.partial(taps, r * CONV_ROWS, slice(c * LANES, (c + 1) * LANES)))
    cb, g, b = cb_ref[...], g_ref[...], b_ref[...]
    for r in range(tile // LN_ROWS):
        r0 = r * LN_ROWS
        y = _layer_norm(y_ref[r0:r0 + LN_ROWS, :] + cb, g, b)
        o_ref[r0:r0 + LN_ROWS, :] = (y * jax.nn.sigmoid(y)).astype(BF16)


def _conv(zc, geom, z_meta, conv_w, conv_b, ln_g, ln_b, tile):
    T, C = zc.shape
    hpt = tile // CONV_HALO
    n_halo = T // CONV_HALO
    full = lambda a: pl.BlockSpec(a.shape, lambda t: (0,) * a.ndim)
    return pl.pallas_call(
        functools.partial(_conv_kernel, tile=tile, geom=geom),
        grid=(T // tile,),
        in_specs=[
            pl.BlockSpec((CONV_HALO, C), lambda t: (jnp.maximum(t * hpt - 1, 0), 0)),
            pl.BlockSpec((tile, C), lambda t: (t, 0)),
            pl.BlockSpec((CONV_HALO, C), lambda t: (jnp.minimum((t + 1) * hpt, n_halo - 1), 0)),
            full(z_meta), full(conv_w), full(conv_b), full(ln_g), full(ln_b),
        ],
        out_specs=pl.BlockSpec((tile, C), lambda t: (t, 0)),
        out_shape=jax.ShapeDtypeStruct((T, C), BF16),
        scratch_shapes=[pltpu.VMEM((tile + 2 * CONV_HALO, C), F32),
                        pltpu.VMEM((7, tile + 2 * CONV_HALO - 8, C), F32),
                        pltpu.VMEM((tile, C), F32)],
        compiler_params=pltpu.CompilerParams(dimension_semantics=("parallel",), vmem_limit_bytes=VMEM_LIMIT),
        name="conv",
    )(zc, zc, zc, z_meta, conv_w, conv_b, ln_g, ln_b)


def _route(r):
    lane_i = lax.broadcasted_iota(jnp.int32, r.shape, 1)
    lane = lane_i.astype(F32)
    big = float(1 << 20)
    is_g = jnp.logical_and(lane_i >= N_EXPERTS, lane_i < N_EXPERTS + N_GROUPS)
    lg = jnp.where(is_g, r, -jnp.inf)
    mg = jnp.max(lg, axis=1, keepdims=True)
    g_w = 1.0 / jnp.sum(jnp.exp(lg - mg), axis=1, keepdims=True)
    g_idx = jnp.min(jnp.where(lg == mg, lane - N_EXPERTS, big), axis=1, keepdims=True)
    lane_group = jnp.right_shift(lane_i, 3).astype(F32)
    in_group = jnp.logical_and(lane_i < N_EXPERTS, lane_group == g_idx)
    le = jnp.where(in_group, r, -jnp.inf)
    m1 = jnp.max(le, axis=1, keepdims=True)
    den = jnp.sum(jnp.exp(le - m1), axis=1, keepdims=True)
    i1 = jnp.min(jnp.where(le == m1, lane, big), axis=1, keepdims=True)
    le2 = jnp.where(lane == i1, -jnp.inf, le)
    m2 = jnp.max(le2, axis=1, keepdims=True)
    i2 = jnp.min(jnp.where(le2 == m2, lane, big), axis=1, keepdims=True)
    p1 = 1.0 / den
    p2 = jnp.exp(m2 - m1) / den
    tot = p1 + p2
    return i1, i2, g_w * (p1 / tot), g_w * (p2 / tot)


def _pack_bf16_pairs(x):
    half = x.shape[1] // 2
    words = []
    for j in range(half // LANES):
        lo = pltpu.bitcast(x[:, j * LANES:(j + 1) * LANES].astype(BF16).astype(F32), jnp.uint32)
        hi = pltpu.bitcast(x[:, half + j * LANES:half + (j + 1) * LANES].astype(BF16).astype(F32), jnp.uint32)
        words.append(hi | (lo >> 16))
    return words


def _unpack_bf16_pairs(words):
    lo = [pltpu.bitcast(w << 16, F32) for w in words]
    hi = [pltpu.bitcast(w & jnp.uint32(0xFFFF0000), F32) for w in words]
    return jnp.concatenate(lo + hi, axis=1)


def _out_kernel(x_ref, att_ref, cz_ref, lng_ref, lnb_ref, wg_ref, watt_ref, wco_ref, wout_ref,
                l1g_ref, l1b_ref, wr_ref, br_ref, before_ref, h1_ref, h1p_ref, rt_ref, fld_ref, cnt_ref, *, tile):
    @pl.when(pl.program_id(0) == 0)
    def _():
        cnt_ref[...] = jnp.zeros_like(cnt_ref)

    h0 = _layer_norm(x_ref[...], lng_ref[...], lnb_ref[...])
    hb = h0.astype(BF16)
    g_att = jax.nn.sigmoid(_dot(hb, wg_ref[:, :D_MODEL]))
    mix = g_att * _dot(att_ref[...], watt_ref[...])
    g_conv = jax.nn.sigmoid(_dot(hb, wg_ref[:, D_MODEL:]))
    mix = mix + g_conv * _dot(cz_ref[...], wco_ref[...])
    m = _dot(mix.astype(BF16), wout_ref[...])
    h1 = _layer_norm(ALPHA * h0 + m, l1g_ref[...], l1b_ref[...])
    h1_ref[...] = h1
    for j, w in enumerate(_pack_bf16_pairs(h1)):
        h1p_ref[:, j, :, :] = w.reshape(tile // 8, 8, LANES)

    r = _dot(h1.astype(BF16), wr_ref[...]) + br_ref[...]
    i1, i2, w1, w2 = _route(r)
    lane = lax.broadcasted_iota(jnp.int32, (tile, ROUTER_LANES), 1)
    lane_f = lane.astype(F32)
    hit1, hit2 = lane_f == i1, lane_f == i2
    onehot = jnp.where(jnp.logical_or(hit1, hit2), 1.0, 0.0)
    seen = _dot(before_ref[...], onehot.astype(BF16)) + cnt_ref[0:1, :]
    rank1 = jnp.sum(jnp.where(hit1, seen, 0.0), axis=1, keepdims=True)
    rank2 = jnp.sum(jnp.where(hit2, seen, 0.0), axis=1, keepdims=True)
    cnt_ref[...] = cnt_ref[...] + jnp.sum(onehot, axis=0, keepdims=True)
    fields = (i1, i2, w1, w2, rank1, rank2)
    rt = jnp.zeros((tile, ROUTER_LANES), F32)
    for k, v in enumerate(fields):
        rt = jnp.where(lane == k, v, rt)
    rt_ref[...] = rt
    fld_ref[...] = jnp.transpose(rt)[0:ROUTE_FIELDS, :]


def _out(x, att, cz, ln_g, ln_b, wg, watt, wco, wout, l1g, l1b, wr, br, tile, first_tile):
    T, D = x.shape
    own = lambda w: pl.BlockSpec((tile, w), lambda i: (i, 0))
    flat = lambda w: pl.BlockSpec((tile, w), lambda i: (first_tile + i, 0))
    full = lambda a: pl.BlockSpec(a.shape, lambda i: (0,) * a.ndim)
    before = (jnp.arange(tile)[None, :] < jnp.arange(tile)[:, None]).astype(BF16)
    return pl.pallas_call(
        functools.partial(_out_kernel, tile=tile),
        grid=(T // tile,),
        in_specs=[own(D), flat(ATT_WIDTH), flat(CONV_WIDTH), full(ln_g), full(ln_b), full(wg),
                  full(watt), full(wco), full(wout), full(l1g), full(l1b), full(wr), full(br), full(before)],
        out_specs=[own(D),
                   pl.BlockSpec((tile // 8, PACK_WORDS, 8, LANES), lambda i: (i, 0, 0, 0)),
                   own(ROUTER_LANES),
                   pl.BlockSpec((ROUTE_FIELDS, tile), lambda i: (0, i)),
                   pl.BlockSpec((8, ROUTER_LANES), lambda i: (0, 0))],
        out_shape=[jax.ShapeDtypeStruct((T, D), F32),
                   jax.ShapeDtypeStruct((T // 8, PACK_WORDS, 8, LANES), jnp.uint32),
                   jax.ShapeDtypeStruct((T, ROUTER_LANES), F32),
                   jax.ShapeDtypeStruct((ROUTE_FIELDS, T), F32),
                   jax.ShapeDtypeStruct((8, ROUTER_LANES), F32)],
        compiler_params=pltpu.CompilerParams(dimension_semantics=("arbitrary",), vmem_limit_bytes=VMEM_LIMIT),
        name="out",
    )(x, att, cz, ln_g, ln_b, wg, watt, wco, wout, l1g, l1b, wr, br, before)


def _sc_scatter2(src, idx_a, idx_b, n_out):
    m = src.shape[0]
    mesh = plsc.VectorSubcoreMesh(core_axis_name="c", subcore_axis_name="s")

    @functools.partial(pl.kernel, out_type=jax.ShapeDtypeStruct((n_out, LANES), src.dtype), mesh=mesh)
    def k(x_hbm, ia_hbm, ib_hbm, o_hbm):
        def body(x_vmem, ia_vmem, ib_vmem):
            pltpu.sync_copy(x_vmem, o_hbm.at[ia_vmem.at[0]])
            pltpu.sync_copy(x_vmem, o_hbm.at[ib_vmem.at[0]])

        pltpu.emit_pipeline(
            body, grid=(m // SC_WINDOW,),
            in_specs=[pl.BlockSpec((SC_WINDOW, LANES), index_map=lambda i: (i, 0)),
                      pl.BlockSpec((1, SC_WINDOW), index_map=lambda i: (0, i)),
                      pl.BlockSpec((1, SC_WINDOW), index_map=lambda i: (0, i))],
            out_specs=[],
            core_axis_name=("c", "s"), dimension_semantics=(pltpu.PARALLEL,),
        )(x_hbm, ia_hbm, ib_hbm)

    return k(src, idx_a.reshape(1, m), idx_b.reshape(1, m))


def _sc_gather(table, idx):
    m = idx.shape[0]
    mesh = plsc.VectorSubcoreMesh(core_axis_name="c", subcore_axis_name="s")

    @functools.partial(pl.kernel, out_type=jax.ShapeDtypeStruct((m, LANES), table.dtype), mesh=mesh)
    def k(x_hbm, i_hbm, o_hbm):
        def body(i_vmem, o_vmem):
            pltpu.sync_copy(x_hbm.at[i_vmem.at[0]], o_vmem)

        pltpu.emit_pipeline(
            body, grid=(m // SC_WINDOW,),
            in_specs=[pl.BlockSpec((1, SC_WINDOW), index_map=lambda i: (0, i))],
            out_specs=[pl.BlockSpec((SC_WINDOW, LANES), index_map=lambda i: (i, 0))],
            core_axis_name=("c", "s"), dimension_semantics=(pltpu.PARALLEL,),
        )(i_hbm, o_hbm)

    return k(table, idx.reshape(1, m))


def _expert_kernel(te_ref, tv_ref, xs_ref, wg_ref, wu_ref, wd_ref, ys_ref, wgb_ref, wub_ref, wdb_ref, *, tile):
    n = pl.program_id(0)
    valid = tv_ref[n]

    @pl.when(jnp.logical_or(n == 0, te_ref[n] != te_ref[jnp.maximum(n - 1, 0)]))
    def _():
        wgb_ref[...] = wg_ref[...].astype(BF16)
        wub_ref[...] = wu_ref[...].astype(BF16)
        wdb_ref[...] = wd_ref[...].astype(BF16)

    @pl.when(valid > 0)
    def _():
        x = _unpack_bf16_pairs([xs_ref[:, j, :, :].reshape(tile, LANES) for j in range(PACK_WORDS)])
        rows = lax.broadcasted_iota(jnp.int32, (tile, 1), 0)
        x = jnp.where(rows < valid, x, 0.0).astype(BF16)
        gt = _dot(x, wgb_ref[...])
        up = _dot(x, wub_ref[...])
        hid = (gt * jax.nn.sigmoid(gt)) * up
        y = _dot(hid.astype(BF16), wdb_ref[...])
        for j, w in enumerate(_pack_bf16_pairs(y)):
            ys_ref[:, j, :, :] = w.reshape(tile // 8, 8, LANES)

    @pl.when(valid <= 0)
    def _():
        ys_ref[...] = jnp.zeros_like(ys_ref)


def _experts(tile_expert, tile_valid, xs, w_gate, w_up, w_down, tile):
    n_tiles = xs.shape[0] * 8 // tile
    blk = pl.BlockSpec((tile // 8, PACK_WORDS, 8, LANES), lambda n, te, tv: (n, 0, 0, 0))
    return pl.pallas_call(
        functools.partial(_expert_kernel, tile=tile),
        grid_spec=pltpu.PrefetchScalarGridSpec(
            num_scalar_prefetch=2,
            grid=(n_tiles,),
            in_specs=[blk,
                      pl.BlockSpec((None, D_MODEL, D_EXPERT), lambda n, te, tv: (te[n], 0, 0)),
                      pl.BlockSpec((None, D_MODEL, D_EXPERT), lambda n, te, tv: (te[n], 0, 0)),
                      pl.BlockSpec((None, D_EXPERT, D_MODEL), lambda n, te, tv: (te[n], 0, 0))],
            out_specs=blk,
            scratch_shapes=[pltpu.VMEM((D_MODEL, D_EXPERT), BF16), pltpu.VMEM((D_MODEL, D_EXPERT), BF16),
                            pltpu.VMEM((D_EXPERT, D_MODEL), BF16)],
        ),
        out_shape=jax.ShapeDtypeStruct(xs.shape, jnp.uint32),
        compiler_params=pltpu.CompilerParams(
            dimension_semantics=("arbitrary",), vmem_limit_bytes=VMEM_LIMIT),
        name="experts",
    )(tile_expert, tile_valid, xs, w_gate, w_up, w_down)


def _final_kernel(h_ref, g_ref, rt_ref, l2g_ref, l2b_ref, o_ref, *, tile):
    rt = rt_ref[...]
    lane = lax.broadcasted_iota(jnp.int32, rt.shape, 1)
    w1 = jnp.sum(jnp.where(lane == 2, rt, 0.0), axis=1, keepdims=True)
    w2 = jnp.sum(jnp.where(lane == 3, rt, 0.0), axis=1, keepdims=True)
    y1 = _unpack_bf16_pairs([g_ref[0, :, j, :, :].reshape(tile, LANES) for j in range(PACK_WORDS)])
    y2 = _unpack_bf16_pairs([g_ref[1, :, j, :, :].reshape(tile, LANES) for j in range(PACK_WORDS)])
    f = w1 * y1 + w2 * y2
    o_ref[...] = _layer_norm(ALPHA * h_ref[...] + f, l2g_ref[...], l2b_ref[...])


def _final(h1, g, rt, l2g, l2b, tile):
    T, D = h1.shape
    row = lambda w: pl.BlockSpec((tile, w), lambda i: (i, 0))
    full = lambda a: pl.BlockSpec(a.shape, lambda i: (0,) * a.ndim)
    return pl.pallas_call(
        functools.partial(_final_kernel, tile=tile),
        grid=(T // tile,),
        in_specs=[row(D),
                  pl.BlockSpec((2, tile // 8, PACK_WORDS, 8, LANES), lambda i: (0, i, 0, 0, 0)),
                  row(ROUTER_LANES), full(l2g), full(l2b)],
        out_specs=row(D),
        out_shape=jax.ShapeDtypeStruct((T, D), F32),
        compiler_params=pltpu.CompilerParams(dimension_semantics=("parallel",), vmem_limit_bytes=VMEM_LIMIT),
        name="final",
    )(h1, g, rt, l2g, l2b)


def _dispatch_plan(fields, counts, n_tokens, tile):
    cnt = counts[0, :N_EXPERTS].astype(jnp.int32)
    padded = (cnt + tile - 1) // tile * tile
    base = jnp.cumsum(padded) - padded
    e_ids = jnp.arange(N_EXPERTS, dtype=jnp.int32)

    def dest(row_e, row_r):
        e = fields[row_e].astype(jnp.int32)
        seg = jnp.sum(jnp.where(e[None, :] == e_ids[:, None], base[:, None], 0), axis=0)
        pos = seg + fields[row_r].astype(jnp.int32)
        p = pos.reshape(n_tokens // 8, 1, 8)
        j = jnp.arange(PACK_WORDS, dtype=jnp.int32).reshape(1, PACK_WORDS, 1)
        return ((p // 8) * (8 * PACK_WORDS) + j * 8 + p % 8).reshape(-1)

    n_tiles = (2 * n_tokens) // tile + N_EXPERTS
    start = jnp.arange(n_tiles, dtype=jnp.int32) * tile
    seg_end = base + padded
    te = jnp.minimum(jnp.sum((start[:, None] >= seg_end[None, :]).astype(jnp.int32), axis=1), N_EXPERTS - 1)
    te_base = jnp.sum(jnp.where(te[:, None] == e_ids[None, :], base[None, :], 0), axis=1)
    te_cnt = jnp.sum(jnp.where(te[:, None] == e_ids[None, :], cnt[None, :], 0), axis=1)
    tv = jnp.clip(te_cnt - (start - te_base), 0, tile)
    return dest(0, 4), dest(1, 5), te, tv, n_tiles


def _t5_bucket(rel):
    half = N_BUCKETS // 2
    max_exact = half // 2
    ret = jnp.where(rel > 0, half, 0)
    n = jnp.abs(rel)
    nf = jnp.maximum(n, 1).astype(F32)
    large = max_exact + (jnp.log(nf / max_exact) / math.log(MAX_DISTANCE / max_exact)
                         * (half - max_exact)).astype(jnp.int32)
    large = jnp.minimum(large, half - 1)
    return ret + jnp.where(n < max_exact, n, large)


def _bucket_bias(rel_bias, bucket):
    rb = rel_bias.astype(F32)
    out = jnp.zeros((N_Q_HEADS,) + bucket.shape, F32)
    for b in range(N_BUCKETS):
        out = out + jnp.where(bucket[None] == b, rb[b][:, None, None], 0.0)
    return out


def _pair_rows(t):
    return jnp.concatenate([t[:N_PAIRS], t[N_PAIRS:]], axis=-1).reshape(N_PAIRS * BLOCK, -1)


def _bias_tables(rel_bias, sink):
    qi = jnp.arange(BLOCK)
    kj = jnp.arange(3 * BLOCK) - BLOCK
    rel = kj[None, :] - qi[:, None]
    band = _bucket_bias(rel_bias, _t5_bucket(rel))
    in_win = (jnp.abs(rel) <= WINDOW)[None]
    not_prev = (kj >= 0)[None, None, :]
    not_next = (kj < BLOCK)[None, None, :]
    variants = [jnp.where(in_win & not_prev, band, NEG),
                jnp.where(in_win, band, NEG),
                jnp.where(in_win & not_next, band, NEG)]
    bias = jnp.stack([_pair_rows(v) for v in variants])

    off = BLOCK - N_META
    mvars = []
    for blk in (1, 2):
        qpos = blk * BLOCK + qi - off
        meta_rel = jnp.arange(N_META)[None, :] - qpos[:, None]
        mvars.append(_pair_rows(_bucket_bias(rel_bias, _t5_bucket(meta_rel))))
    mbias = jnp.stack(mvars)
    s = sink.astype(F32)
    sink_tab = jnp.repeat(jnp.stack([s[:N_PAIRS], s[N_PAIRS:]], axis=-1), BLOCK, axis=0)
    return bias, mbias, sink_tab


def kernel(x_prompt, x_sample, meta, ln_in_g, ln_in_b, rel_bias, w_in, w_att_branch, sink, conv_w, conv_b,
           conv_ln_g, conv_ln_b, w_conv_out, w_out, ln1_g, ln1_b, w_group, b_group, w_router, b_router,
           w_gate, w_up, w_down, ln2_g, ln2_b):
    row = lambda v: v.reshape(1, -1).astype(F32)
    w = w_in[0]
    wq = (w[:, :Q_END].reshape(D_MODEL, 2, N_PAIRS, HEAD_DIM).transpose(0, 2, 1, 3)
          .reshape(D_MODEL, ATT_WIDTH).astype(BF16))
    watt = (w_att_branch[0].reshape(2, N_PAIRS, HEAD_DIM, D_MODEL).transpose(1, 0, 2, 3)
            .reshape(ATT_WIDTH, D_MODEL).astype(BF16))
    wkv = w[:, Q_END:V_END].astype(BF16)
    wglu = w[:, V_END:GLU_END].astype(BF16)
    wg = w[:, GLU_END:].astype(BF16)
    wco = w_conv_out[0].astype(BF16)
    wout = w_out[0].astype(BF16)
    wr = jnp.zeros((D_MODEL, ROUTER_LANES), F32)
    wr = wr.at[:, :N_EXPERTS].set(w_router[0]).at[:, N_EXPERTS:N_EXPERTS + N_GROUPS].set(w_group[0]).astype(BF16)
    br = jnp.zeros((1, ROUTER_LANES), F32)
    br = br.at[0, :N_EXPERTS].set(b_router[0]).at[0, N_EXPERTS:N_EXPERTS + N_GROUPS].set(b_group[0])
    ln_g, ln_b = row(ln_in_g), row(ln_in_b)
    bias, mbias, sink_tab = _bias_tables(rel_bias, sink[0])

    xm = jnp.concatenate([jnp.zeros((BLOCK - N_META, D_MODEL), F32), meta.astype(F32)], axis=0)
    _, kv_m, zc_m = _proj(xm, xm, _Geom(1, 1, 1, 1), ln_g, ln_b, wq, wkv, wglu, BLOCK)
    kv_meta = kv_m[BLOCK - N_META:BLOCK]
    z_meta = zc_m[BLOCK - N_META:BLOCK]

    (bp, sp, _), (bs, ss, _) = x_prompt.shape, x_sample.shape
    xp, xs = x_prompt.reshape(bp * sp, D_MODEL), x_sample.reshape(bs * ss, D_MODEL)
    geom = _geom(x_prompt, x_sample, TILE)
    q, kv, zc = _proj(xp, xs, geom, ln_g, ln_b, wq, wkv, wglu, TILE)
    att = _attn(q, kv, geom, kv_meta, bias, mbias, sink_tab, TILE)
    cz = _conv(zc, geom, z_meta, conv_w[0], row(conv_b[0]), row(conv_ln_g[0]), row(conv_ln_b[0]), TILE)
    l1g, l1b, l2g, l2b = row(ln1_g[0]), row(ln1_b[0]), row(ln2_g[0]), row(ln2_b[0])

    def moe(x, first_tile):
        n = x.shape[0]
        h1, h1p, rt, fields, counts = _out(x, att, cz, ln_g, ln_b, wg, watt, wco, wout, l1g, l1b, wr, br,
                                           TILE, first_tile)
        idx1, idx2, tile_expert, tile_valid, n_tiles = _dispatch_plan(fields, counts, n, TILE_EXPERT)
        n_rows = n_tiles * TILE_EXPERT
        xsorted = _sc_scatter2(h1p.reshape(n * PACK_WORDS, LANES), idx1, idx2, n_rows * PACK_WORDS)
        ys = _experts(tile_expert, tile_valid, xsorted.reshape(n_rows // 8, PACK_WORDS, 8, LANES),
                      w_gate[0], w_up[0], w_down[0], TILE_EXPERT)
        g = _sc_gather(ys.reshape(n_rows * PACK_WORDS, LANES), jnp.concatenate([idx1, idx2]))
        return _final(h1, g.reshape(2, n // 8, PACK_WORDS, 8, LANES), rt, l2g, l2b, TILE)

    return moe(xp, 0).reshape(x_prompt.shape), moe(xs, geom.n_p).reshape(x_sample.shape)
```

```python
import functools
import math
from typing import NamedTuple

import jax
import jax.numpy as jnp
from jax import lax
from jax.experimental import pallas as pl
from jax.experimental.pallas import tpu as pltpu
from jax.experimental.pallas import tpu_sc as plsc

D_MODEL = 1024
N_META = 16
BLOCK = 128
WINDOW = 128
N_Q_HEADS = 8
N_KV_HEADS = 2
HEAD_DIM = 64
ATT_WIDTH = N_Q_HEADS * HEAD_DIM
KV_WIDTH = N_KV_HEADS * HEAD_DIM
CONV_WIDTH = D_MODEL // 2
CONV_K = 31
N_BUCKETS = 32
MAX_DISTANCE = 128
N_GROUPS = 4
EXPERTS_PER_GROUP = 8
N_EXPERTS = N_GROUPS * EXPERTS_PER_GROUP
D_EXPERT = 256
LN_EPS = 1e-5
DEPTH = 1
ALPHA = (2 * DEPTH) ** 0.25
NEG = -1e30
Q_END = ATT_WIDTH
K_END = Q_END + KV_WIDTH
V_END = K_END + KV_WIDTH
GLU_END = V_END + 2 * CONV_WIDTH
GA_END = GLU_END + D_MODEL

N_PAIRS = N_Q_HEADS // 2
LANES = 128
CONV_HALO = 16
ROUTER_LANES = 128

TILE = 512
CONV_ROWS = 128
LN_ROWS = 64
SHIFT_ROWS = 128
TILE_EXPERT = 512
PACK_WORDS = 4
SC_WINDOW = 128
ROUTE_FIELDS = 8
VMEM_LIMIT = 56 * 1024 * 1024

BF16 = jnp.bfloat16
F32 = jnp.float32


def _layer_norm(x, g, b):
    mu = jnp.mean(x, axis=-1, keepdims=True)
    xc = x - mu
    var = jnp.mean(xc * xc, axis=-1, keepdims=True)
    return xc * lax.rsqrt(var + LN_EPS) * g + b


def _dot(a, b):
    return jnp.dot(a, b, preferred_element_type=F32)


def _dot_nt(a, b):
    return lax.dot_general(a, b, (((1,), (1,)), ((), ())), preferred_element_type=F32)


class _Geom(NamedTuple):
    n_p: int
    n_s: int
    tp: int
    ts: int


def _geom(x_prompt, x_sample, tile):
    (bp, sp, _), (bs, ss, _) = x_prompt.shape, x_sample.shape
    return _Geom(bp * sp // tile, bs * ss // tile, sp // tile, ss // tile)


def _seq_pos(t, g):
    is_p = t < g.n_p
    local = jnp.where(is_p, lax.rem(t, g.tp), lax.rem(jnp.maximum(t - g.n_p, 0), g.ts))
    return is_p, local == 0, local == jnp.where(is_p, g.tp - 1, g.ts - 1)


def _x_specs(g, tile, width):
    return [pl.BlockSpec((tile, width), lambda t: (jnp.minimum(t, g.n_p - 1), 0)),
            pl.BlockSpec((tile, width), lambda t: (jnp.maximum(t - g.n_p, 0), 0))]


def _proj_kernel(xp_ref, xs_ref, g_ref, b_ref, wq_ref, wkv_ref, wglu_ref, q_ref, kv_ref, zc_ref, *, geom):
    x = jnp.where(pl.program_id(0) < geom.n_p, xp_ref[...], xs_ref[...])
    h = _layer_norm(x, g_ref[...], b_ref[...]).astype(BF16)
    q_ref[...] = _dot(h, wq_ref[...]).astype(BF16)
    kv_ref[...] = _dot(h, wkv_ref[...]).astype(BF16)
    u = _dot(h, wglu_ref[...])
    zc_ref[...] = (u[:, :CONV_WIDTH] * jax.nn.sigmoid(u[:, CONV_WIDTH:])).astype(BF16)


def _proj(xp, xs, geom, ln_g, ln_b, wq, wkv, wglu, tile):
    T = (geom.n_p + geom.n_s) * tile
    row = lambda w: pl.BlockSpec((tile, w), lambda t: (t, 0))
    full = lambda a: pl.BlockSpec(a.shape, lambda t: (0,) * a.ndim)
    return pl.pallas_call(
        functools.partial(_proj_kernel, geom=geom),
        grid=(geom.n_p + geom.n_s,),
        in_specs=_x_specs(geom, tile, D_MODEL) + [full(ln_g), full(ln_b), full(wq), full(wkv), full(wglu)],
        out_specs=[row(ATT_WIDTH), row(2 * KV_WIDTH), row(CONV_WIDTH)],
        out_shape=[
            jax.ShapeDtypeStruct((T, ATT_WIDTH), BF16),
            jax.ShapeDtypeStruct((T, 2 * KV_WIDTH), BF16),
            jax.ShapeDtypeStruct((T, CONV_WIDTH), BF16),
        ],
        compiler_params=pltpu.CompilerParams(dimension_semantics=("parallel",), vmem_limit_bytes=VMEM_LIMIT),
        name="proj",
    )(xp, xs, ln_g, ln_b, wq, wkv, wglu)


def _attn_kernel(q_ref, kvp_ref, kvc_ref, kvn_ref, kvm_ref, bias_ref, mbias_ref, sink_ref, o_ref, *, tile, geom):
    _, seq_first, seq_last = _seq_pos(pl.program_id(0), geom)
    blocks = tile // BLOCK
    scale = HEAD_DIM ** -0.5

    lane = lax.broadcasted_iota(jnp.int32, (1, LANES), 1)
    lo = lane < HEAD_DIM

    def split_heads(t):
        z = jnp.zeros_like(t)
        return jnp.where(lo, t, z), jnp.where(lo, z, t)

    kv_ext = jnp.concatenate([kvp_ref[...], kvc_ref[...], kvn_ref[...]], axis=0)
    k_ext = kv_ext[:, :KV_WIDTH] * jnp.asarray(scale, BF16)
    v_ext = kv_ext[:, KV_WIDTH:]
    ka, kb = split_heads(k_ext)
    va, vb = split_heads(v_ext)
    kma, kmb = split_heads(kvm_ref[:, :KV_WIDTH] * jnp.asarray(scale, BF16))
    vma, vmb = split_heads(kvm_ref[:, KV_WIDTH:])
    km_cat = jnp.concatenate([kma, kmb], axis=0)
    vm_cat = jnp.concatenate([vma, vmb], axis=0)

    mlane = lax.broadcasted_iota(jnp.int32, (1, 2 * N_META), 1)
    m_first = mlane < N_META
    sink = sink_ref[...]
    lane_o = lax.broadcasted_iota(jnp.int32, (1, LANES), 1) < HEAD_DIM

    for j in range(blocks):
        first = jnp.logical_and(seq_first, j == 0)
        last = jnp.logical_and(seq_last, j == blocks - 1)
        variant = jnp.where(first, 0, jnp.where(last, 2, 1))
        mvariant = jnp.where(first, 0, 1)

        r0 = j * BLOCK
        qb = q_ref[r0:r0 + BLOCK, :]
        q4 = jnp.concatenate([qb[:, p * LANES:(p + 1) * LANES] for p in range(N_PAIRS)], axis=0)
        k_cat = jnp.concatenate([ka[r0:r0 + 3 * BLOCK], kb[r0:r0 + 3 * BLOCK]], axis=0)
        v_cat = jnp.concatenate([va[r0:r0 + 3 * BLOCK], vb[r0:r0 + 3 * BLOCK]], axis=0)

        s = _dot_nt(q4, k_cat) + bias_ref[variant]
        sm = _dot_nt(q4, km_cat) + mbias_ref[mvariant]

        s_a, s_b = s[:, :3 * BLOCK], s[:, 3 * BLOCK:]
        sm_a = jnp.where(m_first, sm, NEG)
        sm_b = jnp.where(m_first, NEG, sm)
        m_a = jnp.maximum(jnp.maximum(jnp.max(s_a, axis=1, keepdims=True),
                                      jnp.max(sm_a, axis=1, keepdims=True)), sink[:, 0:1])
        m_b = jnp.maximum(jnp.maximum(jnp.max(s_b, axis=1, keepdims=True),
                                      jnp.max(sm_b, axis=1, keepdims=True)), sink[:, 1:2])
        p_a = jnp.exp(s_a - m_a)
        p_b = jnp.exp(s_b - m_b)
        pm = jnp.exp(jnp.where(m_first, sm - m_a, sm - m_b))
        l_a = (jnp.sum(p_a, axis=1, keepdims=True) + jnp.sum(jnp.where(m_first, pm, 0.0), axis=1, keepdims=True)
               + jnp.exp(sink[:, 0:1] - m_a))
        l_b = (jnp.sum(p_b, axis=1, keepdims=True) + jnp.sum(jnp.where(m_first, 0.0, pm), axis=1, keepdims=True)
               + jnp.exp(sink[:, 1:2] - m_b))
        p = jnp.concatenate([p_a, p_b], axis=1).astype(BF16)
        o = _dot(p, v_cat) + _dot(pm.astype(BF16), vm_cat)
        o = o * jnp.where(lane_o, 1.0 / l_a, 1.0 / l_b)
        for pr in range(N_PAIRS):
            o_ref[r0:r0 + BLOCK, pr * LANES:(pr + 1) * LANES] = o[pr * BLOCK:(pr + 1) * BLOCK].astype(BF16)


def _attn(q, kv, geom, kv_meta, bias, mbias, sink_tab, tile):
    T = q.shape[0]
    bpt = tile // BLOCK
    n_blocks = T // BLOCK
    full = lambda a: pl.BlockSpec(a.shape, lambda t: (0,) * a.ndim)
    return pl.pallas_call(
        functools.partial(_attn_kernel, tile=tile, geom=geom),
        grid=(T // tile,),
        in_specs=[
            pl.BlockSpec((tile, ATT_WIDTH), lambda t: (t, 0)),
            pl.BlockSpec((BLOCK, 2 * KV_WIDTH), lambda t: (jnp.maximum(t * bpt - 1, 0), 0)),
            pl.BlockSpec((tile, 2 * KV_WIDTH), lambda t: (t, 0)),
            pl.BlockSpec((BLOCK, 2 * KV_WIDTH), lambda t: (jnp.minimum((t + 1) * bpt, n_blocks - 1), 0)),
            full(kv_meta), full(bias), full(mbias), full(sink_tab),
        ],
        out_specs=pl.BlockSpec((tile, ATT_WIDTH), lambda t: (t, 0)),
        out_shape=jax.ShapeDtypeStruct((T, ATT_WIDTH), BF16),
        compiler_params=pltpu.CompilerParams(dimension_semantics=("parallel",), vmem_limit_bytes=VMEM_LIMIT),
        name="attn",
    )(q, kv, kv, kv, kv_meta, bias, mbias, sink_tab)


def _conv_kernel(zp_ref, zc_ref, zn_ref, zm_ref, w_ref, cb_ref, g_ref, b_ref, o_ref, ext_ref, sh_ref, y_ref,
                 *, tile, geom):
    i = pl.program_id(0)
    _, seq_first, seq_last = _seq_pos(i, geom)
    ext_ref[0:CONV_HALO, :] = jnp.where(seq_first, zm_ref[...], zp_ref[...]).astype(F32)
    ext_ref[CONV_HALO:CONV_HALO + tile, :] = zc_ref[...].astype(F32)
    ext_ref[CONV_HALO + tile:, :] = jnp.where(seq_last, 0.0, zn_ref[...].astype(F32))
    off = CONV_HALO - CONV_K // 2
    reach = (off + CONV_K - 1) // 8 * 8
    for p in range(1, 8):
        for r0 in range(0, tile + reach, SHIFT_ROWS):
            n = min(SHIFT_ROWS, tile + reach - r0)
            sh_ref[p - 1, r0:r0 + n, :] = ext_ref[r0 + p:r0 + p + n, :]

    def taps(r0, cs):
        acc = jnp.zeros((CONV_ROWS, LANES), F32)
        for k in range(CONV_K):
            p, a = (off + k) % 8, (off + k) // 8 * 8
            rows = slice(r0 + a, r0 + a + CONV_ROWS)
            win = ext_ref[rows, cs] if p == 0 else sh_ref[p - 1, rows, cs]
            acc = acc + win * w_ref[k:k + 1, cs]
        y_ref[r0:r0 + CONV_ROWS, cs] = acc

    for c in range(CONV_WIDTH // LANES):
        for r in range(tile // CONV_ROWS):
            pl.when(i >= 0)(functools.partial(taps, r * CONV_ROWS, slice(c * LANES, (c + 1) * LANES)))
    cb, g, b = cb_ref[...], g_ref[...], b_ref[...]
    for r in range(tile // LN_ROWS):
        r0 = r * LN_ROWS
        y = _layer_norm(y_ref[r0:r0 + LN_ROWS, :] + cb, g, b)
        o_ref[r0:r0 + LN_ROWS, :] = (y * jax.nn.sigmoid(y)).astype(BF16)


def _conv(zc, geom, z_meta, conv_w, conv_b, ln_g, ln_b, tile):
    T, C = zc.shape
    hpt = tile // CONV_HALO
    n_halo = T // CONV_HALO
    full = lambda a: pl.BlockSpec(a.shape, lambda t: (0,) * a.ndim)
    return pl.pallas_call(
        functools.partial(_conv_kernel, tile=tile, geom=geom),
        grid=(T // tile,),
        in_specs=[
            pl.BlockSpec((CONV_HALO, C), lambda t: (jnp.maximum(t * hpt - 1, 0), 0)),
            pl.BlockSpec((tile, C), lambda t: (t, 0)),
            pl.BlockSpec((CONV_HALO, C), lambda t: (jnp.minimum((t + 1) * hpt, n_halo - 1), 0)),
            full(z_meta), full(conv_w), full(conv_b), full(ln_g), full(ln_b),
        ],
        out_specs=pl.BlockSpec((tile, C), lambda t: (t, 0)),
        out_shape=jax.ShapeDtypeStruct((T, C), BF16),
        scratch_shapes=[pltpu.VMEM((tile + 2 * CONV_HALO, C), F32),
                        pltpu.VMEM((7, tile + 2 * CONV_HALO - 8, C), F32),
                        pltpu.VMEM((tile, C), F32)],
        compiler_params=pltpu.CompilerParams(dimension_semantics=("parallel",), vmem_limit_bytes=VMEM_LIMIT),
        name="conv",
    )(zc, zc, zc, z_meta, conv_w, conv_b, ln_g, ln_b)


def _route(r):
    lane_i = lax.broadcasted_iota(jnp.int32, r.shape, 1)
    lane = lane_i.astype(F32)
    big = float(1 << 20)
    is_g = jnp.logical_and(lane_i >= N_EXPERTS, lane_i < N_EXPERTS + N_GROUPS)
    lg = jnp.where(is_g, r, -jnp.inf)
    mg = jnp.max(lg, axis=1, keepdims=True)
    g_w = 1.0 / jnp.sum(jnp.exp(lg - mg), axis=1, keepdims=True)
    g_idx = jnp.min(jnp.where(lg == mg, lane - N_EXPERTS, big), axis=1, keepdims=True)
    lane_group = jnp.right_shift(lane_i, 3).astype(F32)
    in_group = jnp.logical_and(lane_i < N_EXPERTS, lane_group == g_idx)
    le = jnp.where(in_group, r, -jnp.inf)
    m1 = jnp.max(le, axis=1, keepdims=True)
    den = jnp.sum(jnp.exp(le - m1), axis=1, keepdims=True)
    i1 = jnp.min(jnp.where(le == m1, lane, big), axis=1, keepdims=True)
    le2 = jnp.where(lane == i1, -jnp.inf, le)
    m2 = jnp.max(le2, axis=1, keepdims=True)
    i2 = jnp.min(jnp.where(le2 == m2, lane, big), axis=1, keepdims=True)
    p1 = 1.0 / den
    p2 = jnp.exp(m2 - m1) / den
    tot = p1 + p2
    return i1, i2, g_w * (p1 / tot), g_w * (p2 / tot)


def _pack_bf16_pairs(x):
    half = x.shape[1] // 2
    words = []
    for j in range(half // LANES):
        lo = pltpu.bitcast(x[:, j * LANES:(j + 1) * LANES].astype(BF16).astype(F32), jnp.uint32)
        hi = pltpu.bitcast(x[:, half + j * LANES:half + (j + 1) * LANES].astype(BF16).astype(F32), jnp.uint32)
        words.append(hi | (lo >> 16))
    return words


def _unpack_bf16_pairs(words):
    lo = [pltpu.bitcast(w << 16, F32) for w in words]
    hi = [pltpu.bitcast(w & jnp.uint32(0xFFFF0000), F32) for w in words]
    return jnp.concatenate(lo + hi, axis=1)


def _out_kernel(x_ref, att_ref, cz_ref, lng_ref, lnb_ref, wg_ref, watt_ref, wco_ref, wout_ref,
                l1g_ref, l1b_ref, wr_ref, br_ref, before_ref, h1p_ref, rt_ref, fld_ref, cnt_ref, *, tile):
    @pl.when(pl.program_id(0) == 0)
    def _():
        cnt_ref[...] = jnp.zeros_like(cnt_ref)

    h0 = _layer_norm(x_ref[...], lng_ref[...], lnb_ref[...])
    hb = h0.astype(BF16)
    g_att = jax.nn.sigmoid(_dot(hb, wg_ref[:, :D_MODEL]))
    mix = g_att * _dot(att_ref[...], watt_ref[...])
    g_conv = jax.nn.sigmoid(_dot(hb, wg_ref[:, D_MODEL:]))
    mix = mix + g_conv * _dot(cz_ref[...], wco_ref[...])
    m = _dot(mix.astype(BF16), wout_ref[...])
    h1 = _layer_norm(ALPHA * h0 + m, l1g_ref[...], l1b_ref[...])
    for j, w in enumerate(_pack_bf16_pairs(h1)):
        h1p_ref[:, j, :, :] = w.reshape(tile // 8, 8, LANES)

    r = _dot(h1.astype(BF16), wr_ref[...]) + br_ref[...]
    i1, i2, w1, w2 = _route(r)
    lane = lax.broadcasted_iota(jnp.int32, (tile, ROUTER_LANES), 1)
    lane_f = lane.astype(F32)
    hit1, hit2 = lane_f == i1, lane_f == i2
    onehot = jnp.where(jnp.logical_or(hit1, hit2), 1.0, 0.0)
    seen = _dot(before_ref[...], onehot.astype(BF16)) + cnt_ref[0:1, :]
    rank1 = jnp.sum(jnp.where(hit1, seen, 0.0), axis=1, keepdims=True)
    rank2 = jnp.sum(jnp.where(hit2, seen, 0.0), axis=1, keepdims=True)
    cnt_ref[...] = cnt_ref[...] + jnp.sum(onehot, axis=0, keepdims=True)
    fields = (i1, i2, w1, w2, rank1, rank2)
    rt = jnp.zeros((tile, ROUTER_LANES), F32)
    for k, v in enumerate(fields):
        rt = jnp.where(lane == k, v, rt)
    rt_ref[...] = rt
    fld_ref[...] = jnp.transpose(rt)[0:ROUTE_FIELDS, :]


def _out(x, att, cz, ln_g, ln_b, wg, watt, wco, wout, l1g, l1b, wr, br, tile, first_tile):
    T, D = x.shape
    own = lambda w: pl.BlockSpec((tile, w), lambda i: (i, 0))
    flat = lambda w: pl.BlockSpec((tile, w), lambda i: (first_tile + i, 0))
    full = lambda a: pl.BlockSpec(a.shape, lambda i: (0,) * a.ndim)
    before = (jnp.arange(tile)[None, :] < jnp.arange(tile)[:, None]).astype(BF16)
    return pl.pallas_call(
        functools.partial(_out_kernel, tile=tile),
        grid=(T // tile,),
        in_specs=[own(D), flat(ATT_WIDTH), flat(CONV_WIDTH), full(ln_g), full(ln_b), full(wg),
                  full(watt), full(wco), full(wout), full(l1g), full(l1b), full(wr), full(br), full(before)],
        out_specs=[pl.BlockSpec((tile // 8, PACK_WORDS, 8, LANES), lambda i: (i, 0, 0, 0)),
                   own(ROUTER_LANES),
                   pl.BlockSpec((ROUTE_FIELDS, tile), lambda i: (0, i)),
                   pl.BlockSpec((8, ROUTER_LANES), lambda i: (0, 0))],
        out_shape=[jax.ShapeDtypeStruct((T // 8, PACK_WORDS, 8, LANES), jnp.uint32),
                   jax.ShapeDtypeStruct((T, ROUTER_LANES), F32),
                   jax.ShapeDtypeStruct((ROUTE_FIELDS, T), F32),
                   jax.ShapeDtypeStruct((8, ROUTER_LANES), F32)],
        compiler_params=pltpu.CompilerParams(dimension_semantics=("arbitrary",), vmem_limit_bytes=VMEM_LIMIT),
        name="out",
    )(x, att, cz, ln_g, ln_b, wg, watt, wco, wout, l1g, l1b, wr, br, before)


def _sc_scatter2(src, idx_a, idx_b, n_out):
    m = src.shape[0]
    mesh = plsc.VectorSubcoreMesh(core_axis_name="c", subcore_axis_name="s")

    @functools.partial(pl.kernel, out_type=jax.ShapeDtypeStruct((n_out, LANES), src.dtype), mesh=mesh)
    def k(x_hbm, ia_hbm, ib_hbm, o_hbm):
        def body(x_vmem, ia_vmem, ib_vmem):
            pltpu.sync_copy(x_vmem, o_hbm.at[ia_vmem.at[0]])
            pltpu.sync_copy(x_vmem, o_hbm.at[ib_vmem.at[0]])

        pltpu.emit_pipeline(
            body, grid=(m // SC_WINDOW,),
            in_specs=[pl.BlockSpec((SC_WINDOW, LANES), index_map=lambda i: (i, 0)),
                      pl.BlockSpec((1, SC_WINDOW), index_map=lambda i: (0, i)),
                      pl.BlockSpec((1, SC_WINDOW), index_map=lambda i: (0, i))],
            out_specs=[],
            core_axis_name=("c", "s"), dimension_semantics=(pltpu.PARALLEL,),
        )(x_hbm, ia_hbm, ib_hbm)

    return k(src, idx_a.reshape(1, m), idx_b.reshape(1, m))


def _sc_gather(table, idx):
    m = idx.shape[0]
    mesh = plsc.VectorSubcoreMesh(core_axis_name="c", subcore_axis_name="s")

    @functools.partial(pl.kernel, out_type=jax.ShapeDtypeStruct((m, LANES), table.dtype), mesh=mesh)
    def k(x_hbm, i_hbm, o_hbm):
        def body(i_vmem, o_vmem):
            pltpu.sync_copy(x_hbm.at[i_vmem.at[0]], o_vmem)

        pltpu.emit_pipeline(
            body, grid=(m // SC_WINDOW,),
            in_specs=[pl.BlockSpec((1, SC_WINDOW), index_map=lambda i: (0, i))],
            out_specs=[pl.BlockSpec((SC_WINDOW, LANES), index_map=lambda i: (i, 0))],
            core_axis_name=("c", "s"), dimension_semantics=(pltpu.PARALLEL,),
        )(i_hbm, o_hbm)

    return k(table, idx.reshape(1, m))


def _expert_kernel(te_ref, tv_ref, xs_ref, wg_ref, wu_ref, wd_ref, ys_ref, wgb_ref, wub_ref, wdb_ref, *, tile):
    n = pl.program_id(0)
    valid = tv_ref[n]

    @pl.when(jnp.logical_or(n == 0, te_ref[n] != te_ref[jnp.maximum(n - 1, 0)]))
    def _():
        wgb_ref[...] = wg_ref[...].astype(BF16)
        wub_ref[...] = wu_ref[...].astype(BF16)
        wdb_ref[...] = wd_ref[...].astype(BF16)

    @pl.when(valid > 0)
    def _():
        x = _unpack_bf16_pairs([xs_ref[:, j, :, :].reshape(tile, LANES) for j in range(PACK_WORDS)])
        rows = lax.broadcasted_iota(jnp.int32, (tile, 1), 0)
        x = jnp.where(rows < valid, x, 0.0).astype(BF16)
        gt = _dot(x, wgb_ref[...])
        up = _dot(x, wub_ref[...])
        hid = (gt * jax.nn.sigmoid(gt)) * up
        y = _dot(hid.astype(BF16), wdb_ref[...])
        for j, w in enumerate(_pack_bf16_pairs(y)):
            ys_ref[:, j, :, :] = w.reshape(tile // 8, 8, LANES)

    @pl.when(valid <= 0)
    def _():
        ys_ref[...] = jnp.zeros_like(ys_ref)


def _experts(tile_expert, tile_valid, xs, w_gate, w_up, w_down, tile):
    n_tiles = xs.shape[0] * 8 // tile
    blk = pl.BlockSpec((tile // 8, PACK_WORDS, 8, LANES), lambda n, te, tv: (n, 0, 0, 0))
    return pl.pallas_call(
        functools.partial(_expert_kernel, tile=tile),
        grid_spec=pltpu.PrefetchScalarGridSpec(
            num_scalar_prefetch=2,
            grid=(n_tiles,),
            in_specs=[blk,
                      pl.BlockSpec((None, D_MODEL, D_EXPERT), lambda n, te, tv: (te[n], 0, 0)),
                      pl.BlockSpec((None, D_MODEL, D_EXPERT), lambda n, te, tv: (te[n], 0, 0)),
                      pl.BlockSpec((None, D_EXPERT, D_MODEL), lambda n, te, tv: (te[n], 0, 0))],
            out_specs=blk,
            scratch_shapes=[pltpu.VMEM((D_MODEL, D_EXPERT), BF16), pltpu.VMEM((D_MODEL, D_EXPERT), BF16),
                            pltpu.VMEM((D_EXPERT, D_MODEL), BF16)],
        ),
        out_shape=jax.ShapeDtypeStruct(xs.shape, jnp.uint32),
        compiler_params=pltpu.CompilerParams(
            dimension_semantics=("arbitrary",), vmem_limit_bytes=VMEM_LIMIT),
        name="experts",
    )(tile_expert, tile_valid, xs, w_gate, w_up, w_down)


def _final_kernel(hp_ref, g_ref, rt_ref, l2g_ref, l2b_ref, o_ref, *, tile):
    rt = rt_ref[...]
    lane = lax.broadcasted_iota(jnp.int32, rt.shape, 1)
    w1 = jnp.sum(jnp.where(lane == 2, rt, 0.0), axis=1, keepdims=True)
    w2 = jnp.sum(jnp.where(lane == 3, rt, 0.0), axis=1, keepdims=True)
    y1 = _unpack_bf16_pairs([g_ref[0, :, j, :, :].reshape(tile, LANES) for j in range(PACK_WORDS)])
    y2 = _unpack_bf16_pairs([g_ref[1, :, j, :, :].reshape(tile, LANES) for j in range(PACK_WORDS)])
    f = w1 * y1 + w2 * y2
    h1 = _unpack_bf16_pairs([hp_ref[:, j, :, :].reshape(tile, LANES) for j in range(PACK_WORDS)])
    o_ref[...] = _layer_norm(ALPHA * h1 + f, l2g_ref[...], l2b_ref[...])


def _final(h1p, g, rt, l2g, l2b, tile):
    T, D = h1p.shape[0] * 8, D_MODEL
    row = lambda w: pl.BlockSpec((tile, w), lambda i: (i, 0))
    full = lambda a: pl.BlockSpec(a.shape, lambda i: (0,) * a.ndim)
    return pl.pallas_call(
        functools.partial(_final_kernel, tile=tile),
        grid=(T // tile,),
        in_specs=[pl.BlockSpec((tile // 8, PACK_WORDS, 8, LANES), lambda i: (i, 0, 0, 0)),
                  pl.BlockSpec((2, tile // 8, PACK_WORDS, 8, LANES), lambda i: (0, i, 0, 0, 0)),
                  row(ROUTER_LANES), full(l2g), full(l2b)],
        out_specs=row(D),
        out_shape=jax.ShapeDtypeStruct((T, D), F32),
        compiler_params=pltpu.CompilerParams(dimension_semantics=("parallel",), vmem_limit_bytes=VMEM_LIMIT),
        name="final",
    )(h1p, g, rt, l2g, l2b)


def _dispatch_plan(fields, counts, n_tokens, tile):
    cnt = counts[0, :N_EXPERTS].astype(jnp.int32)
    padded = (cnt + tile - 1) // tile * tile
    base = jnp.cumsum(padded) - padded
    e_ids = jnp.arange(N_EXPERTS, dtype=jnp.int32)

    def dest(row_e, row_r):
        e = fields[row_e].astype(jnp.int32)
        seg = jnp.sum(jnp.where(e[None, :] == e_ids[:, None], base[:, None], 0), axis=0)
        pos = seg + fields[row_r].astype(jnp.int32)
        p = pos.reshape(n_tokens // 8, 1, 8)
        j = jnp.arange(PACK_WORDS, dtype=jnp.int32).reshape(1, PACK_WORDS, 1)
        return ((p // 8) * (8 * PACK_WORDS) + j * 8 + p % 8).reshape(-1)

    n_tiles = (2 * n_tokens) // tile + N_EXPERTS
    start = jnp.arange(n_tiles, dtype=jnp.int32) * tile
    seg_end = base + padded
    te = jnp.minimum(jnp.sum((start[:, None] >= seg_end[None, :]).astype(jnp.int32), axis=1), N_EXPERTS - 1)
    te_base = jnp.sum(jnp.where(te[:, None] == e_ids[None, :], base[None, :], 0), axis=1)
    te_cnt = jnp.sum(jnp.where(te[:, None] == e_ids[None, :], cnt[None, :], 0), axis=1)
    tv = jnp.clip(te_cnt - (start - te_base), 0, tile)
    return dest(0, 4), dest(1, 5), te, tv, n_tiles


def _t5_bucket(rel):
    half = N_BUCKETS // 2
    max_exact = half // 2
    ret = jnp.where(rel > 0, half, 0)
    n = jnp.abs(rel)
    nf = jnp.maximum(n, 1).astype(F32)
    large = max_exact + (jnp.log(nf / max_exact) / math.log(MAX_DISTANCE / max_exact)
                         * (half - max_exact)).astype(jnp.int32)
    large = jnp.minimum(large, half - 1)
    return ret + jnp.where(n < max_exact, n, large)


def _bucket_bias(rel_bias, bucket):
    rb = rel_bias.astype(F32)
    out = jnp.zeros((N_Q_HEADS,) + bucket.shape, F32)
    for b in range(N_BUCKETS):
        out = out + jnp.where(bucket[None] == b, rb[b][:, None, None], 0.0)
    return out


def _pair_rows(t):
    return jnp.concatenate([t[:N_PAIRS], t[N_PAIRS:]], axis=-1).reshape(N_PAIRS * BLOCK, -1)


def _bias_tables(rel_bias, sink):
    qi = jnp.arange(BLOCK)
    kj = jnp.arange(3 * BLOCK) - BLOCK
    rel = kj[None, :] - qi[:, None]
    band = _bucket_bias(rel_bias, _t5_bucket(rel))
    in_win = (jnp.abs(rel) <= WINDOW)[None]
    not_prev = (kj >= 0)[None, None, :]
    not_next = (kj < BLOCK)[None, None, :]
    variants = [jnp.where(in_win & not_prev, band, NEG),
                jnp.where(in_win, band, NEG),
                jnp.where(in_win & not_next, band, NEG)]
    bias = jnp.stack([_pair_rows(v) for v in variants])

    off = BLOCK - N_META
    mvars = []
    for blk in (1, 2):
        qpos = blk * BLOCK + qi - off
        meta_rel = jnp.arange(N_META)[None, :] - qpos[:, None]
        mvars.append(_pair_rows(_bucket_bias(rel_bias, _t5_bucket(meta_rel))))
    mbias = jnp.stack(mvars)
    s = sink.astype(F32)
    sink_tab = jnp.repeat(jnp.stack([s[:N_PAIRS], s[N_PAIRS:]], axis=-1), BLOCK, axis=0)
    return bias, mbias, sink_tab


def kernel(x_prompt, x_sample, meta, ln_in_g, ln_in_b, rel_bias, w_in, w_att_branch, sink, conv_w, conv_b,
           conv_ln_g, conv_ln_b, w_conv_out, w_out, ln1_g, ln1_b, w_group, b_group, w_router, b_router,
           w_gate, w_up, w_down, ln2_g, ln2_b):
    row = lambda v: v.reshape(1, -1).astype(F32)
    w = w_in[0]
    wq = (w[:, :Q_END].reshape(D_MODEL, 2, N_PAIRS, HEAD_DIM).transpose(0, 2, 1, 3)
          .reshape(D_MODEL, ATT_WIDTH).astype(BF16))
    watt = (w_att_branch[0].reshape(2, N_PAIRS, HEAD_DIM, D_MODEL).transpose(1, 0, 2, 3)
            .reshape(ATT_WIDTH, D_MODEL).astype(BF16))
    wkv = w[:, Q_END:V_END].astype(BF16)
    wglu = w[:, V_END:GLU_END].astype(BF16)
    wg = w[:, GLU_END:].astype(BF16)
    wco = w_conv_out[0].astype(BF16)
    wout = w_out[0].astype(BF16)
    wr = jnp.zeros((D_MODEL, ROUTER_LANES), F32)
    wr = wr.at[:, :N_EXPERTS].set(w_router[0]).at[:, N_EXPERTS:N_EXPERTS + N_GROUPS].set(w_group[0]).astype(BF16)
    br = jnp.zeros((1, ROUTER_LANES), F32)
    br = br.at[0, :N_EXPERTS].set(b_router[0]).at[0, N_EXPERTS:N_EXPERTS + N_GROUPS].set(b_group[0])
    ln_g, ln_b = row(ln_in_g), row(ln_in_b)
    bias, mbias, sink_tab = _bias_tables(rel_bias, sink[0])

    xm = jnp.concatenate([jnp.zeros((BLOCK - N_META, D_MODEL), F32), meta.astype(F32)], axis=0)
    _, kv_m, zc_m = _proj(xm, xm, _Geom(1, 1, 1, 1), ln_g, ln_b, wq, wkv, wglu, BLOCK)
    kv_meta = kv_m[BLOCK - N_META:BLOCK]
    z_meta = zc_m[BLOCK - N_META:BLOCK]

    (bp, sp, _), (bs, ss, _) = x_prompt.shape, x_sample.shape
    xp, xs = x_prompt.reshape(bp * sp, D_MODEL), x_sample.reshape(bs * ss, D_MODEL)
    geom = _geom(x_prompt, x_sample, TILE)
    q, kv, zc = _proj(xp, xs, geom, ln_g, ln_b, wq, wkv, wglu, TILE)
    att = _attn(q, kv, geom, kv_meta, bias, mbias, sink_tab, TILE)
    cz = _conv(zc, geom, z_meta, conv_w[0], row(conv_b[0]), row(conv_ln_g[0]), row(conv_ln_b[0]), TILE)
    l1g, l1b, l2g, l2b = row(ln1_g[0]), row(ln1_b[0]), row(ln2_g[0]), row(ln2_b[0])

    def moe(x, first_tile):
        n = x.shape[0]
        h1p, rt, fields, counts = _out(x, att, cz, ln_g, ln_b, wg, watt, wco, wout, l1g, l1b, wr, br,
                                           TILE, first_tile)
        idx1, idx2, tile_expert, tile_valid, n_tiles = _dispatch_plan(fields, counts, n, TILE_EXPERT)
        n_rows = n_tiles * TILE_EXPERT
        xsorted = _sc_scatter2(h1p.reshape(n * PACK_WORDS, LANES), idx1, idx2, n_rows * PACK_WORDS)
        ys = _experts(tile_expert, tile_valid, xsorted.reshape(n_rows // 8, PACK_WORDS, 8, LANES),
                      w_gate[0], w_up[0], w_down[0], TILE_EXPERT)
        g = _sc_gather(ys.reshape(n_rows * PACK_WORDS, LANES), jnp.concatenate([idx1, idx2]))
        return _final(h1p, g.reshape(2, n // 8, PACK_WORDS, 8, LANES), rt, l2g, l2b, TILE)

    return moe(xp, 0).reshape(x_prompt.shape), moe(xs, geom.n_p).reshape(x_sample.shape)
```

```python
import functools
import math
from typing import NamedTuple

import jax
import jax.numpy as jnp
from jax import lax
from jax.experimental import pallas as pl
from jax.experimental.pallas import tpu as pltpu
from jax.experimental.pallas import tpu_sc as plsc

D_MODEL = 1024
N_META = 16
BLOCK = 128
WINDOW = 128
N_Q_HEADS = 8
N_KV_HEADS = 2
HEAD_DIM = 64
ATT_WIDTH = N_Q_HEADS * HEAD_DIM
KV_WIDTH = N_KV_HEADS * HEAD_DIM
CONV_WIDTH = D_MODEL // 2
CONV_K = 31
N_BUCKETS = 32
MAX_DISTANCE = 128
N_GROUPS = 4
EXPERTS_PER_GROUP = 8
N_EXPERTS = N_GROUPS * EXPERTS_PER_GROUP
D_EXPERT = 256
LN_EPS = 1e-5
DEPTH = 1
ALPHA = (2 * DEPTH) ** 0.25
NEG = -1e30
Q_END = ATT_WIDTH
K_END = Q_END + KV_WIDTH
V_END = K_END + KV_WIDTH
GLU_END = V_END + 2 * CONV_WIDTH
GA_END = GLU_END + D_MODEL

N_PAIRS = N_Q_HEADS // 2
LANES = 128
CONV_HALO = 16
ROUTER_LANES = 128

TILE = 512
CONV_ROWS = 128
LN_ROWS = 64
SHIFT_ROWS = 128
TILE_EXPERT = 512
PACK_WORDS = 4
SC_WINDOW = 128
ROUTE_FIELDS = 8
VMEM_LIMIT = 56 * 1024 * 1024

BF16 = jnp.bfloat16
F32 = jnp.float32


def _layer_norm(x, g, b):
    mu = jnp.mean(x, axis=-1, keepdims=True)
    xc = x - mu
    var = jnp.mean(xc * xc, axis=-1, keepdims=True)
    return xc * lax.rsqrt(var + LN_EPS) * g + b


def _dot(a, b):
    return jnp.dot(a, b, preferred_element_type=F32)


def _dot_nt(a, b):
    return lax.dot_general(a, b, (((1,), (1,)), ((), ())), preferred_element_type=F32)


class _Geom(NamedTuple):
    n_p: int
    n_s: int
    tp: int
    ts: int


def _geom(x_prompt, x_sample, tile):
    (bp, sp, _), (bs, ss, _) = x_prompt.shape, x_sample.shape
    return _Geom(bp * sp // tile, bs * ss // tile, sp // tile, ss // tile)


def _seq_pos(t, g):
    is_p = t < g.n_p
    local = jnp.where(is_p, lax.rem(t, g.tp), lax.rem(jnp.maximum(t - g.n_p, 0), g.ts))
    return is_p, local == 0, local == jnp.where(is_p, g.tp - 1, g.ts - 1)


def _x_specs(g, tile, width):
    return [pl.BlockSpec((tile, width), lambda t: (jnp.minimum(t, g.n_p - 1), 0)),
            pl.BlockSpec((tile, width), lambda t: (jnp.maximum(t - g.n_p, 0), 0))]


def _proj_kernel(xp_ref, xs_ref, g_ref, b_ref, wq_ref, wkv_ref, wglu_ref, h0_ref, q_ref, kv_ref, zc_ref, *, geom):
    x = jnp.where(pl.program_id(0) < geom.n_p, xp_ref[...], xs_ref[...])
    h0 = _layer_norm(x, g_ref[...], b_ref[...])
    h0_ref[...] = h0
    h = h0.astype(BF16)
    q_ref[...] = _dot(h, wq_ref[...]).astype(BF16)
    kv_ref[...] = _dot(h, wkv_ref[...]).astype(BF16)
    u = _dot(h, wglu_ref[...])
    zc_ref[...] = (u[:, :CONV_WIDTH] * jax.nn.sigmoid(u[:, CONV_WIDTH:])).astype(BF16)


def _proj(xp, xs, geom, ln_g, ln_b, wq, wkv, wglu, tile):
    T = (geom.n_p + geom.n_s) * tile
    row = lambda w: pl.BlockSpec((tile, w), lambda t: (t, 0))
    full = lambda a: pl.BlockSpec(a.shape, lambda t: (0,) * a.ndim)
    return pl.pallas_call(
        functools.partial(_proj_kernel, geom=geom),
        grid=(geom.n_p + geom.n_s,),
        in_specs=_x_specs(geom, tile, D_MODEL) + [full(ln_g), full(ln_b), full(wq), full(wkv), full(wglu)],
        out_specs=[row(D_MODEL), row(ATT_WIDTH), row(2 * KV_WIDTH), row(CONV_WIDTH)],
        out_shape=[
            jax.ShapeDtypeStruct((T, D_MODEL), F32),
            jax.ShapeDtypeStruct((T, ATT_WIDTH), BF16),
            jax.ShapeDtypeStruct((T, 2 * KV_WIDTH), BF16),
            jax.ShapeDtypeStruct((T, CONV_WIDTH), BF16),
        ],
        compiler_params=pltpu.CompilerParams(dimension_semantics=("parallel",), vmem_limit_bytes=VMEM_LIMIT),
        name="proj",
    )(xp, xs, ln_g, ln_b, wq, wkv, wglu)


def _attn_kernel(q_ref, kvp_ref, kvc_ref, kvn_ref, kvm_ref, bias_ref, mbias_ref, sink_ref, o_ref, *, tile, geom):
    _, seq_first, seq_last = _seq_pos(pl.program_id(0), geom)
    blocks = tile // BLOCK
    scale = HEAD_DIM ** -0.5

    lane = lax.broadcasted_iota(jnp.int32, (1, LANES), 1)
    lo = lane < HEAD_DIM

    def split_heads(t):
        z = jnp.zeros_like(t)
        return jnp.where(lo, t, z), jnp.where(lo, z, t)

    kv_ext = jnp.concatenate([kvp_ref[...], kvc_ref[...], kvn_ref[...]], axis=0)
    k_ext = kv_ext[:, :KV_WIDTH] * jnp.asarray(scale, BF16)
    v_ext = kv_ext[:, KV_WIDTH:]
    ka, kb = split_heads(k_ext)
    va, vb = split_heads(v_ext)
    kma, kmb = split_heads(kvm_ref[:, :KV_WIDTH] * jnp.asarray(scale, BF16))
    vma, vmb = split_heads(kvm_ref[:, KV_WIDTH:])
    km_cat = jnp.concatenate([kma, kmb], axis=0)
    vm_cat = jnp.concatenate([vma, vmb], axis=0)

    mlane = lax.broadcasted_iota(jnp.int32, (1, 2 * N_META), 1)
    m_first = mlane < N_META
    sink = sink_ref[...]
    lane_o = lax.broadcasted_iota(jnp.int32, (1, LANES), 1) < HEAD_DIM

    for j in range(blocks):
        first = jnp.logical_and(seq_first, j == 0)
        last = jnp.logical_and(seq_last, j == blocks - 1)
        variant = jnp.where(first, 0, jnp.where(last, 2, 1))
        mvariant = jnp.where(first, 0, 1)

        r0 = j * BLOCK
        qb = q_ref[r0:r0 + BLOCK, :]
        q4 = jnp.concatenate([qb[:, p * LANES:(p + 1) * LANES] for p in range(N_PAIRS)], axis=0)
        k_cat = jnp.concatenate([ka[r0:r0 + 3 * BLOCK], kb[r0:r0 + 3 * BLOCK]], axis=0)
        v_cat = jnp.concatenate([va[r0:r0 + 3 * BLOCK], vb[r0:r0 + 3 * BLOCK]], axis=0)

        s = _dot_nt(q4, k_cat) + bias_ref[variant]
        sm = _dot_nt(q4, km_cat) + mbias_ref[mvariant]

        s_a, s_b = s[:, :3 * BLOCK], s[:, 3 * BLOCK:]
        sm_a = jnp.where(m_first, sm, NEG)
        sm_b = jnp.where(m_first, NEG, sm)
        m_a = jnp.maximum(jnp.maximum(jnp.max(s_a, axis=1, keepdims=True),
                                      jnp.max(sm_a, axis=1, keepdims=True)), sink[:, 0:1])
        m_b = jnp.maximum(jnp.maximum(jnp.max(s_b, axis=1, keepdims=True),
                                      jnp.max(sm_b, axis=1, keepdims=True)), sink[:, 1:2])
        p_a = jnp.exp(s_a - m_a)
        p_b = jnp.exp(s_b - m_b)
        pm = jnp.exp(jnp.where(m_first, sm - m_a, sm - m_b))
        l_a = (jnp.sum(p_a, axis=1, keepdims=True) + jnp.sum(jnp.where(m_first, pm, 0.0), axis=1, keepdims=True)
               + jnp.exp(sink[:, 0:1] - m_a))
        l_b = (jnp.sum(p_b, axis=1, keepdims=True) + jnp.sum(jnp.where(m_first, 0.0, pm), axis=1, keepdims=True)
               + jnp.exp(sink[:, 1:2] - m_b))
        p = jnp.concatenate([p_a, p_b], axis=1).astype(BF16)
        o = _dot(p, v_cat) + _dot(pm.astype(BF16), vm_cat)
        o = o * jnp.where(lane_o, 1.0 / l_a, 1.0 / l_b)
        for pr in range(N_PAIRS):
            o_ref[r0:r0 + BLOCK, pr * LANES:(pr + 1) * LANES] = o[pr * BLOCK:(pr + 1) * BLOCK].astype(BF16)


def _attn(q, kv, geom, kv_meta, bias, mbias, sink_tab, tile):
    T = q.shape[0]
    bpt = tile // BLOCK
    n_blocks = T // BLOCK
    full = lambda a: pl.BlockSpec(a.shape, lambda t: (0,) * a.ndim)
    return pl.pallas_call(
        functools.partial(_attn_kernel, tile=tile, geom=geom),
        grid=(T // tile,),
        in_specs=[
            pl.BlockSpec((tile, ATT_WIDTH), lambda t: (t, 0)),
            pl.BlockSpec((BLOCK, 2 * KV_WIDTH), lambda t: (jnp.maximum(t * bpt - 1, 0), 0)),
            pl.BlockSpec((tile, 2 * KV_WIDTH), lambda t: (t, 0)),
            pl.BlockSpec((BLOCK, 2 * KV_WIDTH), lambda t: (jnp.minimum((t + 1) * bpt, n_blocks - 1), 0)),
            full(kv_meta), full(bias), full(mbias), full(sink_tab),
        ],
        out_specs=pl.BlockSpec((tile, ATT_WIDTH), lambda t: (t, 0)),
        out_shape=jax.ShapeDtypeStruct((T, ATT_WIDTH), BF16),
        compiler_params=pltpu.CompilerParams(dimension_semantics=("parallel",), vmem_limit_bytes=VMEM_LIMIT),
        name="attn",
    )(q, kv, kv, kv, kv_meta, bias, mbias, sink_tab)


def _conv_kernel(zp_ref, zc_ref, zn_ref, zm_ref, w_ref, cb_ref, g_ref, b_ref, o_ref, ext_ref, sh_ref, y_ref,
                 *, tile, geom):
    i = pl.program_id(0)
    _, seq_first, seq_last = _seq_pos(i, geom)
    ext_ref[0:CONV_HALO, :] = jnp.where(seq_first, zm_ref[...], zp_ref[...]).astype(F32)
    ext_ref[CONV_HALO:CONV_HALO + tile, :] = zc_ref[...].astype(F32)
    ext_ref[CONV_HALO + tile:, :] = jnp.where(seq_last, 0.0, zn_ref[...].astype(F32))
    off = CONV_HALO - CONV_K // 2
    reach = (off + CONV_K - 1) // 8 * 8
    for p in range(1, 8):
        for r0 in range(0, tile + reach, SHIFT_ROWS):
            n = min(SHIFT_ROWS, tile + reach - r0)
            sh_ref[p - 1, r0:r0 + n, :] = ext_ref[r0 + p:r0 + p + n, :]

    def taps(r0, cs):
        acc = jnp.zeros((CONV_ROWS, LANES), F32)
        for k in range(CONV_K):
            p, a = (off + k) % 8, (off + k) // 8 * 8
            rows = slice(r0 + a, r0 + a + CONV_ROWS)
            win = ext_ref[rows, cs] if p == 0 else sh_ref[p - 1, rows, cs]
            acc = acc + win * w_ref[k:k + 1, cs]
        y_ref[r0:r0 + CONV_ROWS, cs] = acc

    for c in range(CONV_WIDTH // LANES):
        for r in range(tile // CONV_ROWS):
            pl.when(i >= 0)(functools.partial(taps, r * CONV_ROWS, slice(c * LANES, (c + 1) * LANES)))
    cb, g, b = cb_ref[...], g_ref[...], b_ref[...]
    for r in range(tile // LN_ROWS):
        r0 = r * LN_ROWS
        y = _layer_norm(y_ref[r0:r0 + LN_ROWS, :] + cb, g, b)
        o_ref[r0:r0 + LN_ROWS, :] = (y * jax.nn.sigmoid(y)).astype(BF16)


def _conv(zc, geom, z_meta, conv_w, conv_b, ln_g, ln_b, tile):
    T, C = zc.shape
    hpt = tile // CONV_HALO
    n_halo = T // CONV_HALO
    full = lambda a: pl.BlockSpec(a.shape, lambda t: (0,) * a.ndim)
    return pl.pallas_call(
        functools.partial(_conv_kernel, tile=tile, geom=geom),
        grid=(T // tile,),
        in_specs=[
            pl.BlockSpec((CONV_HALO, C), lambda t: (jnp.maximum(t * hpt - 1, 0), 0)),
            pl.BlockSpec((tile, C), lambda t: (t, 0)),
            pl.BlockSpec((CONV_HALO, C), lambda t: (jnp.minimum((t + 1) * hpt, n_halo - 1), 0)),
            full(z_meta), full(conv_w), full(conv_b), full(ln_g), full(ln_b),
        ],
        out_specs=pl.BlockSpec((tile, C), lambda t: (t, 0)),
        out_shape=jax.ShapeDtypeStruct((T, C), BF16),
        scratch_shapes=[pltpu.VMEM((tile + 2 * CONV_HALO, C), F32),
                        pltpu.VMEM((7, tile + 2 * CONV_HALO - 8, C), F32),
                        pltpu.VMEM((tile, C), F32)],
        compiler_params=pltpu.CompilerParams(dimension_semantics=("parallel",), vmem_limit_bytes=VMEM_LIMIT),
        name="conv",
    )(zc, zc, zc, z_meta, conv_w, conv_b, ln_g, ln_b)


def _route(r):
    lane_i = lax.broadcasted_iota(jnp.int32, r.shape, 1)
    lane = lane_i.astype(F32)
    big = float(1 << 20)
    is_g = jnp.logical_and(lane_i >= N_EXPERTS, lane_i < N_EXPERTS + N_GROUPS)
    lg = jnp.where(is_g, r, -jnp.inf)
    mg = jnp.max(lg, axis=1, keepdims=True)
    g_w = 1.0 / jnp.sum(jnp.exp(lg - mg), axis=1, keepdims=True)
    g_idx = jnp.min(jnp.where(lg == mg, lane - N_EXPERTS, big), axis=1, keepdims=True)
    lane_group = jnp.right_shift(lane_i, 3).astype(F32)
    in_group = jnp.logical_and(lane_i < N_EXPERTS, lane_group == g_idx)
    le = jnp.where(in_group, r, -jnp.inf)
    m1 = jnp.max(le, axis=1, keepdims=True)
    den = jnp.sum(jnp.exp(le - m1), axis=1, keepdims=True)
    i1 = jnp.min(jnp.where(le == m1, lane, big), axis=1, keepdims=True)
    le2 = jnp.where(lane == i1, -jnp.inf, le)
    m2 = jnp.max(le2, axis=1, keepdims=True)
    i2 = jnp.min(jnp.where(le2 == m2, lane, big), axis=1, keepdims=True)
    p1 = 1.0 / den
    p2 = jnp.exp(m2 - m1) / den
    tot = p1 + p2
    return i1, i2, g_w * (p1 / tot), g_w * (p2 / tot)


def _pack_bf16_pairs(x):
    half = x.shape[1] // 2
    words = []
    for j in range(half // LANES):
        lo = pltpu.bitcast(x[:, j * LANES:(j + 1) * LANES].astype(BF16).astype(F32), jnp.uint32)
        hi = pltpu.bitcast(x[:, half + j * LANES:half + (j + 1) * LANES].astype(BF16).astype(F32), jnp.uint32)
        words.append(hi | (lo >> 16))
    return words


def _unpack_bf16_pairs(words):
    lo = [pltpu.bitcast(w << 16, F32) for w in words]
    hi = [pltpu.bitcast(w & jnp.uint32(0xFFFF0000), F32) for w in words]
    return jnp.concatenate(lo + hi, axis=1)


def _out_kernel(h0_ref, att_ref, cz_ref, wg_ref, watt_ref, wco_ref, wout_ref,
                l1g_ref, l1b_ref, wr_ref, br_ref, before_ref, h1_ref, h1p_ref, rt_ref, fld_ref, cnt_ref, *, tile):
    @pl.when(pl.program_id(0) == 0)
    def _():
        cnt_ref[...] = jnp.zeros_like(cnt_ref)

    h0 = h0_ref[...]
    hb = h0.astype(BF16)
    g_att = jax.nn.sigmoid(_dot(hb, wg_ref[:, :D_MODEL]))
    mix = g_att * _dot(att_ref[...], watt_ref[...])
    g_conv = jax.nn.sigmoid(_dot(hb, wg_ref[:, D_MODEL:]))
    mix = mix + g_conv * _dot(cz_ref[...], wco_ref[...])
    m = _dot(mix.astype(BF16), wout_ref[...])
    h1 = _layer_norm(ALPHA * h0 + m, l1g_ref[...], l1b_ref[...])
    h1_ref[...] = h1
    for j, w in enumerate(_pack_bf16_pairs(h1)):
        h1p_ref[:, j, :, :] = w.reshape(tile // 8, 8, LANES)

    r = _dot(h1.astype(BF16), wr_ref[...]) + br_ref[...]
    i1, i2, w1, w2 = _route(r)
    lane = lax.broadcasted_iota(jnp.int32, (tile, ROUTER_LANES), 1)
    lane_f = lane.astype(F32)
    hit1, hit2 = lane_f == i1, lane_f == i2
    onehot = jnp.where(jnp.logical_or(hit1, hit2), 1.0, 0.0)
    seen = _dot(before_ref[...], onehot.astype(BF16)) + cnt_ref[0:1, :]
    rank1 = jnp.sum(jnp.where(hit1, seen, 0.0), axis=1, keepdims=True)
    rank2 = jnp.sum(jnp.where(hit2, seen, 0.0), axis=1, keepdims=True)
    cnt_ref[...] = cnt_ref[...] + jnp.sum(onehot, axis=0, keepdims=True)
    fields = (i1, i2, w1, w2, rank1, rank2)
    rt = jnp.zeros((tile, ROUTER_LANES), F32)
    for k, v in enumerate(fields):
        rt = jnp.where(lane == k, v, rt)
    rt_ref[...] = rt
    fld_ref[...] = jnp.transpose(rt)[0:ROUTE_FIELDS, :]


def _out(h0, att, cz, wg, watt, wco, wout, l1g, l1b, wr, br, tile, first_tile, n_tiles):
    T, D = n_tiles * tile, D_MODEL
    own = lambda w: pl.BlockSpec((tile, w), lambda i: (i, 0))
    flat = lambda w: pl.BlockSpec((tile, w), lambda i: (first_tile + i, 0))
    full = lambda a: pl.BlockSpec(a.shape, lambda i: (0,) * a.ndim)
    before = (jnp.arange(tile)[None, :] < jnp.arange(tile)[:, None]).astype(BF16)
    return pl.pallas_call(
        functools.partial(_out_kernel, tile=tile),
        grid=(T // tile,),
        in_specs=[flat(D), flat(ATT_WIDTH), flat(CONV_WIDTH), full(wg),
                  full(watt), full(wco), full(wout), full(l1g), full(l1b), full(wr), full(br), full(before)],
        out_specs=[own(D),
                   pl.BlockSpec((tile // 8, PACK_WORDS, 8, LANES), lambda i: (i, 0, 0, 0)),
                   own(ROUTER_LANES),
                   pl.BlockSpec((ROUTE_FIELDS, tile), lambda i: (0, i)),
                   pl.BlockSpec((8, ROUTER_LANES), lambda i: (0, 0))],
        out_shape=[jax.ShapeDtypeStruct((T, D), F32),
                   jax.ShapeDtypeStruct((T // 8, PACK_WORDS, 8, LANES), jnp.uint32),
                   jax.ShapeDtypeStruct((T, ROUTER_LANES), F32),
                   jax.ShapeDtypeStruct((ROUTE_FIELDS, T), F32),
                   jax.ShapeDtypeStruct((8, ROUTER_LANES), F32)],
        compiler_params=pltpu.CompilerParams(dimension_semantics=("arbitrary",), vmem_limit_bytes=VMEM_LIMIT),
        name="out",
    )(h0, att, cz, wg, watt, wco, wout, l1g, l1b, wr, br, before)


def _sc_scatter2(src, idx_a, idx_b, n_out):
    m = src.shape[0]
    mesh = plsc.VectorSubcoreMesh(core_axis_name="c", subcore_axis_name="s")

    @functools.partial(pl.kernel, out_type=jax.ShapeDtypeStruct((n_out, LANES), src.dtype), mesh=mesh)
    def k(x_hbm, ia_hbm, ib_hbm, o_hbm):
        def body(x_vmem, ia_vmem, ib_vmem):
            pltpu.sync_copy(x_vmem, o_hbm.at[ia_vmem.at[0]])
            pltpu.sync_copy(x_vmem, o_hbm.at[ib_vmem.at[0]])

        pltpu.emit_pipeline(
            body, grid=(m // SC_WINDOW,),
            in_specs=[pl.BlockSpec((SC_WINDOW, LANES), index_map=lambda i: (i, 0)),
                      pl.BlockSpec((1, SC_WINDOW), index_map=lambda i: (0, i)),
                      pl.BlockSpec((1, SC_WINDOW), index_map=lambda i: (0, i))],
            out_specs=[],
            core_axis_name=("c", "s"), dimension_semantics=(pltpu.PARALLEL,),
        )(x_hbm, ia_hbm, ib_hbm)

    return k(src, idx_a.reshape(1, m), idx_b.reshape(1, m))


def _sc_gather(table, idx):
    m = idx.shape[0]
    mesh = plsc.VectorSubcoreMesh(core_axis_name="c", subcore_axis_name="s")

    @functools.partial(pl.kernel, out_type=jax.ShapeDtypeStruct((m, LANES), table.dtype), mesh=mesh)
    def k(x_hbm, i_hbm, o_hbm):
        def body(i_vmem, o_vmem):
            pltpu.sync_copy(x_hbm.at[i_vmem.at[0]], o_vmem)

        pltpu.emit_pipeline(
            body, grid=(m // SC_WINDOW,),
            in_specs=[pl.BlockSpec((1, SC_WINDOW), index_map=lambda i: (0, i))],
            out_specs=[pl.BlockSpec((SC_WINDOW, LANES), index_map=lambda i: (i, 0))],
            core_axis_name=("c", "s"), dimension_semantics=(pltpu.PARALLEL,),
        )(i_hbm, o_hbm)

    return k(table, idx.reshape(1, m))


def _expert_kernel(te_ref, tv_ref, xs_ref, wg_ref, wu_ref, wd_ref, ys_ref, wgb_ref, wub_ref, wdb_ref, *, tile):
    n = pl.program_id(0)
    valid = tv_ref[n]

    @pl.when(jnp.logical_or(n == 0, te_ref[n] != te_ref[jnp.maximum(n - 1, 0)]))
    def _():
        wgb_ref[...] = wg_ref[...].astype(BF16)
        wub_ref[...] = wu_ref[...].astype(BF16)
        wdb_ref[...] = wd_ref[...].astype(BF16)

    @pl.when(valid > 0)
    def _():
        x = _unpack_bf16_pairs([xs_ref[:, j, :, :].reshape(tile, LANES) for j in range(PACK_WORDS)])
        rows = lax.broadcasted_iota(jnp.int32, (tile, 1), 0)
        x = jnp.where(rows < valid, x, 0.0).astype(BF16)
        gt = _dot(x, wgb_ref[...])
        up = _dot(x, wub_ref[...])
        hid = (gt * jax.nn.sigmoid(gt)) * up
        y = _dot(hid.astype(BF16), wdb_ref[...])
        for j, w in enumerate(_pack_bf16_pairs(y)):
            ys_ref[:, j, :, :] = w.reshape(tile // 8, 8, LANES)

    @pl.when(valid <= 0)
    def _():
        ys_ref[...] = jnp.zeros_like(ys_ref)


def _experts(tile_expert, tile_valid, xs, w_gate, w_up, w_down, tile):
    n_tiles = xs.shape[0] * 8 // tile
    blk = pl.BlockSpec((tile // 8, PACK_WORDS, 8, LANES), lambda n, te, tv: (n, 0, 0, 0))
    return pl.pallas_call(
        functools.partial(_expert_kernel, tile=tile),
        grid_spec=pltpu.PrefetchScalarGridSpec(
            num_scalar_prefetch=2,
            grid=(n_tiles,),
            in_specs=[blk,
                      pl.BlockSpec((None, D_MODEL, D_EXPERT), lambda n, te, tv: (te[n], 0, 0)),
                      pl.BlockSpec((None, D_MODEL, D_EXPERT), lambda n, te, tv: (te[n], 0, 0)),
                      pl.BlockSpec((None, D_EXPERT, D_MODEL), lambda n, te, tv: (te[n], 0, 0))],
            out_specs=blk,
            scratch_shapes=[pltpu.VMEM((D_MODEL, D_EXPERT), BF16), pltpu.VMEM((D_MODEL, D_EXPERT), BF16),
                            pltpu.VMEM((D_EXPERT, D_MODEL), BF16)],
        ),
        out_shape=jax.ShapeDtypeStruct(xs.shape, jnp.uint32),
        compiler_params=pltpu.CompilerParams(
            dimension_semantics=("arbitrary",), vmem_limit_bytes=VMEM_LIMIT),
        name="experts",
    )(tile_expert, tile_valid, xs, w_gate, w_up, w_down)


def _final_kernel(h_ref, g_ref, rt_ref, l2g_ref, l2b_ref, o_ref, *, tile):
    rt = rt_ref[...]
    lane = lax.broadcasted_iota(jnp.int32, rt.shape, 1)
    w1 = jnp.sum(jnp.where(lane == 2, rt, 0.0), axis=1, keepdims=True)
    w2 = jnp.sum(jnp.where(lane == 3, rt, 0.0), axis=1, keepdims=True)
    y1 = _unpack_bf16_pairs([g_ref[0, :, j, :, :].reshape(tile, LANES) for j in range(PACK_WORDS)])
    y2 = _unpack_bf16_pairs([g_ref[1, :, j, :, :].reshape(tile, LANES) for j in range(PACK_WORDS)])
    f = w1 * y1 + w2 * y2
    o_ref[...] = _layer_norm(ALPHA * h_ref[...] + f, l2g_ref[...], l2b_ref[...])


def _final(h1, g, rt, l2g, l2b, tile):
    T, D = h1.shape
    row = lambda w: pl.BlockSpec((tile, w), lambda i: (i, 0))
    full = lambda a: pl.BlockSpec(a.shape, lambda i: (0,) * a.ndim)
    return pl.pallas_call(
        functools.partial(_final_kernel, tile=tile),
        grid=(T // tile,),
        in_specs=[row(D),
                  pl.BlockSpec((2, tile // 8, PACK_WORDS, 8, LANES), lambda i: (0, i, 0, 0, 0)),
                  row(ROUTER_LANES), full(l2g), full(l2b)],
        out_specs=row(D),
        out_shape=jax.ShapeDtypeStruct((T, D), F32),
        compiler_params=pltpu.CompilerParams(dimension_semantics=("parallel",), vmem_limit_bytes=VMEM_LIMIT),
        name="final",
    )(h1, g, rt, l2g, l2b)


def _dispatch_plan(fields, counts, n_tokens, tile):
    cnt = counts[0, :N_EXPERTS].astype(jnp.int32)
    padded = (cnt + tile - 1) // tile * tile
    base = jnp.cumsum(padded) - padded
    e_ids = jnp.arange(N_EXPERTS, dtype=jnp.int32)

    def dest(row_e, row_r):
        e = fields[row_e].astype(jnp.int32)
        seg = jnp.sum(jnp.where(e[None, :] == e_ids[:, None], base[:, None], 0), axis=0)
        pos = seg + fields[row_r].astype(jnp.int32)
        p = pos.reshape(n_tokens // 8, 1, 8)
        j = jnp.arange(PACK_WORDS, dtype=jnp.int32).reshape(1, PACK_WORDS, 1)
        return ((p // 8) * (8 * PACK_WORDS) + j * 8 + p % 8).reshape(-1)

    n_tiles = (2 * n_tokens) // tile + N_EXPERTS
    start = jnp.arange(n_tiles, dtype=jnp.int32) * tile
    seg_end = base + padded
    te = jnp.minimum(jnp.sum((start[:, None] >= seg_end[None, :]).astype(jnp.int32), axis=1), N_EXPERTS - 1)
    te_base = jnp.sum(jnp.where(te[:, None] == e_ids[None, :], base[None, :], 0), axis=1)
    te_cnt = jnp.sum(jnp.where(te[:, None] == e_ids[None, :], cnt[None, :], 0), axis=1)
    tv = jnp.clip(te_cnt - (start - te_base), 0, tile)
    return dest(0, 4), dest(1, 5), te, tv, n_tiles


def _t5_bucket(rel):
    half = N_BUCKETS // 2
    max_exact = half // 2
    ret = jnp.where(rel > 0, half, 0)
    n = jnp.abs(rel)
    nf = jnp.maximum(n, 1).astype(F32)
    large = max_exact + (jnp.log(nf / max_exact) / math.log(MAX_DISTANCE / max_exact)
                         * (half - max_exact)).astype(jnp.int32)
    large = jnp.minimum(large, half - 1)
    return ret + jnp.where(n < max_exact, n, large)


def _bucket_bias(rel_bias, bucket):
    rb = rel_bias.astype(F32)
    out = jnp.zeros((N_Q_HEADS,) + bucket.shape, F32)
    for b in range(N_BUCKETS):
        out = out + jnp.where(bucket[None] == b, rb[b][:, None, None], 0.0)
    return out


def _pair_rows(t):
    return jnp.concatenate([t[:N_PAIRS], t[N_PAIRS:]], axis=-1).reshape(N_PAIRS * BLOCK, -1)


def _bias_tables(rel_bias, sink):
    qi = jnp.arange(BLOCK)
    kj = jnp.arange(3 * BLOCK) - BLOCK
    rel = kj[None, :] - qi[:, None]
    band = _bucket_bias(rel_bias, _t5_bucket(rel))
    in_win = (jnp.abs(rel) <= WINDOW)[None]
    not_prev = (kj >= 0)[None, None, :]
    not_next = (kj < BLOCK)[None, None, :]
    variants = [jnp.where(in_win & not_prev, band, NEG),
                jnp.where(in_win, band, NEG),
                jnp.where(in_win & not_next, band, NEG)]
    bias = jnp.stack([_pair_rows(v) for v in variants])

    off = BLOCK - N_META
    mvars = []
    for blk in (1, 2):
        qpos = blk * BLOCK + qi - off
        meta_rel = jnp.arange(N_META)[None, :] - qpos[:, None]
        mvars.append(_pair_rows(_bucket_bias(rel_bias, _t5_bucket(meta_rel))))
    mbias = jnp.stack(mvars)
    s = sink.astype(F32)
    sink_tab = jnp.repeat(jnp.stack([s[:N_PAIRS], s[N_PAIRS:]], axis=-1), BLOCK, axis=0)
    return bias, mbias, sink_tab


def kernel(x_prompt, x_sample, meta, ln_in_g, ln_in_b, rel_bias, w_in, w_att_branch, sink, conv_w, conv_b,
           conv_ln_g, conv_ln_b, w_conv_out, w_out, ln1_g, ln1_b, w_group, b_group, w_router, b_router,
           w_gate, w_up, w_down, ln2_g, ln2_b):
    row = lambda v: v.reshape(1, -1).astype(F32)
    w = w_in[0]
    wq = (w[:, :Q_END].reshape(D_MODEL, 2, N_PAIRS, HEAD_DIM).transpose(0, 2, 1, 3)
          .reshape(D_MODEL, ATT_WIDTH).astype(BF16))
    watt = (w_att_branch[0].reshape(2, N_PAIRS, HEAD_DIM, D_MODEL).transpose(1, 0, 2, 3)
            .reshape(ATT_WIDTH, D_MODEL).astype(BF16))
    wkv = w[:, Q_END:V_END].astype(BF16)
    wglu = w[:, V_END:GLU_END].astype(BF16)
    wg = w[:, GLU_END:].astype(BF16)
    wco = w_conv_out[0].astype(BF16)
    wout = w_out[0].astype(BF16)
    wr = jnp.zeros((D_MODEL, ROUTER_LANES), F32)
    wr = wr.at[:, :N_EXPERTS].set(w_router[0]).at[:, N_EXPERTS:N_EXPERTS + N_GROUPS].set(w_group[0]).astype(BF16)
    br = jnp.zeros((1, ROUTER_LANES), F32)
    br = br.at[0, :N_EXPERTS].set(b_router[0]).at[0, N_EXPERTS:N_EXPERTS + N_GROUPS].set(b_group[0])
    ln_g, ln_b = row(ln_in_g), row(ln_in_b)
    bias, mbias, sink_tab = _bias_tables(rel_bias, sink[0])

    xm = jnp.concatenate([jnp.zeros((BLOCK - N_META, D_MODEL), F32), meta.astype(F32)], axis=0)
    _, _, kv_m, zc_m = _proj(xm, xm, _Geom(1, 1, 1, 1), ln_g, ln_b, wq, wkv, wglu, BLOCK)
    kv_meta = kv_m[BLOCK - N_META:BLOCK]
    z_meta = zc_m[BLOCK - N_META:BLOCK]

    (bp, sp, _), (bs, ss, _) = x_prompt.shape, x_sample.shape
    xp, xs = x_prompt.reshape(bp * sp, D_MODEL), x_sample.reshape(bs * ss, D_MODEL)
    geom = _geom(x_prompt, x_sample, TILE)
    h0, q, kv, zc = _proj(xp, xs, geom, ln_g, ln_b, wq, wkv, wglu, TILE)
    att = _attn(q, kv, geom, kv_meta, bias, mbias, sink_tab, TILE)
    cz = _conv(zc, geom, z_meta, conv_w[0], row(conv_b[0]), row(conv_ln_g[0]), row(conv_ln_b[0]), TILE)
    l1g, l1b, l2g, l2b = row(ln1_g[0]), row(ln1_b[0]), row(ln2_g[0]), row(ln2_b[0])

    def moe(first_tile, n_tiles):
        n = n_tiles * TILE
        h1, h1p, rt, fields, counts = _out(h0, att, cz, wg, watt, wco, wout, l1g, l1b, wr, br,
                                           TILE, first_tile, n_tiles)
        idx1, idx2, tile_expert, tile_valid, n_tiles = _dispatch_plan(fields, counts, n, TILE_EXPERT)
        n_rows = n_tiles * TILE_EXPERT
        xsorted = _sc_scatter2(h1p.reshape(n * PACK_WORDS, LANES), idx1, idx2, n_rows * PACK_WORDS)
        ys = _experts(tile_expert, tile_valid, xsorted.reshape(n_rows // 8, PACK_WORDS, 8, LANES),
                      w_gate[0], w_up[0], w_down[0], TILE_EXPERT)
        g = _sc_gather(ys.reshape(n_rows * PACK_WORDS, LANES), jnp.concatenate([idx1, idx2]))
        return _final(h1, g.reshape(2, n // 8, PACK_WORDS, 8, LANES), rt, l2g, l2b, TILE)

    return moe(0, geom.n_p).reshape(x_prompt.shape), moe(geom.n_p, geom.n_s).reshape(x_sample.shape)
```

```python
import functools
import math
from typing import NamedTuple

import jax
import jax.numpy as jnp
from jax import lax
from jax.experimental import pallas as pl
from jax.experimental.pallas import tpu as pltpu
from jax.experimental.pallas import tpu_sc as plsc

D_MODEL = 1024
N_META = 16
BLOCK = 128
WINDOW = 128
N_Q_HEADS = 8
N_KV_HEADS = 2
HEAD_DIM = 64
ATT_WIDTH = N_Q_HEADS * HEAD_DIM
KV_WIDTH = N_KV_HEADS * HEAD_DIM
CONV_WIDTH = D_MODEL // 2
CONV_K = 31
N_BUCKETS = 32
MAX_DISTANCE = 128
N_GROUPS = 4
EXPERTS_PER_GROUP = 8
N_EXPERTS = N_GROUPS * EXPERTS_PER_GROUP
D_EXPERT = 256
LN_EPS = 1e-5
DEPTH = 1
ALPHA = (2 * DEPTH) ** 0.25
NEG = -1e30
Q_END = ATT_WIDTH
K_END = Q_END + KV_WIDTH
V_END = K_END + KV_WIDTH
GLU_END = V_END + 2 * CONV_WIDTH
GA_END = GLU_END + D_MODEL

N_PAIRS = N_Q_HEADS // 2
LANES = 128
CONV_HALO = 16
ROUTER_LANES = 128

TILE = 512
CONV_ROWS = 128
LN_ROWS = 64
SHIFT_ROWS = 128
TILE_EXPERT = 256
PACK_WORDS = 4
SC_WINDOW = 128
ROUTE_FIELDS = 8
VMEM_LIMIT = 56 * 1024 * 1024

BF16 = jnp.bfloat16
F32 = jnp.float32


def _layer_norm(x, g, b):
    mu = jnp.mean(x, axis=-1, keepdims=True)
    xc = x - mu
    var = jnp.mean(xc * xc, axis=-1, keepdims=True)
    return xc * lax.rsqrt(var + LN_EPS) * g + b


def _dot(a, b):
    return jnp.dot(a, b, preferred_element_type=F32)


def _dot_nt(a, b):
    return lax.dot_general(a, b, (((1,), (1,)), ((), ())), preferred_element_type=F32)


class _Geom(NamedTuple):
    n_p: int
    n_s: int
    tp: int
    ts: int


def _geom(x_prompt, x_sample, tile):
    (bp, sp, _), (bs, ss, _) = x_prompt.shape, x_sample.shape
    return _Geom(bp * sp // tile, bs * ss // tile, sp // tile, ss // tile)


def _seq_pos(t, g):
    is_p = t < g.n_p
    local = jnp.where(is_p, lax.rem(t, g.tp), lax.rem(jnp.maximum(t - g.n_p, 0), g.ts))
    return is_p, local == 0, local == jnp.where(is_p, g.tp - 1, g.ts - 1)


def _x_specs(g, tile, width):
    return [pl.BlockSpec((tile, width), lambda t: (jnp.minimum(t, g.n_p - 1), 0)),
            pl.BlockSpec((tile, width), lambda t: (jnp.maximum(t - g.n_p, 0), 0))]


def _proj_kernel(xp_ref, xs_ref, g_ref, b_ref, wq_ref, wkv_ref, wglu_ref, h0_ref, q_ref, kv_ref, zc_ref, *, geom):
    x = jnp.where(pl.program_id(0) < geom.n_p, xp_ref[...], xs_ref[...])
    h0 = _layer_norm(x, g_ref[...], b_ref[...])
    h0_ref[...] = h0
    h = h0.astype(BF16)
    q_ref[...] = _dot(h, wq_ref[...]).astype(BF16)
    kv_ref[...] = _dot(h, wkv_ref[...]).astype(BF16)
    u = _dot(h, wglu_ref[...])
    zc_ref[...] = (u[:, :CONV_WIDTH] * jax.nn.sigmoid(u[:, CONV_WIDTH:])).astype(BF16)


def _proj(xp, xs, geom, ln_g, ln_b, wq, wkv, wglu, tile):
    T = (geom.n_p + geom.n_s) * tile
    row = lambda w: pl.BlockSpec((tile, w), lambda t: (t, 0))
    full = lambda a: pl.BlockSpec(a.shape, lambda t: (0,) * a.ndim)
    return pl.pallas_call(
        functools.partial(_proj_kernel, geom=geom),
        grid=(geom.n_p + geom.n_s,),
        in_specs=_x_specs(geom, tile, D_MODEL) + [full(ln_g), full(ln_b), full(wq), full(wkv), full(wglu)],
        out_specs=[row(D_MODEL), row(ATT_WIDTH), row(2 * KV_WIDTH), row(CONV_WIDTH)],
        out_shape=[
            jax.ShapeDtypeStruct((T, D_MODEL), F32),
            jax.ShapeDtypeStruct((T, ATT_WIDTH), BF16),
            jax.ShapeDtypeStruct((T, 2 * KV_WIDTH), BF16),
            jax.ShapeDtypeStruct((T, CONV_WIDTH), BF16),
        ],
        compiler_params=pltpu.CompilerParams(dimension_semantics=("parallel",), vmem_limit_bytes=VMEM_LIMIT),
        name="proj",
    )(xp, xs, ln_g, ln_b, wq, wkv, wglu)


def _attn_kernel(q_ref, kvp_ref, kvc_ref, kvn_ref, kvm_ref, bias_ref, mbias_ref, sink_ref, o_ref, *, tile, geom):
    _, seq_first, seq_last = _seq_pos(pl.program_id(0), geom)
    blocks = tile // BLOCK
    scale = HEAD_DIM ** -0.5

    lane = lax.broadcasted_iota(jnp.int32, (1, LANES), 1)
    lo = lane < HEAD_DIM

    def split_heads(t):
        z = jnp.zeros_like(t)
        return jnp.where(lo, t, z), jnp.where(lo, z, t)

    kv_ext = jnp.concatenate([kvp_ref[...], kvc_ref[...], kvn_ref[...]], axis=0)
    k_ext = kv_ext[:, :KV_WIDTH] * jnp.asarray(scale, BF16)
    v_ext = kv_ext[:, KV_WIDTH:]
    ka, kb = split_heads(k_ext)
    va, vb = split_heads(v_ext)
    kma, kmb = split_heads(kvm_ref[:, :KV_WIDTH] * jnp.asarray(scale, BF16))
    vma, vmb = split_heads(kvm_ref[:, KV_WIDTH:])
    km_cat = jnp.concatenate([kma, kmb], axis=0)
    vm_cat = jnp.concatenate([vma, vmb], axis=0)

    mlane = lax.broadcasted_iota(jnp.int32, (1, 2 * N_META), 1)
    m_first = mlane < N_META
    sink = sink_ref[...]
    lane_o = lax.broadcasted_iota(jnp.int32, (1, LANES), 1) < HEAD_DIM

    for j in range(blocks):
        first = jnp.logical_and(seq_first, j == 0)
        last = jnp.logical_and(seq_last, j == blocks - 1)
        variant = jnp.where(first, 0, jnp.where(last, 2, 1))
        mvariant = jnp.where(first, 0, 1)

        r0 = j * BLOCK
        qb = q_ref[r0:r0 + BLOCK, :]
        q4 = jnp.concatenate([qb[:, p * LANES:(p + 1) * LANES] for p in range(N_PAIRS)], axis=0)
        k_cat = jnp.concatenate([ka[r0:r0 + 3 * BLOCK], kb[r0:r0 + 3 * BLOCK]], axis=0)
        v_cat = jnp.concatenate([va[r0:r0 + 3 * BLOCK], vb[r0:r0 + 3 * BLOCK]], axis=0)

        s = _dot_nt(q4, k_cat) + bias_ref[variant]
        sm = _dot_nt(q4, km_cat) + mbias_ref[mvariant]

        s_a, s_b = s[:, :3 * BLOCK], s[:, 3 * BLOCK:]
        sm_a = jnp.where(m_first, sm, NEG)
        sm_b = jnp.where(m_first, NEG, sm)
        m_a = jnp.maximum(jnp.maximum(jnp.max(s_a, axis=1, keepdims=True),
                                      jnp.max(sm_a, axis=1, keepdims=True)), sink[:, 0:1])
        m_b = jnp.maximum(jnp.maximum(jnp.max(s_b, axis=1, keepdims=True),
                                      jnp.max(sm_b, axis=1, keepdims=True)), sink[:, 1:2])
        p_a = jnp.exp(s_a - m_a)
        p_b = jnp.exp(s_b - m_b)
        pm = jnp.exp(jnp.where(m_first, sm - m_a, sm - m_b))
        l_a = (jnp.sum(p_a, axis=1, keepdims=True) + jnp.sum(jnp.where(m_first, pm, 0.0), axis=1, keepdims=True)
               + jnp.exp(sink[:, 0:1] - m_a))
        l_b = (jnp.sum(p_b, axis=1, keepdims=True) + jnp.sum(jnp.where(m_first, 0.0, pm), axis=1, keepdims=True)
               + jnp.exp(sink[:, 1:2] - m_b))
        p = jnp.concatenate([p_a, p_b], axis=1).astype(BF16)
        o = _dot(p, v_cat) + _dot(pm.astype(BF16), vm_cat)
        o = o * jnp.where(lane_o, 1.0 / l_a, 1.0 / l_b)
        for pr in range(N_PAIRS):
            o_ref[r0:r0 + BLOCK, pr * LANES:(pr + 1) * LANES] = o[pr * BLOCK:(pr + 1) * BLOCK].astype(BF16)


def _attn(q, kv, geom, kv_meta, bias, mbias, sink_tab, tile):
    T = q.shape[0]
    bpt = tile // BLOCK
    n_blocks = T // BLOCK
    full = lambda a: pl.BlockSpec(a.shape, lambda t: (0,) * a.ndim)
    return pl.pallas_call(
        functools.partial(_attn_kernel, tile=tile, geom=geom),
        grid=(T // tile,),
        in_specs=[
            pl.BlockSpec((tile, ATT_WIDTH), lambda t: (t, 0)),
            pl.BlockSpec((BLOCK, 2 * KV_WIDTH), lambda t: (jnp.maximum(t * bpt - 1, 0), 0)),
            pl.BlockSpec((tile, 2 * KV_WIDTH), lambda t: (t, 0)),
            pl.BlockSpec((BLOCK, 2 * KV_WIDTH), lambda t: (jnp.minimum((t + 1) * bpt, n_blocks - 1), 0)),
            full(kv_meta), full(bias), full(mbias), full(sink_tab),
        ],
        out_specs=pl.BlockSpec((tile, ATT_WIDTH), lambda t: (t, 0)),
        out_shape=jax.ShapeDtypeStruct((T, ATT_WIDTH), BF16),
        compiler_params=pltpu.CompilerParams(dimension_semantics=("parallel",), vmem_limit_bytes=VMEM_LIMIT),
        name="attn",
    )(q, kv, kv, kv, kv_meta, bias, mbias, sink_tab)


def _conv_kernel(zp_ref, zc_ref, zn_ref, zm_ref, w_ref, cb_ref, g_ref, b_ref, o_ref, ext_ref, sh_ref, y_ref,
                 *, tile, geom):
    i = pl.program_id(0)
    _, seq_first, seq_last = _seq_pos(i, geom)
    ext_ref[0:CONV_HALO, :] = jnp.where(seq_first, zm_ref[...], zp_ref[...]).astype(F32)
    ext_ref[CONV_HALO:CONV_HALO + tile, :] = zc_ref[...].astype(F32)
    ext_ref[CONV_HALO + tile:, :] = jnp.where(seq_last, 0.0, zn_ref[...].astype(F32))
    off = CONV_HALO - CONV_K // 2
    reach = (off + CONV_K - 1) // 8 * 8
    for p in range(1, 8):
        for r0 in range(0, tile + reach, SHIFT_ROWS):
            n = min(SHIFT_ROWS, tile + reach - r0)
            sh_ref[p - 1, r0:r0 + n, :] = ext_ref[r0 + p:r0 + p + n, :]

    def taps(r0, cs):
        acc = jnp.zeros((CONV_ROWS, LANES), F32)
        for k in range(CONV_K):
            p, a = (off + k) % 8, (off + k) // 8 * 8
            rows = slice(r0 + a, r0 + a + CONV_ROWS)
            win = ext_ref[rows, cs] if p == 0 else sh_ref[p - 1, rows, cs]
            acc = acc + win * w_ref[k:k + 1, cs]
        y_ref[r0:r0 + CONV_ROWS, cs] = acc

    for c in range(CONV_WIDTH // LANES):
        for r in range(tile // CONV_ROWS):
            pl.when(i >= 0)(functools.partial(taps, r * CONV_ROWS, slice(c * LANES, (c + 1) * LANES)))
    cb, g, b = cb_ref[...], g_ref[...], b_ref[...]
    for r in range(tile // LN_ROWS):
        r0 = r * LN_ROWS
        y = _layer_norm(y_ref[r0:r0 + LN_ROWS, :] + cb, g, b)
        o_ref[r0:r0 + LN_ROWS, :] = (y * jax.nn.sigmoid(y)).astype(BF16)


def _conv(zc, geom, z_meta, conv_w, conv_b, ln_g, ln_b, tile):
    T, C = zc.shape
    hpt = tile // CONV_HALO
    n_halo = T // CONV_HALO
    full = lambda a: pl.BlockSpec(a.shape, lambda t: (0,) * a.ndim)
    return pl.pallas_call(
        functools.partial(_conv_kernel, tile=tile, geom=geom),
        grid=(T // tile,),
        in_specs=[
            pl.BlockSpec((CONV_HALO, C), lambda t: (jnp.maximum(t * hpt - 1, 0), 0)),
            pl.BlockSpec((tile, C), lambda t: (t, 0)),
            pl.BlockSpec((CONV_HALO, C), lambda t: (jnp.minimum((t + 1) * hpt, n_halo - 1), 0)),
            full(z_meta), full(conv_w), full(conv_b), full(ln_g), full(ln_b),
        ],
        out_specs=pl.BlockSpec((tile, C), lambda t: (t, 0)),
        out_shape=jax.ShapeDtypeStruct((T, C), BF16),
        scratch_shapes=[pltpu.VMEM((tile + 2 * CONV_HALO, C), F32),
                        pltpu.VMEM((7, tile + 2 * CONV_HALO - 8, C), F32),
                        pltpu.VMEM((tile, C), F32)],
        compiler_params=pltpu.CompilerParams(dimension_semantics=("parallel",), vmem_limit_bytes=VMEM_LIMIT),
        name="conv",
    )(zc, zc, zc, z_meta, conv_w, conv_b, ln_g, ln_b)


def _route(r):
    lane_i = lax.broadcasted_iota(jnp.int32, r.shape, 1)
    lane = lane_i.astype(F32)
    big = float(1 << 20)
    is_g = jnp.logical_and(lane_i >= N_EXPERTS, lane_i < N_EXPERTS + N_GROUPS)
    lg = jnp.where(is_g, r, -jnp.inf)
    mg = jnp.max(lg, axis=1, keepdims=True)
    g_w = 1.0 / jnp.sum(jnp.exp(lg - mg), axis=1, keepdims=True)
    g_idx = jnp.min(jnp.where(lg == mg, lane - N_EXPERTS, big), axis=1, keepdims=True)
    lane_group = jnp.right_shift(lane_i, 3).astype(F32)
    in_group = jnp.logical_and(lane_i < N_EXPERTS, lane_group == g_idx)
    le = jnp.where(in_group, r, -jnp.inf)
    m1 = jnp.max(le, axis=1, keepdims=True)
    den = jnp.sum(jnp.exp(le - m1), axis=1, keepdims=True)
    i1 = jnp.min(jnp.where(le == m1, lane, big), axis=1, keepdims=True)
    le2 = jnp.where(lane == i1, -jnp.inf, le)
    m2 = jnp.max(le2, axis=1, keepdims=True)
    i2 = jnp.min(jnp.where(le2 == m2, lane, big), axis=1, keepdims=True)
    p1 = 1.0 / den
    p2 = jnp.exp(m2 - m1) / den
    tot = p1 + p2
    return i1, i2, g_w * (p1 / tot), g_w * (p2 / tot)


def _pack_bf16_pairs(x):
    half = x.shape[1] // 2
    words = []
    for j in range(half // LANES):
        lo = pltpu.bitcast(x[:, j * LANES:(j + 1) * LANES].astype(BF16).astype(F32), jnp.uint32)
        hi = pltpu.bitcast(x[:, half + j * LANES:half + (j + 1) * LANES].astype(BF16).astype(F32), jnp.uint32)
        words.append(hi | (lo >> 16))
    return words


def _unpack_bf16_pairs(words):
    lo = [pltpu.bitcast(w << 16, F32) for w in words]
    hi = [pltpu.bitcast(w & jnp.uint32(0xFFFF0000), F32) for w in words]
    return jnp.concatenate(lo + hi, axis=1)


def _out_kernel(h0_ref, att_ref, cz_ref, wg_ref, watt_ref, wco_ref, wout_ref,
                l1g_ref, l1b_ref, wr_ref, br_ref, before_ref, h1_ref, h1p_ref, rt_ref, fld_ref, cnt_ref, *, tile):
    @pl.when(pl.program_id(0) == 0)
    def _():
        cnt_ref[...] = jnp.zeros_like(cnt_ref)

    h0 = h0_ref[...]
    hb = h0.astype(BF16)
    g_att = jax.nn.sigmoid(_dot(hb, wg_ref[:, :D_MODEL]))
    mix = g_att * _dot(att_ref[...], watt_ref[...])
    g_conv = jax.nn.sigmoid(_dot(hb, wg_ref[:, D_MODEL:]))
    mix = mix + g_conv * _dot(cz_ref[...], wco_ref[...])
    m = _dot(mix.astype(BF16), wout_ref[...])
    h1 = _layer_norm(ALPHA * h0 + m, l1g_ref[...], l1b_ref[...])
    h1_ref[...] = h1
    for j, w in enumerate(_pack_bf16_pairs(h1)):
        h1p_ref[:, j, :, :] = w.reshape(tile // 8, 8, LANES)

    r = _dot(h1.astype(BF16), wr_ref[...]) + br_ref[...]
    i1, i2, w1, w2 = _route(r)
    lane = lax.broadcasted_iota(jnp.int32, (tile, ROUTER_LANES), 1)
    lane_f = lane.astype(F32)
    hit1, hit2 = lane_f == i1, lane_f == i2
    onehot = jnp.where(jnp.logical_or(hit1, hit2), 1.0, 0.0)
    seen = _dot(before_ref[...], onehot.astype(BF16)) + cnt_ref[0:1, :]
    rank1 = jnp.sum(jnp.where(hit1, seen, 0.0), axis=1, keepdims=True)
    rank2 = jnp.sum(jnp.where(hit2, seen, 0.0), axis=1, keepdims=True)
    cnt_ref[...] = cnt_ref[...] + jnp.sum(onehot, axis=0, keepdims=True)
    fields = (i1, i2, w1, w2, rank1, rank2)
    rt = jnp.zeros((tile, ROUTER_LANES), F32)
    for k, v in enumerate(fields):
        rt = jnp.where(lane == k, v, rt)
    rt_ref[...] = rt
    fld_ref[...] = jnp.transpose(rt)[0:ROUTE_FIELDS, :]


def _out(h0, att, cz, wg, watt, wco, wout, l1g, l1b, wr, br, tile, first_tile, n_tiles):
    T, D = n_tiles * tile, D_MODEL
    own = lambda w: pl.BlockSpec((tile, w), lambda i: (i, 0))
    flat = lambda w: pl.BlockSpec((tile, w), lambda i: (first_tile + i, 0))
    full = lambda a: pl.BlockSpec(a.shape, lambda i: (0,) * a.ndim)
    before = (jnp.arange(tile)[None, :] < jnp.arange(tile)[:, None]).astype(BF16)
    return pl.pallas_call(
        functools.partial(_out_kernel, tile=tile),
        grid=(T // tile,),
        in_specs=[flat(D), flat(ATT_WIDTH), flat(CONV_WIDTH), full(wg),
                  full(watt), full(wco), full(wout), full(l1g), full(l1b), full(wr), full(br), full(before)],
        out_specs=[own(D),
                   pl.BlockSpec((tile // 8, PACK_WORDS, 8, LANES), lambda i: (i, 0, 0, 0)),
                   own(ROUTER_LANES),
                   pl.BlockSpec((ROUTE_FIELDS, tile), lambda i: (0, i)),
                   pl.BlockSpec((8, ROUTER_LANES), lambda i: (0, 0))],
        out_shape=[jax.ShapeDtypeStruct((T, D), F32),
                   jax.ShapeDtypeStruct((T // 8, PACK_WORDS, 8, LANES), jnp.uint32),
                   jax.ShapeDtypeStruct((T, ROUTER_LANES), F32),
                   jax.ShapeDtypeStruct((ROUTE_FIELDS, T), F32),
                   jax.ShapeDtypeStruct((8, ROUTER_LANES), F32)],
        compiler_params=pltpu.CompilerParams(dimension_semantics=("arbitrary",), vmem_limit_bytes=VMEM_LIMIT),
        name="out",
    )(h0, att, cz, wg, watt, wco, wout, l1g, l1b, wr, br, before)


def _sc_scatter2(src, idx_a, idx_b, n_out):
    m = src.shape[0]
    mesh = plsc.VectorSubcoreMesh(core_axis_name="c", subcore_axis_name="s")

    @functools.partial(pl.kernel, out_type=jax.ShapeDtypeStruct((n_out, LANES), src.dtype), mesh=mesh)
    def k(x_hbm, ia_hbm, ib_hbm, o_hbm):
        def body(x_vmem, ia_vmem, ib_vmem):
            pltpu.sync_copy(x_vmem, o_hbm.at[ia_vmem.at[0]])
            pltpu.sync_copy(x_vmem, o_hbm.at[ib_vmem.at[0]])

        pltpu.emit_pipeline(
            body, grid=(m // SC_WINDOW,),
            in_specs=[pl.BlockSpec((SC_WINDOW, LANES), index_map=lambda i: (i, 0)),
                      pl.BlockSpec((1, SC_WINDOW), index_map=lambda i: (0, i)),
                      pl.BlockSpec((1, SC_WINDOW), index_map=lambda i: (0, i))],
            out_specs=[],
            core_axis_name=("c", "s"), dimension_semantics=(pltpu.PARALLEL,),
        )(x_hbm, ia_hbm, ib_hbm)

    return k(src, idx_a.reshape(1, m), idx_b.reshape(1, m))


def _sc_gather(table, idx):
    m = idx.shape[0]
    mesh = plsc.VectorSubcoreMesh(core_axis_name="c", subcore_axis_name="s")

    @functools.partial(pl.kernel, out_type=jax.ShapeDtypeStruct((m, LANES), table.dtype), mesh=mesh)
    def k(x_hbm, i_hbm, o_hbm):
        def body(i_vmem, o_vmem):
            pltpu.sync_copy(x_hbm.at[i_vmem.at[0]], o_vmem)

        pltpu.emit_pipeline(
            body, grid=(m // SC_WINDOW,),
            in_specs=[pl.BlockSpec((1, SC_WINDOW), index_map=lambda i: (0, i))],
            out_specs=[pl.BlockSpec((SC_WINDOW, LANES), index_map=lambda i: (i, 0))],
            core_axis_name=("c", "s"), dimension_semantics=(pltpu.PARALLEL,),
        )(i_hbm, o_hbm)

    return k(table, idx.reshape(1, m))


def _expert_kernel(te_ref, tv_ref, xs_ref, wg_ref, wu_ref, wd_ref, ys_ref, wgb_ref, wub_ref, wdb_ref, *, tile):
    n = pl.program_id(0)
    valid = tv_ref[n]

    @pl.when(jnp.logical_or(n == 0, te_ref[n] != te_ref[jnp.maximum(n - 1, 0)]))
    def _():
        wgb_ref[...] = wg_ref[...].astype(BF16)
        wub_ref[...] = wu_ref[...].astype(BF16)
        wdb_ref[...] = wd_ref[...].astype(BF16)

    @pl.when(valid > 0)
    def _():
        x = _unpack_bf16_pairs([xs_ref[:, j, :, :].reshape(tile, LANES) for j in range(PACK_WORDS)])
        rows = lax.broadcasted_iota(jnp.int32, (tile, 1), 0)
        x = jnp.where(rows < valid, x, 0.0).astype(BF16)
        gt = _dot(x, wgb_ref[...])
        up = _dot(x, wub_ref[...])
        hid = (gt * jax.nn.sigmoid(gt)) * up
        y = _dot(hid.astype(BF16), wdb_ref[...])
        for j, w in enumerate(_pack_bf16_pairs(y)):
            ys_ref[:, j, :, :] = w.reshape(tile // 8, 8, LANES)

    @pl.when(valid <= 0)
    def _():
        ys_ref[...] = jnp.zeros_like(ys_ref)


def _experts(tile_expert, tile_valid, xs, w_gate, w_up, w_down, tile):
    n_tiles = xs.shape[0] * 8 // tile
    blk = pl.BlockSpec((tile // 8, PACK_WORDS, 8, LANES), lambda n, te, tv: (n, 0, 0, 0))
    return pl.pallas_call(
        functools.partial(_expert_kernel, tile=tile),
        grid_spec=pltpu.PrefetchScalarGridSpec(
            num_scalar_prefetch=2,
            grid=(n_tiles,),
            in_specs=[blk,
                      pl.BlockSpec((None, D_MODEL, D_EXPERT), lambda n, te, tv: (te[n], 0, 0)),
                      pl.BlockSpec((None, D_MODEL, D_EXPERT), lambda n, te, tv: (te[n], 0, 0)),
                      pl.BlockSpec((None, D_EXPERT, D_MODEL), lambda n, te, tv: (te[n], 0, 0))],
            out_specs=blk,
            scratch_shapes=[pltpu.VMEM((D_MODEL, D_EXPERT), BF16), pltpu.VMEM((D_MODEL, D_EXPERT), BF16),
                            pltpu.VMEM((D_EXPERT, D_MODEL), BF16)],
        ),
        out_shape=jax.ShapeDtypeStruct(xs.shape, jnp.uint32),
        compiler_params=pltpu.CompilerParams(
            dimension_semantics=("arbitrary",), vmem_limit_bytes=VMEM_LIMIT),
        name="experts",
    )(tile_expert, tile_valid, xs, w_gate, w_up, w_down)


def _final_kernel(h_ref, g_ref, rt_ref, l2g_ref, l2b_ref, o_ref, *, tile):
    rt = rt_ref[...]
    lane = lax.broadcasted_iota(jnp.int32, rt.shape, 1)
    w1 = jnp.sum(jnp.where(lane == 2, rt, 0.0), axis=1, keepdims=True)
    w2 = jnp.sum(jnp.where(lane == 3, rt, 0.0), axis=1, keepdims=True)
    y1 = _unpack_bf16_pairs([g_ref[0, :, j, :, :].reshape(tile, LANES) for j in range(PACK_WORDS)])
    y2 = _unpack_bf16_pairs([g_ref[1, :, j, :, :].reshape(tile, LANES) for j in range(PACK_WORDS)])
    f = w1 * y1 + w2 * y2
    o_ref[...] = _layer_norm(ALPHA * h_ref[...] + f, l2g_ref[...], l2b_ref[...])


def _final(h1, g, rt, l2g, l2b, tile):
    T, D = h1.shape
    row = lambda w: pl.BlockSpec((tile, w), lambda i: (i, 0))
    full = lambda a: pl.BlockSpec(a.shape, lambda i: (0,) * a.ndim)
    return pl.pallas_call(
        functools.partial(_final_kernel, tile=tile),
        grid=(T // tile,),
        in_specs=[row(D),
                  pl.BlockSpec((2, tile // 8, PACK_WORDS, 8, LANES), lambda i: (0, i, 0, 0, 0)),
                  row(ROUTER_LANES), full(l2g), full(l2b)],
        out_specs=row(D),
        out_shape=jax.ShapeDtypeStruct((T, D), F32),
        compiler_params=pltpu.CompilerParams(dimension_semantics=("parallel",), vmem_limit_bytes=VMEM_LIMIT),
        name="final",
    )(h1, g, rt, l2g, l2b)


def _dispatch_plan(fields, counts, n_tokens, tile):
    cnt = counts[0, :N_EXPERTS].astype(jnp.int32)
    padded = (cnt + tile - 1) // tile * tile
    base = jnp.cumsum(padded) - padded
    e_ids = jnp.arange(N_EXPERTS, dtype=jnp.int32)

    def dest(row_e, row_r):
        e = fields[row_e].astype(jnp.int32)
        seg = jnp.sum(jnp.where(e[None, :] == e_ids[:, None], base[:, None], 0), axis=0)
        pos = seg + fields[row_r].astype(jnp.int32)
        p = pos.reshape(n_tokens // 8, 1, 8)
        j = jnp.arange(PACK_WORDS, dtype=jnp.int32).reshape(1, PACK_WORDS, 1)
        return ((p // 8) * (8 * PACK_WORDS) + j * 8 + p % 8).reshape(-1)

    n_tiles = (2 * n_tokens) // tile + N_EXPERTS
    start = jnp.arange(n_tiles, dtype=jnp.int32) * tile
    seg_end = base + padded
    te = jnp.minimum(jnp.sum((start[:, None] >= seg_end[None, :]).astype(jnp.int32), axis=1), N_EXPERTS - 1)
    te_base = jnp.sum(jnp.where(te[:, None] == e_ids[None, :], base[None, :], 0), axis=1)
    te_cnt = jnp.sum(jnp.where(te[:, None] == e_ids[None, :], cnt[None, :], 0), axis=1)
    tv = jnp.clip(te_cnt - (start - te_base), 0, tile)
    return dest(0, 4), dest(1, 5), te, tv, n_tiles


def _t5_bucket(rel):
    half = N_BUCKETS // 2
    max_exact = half // 2
    ret = jnp.where(rel > 0, half, 0)
    n = jnp.abs(rel)
    nf = jnp.maximum(n, 1).astype(F32)
    large = max_exact + (jnp.log(nf / max_exact) / math.log(MAX_DISTANCE / max_exact)
                         * (half - max_exact)).astype(jnp.int32)
    large = jnp.minimum(large, half - 1)
    return ret + jnp.where(n < max_exact, n, large)


def _bucket_bias(rel_bias, bucket):
    rb = rel_bias.astype(F32)
    out = jnp.zeros((N_Q_HEADS,) + bucket.shape, F32)
    for b in range(N_BUCKETS):
        out = out + jnp.where(bucket[None] == b, rb[b][:, None, None], 0.0)
    return out


def _pair_rows(t):
    return jnp.concatenate([t[:N_PAIRS], t[N_PAIRS:]], axis=-1).reshape(N_PAIRS * BLOCK, -1)


def _bias_tables(rel_bias, sink):
    qi = jnp.arange(BLOCK)
    kj = jnp.arange(3 * BLOCK) - BLOCK
    rel = kj[None, :] - qi[:, None]
    band = _bucket_bias(rel_bias, _t5_bucket(rel))
    in_win = (jnp.abs(rel) <= WINDOW)[None]
    not_prev = (kj >= 0)[None, None, :]
    not_next = (kj < BLOCK)[None, None, :]
    variants = [jnp.where(in_win & not_prev, band, NEG),
                jnp.where(in_win, band, NEG),
                jnp.where(in_win & not_next, band, NEG)]
    bias = jnp.stack([_pair_rows(v) for v in variants])

    off = BLOCK - N_META
    mvars = []
    for blk in (1, 2):
        qpos = blk * BLOCK + qi - off
        meta_rel = jnp.arange(N_META)[None, :] - qpos[:, None]
        mvars.append(_pair_rows(_bucket_bias(rel_bias, _t5_bucket(meta_rel))))
    mbias = jnp.stack(mvars)
    s = sink.astype(F32)
    sink_tab = jnp.repeat(jnp.stack([s[:N_PAIRS], s[N_PAIRS:]], axis=-1), BLOCK, axis=0)
    return bias, mbias, sink_tab


def kernel(x_prompt, x_sample, meta, ln_in_g, ln_in_b, rel_bias, w_in, w_att_branch, sink, conv_w, conv_b,
           conv_ln_g, conv_ln_b, w_conv_out, w_out, ln1_g, ln1_b, w_group, b_group, w_router, b_router,
           w_gate, w_up, w_down, ln2_g, ln2_b):
    row = lambda v: v.reshape(1, -1).astype(F32)
    w = w_in[0]
    wq = (w[:, :Q_END].reshape(D_MODEL, 2, N_PAIRS, HEAD_DIM).transpose(0, 2, 1, 3)
          .reshape(D_MODEL, ATT_WIDTH).astype(BF16))
    watt = (w_att_branch[0].reshape(2, N_PAIRS, HEAD_DIM, D_MODEL).transpose(1, 0, 2, 3)
            .reshape(ATT_WIDTH, D_MODEL).astype(BF16))
    wkv = w[:, Q_END:V_END].astype(BF16)
    wglu = w[:, V_END:GLU_END].astype(BF16)
    wg = w[:, GLU_END:].astype(BF16)
    wco = w_conv_out[0].astype(BF16)
    wout = w_out[0].astype(BF16)
    wr = jnp.zeros((D_MODEL, ROUTER_LANES), F32)
    wr = wr.at[:, :N_EXPERTS].set(w_router[0]).at[:, N_EXPERTS:N_EXPERTS + N_GROUPS].set(w_group[0]).astype(BF16)
    br = jnp.zeros((1, ROUTER_LANES), F32)
    br = br.at[0, :N_EXPERTS].set(b_router[0]).at[0, N_EXPERTS:N_EXPERTS + N_GROUPS].set(b_group[0])
    ln_g, ln_b = row(ln_in_g), row(ln_in_b)
    bias, mbias, sink_tab = _bias_tables(rel_bias, sink[0])

    xm = jnp.concatenate([jnp.zeros((BLOCK - N_META, D_MODEL), F32), meta.astype(F32)], axis=0)
    _, _, kv_m, zc_m = _proj(xm, xm, _Geom(1, 1, 1, 1), ln_g, ln_b, wq, wkv, wglu, BLOCK)
    kv_meta = kv_m[BLOCK - N_META:BLOCK]
    z_meta = zc_m[BLOCK - N_META:BLOCK]

    (bp, sp, _), (bs, ss, _) = x_prompt.shape, x_sample.shape
    xp, xs = x_prompt.reshape(bp * sp, D_MODEL), x_sample.reshape(bs * ss, D_MODEL)
    geom = _geom(x_prompt, x_sample, TILE)
    h0, q, kv, zc = _proj(xp, xs, geom, ln_g, ln_b, wq, wkv, wglu, TILE)
    att = _attn(q, kv, geom, kv_meta, bias, mbias, sink_tab, TILE)
    cz = _conv(zc, geom, z_meta, conv_w[0], row(conv_b[0]), row(conv_ln_g[0]), row(conv_ln_b[0]), TILE)
    l1g, l1b, l2g, l2b = row(ln1_g[0]), row(ln1_b[0]), row(ln2_g[0]), row(ln2_b[0])

    def moe(first_tile, n_tiles):
        n = n_tiles * TILE
        h1, h1p, rt, fields, counts = _out(h0, att, cz, wg, watt, wco, wout, l1g, l1b, wr, br,
                                           TILE, first_tile, n_tiles)
        idx1, idx2, tile_expert, tile_valid, n_tiles = _dispatch_plan(fields, counts, n, TILE_EXPERT)
        n_rows = n_tiles * TILE_EXPERT
        xsorted = _sc_scatter2(h1p.reshape(n * PACK_WORDS, LANES), idx1, idx2, n_rows * PACK_WORDS)
        ys = _experts(tile_expert, tile_valid, xsorted.reshape(n_rows // 8, PACK_WORDS, 8, LANES),
                      w_gate[0], w_up[0], w_down[0], TILE_EXPERT)
        g = _sc_gather(ys.reshape(n_rows * PACK_WORDS, LANES), jnp.concatenate([idx1, idx2]))
        return _final(h1, g.reshape(2, n // 8, PACK_WORDS, 8, LANES), rt, l2g, l2b, TILE)

    return moe(0, geom.n_p).reshape(x_prompt.shape), moe(geom.n_p, geom.n_s).reshape(x_sample.shape)
```

```python
import functools
import math
from typing import NamedTuple

import jax
import jax.numpy as jnp
from jax import lax
from jax.experimental import pallas as pl
from jax.experimental.pallas import tpu as pltpu
from jax.experimental.pallas import tpu_sc as plsc

D_MODEL = 1024
N_META = 16
BLOCK = 128
WINDOW = 128
N_Q_HEADS = 8
N_KV_HEADS = 2
HEAD_DIM = 64
ATT_WIDTH = N_Q_HEADS * HEAD_DIM
KV_WIDTH = N_KV_HEADS * HEAD_DIM
CONV_WIDTH = D_MODEL // 2
CONV_K = 31
N_BUCKETS = 32
MAX_DISTANCE = 128
N_GROUPS = 4
EXPERTS_PER_GROUP = 8
N_EXPERTS = N_GROUPS * EXPERTS_PER_GROUP
D_EXPERT = 256
LN_EPS = 1e-5
DEPTH = 1
ALPHA = (2 * DEPTH) ** 0.25
NEG = -1e30
Q_END = ATT_WIDTH
K_END = Q_END + KV_WIDTH
V_END = K_END + KV_WIDTH
GLU_END = V_END + 2 * CONV_WIDTH
GA_END = GLU_END + D_MODEL

N_PAIRS = N_Q_HEADS // 2
LANES = 128
CONV_HALO = 16
ROUTER_LANES = 128

TILE = 512
CONV_ROWS = 128
LN_ROWS = 64
SHIFT_ROWS = 128
TILE_ATTN = 1024
TILE_EXPERT = 512
PACK_WORDS = 4
SC_WINDOW = 128
ROUTE_FIELDS = 8
VMEM_LIMIT = 56 * 1024 * 1024

BF16 = jnp.bfloat16
F32 = jnp.float32


def _layer_norm(x, g, b):
    mu = jnp.mean(x, axis=-1, keepdims=True)
    xc = x - mu
    var = jnp.mean(xc * xc, axis=-1, keepdims=True)
    return xc * lax.rsqrt(var + LN_EPS) * g + b


def _dot(a, b):
    return jnp.dot(a, b, preferred_element_type=F32)


def _dot_nt(a, b):
    return lax.dot_general(a, b, (((1,), (1,)), ((), ())), preferred_element_type=F32)


class _Geom(NamedTuple):
    n_p: int
    n_s: int
    tp: int
    ts: int


def _geom(x_prompt, x_sample, tile):
    (bp, sp, _), (bs, ss, _) = x_prompt.shape, x_sample.shape
    return _Geom(bp * sp // tile, bs * ss // tile, sp // tile, ss // tile)


def _seq_pos(t, g):
    is_p = t < g.n_p
    local = jnp.where(is_p, lax.rem(t, g.tp), lax.rem(jnp.maximum(t - g.n_p, 0), g.ts))
    return is_p, local == 0, local == jnp.where(is_p, g.tp - 1, g.ts - 1)


def _x_specs(g, tile, width):
    return [pl.BlockSpec((tile, width), lambda t: (jnp.minimum(t, g.n_p - 1), 0)),
            pl.BlockSpec((tile, width), lambda t: (jnp.maximum(t - g.n_p, 0), 0))]


def _proj_kernel(xp_ref, xs_ref, g_ref, b_ref, wq_ref, wkv_ref, wglu_ref, h0_ref, q_ref, kv_ref, zc_ref, *, geom):
    x = jnp.where(pl.program_id(0) < geom.n_p, xp_ref[...], xs_ref[...])
    h0 = _layer_norm(x, g_ref[...], b_ref[...])
    h0_ref[...] = h0
    h = h0.astype(BF16)
    q_ref[...] = _dot(h, wq_ref[...]).astype(BF16)
    kv_ref[...] = _dot(h, wkv_ref[...]).astype(BF16)
    u = _dot(h, wglu_ref[...])
    zc_ref[...] = (u[:, :CONV_WIDTH] * jax.nn.sigmoid(u[:, CONV_WIDTH:])).astype(BF16)


def _proj(xp, xs, geom, ln_g, ln_b, wq, wkv, wglu, tile):
    T = (geom.n_p + geom.n_s) * tile
    row = lambda w: pl.BlockSpec((tile, w), lambda t: (t, 0))
    full = lambda a: pl.BlockSpec(a.shape, lambda t: (0,) * a.ndim)
    return pl.pallas_call(
        functools.partial(_proj_kernel, geom=geom),
        grid=(geom.n_p + geom.n_s,),
        in_specs=_x_specs(geom, tile, D_MODEL) + [full(ln_g), full(ln_b), full(wq), full(wkv), full(wglu)],
        out_specs=[row(D_MODEL), row(ATT_WIDTH), row(2 * KV_WIDTH), row(CONV_WIDTH)],
        out_shape=[
            jax.ShapeDtypeStruct((T, D_MODEL), F32),
            jax.ShapeDtypeStruct((T, ATT_WIDTH), BF16),
            jax.ShapeDtypeStruct((T, 2 * KV_WIDTH), BF16),
            jax.ShapeDtypeStruct((T, CONV_WIDTH), BF16),
        ],
        compiler_params=pltpu.CompilerParams(dimension_semantics=("parallel",), vmem_limit_bytes=VMEM_LIMIT),
        name="proj",
    )(xp, xs, ln_g, ln_b, wq, wkv, wglu)


def _attn_kernel(q_ref, kvp_ref, kvc_ref, kvn_ref, kvm_ref, bias_ref, mbias_ref, sink_ref, o_ref, *, tile, geom):
    _, seq_first, seq_last = _seq_pos(pl.program_id(0), geom)
    blocks = tile // BLOCK
    scale = HEAD_DIM ** -0.5

    lane = lax.broadcasted_iota(jnp.int32, (1, LANES), 1)
    lo = lane < HEAD_DIM

    def split_heads(t):
        z = jnp.zeros_like(t)
        return jnp.where(lo, t, z), jnp.where(lo, z, t)

    kv_ext = jnp.concatenate([kvp_ref[...], kvc_ref[...], kvn_ref[...]], axis=0)
    k_ext = kv_ext[:, :KV_WIDTH] * jnp.asarray(scale, BF16)
    v_ext = kv_ext[:, KV_WIDTH:]
    ka, kb = split_heads(k_ext)
    va, vb = split_heads(v_ext)
    kma, kmb = split_heads(kvm_ref[:, :KV_WIDTH] * jnp.asarray(scale, BF16))
    vma, vmb = split_heads(kvm_ref[:, KV_WIDTH:])
    km_cat = jnp.concatenate([kma, kmb], axis=0)
    vm_cat = jnp.concatenate([vma, vmb], axis=0)

    mlane = lax.broadcasted_iota(jnp.int32, (1, 2 * N_META), 1)
    m_first = mlane < N_META
    sink = sink_ref[...]
    lane_o = lax.broadcasted_iota(jnp.int32, (1, LANES), 1) < HEAD_DIM

    for j in range(blocks):
        first = jnp.logical_and(seq_first, j == 0)
        last = jnp.logical_and(seq_last, j == blocks - 1)
        variant = jnp.where(first, 0, jnp.where(last, 2, 1))
        mvariant = jnp.where(first, 0, 1)

        r0 = j * BLOCK
        qb = q_ref[r0:r0 + BLOCK, :]
        q4 = jnp.concatenate([qb[:, p * LANES:(p + 1) * LANES] for p in range(N_PAIRS)], axis=0)
        k_cat = jnp.concatenate([ka[r0:r0 + 3 * BLOCK], kb[r0:r0 + 3 * BLOCK]], axis=0)
        v_cat = jnp.concatenate([va[r0:r0 + 3 * BLOCK], vb[r0:r0 + 3 * BLOCK]], axis=0)

        s = _dot_nt(q4, k_cat) + bias_ref[variant]
        sm = _dot_nt(q4, km_cat) + mbias_ref[mvariant]

        s_a, s_b = s[:, :3 * BLOCK], s[:, 3 * BLOCK:]
        sm_a = jnp.where(m_first, sm, NEG)
        sm_b = jnp.where(m_first, NEG, sm)
        m_a = jnp.maximum(jnp.maximum(jnp.max(s_a, axis=1, keepdims=True),
                                      jnp.max(sm_a, axis=1, keepdims=True)), sink[:, 0:1])
        m_b = jnp.maximum(jnp.maximum(jnp.max(s_b, axis=1, keepdims=True),
                                      jnp.max(sm_b, axis=1, keepdims=True)), sink[:, 1:2])
        p_a = jnp.exp(s_a - m_a)
        p_b = jnp.exp(s_b - m_b)
        pm = jnp.exp(jnp.where(m_first, sm - m_a, sm - m_b))
        l_a = (jnp.sum(p_a, axis=1, keepdims=True) + jnp.sum(jnp.where(m_first, pm, 0.0), axis=1, keepdims=True)
               + jnp.exp(sink[:, 0:1] - m_a))
        l_b = (jnp.sum(p_b, axis=1, keepdims=True) + jnp.sum(jnp.where(m_first, 0.0, pm), axis=1, keepdims=True)
               + jnp.exp(sink[:, 1:2] - m_b))
        p = jnp.concatenate([p_a, p_b], axis=1).astype(BF16)
        o = _dot(p, v_cat) + _dot(pm.astype(BF16), vm_cat)
        o = o * jnp.where(lane_o, 1.0 / l_a, 1.0 / l_b)
        for pr in range(N_PAIRS):
            o_ref[r0:r0 + BLOCK, pr * LANES:(pr + 1) * LANES] = o[pr * BLOCK:(pr + 1) * BLOCK].astype(BF16)


def _attn(q, kv, geom, kv_meta, bias, mbias, sink_tab, tile):
    T = q.shape[0]
    bpt = tile // BLOCK
    n_blocks = T // BLOCK
    full = lambda a: pl.BlockSpec(a.shape, lambda t: (0,) * a.ndim)
    return pl.pallas_call(
        functools.partial(_attn_kernel, tile=tile, geom=geom),
        grid=(T // tile,),
        in_specs=[
            pl.BlockSpec((tile, ATT_WIDTH), lambda t: (t, 0)),
            pl.BlockSpec((BLOCK, 2 * KV_WIDTH), lambda t: (jnp.maximum(t * bpt - 1, 0), 0)),
            pl.BlockSpec((tile, 2 * KV_WIDTH), lambda t: (t, 0)),
            pl.BlockSpec((BLOCK, 2 * KV_WIDTH), lambda t: (jnp.minimum((t + 1) * bpt, n_blocks - 1), 0)),
            full(kv_meta), full(bias), full(mbias), full(sink_tab),
        ],
        out_specs=pl.BlockSpec((tile, ATT_WIDTH), lambda t: (t, 0)),
        out_shape=jax.ShapeDtypeStruct((T, ATT_WIDTH), BF16),
        compiler_params=pltpu.CompilerParams(dimension_semantics=("parallel",), vmem_limit_bytes=VMEM_LIMIT),
        name="attn",
    )(q, kv, kv, kv, kv_meta, bias, mbias, sink_tab)


def _conv_kernel(zp_ref, zc_ref, zn_ref, zm_ref, w_ref, cb_ref, g_ref, b_ref, o_ref, ext_ref, sh_ref, y_ref,
                 *, tile, geom):
    i = pl.program_id(0)
    _, seq_first, seq_last = _seq_pos(i, geom)
    ext_ref[0:CONV_HALO, :] = jnp.where(seq_first, zm_ref[...], zp_ref[...]).astype(F32)
    ext_ref[CONV_HALO:CONV_HALO + tile, :] = zc_ref[...].astype(F32)
    ext_ref[CONV_HALO + tile:, :] = jnp.where(seq_last, 0.0, zn_ref[...].astype(F32))
    off = CONV_HALO - CONV_K // 2
    reach = (off + CONV_K - 1) // 8 * 8
    for p in range(1, 8):
        for r0 in range(0, tile + reach, SHIFT_ROWS):
            n = min(SHIFT_ROWS, tile + reach - r0)
            sh_ref[p - 1, r0:r0 + n, :] = ext_ref[r0 + p:r0 + p + n, :]

    def taps(r0, cs):
        acc = jnp.zeros((CONV_ROWS, LANES), F32)
        for k in range(CONV_K):
            p, a = (off + k) % 8, (off + k) // 8 * 8
            rows = slice(r0 + a, r0 + a + CONV_ROWS)
            win = ext_ref[rows, cs] if p == 0 else sh_ref[p - 1, rows, cs]
            acc = acc + win * w_ref[k:k + 1, cs]
        y_ref[r0:r0 + CONV_ROWS, cs] = acc

    for c in range(CONV_WIDTH // LANES):
        for r in range(tile // CONV_ROWS):
            pl.when(i >= 0)(functools.partial(taps, r * CONV_ROWS, slice(c * LANES, (c + 1) * LANES)))
    cb, g, b = cb_ref[...], g_ref[...], b_ref[...]
    for r in range(tile // LN_ROWS):
        r0 = r * LN_ROWS
        y = _layer_norm(y_ref[r0:r0 + LN_ROWS, :] + cb, g, b)
        o_ref[r0:r0 + LN_ROWS, :] = (y * jax.nn.sigmoid(y)).astype(BF16)


def _conv(zc, geom, z_meta, conv_w, conv_b, ln_g, ln_b, tile):
    T, C = zc.shape
    hpt = tile // CONV_HALO
    n_halo = T // CONV_HALO
    full = lambda a: pl.BlockSpec(a.shape, lambda t: (0,) * a.ndim)
    return pl.pallas_call(
        functools.partial(_conv_kernel, tile=tile, geom=geom),
        grid=(T // tile,),
        in_specs=[
            pl.BlockSpec((CONV_HALO, C), lambda t: (jnp.maximum(t * hpt - 1, 0), 0)),
            pl.BlockSpec((tile, C), lambda t: (t, 0)),
            pl.BlockSpec((CONV_HALO, C), lambda t: (jnp.minimum((t + 1) * hpt, n_halo - 1), 0)),
            full(z_meta), full(conv_w), full(conv_b), full(ln_g), full(ln_b),
        ],
        out_specs=pl.BlockSpec((tile, C), lambda t: (t, 0)),
        out_shape=jax.ShapeDtypeStruct((T, C), BF16),
        scratch_shapes=[pltpu.VMEM((tile + 2 * CONV_HALO, C), F32),
                        pltpu.VMEM((7, tile + 2 * CONV_HALO - 8, C), F32),
                        pltpu.VMEM((tile, C), F32)],
        compiler_params=pltpu.CompilerParams(dimension_semantics=("parallel",), vmem_limit_bytes=VMEM_LIMIT),
        name="conv",
    )(zc, zc, zc, z_meta, conv_w, conv_b, ln_g, ln_b)


def _route(r):
    lane_i = lax.broadcasted_iota(jnp.int32, r.shape, 1)
    lane = lane_i.astype(F32)
    big = float(1 << 20)
    is_g = jnp.logical_and(lane_i >= N_EXPERTS, lane_i < N_EXPERTS + N_GROUPS)
    lg = jnp.where(is_g, r, -jnp.inf)
    mg = jnp.max(lg, axis=1, keepdims=True)
    g_w = 1.0 / jnp.sum(jnp.exp(lg - mg), axis=1, keepdims=True)
    g_idx = jnp.min(jnp.where(lg == mg, lane - N_EXPERTS, big), axis=1, keepdims=True)
    lane_group = jnp.right_shift(lane_i, 3).astype(F32)
    in_group = jnp.logical_and(lane_i < N_EXPERTS, lane_group == g_idx)
    le = jnp.where(in_group, r, -jnp.inf)
    m1 = jnp.max(le, axis=1, keepdims=True)
    den = jnp.sum(jnp.exp(le - m1), axis=1, keepdims=True)
    i1 = jnp.min(jnp.where(le == m1, lane, big), axis=1, keepdims=True)
    le2 = jnp.where(lane == i1, -jnp.inf, le)
    m2 = jnp.max(le2, axis=1, keepdims=True)
    i2 = jnp.min(jnp.where(le2 == m2, lane, big), axis=1, keepdims=True)
    p1 = 1.0 / den
    p2 = jnp.exp(m2 - m1) / den
    tot = p1 + p2
    return i1, i2, g_w * (p1 / tot), g_w * (p2 / tot)


def _pack_bf16_pairs(x):
    half = x.shape[1] // 2
    words = []
    for j in range(half // LANES):
        lo = pltpu.bitcast(x[:, j * LANES:(j + 1) * LANES].astype(BF16).astype(F32), jnp.uint32)
        hi = pltpu.bitcast(x[:, half + j * LANES:half + (j + 1) * LANES].astype(BF16).astype(F32), jnp.uint32)
        words.append(hi | (lo >> 16))
    return words


def _unpack_bf16_pairs(words):
    lo = [pltpu.bitcast(w << 16, F32) for w in words]
    hi = [pltpu.bitcast(w & jnp.uint32(0xFFFF0000), F32) for w in words]
    return jnp.concatenate(lo + hi, axis=1)


def _out_kernel(h0_ref, att_ref, cz_ref, wg_ref, watt_ref, wco_ref, wout_ref,
                l1g_ref, l1b_ref, wr_ref, br_ref, before_ref, h1_ref, h1p_ref, rt_ref, fld_ref, cnt_ref, *, tile):
    @pl.when(pl.program_id(0) == 0)
    def _():
        cnt_ref[...] = jnp.zeros_like(cnt_ref)

    h0 = h0_ref[...]
    hb = h0.astype(BF16)
    g_att = jax.nn.sigmoid(_dot(hb, wg_ref[:, :D_MODEL]))
    mix = g_att * _dot(att_ref[...], watt_ref[...])
    g_conv = jax.nn.sigmoid(_dot(hb, wg_ref[:, D_MODEL:]))
    mix = mix + g_conv * _dot(cz_ref[...], wco_ref[...])
    m = _dot(mix.astype(BF16), wout_ref[...])
    h1 = _layer_norm(ALPHA * h0 + m, l1g_ref[...], l1b_ref[...])
    h1_ref[...] = h1
    for j, w in enumerate(_pack_bf16_pairs(h1)):
        h1p_ref[:, j, :, :] = w.reshape(tile // 8, 8, LANES)

    r = _dot(h1.astype(BF16), wr_ref[...]) + br_ref[...]
    i1, i2, w1, w2 = _route(r)
    lane = lax.broadcasted_iota(jnp.int32, (tile, ROUTER_LANES), 1)
    lane_f = lane.astype(F32)
    hit1, hit2 = lane_f == i1, lane_f == i2
    onehot = jnp.where(jnp.logical_or(hit1, hit2), 1.0, 0.0)
    seen = _dot(before_ref[...], onehot.astype(BF16)) + cnt_ref[0:1, :]
    rank1 = jnp.sum(jnp.where(hit1, seen, 0.0), axis=1, keepdims=True)
    rank2 = jnp.sum(jnp.where(hit2, seen, 0.0), axis=1, keepdims=True)
    cnt_ref[...] = cnt_ref[...] + jnp.sum(onehot, axis=0, keepdims=True)
    fields = (i1, i2, w1, w2, rank1, rank2)
    rt = jnp.zeros((tile, ROUTER_LANES), F32)
    for k, v in enumerate(fields):
        rt = jnp.where(lane == k, v, rt)
    rt_ref[...] = rt
    fld_ref[...] = jnp.transpose(rt)[0:ROUTE_FIELDS, :]


def _out(h0, att, cz, wg, watt, wco, wout, l1g, l1b, wr, br, tile, first_tile, n_tiles):
    T, D = n_tiles * tile, D_MODEL
    own = lambda w: pl.BlockSpec((tile, w), lambda i: (i, 0))
    flat = lambda w: pl.BlockSpec((tile, w), lambda i: (first_tile + i, 0))
    full = lambda a: pl.BlockSpec(a.shape, lambda i: (0,) * a.ndim)
    before = (jnp.arange(tile)[None, :] < jnp.arange(tile)[:, None]).astype(BF16)
    return pl.pallas_call(
        functools.partial(_out_kernel, tile=tile),
        grid=(T // tile,),
        in_specs=[flat(D), flat(ATT_WIDTH), flat(CONV_WIDTH), full(wg),
                  full(watt), full(wco), full(wout), full(l1g), full(l1b), full(wr), full(br), full(before)],
        out_specs=[own(D),
                   pl.BlockSpec((tile // 8, PACK_WORDS, 8, LANES), lambda i: (i, 0, 0, 0)),
                   own(ROUTER_LANES),
                   pl.BlockSpec((ROUTE_FIELDS, tile), lambda i: (0, i)),
                   pl.BlockSpec((8, ROUTER_LANES), lambda i: (0, 0))],
        out_shape=[jax.ShapeDtypeStruct((T, D), F32),
                   jax.ShapeDtypeStruct((T // 8, PACK_WORDS, 8, LANES), jnp.uint32),
                   jax.ShapeDtypeStruct((T, ROUTER_LANES), F32),
                   jax.ShapeDtypeStruct((ROUTE_FIELDS, T), F32),
                   jax.ShapeDtypeStruct((8, ROUTER_LANES), F32)],
        compiler_params=pltpu.CompilerParams(dimension_semantics=("arbitrary",), vmem_limit_bytes=VMEM_LIMIT),
        name="out",
    )(h0, att, cz, wg, watt, wco, wout, l1g, l1b, wr, br, before)


def _sc_scatter2(src, idx_a, idx_b, n_out):
    m = src.shape[0]
    mesh = plsc.VectorSubcoreMesh(core_axis_name="c", subcore_axis_name="s")

    @functools.partial(pl.kernel, out_type=jax.ShapeDtypeStruct((n_out, LANES), src.dtype), mesh=mesh)
    def k(x_hbm, ia_hbm, ib_hbm, o_hbm):
        def body(x_vmem, ia_vmem, ib_vmem):
            pltpu.sync_copy(x_vmem, o_hbm.at[ia_vmem.at[0]])
            pltpu.sync_copy(x_vmem, o_hbm.at[ib_vmem.at[0]])

        pltpu.emit_pipeline(
            body, grid=(m // SC_WINDOW,),
            in_specs=[pl.BlockSpec((SC_WINDOW, LANES), index_map=lambda i: (i, 0)),
                      pl.BlockSpec((1, SC_WINDOW), index_map=lambda i: (0, i)),
                      pl.BlockSpec((1, SC_WINDOW), index_map=lambda i: (0, i))],
            out_specs=[],
            core_axis_name=("c", "s"), dimension_semantics=(pltpu.PARALLEL,),
        )(x_hbm, ia_hbm, ib_hbm)

    return k(src, idx_a.reshape(1, m), idx_b.reshape(1, m))


def _sc_gather(table, idx):
    m = idx.shape[0]
    mesh = plsc.VectorSubcoreMesh(core_axis_name="c", subcore_axis_name="s")

    @functools.partial(pl.kernel, out_type=jax.ShapeDtypeStruct((m, LANES), table.dtype), mesh=mesh)
    def k(x_hbm, i_hbm, o_hbm):
        def body(i_vmem, o_vmem):
            pltpu.sync_copy(x_hbm.at[i_vmem.at[0]], o_vmem)

        pltpu.emit_pipeline(
            body, grid=(m // SC_WINDOW,),
            in_specs=[pl.BlockSpec((1, SC_WINDOW), index_map=lambda i: (0, i))],
            out_specs=[pl.BlockSpec((SC_WINDOW, LANES), index_map=lambda i: (i, 0))],
            core_axis_name=("c", "s"), dimension_semantics=(pltpu.PARALLEL,),
        )(i_hbm, o_hbm)

    return k(table, idx.reshape(1, m))


def _expert_kernel(te_ref, tv_ref, xs_ref, wg_ref, wu_ref, wd_ref, ys_ref, wgb_ref, wub_ref, wdb_ref, *, tile):
    n = pl.program_id(0)
    valid = tv_ref[n]

    @pl.when(jnp.logical_or(n == 0, te_ref[n] != te_ref[jnp.maximum(n - 1, 0)]))
    def _():
        wgb_ref[...] = wg_ref[...].astype(BF16)
        wub_ref[...] = wu_ref[...].astype(BF16)
        wdb_ref[...] = wd_ref[...].astype(BF16)

    @pl.when(valid > 0)
    def _():
        x = _unpack_bf16_pairs([xs_ref[:, j, :, :].reshape(tile, LANES) for j in range(PACK_WORDS)])
        rows = lax.broadcasted_iota(jnp.int32, (tile, 1), 0)
        x = jnp.where(rows < valid, x, 0.0).astype(BF16)
        gt = _dot(x, wgb_ref[...])
        up = _dot(x, wub_ref[...])
        hid = (gt * jax.nn.sigmoid(gt)) * up
        y = _dot(hid.astype(BF16), wdb_ref[...])
        for j, w in enumerate(_pack_bf16_pairs(y)):
            ys_ref[:, j, :, :] = w.reshape(tile // 8, 8, LANES)

    @pl.when(valid <= 0)
    def _():
        ys_ref[...] = jnp.zeros_like(ys_ref)


def _experts(tile_expert, tile_valid, xs, w_gate, w_up, w_down, tile):
    n_tiles = xs.shape[0] * 8 // tile
    blk = pl.BlockSpec((tile // 8, PACK_WORDS, 8, LANES), lambda n, te, tv: (n, 0, 0, 0))
    return pl.pallas_call(
        functools.partial(_expert_kernel, tile=tile),
        grid_spec=pltpu.PrefetchScalarGridSpec(
            num_scalar_prefetch=2,
            grid=(n_tiles,),
            in_specs=[blk,
                      pl.BlockSpec((None, D_MODEL, D_EXPERT), lambda n, te, tv: (te[n], 0, 0)),
                      pl.BlockSpec((None, D_MODEL, D_EXPERT), lambda n, te, tv: (te[n], 0, 0)),
                      pl.BlockSpec((None, D_EXPERT, D_MODEL), lambda n, te, tv: (te[n], 0, 0))],
            out_specs=blk,
            scratch_shapes=[pltpu.VMEM((D_MODEL, D_EXPERT), BF16), pltpu.VMEM((D_MODEL, D_EXPERT), BF16),
                            pltpu.VMEM((D_EXPERT, D_MODEL), BF16)],
        ),
        out_shape=jax.ShapeDtypeStruct(xs.shape, jnp.uint32),
        compiler_params=pltpu.CompilerParams(
            dimension_semantics=("arbitrary",), vmem_limit_bytes=VMEM_LIMIT),
        name="experts",
    )(tile_expert, tile_valid, xs, w_gate, w_up, w_down)


def _final_kernel(h_ref, g_ref, rt_ref, l2g_ref, l2b_ref, o_ref, *, tile):
    rt = rt_ref[...]
    lane = lax.broadcasted_iota(jnp.int32, rt.shape, 1)
    w1 = jnp.sum(jnp.where(lane == 2, rt, 0.0), axis=1, keepdims=True)
    w2 = jnp.sum(jnp.where(lane == 3, rt, 0.0), axis=1, keepdims=True)
    y1 = _unpack_bf16_pairs([g_ref[0, :, j, :, :].reshape(tile, LANES) for j in range(PACK_WORDS)])
    y2 = _unpack_bf16_pairs([g_ref[1, :, j, :, :].reshape(tile, LANES) for j in range(PACK_WORDS)])
    f = w1 * y1 + w2 * y2
    o_ref[...] = _layer_norm(ALPHA * h_ref[...] + f, l2g_ref[...], l2b_ref[...])


def _final(h1, g, rt, l2g, l2b, tile):
    T, D = h1.shape
    row = lambda w: pl.BlockSpec((tile, w), lambda i: (i, 0))
    full = lambda a: pl.BlockSpec(a.shape, lambda i: (0,) * a.ndim)
    return pl.pallas_call(
        functools.partial(_final_kernel, tile=tile),
        grid=(T // tile,),
        in_specs=[row(D),
                  pl.BlockSpec((2, tile // 8, PACK_WORDS, 8, LANES), lambda i: (0, i, 0, 0, 0)),
                  row(ROUTER_LANES), full(l2g), full(l2b)],
        out_specs=row(D),
        out_shape=jax.ShapeDtypeStruct((T, D), F32),
        compiler_params=pltpu.CompilerParams(dimension_semantics=("parallel",), vmem_limit_bytes=VMEM_LIMIT),
        name="final",
    )(h1, g, rt, l2g, l2b)


def _dispatch_plan(fields, counts, n_tokens, tile):
    cnt = counts[0, :N_EXPERTS].astype(jnp.int32)
    padded = (cnt + tile - 1) // tile * tile
    base = jnp.cumsum(padded) - padded
    e_ids = jnp.arange(N_EXPERTS, dtype=jnp.int32)

    def dest(row_e, row_r):
        e = fields[row_e].astype(jnp.int32)
        seg = jnp.sum(jnp.where(e[None, :] == e_ids[:, None], base[:, None], 0), axis=0)
        pos = seg + fields[row_r].astype(jnp.int32)
        p = pos.reshape(n_tokens // 8, 1, 8)
        j = jnp.arange(PACK_WORDS, dtype=jnp.int32).reshape(1, PACK_WORDS, 1)
        return ((p // 8) * (8 * PACK_WORDS) + j * 8 + p % 8).reshape(-1)

    n_tiles = (2 * n_tokens) // tile + N_EXPERTS
    start = jnp.arange(n_tiles, dtype=jnp.int32) * tile
    seg_end = base + padded
    te = jnp.minimum(jnp.sum((start[:, None] >= seg_end[None, :]).astype(jnp.int32), axis=1), N_EXPERTS - 1)
    te_base = jnp.sum(jnp.where(te[:, None] == e_ids[None, :], base[None, :], 0), axis=1)
    te_cnt = jnp.sum(jnp.where(te[:, None] == e_ids[None, :], cnt[None, :], 0), axis=1)
    tv = jnp.clip(te_cnt - (start - te_base), 0, tile)
    return dest(0, 4), dest(1, 5), te, tv, n_tiles


def _t5_bucket(rel):
    half = N_BUCKETS // 2
    max_exact = half // 2
    ret = jnp.where(rel > 0, half, 0)
    n = jnp.abs(rel)
    nf = jnp.maximum(n, 1).astype(F32)
    large = max_exact + (jnp.log(nf / max_exact) / math.log(MAX_DISTANCE / max_exact)
                         * (half - max_exact)).astype(jnp.int32)
    large = jnp.minimum(large, half - 1)
    return ret + jnp.where(n < max_exact, n, large)


def _bucket_bias(rel_bias, bucket):
    rb = rel_bias.astype(F32)
    out = jnp.zeros((N_Q_HEADS,) + bucket.shape, F32)
    for b in range(N_BUCKETS):
        out = out + jnp.where(bucket[None] == b, rb[b][:, None, None], 0.0)
    return out


def _pair_rows(t):
    return jnp.concatenate([t[:N_PAIRS], t[N_PAIRS:]], axis=-1).reshape(N_PAIRS * BLOCK, -1)


def _bias_tables(rel_bias, sink):
    qi = jnp.arange(BLOCK)
    kj = jnp.arange(3 * BLOCK) - BLOCK
    rel = kj[None, :] - qi[:, None]
    band = _bucket_bias(rel_bias, _t5_bucket(rel))
    in_win = (jnp.abs(rel) <= WINDOW)[None]
    not_prev = (kj >= 0)[None, None, :]
    not_next = (kj < BLOCK)[None, None, :]
    variants = [jnp.where(in_win & not_prev, band, NEG),
                jnp.where(in_win, band, NEG),
                jnp.where(in_win & not_next, band, NEG)]
    bias = jnp.stack([_pair_rows(v) for v in variants])

    off = BLOCK - N_META
    mvars = []
    for blk in (1, 2):
        qpos = blk * BLOCK + qi - off
        meta_rel = jnp.arange(N_META)[None, :] - qpos[:, None]
        mvars.append(_pair_rows(_bucket_bias(rel_bias, _t5_bucket(meta_rel))))
    mbias = jnp.stack(mvars)
    s = sink.astype(F32)
    sink_tab = jnp.repeat(jnp.stack([s[:N_PAIRS], s[N_PAIRS:]], axis=-1), BLOCK, axis=0)
    return bias, mbias, sink_tab


def kernel(x_prompt, x_sample, meta, ln_in_g, ln_in_b, rel_bias, w_in, w_att_branch, sink, conv_w, conv_b,
           conv_ln_g, conv_ln_b, w_conv_out, w_out, ln1_g, ln1_b, w_group, b_group, w_router, b_router,
           w_gate, w_up, w_down, ln2_g, ln2_b):
    row = lambda v: v.reshape(1, -1).astype(F32)
    w = w_in[0]
    wq = (w[:, :Q_END].reshape(D_MODEL, 2, N_PAIRS, HEAD_DIM).transpose(0, 2, 1, 3)
          .reshape(D_MODEL, ATT_WIDTH).astype(BF16))
    watt = (w_att_branch[0].reshape(2, N_PAIRS, HEAD_DIM, D_MODEL).transpose(1, 0, 2, 3)
            .reshape(ATT_WIDTH, D_MODEL).astype(BF16))
    wkv = w[:, Q_END:V_END].astype(BF16)
    wglu = w[:, V_END:GLU_END].astype(BF16)
    wg = w[:, GLU_END:].astype(BF16)
    wco = w_conv_out[0].astype(BF16)
    wout = w_out[0].astype(BF16)
    wr = jnp.zeros((D_MODEL, ROUTER_LANES), F32)
    wr = wr.at[:, :N_EXPERTS].set(w_router[0]).at[:, N_EXPERTS:N_EXPERTS + N_GROUPS].set(w_group[0]).astype(BF16)
    br = jnp.zeros((1, ROUTER_LANES), F32)
    br = br.at[0, :N_EXPERTS].set(b_router[0]).at[0, N_EXPERTS:N_EXPERTS + N_GROUPS].set(b_group[0])
    ln_g, ln_b = row(ln_in_g), row(ln_in_b)
    bias, mbias, sink_tab = _bias_tables(rel_bias, sink[0])

    xm = jnp.concatenate([jnp.zeros((BLOCK - N_META, D_MODEL), F32), meta.astype(F32)], axis=0)
    _, _, kv_m, zc_m = _proj(xm, xm, _Geom(1, 1, 1, 1), ln_g, ln_b, wq, wkv, wglu, BLOCK)
    kv_meta = kv_m[BLOCK - N_META:BLOCK]
    z_meta = zc_m[BLOCK - N_META:BLOCK]

    (bp, sp, _), (bs, ss, _) = x_prompt.shape, x_sample.shape
    xp, xs = x_prompt.reshape(bp * sp, D_MODEL), x_sample.reshape(bs * ss, D_MODEL)
    geom = _geom(x_prompt, x_sample, TILE)
    h0, q, kv, zc = _proj(xp, xs, geom, ln_g, ln_b, wq, wkv, wglu, TILE)
    att = _attn(q, kv, _geom(x_prompt, x_sample, TILE_ATTN), kv_meta, bias, mbias, sink_tab, TILE_ATTN)
    cz = _conv(zc, geom, z_meta, conv_w[0], row(conv_b[0]), row(conv_ln_g[0]), row(conv_ln_b[0]), TILE)
    l1g, l1b, l2g, l2b = row(ln1_g[0]), row(ln1_b[0]), row(ln2_g[0]), row(ln2_b[0])

    def moe(first_tile, n_tiles):
        n = n_tiles * TILE
        h1, h1p, rt, fields, counts = _out(h0, att, cz, wg, watt, wco, wout, l1g, l1b, wr, br,
                                           TILE, first_tile, n_tiles)
        idx1, idx2, tile_expert, tile_valid, n_tiles = _dispatch_plan(fields, counts, n, TILE_EXPERT)
        n_rows = n_tiles * TILE_EXPERT
        xsorted = _sc_scatter2(h1p.reshape(n * PACK_WORDS, LANES), idx1, idx2, n_rows * PACK_WORDS)
        ys = _experts(tile_expert, tile_valid, xsorted.reshape(n_rows // 8, PACK_WORDS, 8, LANES),
                      w_gate[0], w_up[0], w_down[0], TILE_EXPERT)
        g = _sc_gather(ys.reshape(n_rows * PACK_WORDS, LANES), jnp.concatenate([idx1, idx2]))
        return _final(h1, g.reshape(2, n // 8, PACK_WORDS, 8, LANES), rt, l2g, l2b, TILE)

    return moe(0, geom.n_p).reshape(x_prompt.shape), moe(geom.n_p, geom.n_s).reshape(x_sample.shape)
```

```python
import functools
import math
from typing import NamedTuple

import jax
import jax.numpy as jnp
from jax import lax
from jax.experimental import pallas as pl
from jax.experimental.pallas import tpu as pltpu
from jax.experimental.pallas import tpu_sc as plsc

D_MODEL = 1024
N_META = 16
BLOCK = 128
WINDOW = 128
N_Q_HEADS = 8
N_KV_HEADS = 2
HEAD_DIM = 64
ATT_WIDTH = N_Q_HEADS * HEAD_DIM
KV_WIDTH = N_KV_HEADS * HEAD_DIM
CONV_WIDTH = D_MODEL // 2
CONV_K = 31
N_BUCKETS = 32
MAX_DISTANCE = 128
N_GROUPS = 4
EXPERTS_PER_GROUP = 8
N_EXPERTS = N_GROUPS * EXPERTS_PER_GROUP
D_EXPERT = 256
LN_EPS = 1e-5
DEPTH = 1
ALPHA = (2 * DEPTH) ** 0.25
NEG = -1e30
Q_END = ATT_WIDTH
K_END = Q_END + KV_WIDTH
V_END = K_END + KV_WIDTH
GLU_END = V_END + 2 * CONV_WIDTH
GA_END = GLU_END + D_MODEL

N_PAIRS = N_Q_HEADS // 2
LANES = 128
CONV_HALO = 16
ROUTER_LANES = 128

TILE = 512
CONV_ROWS = 128
LN_ROWS = 64
SHIFT_ROWS = 128
TILE_PROJ = 1024
TILE_OUT = 1024
TILE_ATTN = 1024
TILE_EXPERT = 512
PACK_WORDS = 4
SC_WINDOW = 128
ROUTE_FIELDS = 8
VMEM_LIMIT = 56 * 1024 * 1024

BF16 = jnp.bfloat16
F32 = jnp.float32


def _layer_norm(x, g, b):
    mu = jnp.mean(x, axis=-1, keepdims=True)
    xc = x - mu
    var = jnp.mean(xc * xc, axis=-1, keepdims=True)
    return xc * lax.rsqrt(var + LN_EPS) * g + b


def _dot(a, b):
    return jnp.dot(a, b, preferred_element_type=F32)


def _dot_nt(a, b):
    return lax.dot_general(a, b, (((1,), (1,)), ((), ())), preferred_element_type=F32)


class _Geom(NamedTuple):
    n_p: int
    n_s: int
    tp: int
    ts: int


def _geom(x_prompt, x_sample, tile):
    (bp, sp, _), (bs, ss, _) = x_prompt.shape, x_sample.shape
    return _Geom(bp * sp // tile, bs * ss // tile, sp // tile, ss // tile)


def _seq_pos(t, g):
    is_p = t < g.n_p
    local = jnp.where(is_p, lax.rem(t, g.tp), lax.rem(jnp.maximum(t - g.n_p, 0), g.ts))
    return is_p, local == 0, local == jnp.where(is_p, g.tp - 1, g.ts - 1)


def _x_specs(g, tile, width):
    return [pl.BlockSpec((tile, width), lambda t: (jnp.minimum(t, g.n_p - 1), 0)),
            pl.BlockSpec((tile, width), lambda t: (jnp.maximum(t - g.n_p, 0), 0))]


def _proj_kernel(xp_ref, xs_ref, g_ref, b_ref, wq_ref, wkv_ref, wglu_ref, h0_ref, q_ref, kv_ref, zc_ref, *, geom):
    x = jnp.where(pl.program_id(0) < geom.n_p, xp_ref[...], xs_ref[...])
    h0 = _layer_norm(x, g_ref[...], b_ref[...])
    h0_ref[...] = h0
    h = h0.astype(BF16)
    q_ref[...] = _dot(h, wq_ref[...]).astype(BF16)
    kv_ref[...] = _dot(h, wkv_ref[...]).astype(BF16)
    u = _dot(h, wglu_ref[...])
    zc_ref[...] = (u[:, :CONV_WIDTH] * jax.nn.sigmoid(u[:, CONV_WIDTH:])).astype(BF16)


def _proj(xp, xs, geom, ln_g, ln_b, wq, wkv, wglu, tile):
    T = (geom.n_p + geom.n_s) * tile
    row = lambda w: pl.BlockSpec((tile, w), lambda t: (t, 0))
    full = lambda a: pl.BlockSpec(a.shape, lambda t: (0,) * a.ndim)
    return pl.pallas_call(
        functools.partial(_proj_kernel, geom=geom),
        grid=(geom.n_p + geom.n_s,),
        in_specs=_x_specs(geom, tile, D_MODEL) + [full(ln_g), full(ln_b), full(wq), full(wkv), full(wglu)],
        out_specs=[row(D_MODEL), row(ATT_WIDTH), row(2 * KV_WIDTH), row(CONV_WIDTH)],
        out_shape=[
            jax.ShapeDtypeStruct((T, D_MODEL), F32),
            jax.ShapeDtypeStruct((T, ATT_WIDTH), BF16),
            jax.ShapeDtypeStruct((T, 2 * KV_WIDTH), BF16),
            jax.ShapeDtypeStruct((T, CONV_WIDTH), BF16),
        ],
        compiler_params=pltpu.CompilerParams(dimension_semantics=("parallel",), vmem_limit_bytes=VMEM_LIMIT),
        name="proj",
    )(xp, xs, ln_g, ln_b, wq, wkv, wglu)


def _attn_kernel(q_ref, kvp_ref, kvc_ref, kvn_ref, kvm_ref, bias_ref, mbias_ref, sink_ref, o_ref, *, tile, geom):
    _, seq_first, seq_last = _seq_pos(pl.program_id(0), geom)
    blocks = tile // BLOCK
    scale = HEAD_DIM ** -0.5

    lane = lax.broadcasted_iota(jnp.int32, (1, LANES), 1)
    lo = lane < HEAD_DIM

    def split_heads(t):
        z = jnp.zeros_like(t)
        return jnp.where(lo, t, z), jnp.where(lo, z, t)

    kv_ext = jnp.concatenate([kvp_ref[...], kvc_ref[...], kvn_ref[...]], axis=0)
    k_ext = kv_ext[:, :KV_WIDTH] * jnp.asarray(scale, BF16)
    v_ext = kv_ext[:, KV_WIDTH:]
    ka, kb = split_heads(k_ext)
    va, vb = split_heads(v_ext)
    kma, kmb = split_heads(kvm_ref[:, :KV_WIDTH] * jnp.asarray(scale, BF16))
    vma, vmb = split_heads(kvm_ref[:, KV_WIDTH:])
    km_cat = jnp.concatenate([kma, kmb], axis=0)
    vm_cat = jnp.concatenate([vma, vmb], axis=0)

    mlane = lax.broadcasted_iota(jnp.int32, (1, 2 * N_META), 1)
    m_first = mlane < N_META
    sink = sink_ref[...]
    lane_o = lax.broadcasted_iota(jnp.int32, (1, LANES), 1) < HEAD_DIM

    for j in range(blocks):
        first = jnp.logical_and(seq_first, j == 0)
        last = jnp.logical_and(seq_last, j == blocks - 1)
        variant = jnp.where(first, 0, jnp.where(last, 2, 1))
        mvariant = jnp.where(first, 0, 1)

        r0 = j * BLOCK
        qb = q_ref[r0:r0 + BLOCK, :]
        q4 = jnp.concatenate([qb[:, p * LANES:(p + 1) * LANES] for p in range(N_PAIRS)], axis=0)
        k_cat = jnp.concatenate([ka[r0:r0 + 3 * BLOCK], kb[r0:r0 + 3 * BLOCK]], axis=0)
        v_cat = jnp.concatenate([va[r0:r0 + 3 * BLOCK], vb[r0:r0 + 3 * BLOCK]], axis=0)

        s = _dot_nt(q4, k_cat) + bias_ref[variant]
        sm = _dot_nt(q4, km_cat) + mbias_ref[mvariant]

        s_a, s_b = s[:, :3 * BLOCK], s[:, 3 * BLOCK:]
        sm_a = jnp.where(m_first, sm, NEG)
        sm_b = jnp.where(m_first, NEG, sm)
        m_a = jnp.maximum(jnp.maximum(jnp.max(s_a, axis=1, keepdims=True),
                                      jnp.max(sm_a, axis=1, keepdims=True)), sink[:, 0:1])
        m_b = jnp.maximum(jnp.maximum(jnp.max(s_b, axis=1, keepdims=True),
                                      jnp.max(sm_b, axis=1, keepdims=True)), sink[:, 1:2])
        p_a = jnp.exp(s_a - m_a)
        p_b = jnp.exp(s_b - m_b)
        pm = jnp.exp(jnp.where(m_first, sm - m_a, sm - m_b))
        l_a = (jnp.sum(p_a, axis=1, keepdims=True) + jnp.sum(jnp.where(m_first, pm, 0.0), axis=1, keepdims=True)
               + jnp.exp(sink[:, 0:1] - m_a))
        l_b = (jnp.sum(p_b, axis=1, keepdims=True) + jnp.sum(jnp.where(m_first, 0.0, pm), axis=1, keepdims=True)
               + jnp.exp(sink[:, 1:2] - m_b))
        p = jnp.concatenate([p_a, p_b], axis=1).astype(BF16)
        o = _dot(p, v_cat) + _dot(pm.astype(BF16), vm_cat)
        o = o * jnp.where(lane_o, 1.0 / l_a, 1.0 / l_b)
        for pr in range(N_PAIRS):
            o_ref[r0:r0 + BLOCK, pr * LANES:(pr + 1) * LANES] = o[pr * BLOCK:(pr + 1) * BLOCK].astype(BF16)


def _attn(q, kv, geom, kv_meta, bias, mbias, sink_tab, tile):
    T = q.shape[0]
    bpt = tile // BLOCK
    n_blocks = T // BLOCK
    full = lambda a: pl.BlockSpec(a.shape, lambda t: (0,) * a.ndim)
    return pl.pallas_call(
        functools.partial(_attn_kernel, tile=tile, geom=geom),
        grid=(T // tile,),
        in_specs=[
            pl.BlockSpec((tile, ATT_WIDTH), lambda t: (t, 0)),
            pl.BlockSpec((BLOCK, 2 * KV_WIDTH), lambda t: (jnp.maximum(t * bpt - 1, 0), 0)),
            pl.BlockSpec((tile, 2 * KV_WIDTH), lambda t: (t, 0)),
            pl.BlockSpec((BLOCK, 2 * KV_WIDTH), lambda t: (jnp.minimum((t + 1) * bpt, n_blocks - 1), 0)),
            full(kv_meta), full(bias), full(mbias), full(sink_tab),
        ],
        out_specs=pl.BlockSpec((tile, ATT_WIDTH), lambda t: (t, 0)),
        out_shape=jax.ShapeDtypeStruct((T, ATT_WIDTH), BF16),
        compiler_params=pltpu.CompilerParams(dimension_semantics=("parallel",), vmem_limit_bytes=VMEM_LIMIT),
        name="attn",
    )(q, kv, kv, kv, kv_meta, bias, mbias, sink_tab)


def _conv_kernel(zp_ref, zc_ref, zn_ref, zm_ref, w_ref, cb_ref, g_ref, b_ref, o_ref, ext_ref, sh_ref, y_ref,
                 *, tile, geom):
    i = pl.program_id(0)
    _, seq_first, seq_last = _seq_pos(i, geom)
    ext_ref[0:CONV_HALO, :] = jnp.where(seq_first, zm_ref[...], zp_ref[...]).astype(F32)
    ext_ref[CONV_HALO:CONV_HALO + tile, :] = zc_ref[...].astype(F32)
    ext_ref[CONV_HALO + tile:, :] = jnp.where(seq_last, 0.0, zn_ref[...].astype(F32))
    off = CONV_HALO - CONV_K // 2
    reach = (off + CONV_K - 1) // 8 * 8
    for p in range(1, 8):
        for r0 in range(0, tile + reach, SHIFT_ROWS):
            n = min(SHIFT_ROWS, tile + reach - r0)
            sh_ref[p - 1, r0:r0 + n, :] = ext_ref[r0 + p:r0 + p + n, :]

    def taps(r0, cs):
        acc = jnp.zeros((CONV_ROWS, LANES), F32)
        for k in range(CONV_K):
            p, a = (off + k) % 8, (off + k) // 8 * 8
            rows = slice(r0 + a, r0 + a + CONV_ROWS)
            win = ext_ref[rows, cs] if p == 0 else sh_ref[p - 1, rows, cs]
            acc = acc + win * w_ref[k:k + 1, cs]
        y_ref[r0:r0 + CONV_ROWS, cs] = acc

    for c in range(CONV_WIDTH // LANES):
        for r in range(tile // CONV_ROWS):
            pl.when(i >= 0)(functools.partial(taps, r * CONV_ROWS, slice(c * LANES, (c + 1) * LANES)))
    cb, g, b = cb_ref[...], g_ref[...], b_ref[...]
    for r in range(tile // LN_ROWS):
        r0 = r * LN_ROWS
        y = _layer_norm(y_ref[r0:r0 + LN_ROWS, :] + cb, g, b)
        o_ref[r0:r0 + LN_ROWS, :] = (y * jax.nn.sigmoid(y)).astype(BF16)


def _conv(zc, geom, z_meta, conv_w, conv_b, ln_g, ln_b, tile):
    T, C = zc.shape
    hpt = tile // CONV_HALO
    n_halo = T // CONV_HALO
    full = lambda a: pl.BlockSpec(a.shape, lambda t: (0,) * a.ndim)
    return pl.pallas_call(
        functools.partial(_conv_kernel, tile=tile, geom=geom),
        grid=(T // tile,),
        in_specs=[
            pl.BlockSpec((CONV_HALO, C), lambda t: (jnp.maximum(t * hpt - 1, 0), 0)),
            pl.BlockSpec((tile, C), lambda t: (t, 0)),
            pl.BlockSpec((CONV_HALO, C), lambda t: (jnp.minimum((t + 1) * hpt, n_halo - 1), 0)),
            full(z_meta), full(conv_w), full(conv_b), full(ln_g), full(ln_b),
        ],
        out_specs=pl.BlockSpec((tile, C), lambda t: (t, 0)),
        out_shape=jax.ShapeDtypeStruct((T, C), BF16),
        scratch_shapes=[pltpu.VMEM((tile + 2 * CONV_HALO, C), F32),
                        pltpu.VMEM((7, tile + 2 * CONV_HALO - 8, C), F32),
                        pltpu.VMEM((tile, C), F32)],
        compiler_params=pltpu.CompilerParams(dimension_semantics=("parallel",), vmem_limit_bytes=VMEM_LIMIT),
        name="conv",
    )(zc, zc, zc, z_meta, conv_w, conv_b, ln_g, ln_b)


def _route(r):
    lane_i = lax.broadcasted_iota(jnp.int32, r.shape, 1)
    lane = lane_i.astype(F32)
    big = float(1 << 20)
    is_g = jnp.logical_and(lane_i >= N_EXPERTS, lane_i < N_EXPERTS + N_GROUPS)
    lg = jnp.where(is_g, r, -jnp.inf)
    mg = jnp.max(lg, axis=1, keepdims=True)
    g_w = 1.0 / jnp.sum(jnp.exp(lg - mg), axis=1, keepdims=True)
    g_idx = jnp.min(jnp.where(lg == mg, lane - N_EXPERTS, big), axis=1, keepdims=True)
    lane_group = jnp.right_shift(lane_i, 3).astype(F32)
    in_group = jnp.logical_and(lane_i < N_EXPERTS, lane_group == g_idx)
    le = jnp.where(in_group, r, -jnp.inf)
    m1 = jnp.max(le, axis=1, keepdims=True)
    den = jnp.sum(jnp.exp(le - m1), axis=1, keepdims=True)
    i1 = jnp.min(jnp.where(le == m1, lane, big), axis=1, keepdims=True)
    le2 = jnp.where(lane == i1, -jnp.inf, le)
    m2 = jnp.max(le2, axis=1, keepdims=True)
    i2 = jnp.min(jnp.where(le2 == m2, lane, big), axis=1, keepdims=True)
    p1 = 1.0 / den
    p2 = jnp.exp(m2 - m1) / den
    tot = p1 + p2
    return i1, i2, g_w * (p1 / tot), g_w * (p2 / tot)


def _pack_bf16_pairs(x):
    half = x.shape[1] // 2
    words = []
    for j in range(half // LANES):
        lo = pltpu.bitcast(x[:, j * LANES:(j + 1) * LANES].astype(BF16).astype(F32), jnp.uint32)
        hi = pltpu.bitcast(x[:, half + j * LANES:half + (j + 1) * LANES].astype(BF16).astype(F32), jnp.uint32)
        words.append(hi | (lo >> 16))
    return words


def _unpack_bf16_pairs(words):
    lo = [pltpu.bitcast(w << 16, F32) for w in words]
    hi = [pltpu.bitcast(w & jnp.uint32(0xFFFF0000), F32) for w in words]
    return jnp.concatenate(lo + hi, axis=1)


def _out_kernel(h0_ref, att_ref, cz_ref, wg_ref, watt_ref, wco_ref, wout_ref,
                l1g_ref, l1b_ref, wr_ref, br_ref, before_ref, h1_ref, h1p_ref, rt_ref, fld_ref, cnt_ref, *, tile):
    @pl.when(pl.program_id(0) == 0)
    def _():
        cnt_ref[...] = jnp.zeros_like(cnt_ref)

    h0 = h0_ref[...]
    hb = h0.astype(BF16)
    g_att = jax.nn.sigmoid(_dot(hb, wg_ref[:, :D_MODEL]))
    mix = g_att * _dot(att_ref[...], watt_ref[...])
    g_conv = jax.nn.sigmoid(_dot(hb, wg_ref[:, D_MODEL:]))
    mix = mix + g_conv * _dot(cz_ref[...], wco_ref[...])
    m = _dot(mix.astype(BF16), wout_ref[...])
    h1 = _layer_norm(ALPHA * h0 + m, l1g_ref[...], l1b_ref[...])
    h1_ref[...] = h1
    for j, w in enumerate(_pack_bf16_pairs(h1)):
        h1p_ref[:, j, :, :] = w.reshape(tile // 8, 8, LANES)

    r = _dot(h1.astype(BF16), wr_ref[...]) + br_ref[...]
    i1, i2, w1, w2 = _route(r)
    lane = lax.broadcasted_iota(jnp.int32, (tile, ROUTER_LANES), 1)
    lane_f = lane.astype(F32)
    hit1, hit2 = lane_f == i1, lane_f == i2
    onehot = jnp.where(jnp.logical_or(hit1, hit2), 1.0, 0.0)
    seen = _dot(before_ref[...], onehot.astype(BF16)) + cnt_ref[0:1, :]
    rank1 = jnp.sum(jnp.where(hit1, seen, 0.0), axis=1, keepdims=True)
    rank2 = jnp.sum(jnp.where(hit2, seen, 0.0), axis=1, keepdims=True)
    cnt_ref[...] = cnt_ref[...] + jnp.sum(onehot, axis=0, keepdims=True)
    fields = (i1, i2, w1, w2, rank1, rank2)
    rt = jnp.zeros((tile, ROUTER_LANES), F32)
    for k, v in enumerate(fields):
        rt = jnp.where(lane == k, v, rt)
    rt_ref[...] = rt
    fld_ref[...] = jnp.transpose(rt)[0:ROUTE_FIELDS, :]


def _out(h0, att, cz, wg, watt, wco, wout, l1g, l1b, wr, br, tile, first_tile, n_tiles):
    T, D = n_tiles * tile, D_MODEL
    own = lambda w: pl.BlockSpec((tile, w), lambda i: (i, 0))
    flat = lambda w: pl.BlockSpec((tile, w), lambda i: (first_tile + i, 0))
    full = lambda a: pl.BlockSpec(a.shape, lambda i: (0,) * a.ndim, pipeline_mode=pl.Buffered(1))
    before = (jnp.arange(tile)[None, :] < jnp.arange(tile)[:, None]).astype(BF16)
    return pl.pallas_call(
        functools.partial(_out_kernel, tile=tile),
        grid=(T // tile,),
        in_specs=[flat(D), flat(ATT_WIDTH), flat(CONV_WIDTH), full(wg),
                  full(watt), full(wco), full(wout), full(l1g), full(l1b), full(wr), full(br), full(before)],
        out_specs=[own(D),
                   pl.BlockSpec((tile // 8, PACK_WORDS, 8, LANES), lambda i: (i, 0, 0, 0)),
                   own(ROUTER_LANES),
                   pl.BlockSpec((ROUTE_FIELDS, tile), lambda i: (0, i)),
                   pl.BlockSpec((8, ROUTER_LANES), lambda i: (0, 0))],
        out_shape=[jax.ShapeDtypeStruct((T, D), F32),
                   jax.ShapeDtypeStruct((T // 8, PACK_WORDS, 8, LANES), jnp.uint32),
                   jax.ShapeDtypeStruct((T, ROUTER_LANES), F32),
                   jax.ShapeDtypeStruct((ROUTE_FIELDS, T), F32),
                   jax.ShapeDtypeStruct((8, ROUTER_LANES), F32)],
        compiler_params=pltpu.CompilerParams(dimension_semantics=("arbitrary",), vmem_limit_bytes=VMEM_LIMIT),
        name="out",
    )(h0, att, cz, wg, watt, wco, wout, l1g, l1b, wr, br, before)


def _sc_scatter2(src, idx_a, idx_b, n_out):
    m = src.shape[0]
    mesh = plsc.VectorSubcoreMesh(core_axis_name="c", subcore_axis_name="s")

    @functools.partial(pl.kernel, out_type=jax.ShapeDtypeStruct((n_out, LANES), src.dtype), mesh=mesh)
    def k(x_hbm, ia_hbm, ib_hbm, o_hbm):
        def body(x_vmem, ia_vmem, ib_vmem):
            pltpu.sync_copy(x_vmem, o_hbm.at[ia_vmem.at[0]])
            pltpu.sync_copy(x_vmem, o_hbm.at[ib_vmem.at[0]])

        pltpu.emit_pipeline(
            body, grid=(m // SC_WINDOW,),
            in_specs=[pl.BlockSpec((SC_WINDOW, LANES), index_map=lambda i: (i, 0)),
                      pl.BlockSpec((1, SC_WINDOW), index_map=lambda i: (0, i)),
                      pl.BlockSpec((1, SC_WINDOW), index_map=lambda i: (0, i))],
            out_specs=[],
            core_axis_name=("c", "s"), dimension_semantics=(pltpu.PARALLEL,),
        )(x_hbm, ia_hbm, ib_hbm)

    return k(src, idx_a.reshape(1, m), idx_b.reshape(1, m))


def _sc_gather(table, idx):
    m = idx.shape[0]
    mesh = plsc.VectorSubcoreMesh(core_axis_name="c", subcore_axis_name="s")

    @functools.partial(pl.kernel, out_type=jax.ShapeDtypeStruct((m, LANES), table.dtype), mesh=mesh)
    def k(x_hbm, i_hbm, o_hbm):
        def body(i_vmem, o_vmem):
            pltpu.sync_copy(x_hbm.at[i_vmem.at[0]], o_vmem)

        pltpu.emit_pipeline(
            body, grid=(m // SC_WINDOW,),
            in_specs=[pl.BlockSpec((1, SC_WINDOW), index_map=lambda i: (0, i))],
            out_specs=[pl.BlockSpec((SC_WINDOW, LANES), index_map=lambda i: (i, 0))],
            core_axis_name=("c", "s"), dimension_semantics=(pltpu.PARALLEL,),
        )(i_hbm, o_hbm)

    return k(table, idx.reshape(1, m))


def _expert_kernel(te_ref, tv_ref, xs_ref, wg_ref, wu_ref, wd_ref, ys_ref, wgb_ref, wub_ref, wdb_ref, *, tile):
    n = pl.program_id(0)
    valid = tv_ref[n]

    @pl.when(jnp.logical_or(n == 0, te_ref[n] != te_ref[jnp.maximum(n - 1, 0)]))
    def _():
        wgb_ref[...] = wg_ref[...].astype(BF16)
        wub_ref[...] = wu_ref[...].astype(BF16)
        wdb_ref[...] = wd_ref[...].astype(BF16)

    @pl.when(valid > 0)
    def _():
        x = _unpack_bf16_pairs([xs_ref[:, j, :, :].reshape(tile, LANES) for j in range(PACK_WORDS)])
        rows = lax.broadcasted_iota(jnp.int32, (tile, 1), 0)
        x = jnp.where(rows < valid, x, 0.0).astype(BF16)
        gt = _dot(x, wgb_ref[...])
        up = _dot(x, wub_ref[...])
        hid = (gt * jax.nn.sigmoid(gt)) * up
        y = _dot(hid.astype(BF16), wdb_ref[...])
        for j, w in enumerate(_pack_bf16_pairs(y)):
            ys_ref[:, j, :, :] = w.reshape(tile // 8, 8, LANES)

    @pl.when(valid <= 0)
    def _():
        ys_ref[...] = jnp.zeros_like(ys_ref)


def _experts(tile_expert, tile_valid, xs, w_gate, w_up, w_down, tile):
    n_tiles = xs.shape[0] * 8 // tile
    blk = pl.BlockSpec((tile // 8, PACK_WORDS, 8, LANES), lambda n, te, tv: (n, 0, 0, 0))
    return pl.pallas_call(
        functools.partial(_expert_kernel, tile=tile),
        grid_spec=pltpu.PrefetchScalarGridSpec(
            num_scalar_prefetch=2,
            grid=(n_tiles,),
            in_specs=[blk,
                      pl.BlockSpec((None, D_MODEL, D_EXPERT), lambda n, te, tv: (te[n], 0, 0)),
                      pl.BlockSpec((None, D_MODEL, D_EXPERT), lambda n, te, tv: (te[n], 0, 0)),
                      pl.BlockSpec((None, D_EXPERT, D_MODEL), lambda n, te, tv: (te[n], 0, 0))],
            out_specs=blk,
            scratch_shapes=[pltpu.VMEM((D_MODEL, D_EXPERT), BF16), pltpu.VMEM((D_MODEL, D_EXPERT), BF16),
                            pltpu.VMEM((D_EXPERT, D_MODEL), BF16)],
        ),
        out_shape=jax.ShapeDtypeStruct(xs.shape, jnp.uint32),
        compiler_params=pltpu.CompilerParams(
            dimension_semantics=("arbitrary",), vmem_limit_bytes=VMEM_LIMIT),
        name="experts",
    )(tile_expert, tile_valid, xs, w_gate, w_up, w_down)


def _final_kernel(h_ref, g_ref, rt_ref, l2g_ref, l2b_ref, o_ref, *, tile):
    rt = rt_ref[...]
    lane = lax.broadcasted_iota(jnp.int32, rt.shape, 1)
    w1 = jnp.sum(jnp.where(lane == 2, rt, 0.0), axis=1, keepdims=True)
    w2 = jnp.sum(jnp.where(lane == 3, rt, 0.0), axis=1, keepdims=True)
    y1 = _unpack_bf16_pairs([g_ref[0, :, j, :, :].reshape(tile, LANES) for j in range(PACK_WORDS)])
    y2 = _unpack_bf16_pairs([g_ref[1, :, j, :, :].reshape(tile, LANES) for j in range(PACK_WORDS)])
    f = w1 * y1 + w2 * y2
    o_ref[...] = _layer_norm(ALPHA * h_ref[...] + f, l2g_ref[...], l2b_ref[...])


def _final(h1, g, rt, l2g, l2b, tile):
    T, D = h1.shape
    row = lambda w: pl.BlockSpec((tile, w), lambda i: (i, 0))
    full = lambda a: pl.BlockSpec(a.shape, lambda i: (0,) * a.ndim)
    return pl.pallas_call(
        functools.partial(_final_kernel, tile=tile),
        grid=(T // tile,),
        in_specs=[row(D),
                  pl.BlockSpec((2, tile // 8, PACK_WORDS, 8, LANES), lambda i: (0, i, 0, 0, 0)),
                  row(ROUTER_LANES), full(l2g), full(l2b)],
        out_specs=row(D),
        out_shape=jax.ShapeDtypeStruct((T, D), F32),
        compiler_params=pltpu.CompilerParams(dimension_semantics=("parallel",), vmem_limit_bytes=VMEM_LIMIT),
        name="final",
    )(h1, g, rt, l2g, l2b)


def _dispatch_plan(fields, counts, n_tokens, tile):
    cnt = counts[0, :N_EXPERTS].astype(jnp.int32)
    padded = (cnt + tile - 1) // tile * tile
    base = jnp.cumsum(padded) - padded
    e_ids = jnp.arange(N_EXPERTS, dtype=jnp.int32)

    def dest(row_e, row_r):
        e = fields[row_e].astype(jnp.int32)
        seg = jnp.sum(jnp.where(e[None, :] == e_ids[:, None], base[:, None], 0), axis=0)
        pos = seg + fields[row_r].astype(jnp.int32)
        p = pos.reshape(n_tokens // 8, 1, 8)
        j = jnp.arange(PACK_WORDS, dtype=jnp.int32).reshape(1, PACK_WORDS, 1)
        return ((p // 8) * (8 * PACK_WORDS) + j * 8 + p % 8).reshape(-1)

    n_tiles = (2 * n_tokens) // tile + N_EXPERTS
    start = jnp.arange(n_tiles, dtype=jnp.int32) * tile
    seg_end = base + padded
    te = jnp.minimum(jnp.sum((start[:, None] >= seg_end[None, :]).astype(jnp.int32), axis=1), N_EXPERTS - 1)
    te_base = jnp.sum(jnp.where(te[:, None] == e_ids[None, :], base[None, :], 0), axis=1)
    te_cnt = jnp.sum(jnp.where(te[:, None] == e_ids[None, :], cnt[None, :], 0), axis=1)
    tv = jnp.clip(te_cnt - (start - te_base), 0, tile)
    return dest(0, 4), dest(1, 5), te, tv, n_tiles


def _t5_bucket(rel):
    half = N_BUCKETS // 2
    max_exact = half // 2
    ret = jnp.where(rel > 0, half, 0)
    n = jnp.abs(rel)
    nf = jnp.maximum(n, 1).astype(F32)
    large = max_exact + (jnp.log(nf / max_exact) / math.log(MAX_DISTANCE / max_exact)
                         * (half - max_exact)).astype(jnp.int32)
    large = jnp.minimum(large, half - 1)
    return ret + jnp.where(n < max_exact, n, large)


def _bucket_bias(rel_bias, bucket):
    rb = rel_bias.astype(F32)
    out = jnp.zeros((N_Q_HEADS,) + bucket.shape, F32)
    for b in range(N_BUCKETS):
        out = out + jnp.where(bucket[None] == b, rb[b][:, None, None], 0.0)
    return out


def _pair_rows(t):
    return jnp.concatenate([t[:N_PAIRS], t[N_PAIRS:]], axis=-1).reshape(N_PAIRS * BLOCK, -1)


def _bias_tables(rel_bias, sink):
    qi = jnp.arange(BLOCK)
    kj = jnp.arange(3 * BLOCK) - BLOCK
    rel = kj[None, :] - qi[:, None]
    band = _bucket_bias(rel_bias, _t5_bucket(rel))
    in_win = (jnp.abs(rel) <= WINDOW)[None]
    not_prev = (kj >= 0)[None, None, :]
    not_next = (kj < BLOCK)[None, None, :]
    variants = [jnp.where(in_win & not_prev, band, NEG),
                jnp.where(in_win, band, NEG),
                jnp.where(in_win & not_next, band, NEG)]
    bias = jnp.stack([_pair_rows(v) for v in variants])

    off = BLOCK - N_META
    mvars = []
    for blk in (1, 2):
        qpos = blk * BLOCK + qi - off
        meta_rel = jnp.arange(N_META)[None, :] - qpos[:, None]
        mvars.append(_pair_rows(_bucket_bias(rel_bias, _t5_bucket(meta_rel))))
    mbias = jnp.stack(mvars)
    s = sink.astype(F32)
    sink_tab = jnp.repeat(jnp.stack([s[:N_PAIRS], s[N_PAIRS:]], axis=-1), BLOCK, axis=0)
    return bias, mbias, sink_tab


def kernel(x_prompt, x_sample, meta, ln_in_g, ln_in_b, rel_bias, w_in, w_att_branch, sink, conv_w, conv_b,
           conv_ln_g, conv_ln_b, w_conv_out, w_out, ln1_g, ln1_b, w_group, b_group, w_router, b_router,
           w_gate, w_up, w_down, ln2_g, ln2_b):
    row = lambda v: v.reshape(1, -1).astype(F32)
    w = w_in[0]
    wq = (w[:, :Q_END].reshape(D_MODEL, 2, N_PAIRS, HEAD_DIM).transpose(0, 2, 1, 3)
          .reshape(D_MODEL, ATT_WIDTH).astype(BF16))
    watt = (w_att_branch[0].reshape(2, N_PAIRS, HEAD_DIM, D_MODEL).transpose(1, 0, 2, 3)
            .reshape(ATT_WIDTH, D_MODEL).astype(BF16))
    wkv = w[:, Q_END:V_END].astype(BF16)
    wglu = w[:, V_END:GLU_END].astype(BF16)
    wg = w[:, GLU_END:].astype(BF16)
    wco = w_conv_out[0].astype(BF16)
    wout = w_out[0].astype(BF16)
    wr = jnp.zeros((D_MODEL, ROUTER_LANES), F32)
    wr = wr.at[:, :N_EXPERTS].set(w_router[0]).at[:, N_EXPERTS:N_EXPERTS + N_GROUPS].set(w_group[0]).astype(BF16)
    br = jnp.zeros((1, ROUTER_LANES), F32)
    br = br.at[0, :N_EXPERTS].set(b_router[0]).at[0, N_EXPERTS:N_EXPERTS + N_GROUPS].set(b_group[0])
    ln_g, ln_b = row(ln_in_g), row(ln_in_b)
    bias, mbias, sink_tab = _bias_tables(rel_bias, sink[0])

    xm = jnp.concatenate([jnp.zeros((BLOCK - N_META, D_MODEL), F32), meta.astype(F32)], axis=0)
    _, _, kv_m, zc_m = _proj(xm, xm, _Geom(1, 1, 1, 1), ln_g, ln_b, wq, wkv, wglu, BLOCK)
    kv_meta = kv_m[BLOCK - N_META:BLOCK]
    z_meta = zc_m[BLOCK - N_META:BLOCK]

    (bp, sp, _), (bs, ss, _) = x_prompt.shape, x_sample.shape
    xp, xs = x_prompt.reshape(bp * sp, D_MODEL), x_sample.reshape(bs * ss, D_MODEL)
    geom = _geom(x_prompt, x_sample, TILE)
    h0, q, kv, zc = _proj(xp, xs, _geom(x_prompt, x_sample, TILE_PROJ), ln_g, ln_b, wq, wkv, wglu, TILE_PROJ)
    att = _attn(q, kv, _geom(x_prompt, x_sample, TILE_ATTN), kv_meta, bias, mbias, sink_tab, TILE_ATTN)
    cz = _conv(zc, geom, z_meta, conv_w[0], row(conv_b[0]), row(conv_ln_g[0]), row(conv_ln_b[0]), TILE)
    l1g, l1b, l2g, l2b = row(ln1_g[0]), row(ln1_b[0]), row(ln2_g[0]), row(ln2_b[0])

    def moe(first_tile, n_tiles):
        n = n_tiles * TILE
        h1, h1p, rt, fields, counts = _out(h0, att, cz, wg, watt, wco, wout, l1g, l1b, wr, br,
                                           TILE_OUT, first_tile * TILE // TILE_OUT, n_tiles * TILE // TILE_OUT)
        idx1, idx2, tile_expert, tile_valid, n_tiles = _dispatch_plan(fields, counts, n, TILE_EXPERT)
        n_rows = n_tiles * TILE_EXPERT
        xsorted = _sc_scatter2(h1p.reshape(n * PACK_WORDS, LANES), idx1, idx2, n_rows * PACK_WORDS)
        ys = _experts(tile_expert, tile_valid, xsorted.reshape(n_rows // 8, PACK_WORDS, 8, LANES),
                      w_gate[0], w_up[0], w_down[0], TILE_EXPERT)
        g = _sc_gather(ys.reshape(n_rows * PACK_WORDS, LANES), jnp.concatenate([idx1, idx2]))
        return _final(h1, g.reshape(2, n // 8, PACK_WORDS, 8, LANES), rt, l2g, l2b, TILE)

    return moe(0, geom.n_p).reshape(x_prompt.shape), moe(geom.n_p, geom.n_s).reshape(x_sample.shape)
```

```python
import functools
import math
from typing import NamedTuple

import jax
import jax.numpy as jnp
from jax import lax
from jax.experimental import pallas as pl
from jax.experimental.pallas import tpu as pltpu
from jax.experimental.pallas import tpu_sc as plsc

D_MODEL = 1024
N_META = 16
BLOCK = 128
WINDOW = 128
N_Q_HEADS = 8
N_KV_HEADS = 2
HEAD_DIM = 64
ATT_WIDTH = N_Q_HEADS * HEAD_DIM
KV_WIDTH = N_KV_HEADS * HEAD_DIM
CONV_WIDTH = D_MODEL // 2
CONV_K = 31
N_BUCKETS = 32
MAX_DISTANCE = 128
N_GROUPS = 4
EXPERTS_PER_GROUP = 8
N_EXPERTS = N_GROUPS * EXPERTS_PER_GROUP
D_EXPERT = 256
LN_EPS = 1e-5
DEPTH = 1
ALPHA = (2 * DEPTH) ** 0.25
NEG = -1e30
Q_END = ATT_WIDTH
K_END = Q_END + KV_WIDTH
V_END = K_END + KV_WIDTH
GLU_END = V_END + 2 * CONV_WIDTH
GA_END = GLU_END + D_MODEL

N_PAIRS = N_Q_HEADS // 2
LANES = 128
CONV_HALO = 16
ROUTER_LANES = 128

TILE = 512
CONV_ROWS = 128
LN_ROWS = 64
SHIFT_ROWS = 128
TILE_PROJ = 1024
TILE_OUT = 1024
TILE_ATTN = 1024
TILE_EXPERT = 512
PACK_WORDS = 4
SC_WINDOW = 128
ROUTE_FIELDS = 8
VMEM_LIMIT = 56 * 1024 * 1024

BF16 = jnp.bfloat16
F32 = jnp.float32


def _layer_norm(x, g, b):
    mu = jnp.mean(x, axis=-1, keepdims=True)
    xc = x - mu
    var = jnp.mean(xc * xc, axis=-1, keepdims=True)
    return xc * lax.rsqrt(var + LN_EPS) * g + b


def _dot(a, b):
    return jnp.dot(a, b, preferred_element_type=F32)


def _dot_nt(a, b):
    return lax.dot_general(a, b, (((1,), (1,)), ((), ())), preferred_element_type=F32)


class _Geom(NamedTuple):
    n_p: int
    n_s: int
    tp: int
    ts: int


def _geom(x_prompt, x_sample, tile):
    (bp, sp, _), (bs, ss, _) = x_prompt.shape, x_sample.shape
    return _Geom(bp * sp // tile, bs * ss // tile, sp // tile, ss // tile)


def _seq_pos(t, g):
    is_p = t < g.n_p
    local = jnp.where(is_p, lax.rem(t, g.tp), lax.rem(jnp.maximum(t - g.n_p, 0), g.ts))
    return is_p, local == 0, local == jnp.where(is_p, g.tp - 1, g.ts - 1)


def _x_specs(g, tile, width):
    return [pl.BlockSpec((tile, width), lambda t: (jnp.minimum(t, g.n_p - 1), 0)),
            pl.BlockSpec((tile, width), lambda t: (jnp.maximum(t - g.n_p, 0), 0))]


def _proj_kernel(xp_ref, xs_ref, g_ref, b_ref, wq_ref, wkv_ref, wglu_ref, h0_ref, q_ref, kv_ref, zc_ref, *, geom):
    x = jnp.where(pl.program_id(0) < geom.n_p, xp_ref[...], xs_ref[...])
    h0 = _layer_norm(x, g_ref[...], b_ref[...])
    h0_ref[...] = h0
    h = h0.astype(BF16)
    q_ref[...] = _dot(h, wq_ref[...]).astype(BF16)
    kv_ref[...] = _dot(h, wkv_ref[...]).astype(BF16)
    u = _dot(h, wglu_ref[...])
    zc_ref[...] = (u[:, :CONV_WIDTH] * jax.nn.sigmoid(u[:, CONV_WIDTH:])).astype(BF16)


def _proj(xp, xs, geom, ln_g, ln_b, wq, wkv, wglu, tile):
    T = (geom.n_p + geom.n_s) * tile
    row = lambda w: pl.BlockSpec((tile, w), lambda t: (t, 0))
    full = lambda a: pl.BlockSpec(a.shape, lambda t: (0,) * a.ndim)
    return pl.pallas_call(
        functools.partial(_proj_kernel, geom=geom),
        grid=(geom.n_p + geom.n_s,),
        in_specs=_x_specs(geom, tile, D_MODEL) + [full(ln_g), full(ln_b), full(wq), full(wkv), full(wglu)],
        out_specs=[row(D_MODEL), row(ATT_WIDTH), row(2 * KV_WIDTH), row(CONV_WIDTH)],
        out_shape=[
            jax.ShapeDtypeStruct((T, D_MODEL), F32),
            jax.ShapeDtypeStruct((T, ATT_WIDTH), BF16),
            jax.ShapeDtypeStruct((T, 2 * KV_WIDTH), BF16),
            jax.ShapeDtypeStruct((T, CONV_WIDTH), BF16),
        ],
        compiler_params=pltpu.CompilerParams(dimension_semantics=("parallel",), vmem_limit_bytes=VMEM_LIMIT),
        name="proj",
    )(xp, xs, ln_g, ln_b, wq, wkv, wglu)


def _attn_kernel(q_ref, kvp_ref, kvc_ref, kvn_ref, kvm_ref, bias_ref, mbias_ref, sink_ref, o_ref, *, tile, geom):
    _, seq_first, seq_last = _seq_pos(pl.program_id(0), geom)
    blocks = tile // BLOCK
    scale = HEAD_DIM ** -0.5

    lane = lax.broadcasted_iota(jnp.int32, (1, LANES), 1)
    lo = lane < HEAD_DIM

    def split_heads(t):
        z = jnp.zeros_like(t)
        return jnp.where(lo, t, z), jnp.where(lo, z, t)

    kv_ext = jnp.concatenate([kvp_ref[...], kvc_ref[...], kvn_ref[...]], axis=0)
    k_ext = kv_ext[:, :KV_WIDTH] * jnp.asarray(scale, BF16)
    v_ext = kv_ext[:, KV_WIDTH:]
    ka, kb = split_heads(k_ext)
    va, vb = split_heads(v_ext)
    kma, kmb = split_heads(kvm_ref[:, :KV_WIDTH] * jnp.asarray(scale, BF16))
    vma, vmb = split_heads(kvm_ref[:, KV_WIDTH:])
    km_cat = jnp.concatenate([kma, kmb], axis=0)
    vm_cat = jnp.concatenate([vma, vmb], axis=0)

    mlane = lax.broadcasted_iota(jnp.int32, (1, 2 * N_META), 1)
    m_first = mlane < N_META
    sink = sink_ref[...]
    lane_o = lax.broadcasted_iota(jnp.int32, (1, LANES), 1) < HEAD_DIM

    for j in range(blocks):
        first = jnp.logical_and(seq_first, j == 0)
        last = jnp.logical_and(seq_last, j == blocks - 1)
        variant = jnp.where(first, 0, jnp.where(last, 2, 1))
        mvariant = jnp.where(first, 0, 1)

        r0 = j * BLOCK
        qb = q_ref[r0:r0 + BLOCK, :]
        q4 = jnp.concatenate([qb[:, p * LANES:(p + 1) * LANES] for p in range(N_PAIRS)], axis=0)
        k_cat = jnp.concatenate([ka[r0:r0 + 3 * BLOCK], kb[r0:r0 + 3 * BLOCK]], axis=0)
        v_cat = jnp.concatenate([va[r0:r0 + 3 * BLOCK], vb[r0:r0 + 3 * BLOCK]], axis=0)

        s = _dot_nt(q4, k_cat) + bias_ref[variant]
        sm = _dot_nt(q4, km_cat) + mbias_ref[mvariant]

        s_a, s_b = s[:, :3 * BLOCK], s[:, 3 * BLOCK:]
        sm_a = jnp.where(m_first, sm, NEG)
        sm_b = jnp.where(m_first, NEG, sm)
        m_a = jnp.maximum(jnp.maximum(jnp.max(s_a, axis=1, keepdims=True),
                                      jnp.max(sm_a, axis=1, keepdims=True)), sink[:, 0:1])
        m_b = jnp.maximum(jnp.maximum(jnp.max(s_b, axis=1, keepdims=True),
                                      jnp.max(sm_b, axis=1, keepdims=True)), sink[:, 1:2])
        p_a = jnp.exp(s_a - m_a)
        p_b = jnp.exp(s_b - m_b)
        pm = jnp.exp(jnp.where(m_first, sm - m_a, sm - m_b))
        l_a = (jnp.sum(p_a, axis=1, keepdims=True) + jnp.sum(jnp.where(m_first, pm, 0.0), axis=1, keepdims=True)
               + jnp.exp(sink[:, 0:1] - m_a))
        l_b = (jnp.sum(p_b, axis=1, keepdims=True) + jnp.sum(jnp.where(m_first, 0.0, pm), axis=1, keepdims=True)
               + jnp.exp(sink[:, 1:2] - m_b))
        p = jnp.concatenate([p_a, p_b], axis=1).astype(BF16)
        o = _dot(p, v_cat) + _dot(pm.astype(BF16), vm_cat)
        o = o * jnp.where(lane_o, 1.0 / l_a, 1.0 / l_b)
        for pr in range(N_PAIRS):
            o_ref[r0:r0 + BLOCK, pr * LANES:(pr + 1) * LANES] = o[pr * BLOCK:(pr + 1) * BLOCK].astype(BF16)


def _attn(q, kv, geom, kv_meta, bias, mbias, sink_tab, tile):
    T = q.shape[0]
    bpt = tile // BLOCK
    n_blocks = T // BLOCK
    full = lambda a: pl.BlockSpec(a.shape, lambda t: (0,) * a.ndim)
    return pl.pallas_call(
        functools.partial(_attn_kernel, tile=tile, geom=geom),
        grid=(T // tile,),
        in_specs=[
            pl.BlockSpec((tile, ATT_WIDTH), lambda t: (t, 0)),
            pl.BlockSpec((BLOCK, 2 * KV_WIDTH), lambda t: (jnp.maximum(t * bpt - 1, 0), 0)),
            pl.BlockSpec((tile, 2 * KV_WIDTH), lambda t: (t, 0)),
            pl.BlockSpec((BLOCK, 2 * KV_WIDTH), lambda t: (jnp.minimum((t + 1) * bpt, n_blocks - 1), 0)),
            full(kv_meta), full(bias), full(mbias), full(sink_tab),
        ],
        out_specs=pl.BlockSpec((tile, ATT_WIDTH), lambda t: (t, 0)),
        out_shape=jax.ShapeDtypeStruct((T, ATT_WIDTH), BF16),
        compiler_params=pltpu.CompilerParams(dimension_semantics=("parallel",), vmem_limit_bytes=VMEM_LIMIT),
        name="attn",
    )(q, kv, kv, kv, kv_meta, bias, mbias, sink_tab)


def _conv_kernel(zp_ref, zc_ref, zn_ref, zm_ref, w_ref, cb_ref, g_ref, b_ref, o_ref, ext_ref, sh_ref, y_ref,
                 *, tile, geom):
    i = pl.program_id(0)
    _, seq_first, seq_last = _seq_pos(i, geom)
    ext_ref[0:CONV_HALO, :] = jnp.where(seq_first, zm_ref[...], zp_ref[...]).astype(F32)
    ext_ref[CONV_HALO:CONV_HALO + tile, :] = zc_ref[...].astype(F32)
    ext_ref[CONV_HALO + tile:, :] = jnp.where(seq_last, 0.0, zn_ref[...].astype(F32))
    off = CONV_HALO - CONV_K // 2
    reach = (off + CONV_K - 1) // 8 * 8
    for p in range(1, 8):
        for r0 in range(0, tile + reach, SHIFT_ROWS):
            n = min(SHIFT_ROWS, tile + reach - r0)
            sh_ref[p - 1, r0:r0 + n, :] = ext_ref[r0 + p:r0 + p + n, :]

    def taps(r0, cs):
        acc = jnp.zeros((CONV_ROWS, LANES), F32)
        for k in range(CONV_K):
            p, a = (off + k) % 8, (off + k) // 8 * 8
            rows = slice(r0 + a, r0 + a + CONV_ROWS)
            win = ext_ref[rows, cs] if p == 0 else sh_ref[p - 1, rows, cs]
            acc = acc + win * w_ref[k:k + 1, cs]
        y_ref[r0:r0 + CONV_ROWS, cs] = acc

    for c in range(CONV_WIDTH // LANES):
        for r in range(tile // CONV_ROWS):
            pl.when(i >= 0)(functools.partial(taps, r * CONV_ROWS, slice(c * LANES, (c + 1) * LANES)))
    cb, g, b = cb_ref[...], g_ref[...], b_ref[...]
    for r in range(tile // LN_ROWS):
        r0 = r * LN_ROWS
        y = _layer_norm(y_ref[r0:r0 + LN_ROWS, :] + cb, g, b)
        o_ref[r0:r0 + LN_ROWS, :] = (y * jax.nn.sigmoid(y)).astype(BF16)


def _conv(zc, geom, z_meta, conv_w, conv_b, ln_g, ln_b, tile):
    T, C = zc.shape
    hpt = tile // CONV_HALO
    n_halo = T // CONV_HALO
    full = lambda a: pl.BlockSpec(a.shape, lambda t: (0,) * a.ndim)
    return pl.pallas_call(
        functools.partial(_conv_kernel, tile=tile, geom=geom),
        grid=(T // tile,),
        in_specs=[
            pl.BlockSpec((CONV_HALO, C), lambda t: (jnp.maximum(t * hpt - 1, 0), 0)),
            pl.BlockSpec((tile, C), lambda t: (t, 0)),
            pl.BlockSpec((CONV_HALO, C), lambda t: (jnp.minimum((t + 1) * hpt, n_halo - 1), 0)),
            full(z_meta), full(conv_w), full(conv_b), full(ln_g), full(ln_b),
        ],
        out_specs=pl.BlockSpec((tile, C), lambda t: (t, 0)),
        out_shape=jax.ShapeDtypeStruct((T, C), BF16),
        scratch_shapes=[pltpu.VMEM((tile + 2 * CONV_HALO, C), F32),
                        pltpu.VMEM((7, tile + 2 * CONV_HALO - 8, C), F32),
                        pltpu.VMEM((tile, C), F32)],
        compiler_params=pltpu.CompilerParams(dimension_semantics=("parallel",), vmem_limit_bytes=VMEM_LIMIT),
        name="conv",
    )(zc, zc, zc, z_meta, conv_w, conv_b, ln_g, ln_b)


def _route(r):
    lane_i = lax.broadcasted_iota(jnp.int32, r.shape, 1)
    lane = lane_i.astype(F32)
    big = float(1 << 20)
    is_g = jnp.logical_and(lane_i >= N_EXPERTS, lane_i < N_EXPERTS + N_GROUPS)
    lg = jnp.where(is_g, r, -jnp.inf)
    mg = jnp.max(lg, axis=1, keepdims=True)
    g_w = 1.0 / jnp.sum(jnp.exp(lg - mg), axis=1, keepdims=True)
    g_idx = jnp.min(jnp.where(lg == mg, lane - N_EXPERTS, big), axis=1, keepdims=True)
    lane_group = jnp.right_shift(lane_i, 3).astype(F32)
    in_group = jnp.logical_and(lane_i < N_EXPERTS, lane_group == g_idx)
    le = jnp.where(in_group, r, -jnp.inf)
    m1 = jnp.max(le, axis=1, keepdims=True)
    den = jnp.sum(jnp.exp(le - m1), axis=1, keepdims=True)
    i1 = jnp.min(jnp.where(le == m1, lane, big), axis=1, keepdims=True)
    le2 = jnp.where(lane == i1, -jnp.inf, le)
    m2 = jnp.max(le2, axis=1, keepdims=True)
    i2 = jnp.min(jnp.where(le2 == m2, lane, big), axis=1, keepdims=True)
    p1 = 1.0 / den
    p2 = jnp.exp(m2 - m1) / den
    tot = p1 + p2
    return i1, i2, g_w * (p1 / tot), g_w * (p2 / tot)


def _pack_bf16_pairs(x):
    half = x.shape[1] // 2
    words = []
    for j in range(half // LANES):
        lo = pltpu.bitcast(x[:, j * LANES:(j + 1) * LANES].astype(BF16).astype(F32), jnp.uint32)
        hi = pltpu.bitcast(x[:, half + j * LANES:half + (j + 1) * LANES].astype(BF16).astype(F32), jnp.uint32)
        words.append(hi | (lo >> 16))
    return words


def _unpack_bf16_pairs(words):
    lo = [pltpu.bitcast(w << 16, F32) for w in words]
    hi = [pltpu.bitcast(w & jnp.uint32(0xFFFF0000), F32) for w in words]
    return jnp.concatenate(lo + hi, axis=1)


def _out_kernel(h0_ref, att_ref, cz_ref, wg_ref, watt_ref, wco_ref, wout_ref,
                l1g_ref, l1b_ref, wr_ref, br_ref, before_ref, h1_ref, h1p_ref, rt_ref, fld_ref, cnt_ref, *, tile):
    @pl.when(pl.program_id(0) == 0)
    def _():
        cnt_ref[...] = jnp.zeros_like(cnt_ref)

    h0 = h0_ref[...]
    hb = h0.astype(BF16)
    g_att = jax.nn.sigmoid(_dot(hb, wg_ref[:, :D_MODEL]))
    mix = g_att * _dot(att_ref[...], watt_ref[...])
    g_conv = jax.nn.sigmoid(_dot(hb, wg_ref[:, D_MODEL:]))
    mix = mix + g_conv * _dot(cz_ref[...], wco_ref[...])
    m = _dot(mix.astype(BF16), wout_ref[...])
    h1 = _layer_norm(ALPHA * h0 + m, l1g_ref[...], l1b_ref[...])
    h1_ref[...] = h1
    for j, w in enumerate(_pack_bf16_pairs(h1)):
        h1p_ref[:, j, :, :] = w.reshape(tile // 8, 8, LANES)

    r = _dot(h1.astype(BF16), wr_ref[...]) + br_ref[...]
    i1, i2, w1, w2 = _route(r)
    lane = lax.broadcasted_iota(jnp.int32, (tile, ROUTER_LANES), 1)
    lane_f = lane.astype(F32)
    hit1, hit2 = lane_f == i1, lane_f == i2
    onehot = jnp.where(jnp.logical_or(hit1, hit2), 1.0, 0.0)
    seen = _dot(before_ref[...], onehot.astype(BF16)) + cnt_ref[0:1, :]
    rank1 = jnp.sum(jnp.where(hit1, seen, 0.0), axis=1, keepdims=True)
    rank2 = jnp.sum(jnp.where(hit2, seen, 0.0), axis=1, keepdims=True)
    cnt_ref[...] = cnt_ref[...] + jnp.sum(onehot, axis=0, keepdims=True)
    fields = (i1, i2, w1, w2, rank1, rank2)
    rt = jnp.zeros((tile, ROUTER_LANES), F32)
    for k, v in enumerate(fields):
        rt = jnp.where(lane == k, v, rt)
    rt_ref[...] = rt
    fld_ref[...] = jnp.transpose(rt)[0:ROUTE_FIELDS, :]


def _out(h0, att, cz, wg, watt, wco, wout, l1g, l1b, wr, br, tile, first_tile, n_tiles):
    T, D = n_tiles * tile, D_MODEL
    own = lambda w: pl.BlockSpec((tile, w), lambda i: (i, 0))
    flat = lambda w: pl.BlockSpec((tile, w), lambda i: (first_tile + i, 0))
    full = lambda a: pl.BlockSpec(a.shape, lambda i: (0,) * a.ndim, pipeline_mode=pl.Buffered(1))
    before = (jnp.arange(tile)[None, :] < jnp.arange(tile)[:, None]).astype(BF16)
    return pl.pallas_call(
        functools.partial(_out_kernel, tile=tile),
        grid=(T // tile,),
        in_specs=[flat(D), flat(ATT_WIDTH), flat(CONV_WIDTH), full(wg),
                  full(watt), full(wco), full(wout), full(l1g), full(l1b), full(wr), full(br), full(before)],
        out_specs=[own(D),
                   pl.BlockSpec((tile // 8, PACK_WORDS, 8, LANES), lambda i: (i, 0, 0, 0)),
                   own(ROUTER_LANES),
                   pl.BlockSpec((ROUTE_FIELDS, tile), lambda i: (0, i)),
                   pl.BlockSpec((8, ROUTER_LANES), lambda i: (0, 0))],
        out_shape=[jax.ShapeDtypeStruct((T, D), F32),
                   jax.ShapeDtypeStruct((T // 8, PACK_WORDS, 8, LANES), jnp.uint32),
                   jax.ShapeDtypeStruct((T, ROUTER_LANES), F32),
                   jax.ShapeDtypeStruct((ROUTE_FIELDS, T), F32),
                   jax.ShapeDtypeStruct((8, ROUTER_LANES), F32)],
        compiler_params=pltpu.CompilerParams(dimension_semantics=("arbitrary",), vmem_limit_bytes=VMEM_LIMIT),
        name="out",
    )(h0, att, cz, wg, watt, wco, wout, l1g, l1b, wr, br, before)


def _sc_scatter2(src, idx_a, idx_b, n_out):
    m = src.shape[0]
    mesh = plsc.VectorSubcoreMesh(core_axis_name="c", subcore_axis_name="s")

    @functools.partial(pl.kernel, out_type=jax.ShapeDtypeStruct((n_out, LANES), src.dtype), mesh=mesh)
    def k(x_hbm, ia_hbm, ib_hbm, o_hbm):
        def body(x_vmem, ia_vmem, ib_vmem):
            pltpu.sync_copy(x_vmem, o_hbm.at[ia_vmem.at[0]])
            pltpu.sync_copy(x_vmem, o_hbm.at[ib_vmem.at[0]])

        pltpu.emit_pipeline(
            body, grid=(m // SC_WINDOW,),
            in_specs=[pl.BlockSpec((SC_WINDOW, LANES), index_map=lambda i: (i, 0)),
                      pl.BlockSpec((1, SC_WINDOW), index_map=lambda i: (i, 0)),
                      pl.BlockSpec((1, SC_WINDOW), index_map=lambda i: (i, 0))],
            out_specs=[],
            core_axis_name=("c", "s"), dimension_semantics=(pltpu.PARALLEL,),
        )(x_hbm, ia_hbm, ib_hbm)

    return k(src, idx_a, idx_b)


def _sc_gather2(table, idx_a, idx_b):
    windows = idx_a.shape[0]
    m = windows * SC_WINDOW
    mesh = plsc.VectorSubcoreMesh(core_axis_name="c", subcore_axis_name="s")

    @functools.partial(pl.kernel, out_type=jax.ShapeDtypeStruct((2 * m, LANES), table.dtype), mesh=mesh)
    def k(x_hbm, ia_hbm, ib_hbm, o_hbm):
        def body(i_vmem, o_vmem):
            pltpu.sync_copy(x_hbm.at[i_vmem.at[0]], o_vmem)

        for half, i_hbm in enumerate((ia_hbm, ib_hbm)):
            pltpu.emit_pipeline(
                body, grid=(windows,),
                in_specs=[pl.BlockSpec((1, SC_WINDOW), index_map=lambda i: (i, 0))],
                out_specs=[pl.BlockSpec((SC_WINDOW, LANES), index_map=lambda i, half=half: (half * windows + i, 0))],
                core_axis_name=("c", "s"), dimension_semantics=(pltpu.PARALLEL,),
            )(i_hbm, o_hbm)

    return k(table, idx_a, idx_b).reshape(2, m, LANES)


def _expert_kernel(te_ref, tv_ref, xs_ref, wg_ref, wu_ref, wd_ref, ys_ref, wgb_ref, wub_ref, wdb_ref, *, tile):
    n = pl.program_id(0)
    valid = tv_ref[n]

    @pl.when(jnp.logical_or(n == 0, te_ref[n] != te_ref[jnp.maximum(n - 1, 0)]))
    def _():
        wgb_ref[...] = wg_ref[...].astype(BF16)
        wub_ref[...] = wu_ref[...].astype(BF16)
        wdb_ref[...] = wd_ref[...].astype(BF16)

    @pl.when(valid > 0)
    def _():
        x = _unpack_bf16_pairs([xs_ref[:, j, :, :].reshape(tile, LANES) for j in range(PACK_WORDS)])
        rows = lax.broadcasted_iota(jnp.int32, (tile, 1), 0)
        x = jnp.where(rows < valid, x, 0.0).astype(BF16)
        gt = _dot(x, wgb_ref[...])
        up = _dot(x, wub_ref[...])
        hid = (gt * jax.nn.sigmoid(gt)) * up
        y = _dot(hid.astype(BF16), wdb_ref[...])
        for j, w in enumerate(_pack_bf16_pairs(y)):
            ys_ref[:, j, :, :] = w.reshape(tile // 8, 8, LANES)

    @pl.when(valid <= 0)
    def _():
        ys_ref[...] = jnp.zeros_like(ys_ref)


def _experts(tile_expert, tile_valid, xs, w_gate, w_up, w_down, tile):
    n_tiles = xs.shape[0] * 8 // tile
    blk = pl.BlockSpec((tile // 8, PACK_WORDS, 8, LANES), lambda n, te, tv: (n, 0, 0, 0))
    return pl.pallas_call(
        functools.partial(_expert_kernel, tile=tile),
        grid_spec=pltpu.PrefetchScalarGridSpec(
            num_scalar_prefetch=2,
            grid=(n_tiles,),
            in_specs=[blk,
                      pl.BlockSpec((None, D_MODEL, D_EXPERT), lambda n, te, tv: (te[n], 0, 0)),
                      pl.BlockSpec((None, D_MODEL, D_EXPERT), lambda n, te, tv: (te[n], 0, 0)),
                      pl.BlockSpec((None, D_EXPERT, D_MODEL), lambda n, te, tv: (te[n], 0, 0))],
            out_specs=blk,
            scratch_shapes=[pltpu.VMEM((D_MODEL, D_EXPERT), BF16), pltpu.VMEM((D_MODEL, D_EXPERT), BF16),
                            pltpu.VMEM((D_EXPERT, D_MODEL), BF16)],
        ),
        out_shape=jax.ShapeDtypeStruct(xs.shape, jnp.uint32),
        compiler_params=pltpu.CompilerParams(
            dimension_semantics=("arbitrary",), vmem_limit_bytes=VMEM_LIMIT),
        name="experts",
    )(tile_expert, tile_valid, xs, w_gate, w_up, w_down)


def _final_kernel(h_ref, g_ref, rt_ref, l2g_ref, l2b_ref, o_ref, *, tile):
    rt = rt_ref[...]
    lane = lax.broadcasted_iota(jnp.int32, rt.shape, 1)
    w1 = jnp.sum(jnp.where(lane == 2, rt, 0.0), axis=1, keepdims=True)
    w2 = jnp.sum(jnp.where(lane == 3, rt, 0.0), axis=1, keepdims=True)
    y1 = _unpack_bf16_pairs([g_ref[0, :, j, :, :].reshape(tile, LANES) for j in range(PACK_WORDS)])
    y2 = _unpack_bf16_pairs([g_ref[1, :, j, :, :].reshape(tile, LANES) for j in range(PACK_WORDS)])
    f = w1 * y1 + w2 * y2
    o_ref[...] = _layer_norm(ALPHA * h_ref[...] + f, l2g_ref[...], l2b_ref[...])


def _final(h1, g, rt, l2g, l2b, tile):
    T, D = h1.shape
    row = lambda w: pl.BlockSpec((tile, w), lambda i: (i, 0))
    full = lambda a: pl.BlockSpec(a.shape, lambda i: (0,) * a.ndim)
    return pl.pallas_call(
        functools.partial(_final_kernel, tile=tile),
        grid=(T // tile,),
        in_specs=[row(D),
                  pl.BlockSpec((2, tile // 8, PACK_WORDS, 8, LANES), lambda i: (0, i, 0, 0, 0)),
                  row(ROUTER_LANES), full(l2g), full(l2b)],
        out_specs=row(D),
        out_shape=jax.ShapeDtypeStruct((T, D), F32),
        compiler_params=pltpu.CompilerParams(dimension_semantics=("parallel",), vmem_limit_bytes=VMEM_LIMIT),
        name="final",
    )(h1, g, rt, l2g, l2b)


def _dispatch_plan(fields, counts, n_tokens, tile):
    cnt = counts[0, :N_EXPERTS].astype(jnp.int32)
    padded = (cnt + tile - 1) // tile * tile
    base = jnp.cumsum(padded) - padded
    e_ids = jnp.arange(N_EXPERTS, dtype=jnp.int32)

    def dest(row_e, row_r):
        e = fields[row_e].astype(jnp.int32)
        seg = jnp.sum(jnp.where(e[None, :] == e_ids[:, None], base[:, None], 0), axis=0)
        pos = seg + fields[row_r].astype(jnp.int32)
        p = pos.reshape(n_tokens // 8, 1, 8)
        j = jnp.arange(PACK_WORDS, dtype=jnp.int32).reshape(1, PACK_WORDS, 1)
        return ((p // 8) * (8 * PACK_WORDS) + j * 8 + p % 8).reshape(n_tokens * PACK_WORDS // SC_WINDOW, SC_WINDOW)

    n_tiles = (2 * n_tokens) // tile + N_EXPERTS
    start = jnp.arange(n_tiles, dtype=jnp.int32) * tile
    seg_end = base + padded
    te = jnp.minimum(jnp.sum((start[:, None] >= seg_end[None, :]).astype(jnp.int32), axis=1), N_EXPERTS - 1)
    te_base = jnp.sum(jnp.where(te[:, None] == e_ids[None, :], base[None, :], 0), axis=1)
    te_cnt = jnp.sum(jnp.where(te[:, None] == e_ids[None, :], cnt[None, :], 0), axis=1)
    tv = jnp.clip(te_cnt - (start - te_base), 0, tile)
    return dest(0, 4), dest(1, 5), te, tv, n_tiles


def _t5_bucket(rel):
    half = N_BUCKETS // 2
    max_exact = half // 2
    ret = jnp.where(rel > 0, half, 0)
    n = jnp.abs(rel)
    nf = jnp.maximum(n, 1).astype(F32)
    large = max_exact + (jnp.log(nf / max_exact) / math.log(MAX_DISTANCE / max_exact)
                         * (half - max_exact)).astype(jnp.int32)
    large = jnp.minimum(large, half - 1)
    return ret + jnp.where(n < max_exact, n, large)


def _bucket_bias(rel_bias, bucket):
    rb = rel_bias.astype(F32)
    out = jnp.zeros((N_Q_HEADS,) + bucket.shape, F32)
    for b in range(N_BUCKETS):
        out = out + jnp.where(bucket[None] == b, rb[b][:, None, None], 0.0)
    return out


def _pair_rows(t):
    return jnp.concatenate([t[:N_PAIRS], t[N_PAIRS:]], axis=-1).reshape(N_PAIRS * BLOCK, -1)


def _bias_tables(rel_bias, sink):
    qi = jnp.arange(BLOCK)
    kj = jnp.arange(3 * BLOCK) - BLOCK
    rel = kj[None, :] - qi[:, None]
    band = _bucket_bias(rel_bias, _t5_bucket(rel))
    in_win = (jnp.abs(rel) <= WINDOW)[None]
    not_prev = (kj >= 0)[None, None, :]
    not_next = (kj < BLOCK)[None, None, :]
    variants = [jnp.where(in_win & not_prev, band, NEG),
                jnp.where(in_win, band, NEG),
                jnp.where(in_win & not_next, band, NEG)]
    bias = jnp.stack([_pair_rows(v) for v in variants])

    off = BLOCK - N_META
    mvars = []
    for blk in (1, 2):
        qpos = blk * BLOCK + qi - off
        meta_rel = jnp.arange(N_META)[None, :] - qpos[:, None]
        mvars.append(_pair_rows(_bucket_bias(rel_bias, _t5_bucket(meta_rel))))
    mbias = jnp.stack(mvars)
    s = sink.astype(F32)
    sink_tab = jnp.repeat(jnp.stack([s[:N_PAIRS], s[N_PAIRS:]], axis=-1), BLOCK, axis=0)
    return bias, mbias, sink_tab


def kernel(x_prompt, x_sample, meta, ln_in_g, ln_in_b, rel_bias, w_in, w_att_branch, sink, conv_w, conv_b,
           conv_ln_g, conv_ln_b, w_conv_out, w_out, ln1_g, ln1_b, w_group, b_group, w_router, b_router,
           w_gate, w_up, w_down, ln2_g, ln2_b):
    row = lambda v: v.reshape(1, -1).astype(F32)
    w = w_in[0]
    wq = (w[:, :Q_END].reshape(D_MODEL, 2, N_PAIRS, HEAD_DIM).transpose(0, 2, 1, 3)
          .reshape(D_MODEL, ATT_WIDTH).astype(BF16))
    watt = (w_att_branch[0].reshape(2, N_PAIRS, HEAD_DIM, D_MODEL).transpose(1, 0, 2, 3)
            .reshape(ATT_WIDTH, D_MODEL).astype(BF16))
    wkv = w[:, Q_END:V_END].astype(BF16)
    wglu = w[:, V_END:GLU_END].astype(BF16)
    wg = w[:, GLU_END:].astype(BF16)
    wco = w_conv_out[0].astype(BF16)
    wout = w_out[0].astype(BF16)
    wr = jnp.zeros((D_MODEL, ROUTER_LANES), F32)
    wr = wr.at[:, :N_EXPERTS].set(w_router[0]).at[:, N_EXPERTS:N_EXPERTS + N_GROUPS].set(w_group[0]).astype(BF16)
    br = jnp.zeros((1, ROUTER_LANES), F32)
    br = br.at[0, :N_EXPERTS].set(b_router[0]).at[0, N_EXPERTS:N_EXPERTS + N_GROUPS].set(b_group[0])
    ln_g, ln_b = row(ln_in_g), row(ln_in_b)
    bias, mbias, sink_tab = _bias_tables(rel_bias, sink[0])

    xm = jnp.concatenate([jnp.zeros((BLOCK - N_META, D_MODEL), F32), meta.astype(F32)], axis=0)
    _, _, kv_m, zc_m = _proj(xm, xm, _Geom(1, 1, 1, 1), ln_g, ln_b, wq, wkv, wglu, BLOCK)
    kv_meta = kv_m[BLOCK - N_META:BLOCK]
    z_meta = zc_m[BLOCK - N_META:BLOCK]

    (bp, sp, _), (bs, ss, _) = x_prompt.shape, x_sample.shape
    xp, xs = x_prompt.reshape(bp * sp, D_MODEL), x_sample.reshape(bs * ss, D_MODEL)
    geom = _geom(x_prompt, x_sample, TILE)
    h0, q, kv, zc = _proj(xp, xs, _geom(x_prompt, x_sample, TILE_PROJ), ln_g, ln_b, wq, wkv, wglu, TILE_PROJ)
    att = _attn(q, kv, _geom(x_prompt, x_sample, TILE_ATTN), kv_meta, bias, mbias, sink_tab, TILE_ATTN)
    cz = _conv(zc, geom, z_meta, conv_w[0], row(conv_b[0]), row(conv_ln_g[0]), row(conv_ln_b[0]), TILE)
    l1g, l1b, l2g, l2b = row(ln1_g[0]), row(ln1_b[0]), row(ln2_g[0]), row(ln2_b[0])

    def moe(first_tile, n_tiles):
        n = n_tiles * TILE
        h1, h1p, rt, fields, counts = _out(h0, att, cz, wg, watt, wco, wout, l1g, l1b, wr, br,
                                           TILE_OUT, first_tile * TILE // TILE_OUT, n_tiles * TILE // TILE_OUT)
        idx1, idx2, tile_expert, tile_valid, n_tiles = _dispatch_plan(fields, counts, n, TILE_EXPERT)
        n_rows = n_tiles * TILE_EXPERT
        xsorted = _sc_scatter2(h1p.reshape(n * PACK_WORDS, LANES), idx1, idx2, n_rows * PACK_WORDS)
        ys = _experts(tile_expert, tile_valid, xsorted.reshape(n_rows // 8, PACK_WORDS, 8, LANES),
                      w_gate[0], w_up[0], w_down[0], TILE_EXPERT)
        g = _sc_gather2(ys.reshape(n_rows * PACK_WORDS, LANES), idx1, idx2)
        return _final(h1, g.reshape(2, n // 8, PACK_WORDS, 8, LANES), rt, l2g, l2b, TILE)

    return moe(0, geom.n_p).reshape(x_prompt.shape), moe(geom.n_p, geom.n_s).reshape(x_sample.shape)
```

```python
import functools
import math
from typing import NamedTuple

import jax
import jax.numpy as jnp
from jax import lax
from jax.experimental import pallas as pl
from jax.experimental.pallas import tpu as pltpu
from jax.experimental.pallas import tpu_sc as plsc

D_MODEL = 1024
N_META = 16
BLOCK = 128
WINDOW = 128
N_Q_HEADS = 8
N_KV_HEADS = 2
HEAD_DIM = 64
ATT_WIDTH = N_Q_HEADS * HEAD_DIM
KV_WIDTH = N_KV_HEADS * HEAD_DIM
CONV_WIDTH = D_MODEL // 2
CONV_K = 31
N_BUCKETS = 32
MAX_DISTANCE = 128
N_GROUPS = 4
EXPERTS_PER_GROUP = 8
N_EXPERTS = N_GROUPS * EXPERTS_PER_GROUP
D_EXPERT = 256
LN_EPS = 1e-5
DEPTH = 1
ALPHA = (2 * DEPTH) ** 0.25
NEG = -1e30
Q_END = ATT_WIDTH
K_END = Q_END + KV_WIDTH
V_END = K_END + KV_WIDTH
GLU_END = V_END + 2 * CONV_WIDTH
GA_END = GLU_END + D_MODEL

N_PAIRS = N_Q_HEADS // 2
LANES = 128
CONV_HALO = 16
ROUTER_LANES = 128

TILE = 512
CONV_ROWS = 128
LN_ROWS = 64
SHIFT_ROWS = 128
TILE_PROJ = 1024
TILE_OUT = 1024
TILE_ATTN = 2048
TILE_EXPERT = 512
PACK_WORDS = 4
SC_WINDOW = 128
ROUTE_FIELDS = 8
VMEM_LIMIT = 56 * 1024 * 1024

BF16 = jnp.bfloat16
F32 = jnp.float32


def _layer_norm(x, g, b):
    mu = jnp.mean(x, axis=-1, keepdims=True)
    xc = x - mu
    var = jnp.mean(xc * xc, axis=-1, keepdims=True)
    return xc * lax.rsqrt(var + LN_EPS) * g + b


def _dot(a, b):
    return jnp.dot(a, b, preferred_element_type=F32)


def _dot_nt(a, b):
    return lax.dot_general(a, b, (((1,), (1,)), ((), ())), preferred_element_type=F32)


class _Geom(NamedTuple):
    n_p: int
    n_s: int
    tp: int
    ts: int


def _geom(x_prompt, x_sample, tile):
    (bp, sp, _), (bs, ss, _) = x_prompt.shape, x_sample.shape
    return _Geom(bp * sp // tile, bs * ss // tile, sp // tile, ss // tile)


def _seq_pos(t, g):
    is_p = t < g.n_p
    local = jnp.where(is_p, lax.rem(t, g.tp), lax.rem(jnp.maximum(t - g.n_p, 0), g.ts))
    return is_p, local == 0, local == jnp.where(is_p, g.tp - 1, g.ts - 1)


def _x_specs(g, tile, width):
    return [pl.BlockSpec((tile, width), lambda t: (jnp.minimum(t, g.n_p - 1), 0)),
            pl.BlockSpec((tile, width), lambda t: (jnp.maximum(t - g.n_p, 0), 0))]


def _proj_kernel(xp_ref, xs_ref, g_ref, b_ref, wq_ref, wkv_ref, wglu_ref, h0_ref, q_ref, kv_ref, zc_ref, *, geom):
    x = jnp.where(pl.program_id(0) < geom.n_p, xp_ref[...], xs_ref[...])
    h0 = _layer_norm(x, g_ref[...], b_ref[...])
    h0_ref[...] = h0
    h = h0.astype(BF16)
    q_ref[...] = _dot(h, wq_ref[...]).astype(BF16)
    kv_ref[...] = _dot(h, wkv_ref[...]).astype(BF16)
    u = _dot(h, wglu_ref[...])
    zc_ref[...] = (u[:, :CONV_WIDTH] * jax.nn.sigmoid(u[:, CONV_WIDTH:])).astype(BF16)


def _proj(xp, xs, geom, ln_g, ln_b, wq, wkv, wglu, tile):
    T = (geom.n_p + geom.n_s) * tile
    row = lambda w: pl.BlockSpec((tile, w), lambda t: (t, 0))
    full = lambda a: pl.BlockSpec(a.shape, lambda t: (0,) * a.ndim)
    return pl.pallas_call(
        functools.partial(_proj_kernel, geom=geom),
        grid=(geom.n_p + geom.n_s,),
        in_specs=_x_specs(geom, tile, D_MODEL) + [full(ln_g), full(ln_b), full(wq), full(wkv), full(wglu)],
        out_specs=[row(D_MODEL), row(ATT_WIDTH), row(2 * KV_WIDTH), row(CONV_WIDTH)],
        out_shape=[
            jax.ShapeDtypeStruct((T, D_MODEL), F32),
            jax.ShapeDtypeStruct((T, ATT_WIDTH), BF16),
            jax.ShapeDtypeStruct((T, 2 * KV_WIDTH), BF16),
            jax.ShapeDtypeStruct((T, CONV_WIDTH), BF16),
        ],
        compiler_params=pltpu.CompilerParams(dimension_semantics=("parallel",), vmem_limit_bytes=VMEM_LIMIT),
        name="proj",
    )(xp, xs, ln_g, ln_b, wq, wkv, wglu)


def _attn_kernel(q_ref, kvp_ref, kvc_ref, kvn_ref, kvm_ref, bias_ref, mbias_ref, sink_ref, o_ref, *, tile, geom):
    _, seq_first, seq_last = _seq_pos(pl.program_id(0), geom)
    blocks = tile // BLOCK
    scale = HEAD_DIM ** -0.5

    lane = lax.broadcasted_iota(jnp.int32, (1, LANES), 1)
    lo = lane < HEAD_DIM

    def split_heads(t):
        z = jnp.zeros_like(t)
        return jnp.where(lo, t, z), jnp.where(lo, z, t)

    kv_ext = jnp.concatenate([kvp_ref[...], kvc_ref[...], kvn_ref[...]], axis=0)
    k_ext = kv_ext[:, :KV_WIDTH] * jnp.asarray(scale, BF16)
    v_ext = kv_ext[:, KV_WIDTH:]
    ka, kb = split_heads(k_ext)
    va, vb = split_heads(v_ext)
    kma, kmb = split_heads(kvm_ref[:, :KV_WIDTH] * jnp.asarray(scale, BF16))
    vma, vmb = split_heads(kvm_ref[:, KV_WIDTH:])
    km_cat = jnp.concatenate([kma, kmb], axis=0)
    vm_cat = jnp.concatenate([vma, vmb], axis=0)

    mlane = lax.broadcasted_iota(jnp.int32, (1, 2 * N_META), 1)
    m_first = mlane < N_META
    sink = sink_ref[...]
    lane_o = lax.broadcasted_iota(jnp.int32, (1, LANES), 1) < HEAD_DIM

    for j in range(blocks):
        first = jnp.logical_and(seq_first, j == 0)
        last = jnp.logical_and(seq_last, j == blocks - 1)
        variant = jnp.where(first, 0, jnp.where(last, 2, 1))
        mvariant = jnp.where(first, 0, 1)

        r0 = j * BLOCK
        qb = q_ref[r0:r0 + BLOCK, :]
        q4 = jnp.concatenate([qb[:, p * LANES:(p + 1) * LANES] for p in range(N_PAIRS)], axis=0)
        k_cat = jnp.concatenate([ka[r0:r0 + 3 * BLOCK], kb[r0:r0 + 3 * BLOCK]], axis=0)
        v_cat = jnp.concatenate([va[r0:r0 + 3 * BLOCK], vb[r0:r0 + 3 * BLOCK]], axis=0)

        s = _dot_nt(q4, k_cat) + bias_ref[variant]
        sm = _dot_nt(q4, km_cat) + mbias_ref[mvariant]

        s_a, s_b = s[:, :3 * BLOCK], s[:, 3 * BLOCK:]
        sm_a = jnp.where(m_first, sm, NEG)
        sm_b = jnp.where(m_first, NEG, sm)
        m_a = jnp.maximum(jnp.maximum(jnp.max(s_a, axis=1, keepdims=True),
                                      jnp.max(sm_a, axis=1, keepdims=True)), sink[:, 0:1])
        m_b = jnp.maximum(jnp.maximum(jnp.max(s_b, axis=1, keepdims=True),
                                      jnp.max(sm_b, axis=1, keepdims=True)), sink[:, 1:2])
        p_a = jnp.exp(s_a - m_a)
        p_b = jnp.exp(s_b - m_b)
        pm = jnp.exp(jnp.where(m_first, sm - m_a, sm - m_b))
        l_a = (jnp.sum(p_a, axis=1, keepdims=True) + jnp.sum(jnp.where(m_first, pm, 0.0), axis=1, keepdims=True)
               + jnp.exp(sink[:, 0:1] - m_a))
        l_b = (jnp.sum(p_b, axis=1, keepdims=True) + jnp.sum(jnp.where(m_first, 0.0, pm), axis=1, keepdims=True)
               + jnp.exp(sink[:, 1:2] - m_b))
        p = jnp.concatenate([p_a, p_b], axis=1).astype(BF16)
        o = _dot(p, v_cat) + _dot(pm.astype(BF16), vm_cat)
        o = o * jnp.where(lane_o, 1.0 / l_a, 1.0 / l_b)
        for pr in range(N_PAIRS):
            o_ref[r0:r0 + BLOCK, pr * LANES:(pr + 1) * LANES] = o[pr * BLOCK:(pr + 1) * BLOCK].astype(BF16)


def _attn(q, kv, geom, kv_meta, bias, mbias, sink_tab, tile):
    T = q.shape[0]
    bpt = tile // BLOCK
    n_blocks = T // BLOCK
    full = lambda a: pl.BlockSpec(a.shape, lambda t: (0,) * a.ndim)
    return pl.pallas_call(
        functools.partial(_attn_kernel, tile=tile, geom=geom),
        grid=(T // tile,),
        in_specs=[
            pl.BlockSpec((tile, ATT_WIDTH), lambda t: (t, 0)),
            pl.BlockSpec((BLOCK, 2 * KV_WIDTH), lambda t: (jnp.maximum(t * bpt - 1, 0), 0)),
            pl.BlockSpec((tile, 2 * KV_WIDTH), lambda t: (t, 0)),
            pl.BlockSpec((BLOCK, 2 * KV_WIDTH), lambda t: (jnp.minimum((t + 1) * bpt, n_blocks - 1), 0)),
            full(kv_meta), full(bias), full(mbias), full(sink_tab),
        ],
        out_specs=pl.BlockSpec((tile, ATT_WIDTH), lambda t: (t, 0)),
        out_shape=jax.ShapeDtypeStruct((T, ATT_WIDTH), BF16),
        compiler_params=pltpu.CompilerParams(dimension_semantics=("parallel",), vmem_limit_bytes=VMEM_LIMIT),
        name="attn",
    )(q, kv, kv, kv, kv_meta, bias, mbias, sink_tab)


def _conv_kernel(zp_ref, zc_ref, zn_ref, zm_ref, w_ref, cb_ref, g_ref, b_ref, o_ref, ext_ref, sh_ref, y_ref,
                 *, tile, geom):
    i = pl.program_id(0)
    _, seq_first, seq_last = _seq_pos(i, geom)
    ext_ref[0:CONV_HALO, :] = jnp.where(seq_first, zm_ref[...], zp_ref[...]).astype(F32)
    ext_ref[CONV_HALO:CONV_HALO + tile, :] = zc_ref[...].astype(F32)
    ext_ref[CONV_HALO + tile:, :] = jnp.where(seq_last, 0.0, zn_ref[...].astype(F32))
    off = CONV_HALO - CONV_K // 2
    reach = (off + CONV_K - 1) // 8 * 8
    for p in range(1, 8):
        for r0 in range(0, tile + reach, SHIFT_ROWS):
            n = min(SHIFT_ROWS, tile + reach - r0)
            sh_ref[p - 1, r0:r0 + n, :] = ext_ref[r0 + p:r0 + p + n, :]

    def taps(r0, cs):
        acc = jnp.zeros((CONV_ROWS, LANES), F32)
        for k in range(CONV_K):
            p, a = (off + k) % 8, (off + k) // 8 * 8
            rows = slice(r0 + a, r0 + a + CONV_ROWS)
            win = ext_ref[rows, cs] if p == 0 else sh_ref[p - 1, rows, cs]
            acc = acc + win * w_ref[k:k + 1, cs]
        y_ref[r0:r0 + CONV_ROWS, cs] = acc

    for c in range(CONV_WIDTH // LANES):
        for r in range(tile // CONV_ROWS):
            pl.when(i >= 0)(functools.partial(taps, r * CONV_ROWS, slice(c * LANES, (c + 1) * LANES)))
    cb, g, b = cb_ref[...], g_ref[...], b_ref[...]
    for r in range(tile // LN_ROWS):
        r0 = r * LN_ROWS
        y = _layer_norm(y_ref[r0:r0 + LN_ROWS, :] + cb, g, b)
        o_ref[r0:r0 + LN_ROWS, :] = (y * jax.nn.sigmoid(y)).astype(BF16)


def _conv(zc, geom, z_meta, conv_w, conv_b, ln_g, ln_b, tile):
    T, C = zc.shape
    hpt = tile // CONV_HALO
    n_halo = T // CONV_HALO
    full = lambda a: pl.BlockSpec(a.shape, lambda t: (0,) * a.ndim)
    return pl.pallas_call(
        functools.partial(_conv_kernel, tile=tile, geom=geom),
        grid=(T // tile,),
        in_specs=[
            pl.BlockSpec((CONV_HALO, C), lambda t: (jnp.maximum(t * hpt - 1, 0), 0)),
            pl.BlockSpec((tile, C), lambda t: (t, 0)),
            pl.BlockSpec((CONV_HALO, C), lambda t: (jnp.minimum((t + 1) * hpt, n_halo - 1), 0)),
            full(z_meta), full(conv_w), full(conv_b), full(ln_g), full(ln_b),
        ],
        out_specs=pl.BlockSpec((tile, C), lambda t: (t, 0)),
        out_shape=jax.ShapeDtypeStruct((T, C), BF16),
        scratch_shapes=[pltpu.VMEM((tile + 2 * CONV_HALO, C), F32),
                        pltpu.VMEM((7, tile + 2 * CONV_HALO - 8, C), F32),
                        pltpu.VMEM((tile, C), F32)],
        compiler_params=pltpu.CompilerParams(dimension_semantics=("parallel",), vmem_limit_bytes=VMEM_LIMIT),
        name="conv",
    )(zc, zc, zc, z_meta, conv_w, conv_b, ln_g, ln_b)


def _route(r):
    lane_i = lax.broadcasted_iota(jnp.int32, r.shape, 1)
    lane = lane_i.astype(F32)
    big = float(1 << 20)
    is_g = jnp.logical_and(lane_i >= N_EXPERTS, lane_i < N_EXPERTS + N_GROUPS)
    lg = jnp.where(is_g, r, -jnp.inf)
    mg = jnp.max(lg, axis=1, keepdims=True)
    g_w = 1.0 / jnp.sum(jnp.exp(lg - mg), axis=1, keepdims=True)
    g_idx = jnp.min(jnp.where(lg == mg, lane - N_EXPERTS, big), axis=1, keepdims=True)
    lane_group = jnp.right_shift(lane_i, 3).astype(F32)
    in_group = jnp.logical_and(lane_i < N_EXPERTS, lane_group == g_idx)
    le = jnp.where(in_group, r, -jnp.inf)
    m1 = jnp.max(le, axis=1, keepdims=True)
    den = jnp.sum(jnp.exp(le - m1), axis=1, keepdims=True)
    i1 = jnp.min(jnp.where(le == m1, lane, big), axis=1, keepdims=True)
    le2 = jnp.where(lane == i1, -jnp.inf, le)
    m2 = jnp.max(le2, axis=1, keepdims=True)
    i2 = jnp.min(jnp.where(le2 == m2, lane, big), axis=1, keepdims=True)
    p1 = 1.0 / den
    p2 = jnp.exp(m2 - m1) / den
    tot = p1 + p2
    return i1, i2, g_w * (p1 / tot), g_w * (p2 / tot)


def _pack_bf16_pairs(x):
    half = x.shape[1] // 2
    words = []
    for j in range(half // LANES):
        lo = pltpu.bitcast(x[:, j * LANES:(j + 1) * LANES].astype(BF16).astype(F32), jnp.uint32)
        hi = pltpu.bitcast(x[:, half + j * LANES:half + (j + 1) * LANES].astype(BF16).astype(F32), jnp.uint32)
        words.append(hi | (lo >> 16))
    return words


def _unpack_bf16_pairs(words):
    lo = [pltpu.bitcast(w << 16, F32) for w in words]
    hi = [pltpu.bitcast(w & jnp.uint32(0xFFFF0000), F32) for w in words]
    return jnp.concatenate(lo + hi, axis=1)


def _out_kernel(h0_ref, att_ref, cz_ref, wg_ref, watt_ref, wco_ref, wout_ref,
                l1g_ref, l1b_ref, wr_ref, br_ref, before_ref, h1_ref, h1p_ref, rt_ref, fld_ref, cnt_ref, *, tile):
    @pl.when(pl.program_id(0) == 0)
    def _():
        cnt_ref[...] = jnp.zeros_like(cnt_ref)

    h0 = h0_ref[...]
    hb = h0.astype(BF16)
    g_att = jax.nn.sigmoid(_dot(hb, wg_ref[:, :D_MODEL]))
    mix = g_att * _dot(att_ref[...], watt_ref[...])
    g_conv = jax.nn.sigmoid(_dot(hb, wg_ref[:, D_MODEL:]))
    mix = mix + g_conv * _dot(cz_ref[...], wco_ref[...])
    m = _dot(mix.astype(BF16), wout_ref[...])
    h1 = _layer_norm(ALPHA * h0 + m, l1g_ref[...], l1b_ref[...])
    h1_ref[...] = h1
    for j, w in enumerate(_pack_bf16_pairs(h1)):
        h1p_ref[:, j, :, :] = w.reshape(tile // 8, 8, LANES)

    r = _dot(h1.astype(BF16), wr_ref[...]) + br_ref[...]
    i1, i2, w1, w2 = _route(r)
    lane = lax.broadcasted_iota(jnp.int32, (tile, ROUTER_LANES), 1)
    lane_f = lane.astype(F32)
    hit1, hit2 = lane_f == i1, lane_f == i2
    onehot = jnp.where(jnp.logical_or(hit1, hit2), 1.0, 0.0)
    seen = _dot(before_ref[...], onehot.astype(BF16)) + cnt_ref[0:1, :]
    rank1 = jnp.sum(jnp.where(hit1, seen, 0.0), axis=1, keepdims=True)
    rank2 = jnp.sum(jnp.where(hit2, seen, 0.0), axis=1, keepdims=True)
    cnt_ref[...] = cnt_ref[...] + jnp.sum(onehot, axis=0, keepdims=True)
    fields = (i1, i2, w1, w2, rank1, rank2)
    rt = jnp.zeros((tile, ROUTER_LANES), F32)
    for k, v in enumerate(fields):
        rt = jnp.where(lane == k, v, rt)
    rt_ref[...] = rt
    fld_ref[...] = jnp.transpose(rt)[0:ROUTE_FIELDS, :]


def _out(h0, att, cz, wg, watt, wco, wout, l1g, l1b, wr, br, tile, first_tile, n_tiles):
    T, D = n_tiles * tile, D_MODEL
    own = lambda w: pl.BlockSpec((tile, w), lambda i: (i, 0))
    flat = lambda w: pl.BlockSpec((tile, w), lambda i: (first_tile + i, 0))
    full = lambda a: pl.BlockSpec(a.shape, lambda i: (0,) * a.ndim, pipeline_mode=pl.Buffered(1))
    before = (jnp.arange(tile)[None, :] < jnp.arange(tile)[:, None]).astype(BF16)
    return pl.pallas_call(
        functools.partial(_out_kernel, tile=tile),
        grid=(T // tile,),
        in_specs=[flat(D), flat(ATT_WIDTH), flat(CONV_WIDTH), full(wg),
                  full(watt), full(wco), full(wout), full(l1g), full(l1b), full(wr), full(br), full(before)],
        out_specs=[own(D),
                   pl.BlockSpec((tile // 8, PACK_WORDS, 8, LANES), lambda i: (i, 0, 0, 0)),
                   own(ROUTER_LANES),
                   pl.BlockSpec((ROUTE_FIELDS, tile), lambda i: (0, i)),
                   pl.BlockSpec((8, ROUTER_LANES), lambda i: (0, 0))],
        out_shape=[jax.ShapeDtypeStruct((T, D), F32),
                   jax.ShapeDtypeStruct((T // 8, PACK_WORDS, 8, LANES), jnp.uint32),
                   jax.ShapeDtypeStruct((T, ROUTER_LANES), F32),
                   jax.ShapeDtypeStruct((ROUTE_FIELDS, T), F32),
                   jax.ShapeDtypeStruct((8, ROUTER_LANES), F32)],
        compiler_params=pltpu.CompilerParams(dimension_semantics=("arbitrary",), vmem_limit_bytes=VMEM_LIMIT),
        name="out",
    )(h0, att, cz, wg, watt, wco, wout, l1g, l1b, wr, br, before)


def _sc_scatter2(src, idx_a, idx_b, n_out):
    m = src.shape[0]
    mesh = plsc.VectorSubcoreMesh(core_axis_name="c", subcore_axis_name="s")

    @functools.partial(pl.kernel, out_type=jax.ShapeDtypeStruct((n_out, LANES), src.dtype), mesh=mesh)
    def k(x_hbm, ia_hbm, ib_hbm, o_hbm):
        def body(x_vmem, ia_vmem, ib_vmem):
            pltpu.sync_copy(x_vmem, o_hbm.at[ia_vmem.at[0]])
            pltpu.sync_copy(x_vmem, o_hbm.at[ib_vmem.at[0]])

        pltpu.emit_pipeline(
            body, grid=(m // SC_WINDOW,),
            in_specs=[pl.BlockSpec((SC_WINDOW, LANES), index_map=lambda i: (i, 0)),
                      pl.BlockSpec((1, SC_WINDOW), index_map=lambda i: (i, 0)),
                      pl.BlockSpec((1, SC_WINDOW), index_map=lambda i: (i, 0))],
            out_specs=[],
            core_axis_name=("c", "s"), dimension_semantics=(pltpu.PARALLEL,),
        )(x_hbm, ia_hbm, ib_hbm)

    return k(src, idx_a, idx_b)


def _sc_gather2(table, idx_a, idx_b):
    windows = idx_a.shape[0]
    m = windows * SC_WINDOW
    mesh = plsc.VectorSubcoreMesh(core_axis_name="c", subcore_axis_name="s")

    @functools.partial(pl.kernel, out_type=jax.ShapeDtypeStruct((2 * m, LANES), table.dtype), mesh=mesh)
    def k(x_hbm, ia_hbm, ib_hbm, o_hbm):
        def body(i_vmem, o_vmem):
            pltpu.sync_copy(x_hbm.at[i_vmem.at[0]], o_vmem)

        for half, i_hbm in enumerate((ia_hbm, ib_hbm)):
            pltpu.emit_pipeline(
                body, grid=(windows,),
                in_specs=[pl.BlockSpec((1, SC_WINDOW), index_map=lambda i: (i, 0))],
                out_specs=[pl.BlockSpec((SC_WINDOW, LANES), index_map=lambda i, half=half: (half * windows + i, 0))],
                core_axis_name=("c", "s"), dimension_semantics=(pltpu.PARALLEL,),
            )(i_hbm, o_hbm)

    return k(table, idx_a, idx_b).reshape(2, m, LANES)


def _expert_kernel(te_ref, tv_ref, xs_ref, wg_ref, wu_ref, wd_ref, ys_ref, wgb_ref, wub_ref, wdb_ref, *, tile):
    n = pl.program_id(0)
    valid = tv_ref[n]

    @pl.when(jnp.logical_or(n == 0, te_ref[n] != te_ref[jnp.maximum(n - 1, 0)]))
    def _():
        wgb_ref[...] = wg_ref[...].astype(BF16)
        wub_ref[...] = wu_ref[...].astype(BF16)
        wdb_ref[...] = wd_ref[...].astype(BF16)

    @pl.when(valid > 0)
    def _():
        x = _unpack_bf16_pairs([xs_ref[:, j, :, :].reshape(tile, LANES) for j in range(PACK_WORDS)])
        rows = lax.broadcasted_iota(jnp.int32, (tile, 1), 0)
        x = jnp.where(rows < valid, x, 0.0).astype(BF16)
        gt = _dot(x, wgb_ref[...])
        up = _dot(x, wub_ref[...])
        hid = (gt * jax.nn.sigmoid(gt)) * up
        y = _dot(hid.astype(BF16), wdb_ref[...])
        for j, w in enumerate(_pack_bf16_pairs(y)):
            ys_ref[:, j, :, :] = w.reshape(tile // 8, 8, LANES)

    @pl.when(valid <= 0)
    def _():
        ys_ref[...] = jnp.zeros_like(ys_ref)


def _experts(tile_expert, tile_valid, xs, w_gate, w_up, w_down, tile):
    n_tiles = xs.shape[0] * 8 // tile
    blk = pl.BlockSpec((tile // 8, PACK_WORDS, 8, LANES), lambda n, te, tv: (n, 0, 0, 0))
    return pl.pallas_call(
        functools.partial(_expert_kernel, tile=tile),
        grid_spec=pltpu.PrefetchScalarGridSpec(
            num_scalar_prefetch=2,
            grid=(n_tiles,),
            in_specs=[blk,
                      pl.BlockSpec((None, D_MODEL, D_EXPERT), lambda n, te, tv: (te[n], 0, 0)),
                      pl.BlockSpec((None, D_MODEL, D_EXPERT), lambda n, te, tv: (te[n], 0, 0)),
                      pl.BlockSpec((None, D_EXPERT, D_MODEL), lambda n, te, tv: (te[n], 0, 0))],
            out_specs=blk,
            scratch_shapes=[pltpu.VMEM((D_MODEL, D_EXPERT), BF16), pltpu.VMEM((D_MODEL, D_EXPERT), BF16),
                            pltpu.VMEM((D_EXPERT, D_MODEL), BF16)],
        ),
        out_shape=jax.ShapeDtypeStruct(xs.shape, jnp.uint32),
        compiler_params=pltpu.CompilerParams(
            dimension_semantics=("arbitrary",), vmem_limit_bytes=VMEM_LIMIT),
        name="experts",
    )(tile_expert, tile_valid, xs, w_gate, w_up, w_down)


def _final_kernel(h_ref, g_ref, rt_ref, l2g_ref, l2b_ref, o_ref, *, tile):
    rt = rt_ref[...]
    lane = lax.broadcasted_iota(jnp.int32, rt.shape, 1)
    w1 = jnp.sum(jnp.where(lane == 2, rt, 0.0), axis=1, keepdims=True)
    w2 = jnp.sum(jnp.where(lane == 3, rt, 0.0), axis=1, keepdims=True)
    y1 = _unpack_bf16_pairs([g_ref[0, :, j, :, :].reshape(tile, LANES) for j in range(PACK_WORDS)])
    y2 = _unpack_bf16_pairs([g_ref[1, :, j, :, :].reshape(tile, LANES) for j in range(PACK_WORDS)])
    f = w1 * y1 + w2 * y2
    o_ref[...] = _layer_norm(ALPHA * h_ref[...] + f, l2g_ref[...], l2b_ref[...])


def _final(h1, g, rt, l2g, l2b, tile):
    T, D = h1.shape
    row = lambda w: pl.BlockSpec((tile, w), lambda i: (i, 0))
    full = lambda a: pl.BlockSpec(a.shape, lambda i: (0,) * a.ndim)
    return pl.pallas_call(
        functools.partial(_final_kernel, tile=tile),
        grid=(T // tile,),
        in_specs=[row(D),
                  pl.BlockSpec((2, tile // 8, PACK_WORDS, 8, LANES), lambda i: (0, i, 0, 0, 0)),
                  row(ROUTER_LANES), full(l2g), full(l2b)],
        out_specs=row(D),
        out_shape=jax.ShapeDtypeStruct((T, D), F32),
        compiler_params=pltpu.CompilerParams(dimension_semantics=("parallel",), vmem_limit_bytes=VMEM_LIMIT),
        name="final",
    )(h1, g, rt, l2g, l2b)


def _dispatch_plan(fields, counts, n_tokens, tile):
    cnt = counts[0, :N_EXPERTS].astype(jnp.int32)
    padded = (cnt + tile - 1) // tile * tile
    base = jnp.cumsum(padded) - padded
    e_ids = jnp.arange(N_EXPERTS, dtype=jnp.int32)

    def dest(row_e, row_r):
        e = fields[row_e].astype(jnp.int32)
        seg = jnp.sum(jnp.where(e[None, :] == e_ids[:, None], base[:, None], 0), axis=0)
        pos = seg + fields[row_r].astype(jnp.int32)
        p = pos.reshape(n_tokens // 8, 1, 8)
        j = jnp.arange(PACK_WORDS, dtype=jnp.int32).reshape(1, PACK_WORDS, 1)
        return ((p // 8) * (8 * PACK_WORDS) + j * 8 + p % 8).reshape(n_tokens * PACK_WORDS // SC_WINDOW, SC_WINDOW)

    n_tiles = (2 * n_tokens) // tile + N_EXPERTS
    start = jnp.arange(n_tiles, dtype=jnp.int32) * tile
    seg_end = base + padded
    te = jnp.minimum(jnp.sum((start[:, None] >= seg_end[None, :]).astype(jnp.int32), axis=1), N_EXPERTS - 1)
    te_base = jnp.sum(jnp.where(te[:, None] == e_ids[None, :], base[None, :], 0), axis=1)
    te_cnt = jnp.sum(jnp.where(te[:, None] == e_ids[None, :], cnt[None, :], 0), axis=1)
    tv = jnp.clip(te_cnt - (start - te_base), 0, tile)
    return dest(0, 4), dest(1, 5), te, tv, n_tiles


def _t5_bucket(rel):
    half = N_BUCKETS // 2
    max_exact = half // 2
    ret = jnp.where(rel > 0, half, 0)
    n = jnp.abs(rel)
    nf = jnp.maximum(n, 1).astype(F32)
    large = max_exact + (jnp.log(nf / max_exact) / math.log(MAX_DISTANCE / max_exact)
                         * (half - max_exact)).astype(jnp.int32)
    large = jnp.minimum(large, half - 1)
    return ret + jnp.where(n < max_exact, n, large)


def _bucket_bias(rel_bias, bucket):
    rb = rel_bias.astype(F32)
    out = jnp.zeros((N_Q_HEADS,) + bucket.shape, F32)
    for b in range(N_BUCKETS):
        out = out + jnp.where(bucket[None] == b, rb[b][:, None, None], 0.0)
    return out


def _pair_rows(t):
    return jnp.concatenate([t[:N_PAIRS], t[N_PAIRS:]], axis=-1).reshape(N_PAIRS * BLOCK, -1)


def _bias_tables(rel_bias, sink):
    qi = jnp.arange(BLOCK)
    kj = jnp.arange(3 * BLOCK) - BLOCK
    rel = kj[None, :] - qi[:, None]
    band = _bucket_bias(rel_bias, _t5_bucket(rel))
    in_win = (jnp.abs(rel) <= WINDOW)[None]
    not_prev = (kj >= 0)[None, None, :]
    not_next = (kj < BLOCK)[None, None, :]
    variants = [jnp.where(in_win & not_prev, band, NEG),
                jnp.where(in_win, band, NEG),
                jnp.where(in_win & not_next, band, NEG)]
    bias = jnp.stack([_pair_rows(v) for v in variants])

    off = BLOCK - N_META
    mvars = []
    for blk in (1, 2):
        qpos = blk * BLOCK + qi - off
        meta_rel = jnp.arange(N_META)[None, :] - qpos[:, None]
        mvars.append(_pair_rows(_bucket_bias(rel_bias, _t5_bucket(meta_rel))))
    mbias = jnp.stack(mvars)
    s = sink.astype(F32)
    sink_tab = jnp.repeat(jnp.stack([s[:N_PAIRS], s[N_PAIRS:]], axis=-1), BLOCK, axis=0)
    return bias, mbias, sink_tab


def kernel(x_prompt, x_sample, meta, ln_in_g, ln_in_b, rel_bias, w_in, w_att_branch, sink, conv_w, conv_b,
           conv_ln_g, conv_ln_b, w_conv_out, w_out, ln1_g, ln1_b, w_group, b_group, w_router, b_router,
           w_gate, w_up, w_down, ln2_g, ln2_b):
    row = lambda v: v.reshape(1, -1).astype(F32)
    w = w_in[0]
    wq = (w[:, :Q_END].reshape(D_MODEL, 2, N_PAIRS, HEAD_DIM).transpose(0, 2, 1, 3)
          .reshape(D_MODEL, ATT_WIDTH).astype(BF16))
    watt = (w_att_branch[0].reshape(2, N_PAIRS, HEAD_DIM, D_MODEL).transpose(1, 0, 2, 3)
            .reshape(ATT_WIDTH, D_MODEL).astype(BF16))
    wkv = w[:, Q_END:V_END].astype(BF16)
    wglu = w[:, V_END:GLU_END].astype(BF16)
    wg = w[:, GLU_END:].astype(BF16)
    wco = w_conv_out[0].astype(BF16)
    wout = w_out[0].astype(BF16)
    wr = jnp.zeros((D_MODEL, ROUTER_LANES), F32)
    wr = wr.at[:, :N_EXPERTS].set(w_router[0]).at[:, N_EXPERTS:N_EXPERTS + N_GROUPS].set(w_group[0]).astype(BF16)
    br = jnp.zeros((1, ROUTER_LANES), F32)
    br = br.at[0, :N_EXPERTS].set(b_router[0]).at[0, N_EXPERTS:N_EXPERTS + N_GROUPS].set(b_group[0])
    ln_g, ln_b = row(ln_in_g), row(ln_in_b)
    bias, mbias, sink_tab = _bias_tables(rel_bias, sink[0])

    xm = jnp.concatenate([jnp.zeros((BLOCK - N_META, D_MODEL), F32), meta.astype(F32)], axis=0)
    _, _, kv_m, zc_m = _proj(xm, xm, _Geom(1, 1, 1, 1), ln_g, ln_b, wq, wkv, wglu, BLOCK)
    kv_meta = kv_m[BLOCK - N_META:BLOCK]
    z_meta = zc_m[BLOCK - N_META:BLOCK]

    (bp, sp, _), (bs, ss, _) = x_prompt.shape, x_sample.shape
    xp, xs = x_prompt.reshape(bp * sp, D_MODEL), x_sample.reshape(bs * ss, D_MODEL)
    geom = _geom(x_prompt, x_sample, TILE)
    h0, q, kv, zc = _proj(xp, xs, _geom(x_prompt, x_sample, TILE_PROJ), ln_g, ln_b, wq, wkv, wglu, TILE_PROJ)
    att = _attn(q, kv, _geom(x_prompt, x_sample, TILE_ATTN), kv_meta, bias, mbias, sink_tab, TILE_ATTN)
    cz = _conv(zc, geom, z_meta, conv_w[0], row(conv_b[0]), row(conv_ln_g[0]), row(conv_ln_b[0]), TILE)
    l1g, l1b, l2g, l2b = row(ln1_g[0]), row(ln1_b[0]), row(ln2_g[0]), row(ln2_b[0])

    def moe(first_tile, n_tiles):
        n = n_tiles * TILE
        h1, h1p, rt, fields, counts = _out(h0, att, cz, wg, watt, wco, wout, l1g, l1b, wr, br,
                                           TILE_OUT, first_tile * TILE // TILE_OUT, n_tiles * TILE // TILE_OUT)
        idx1, idx2, tile_expert, tile_valid, n_tiles = _dispatch_plan(fields, counts, n, TILE_EXPERT)
        n_rows = n_tiles * TILE_EXPERT
        xsorted = _sc_scatter2(h1p.reshape(n * PACK_WORDS, LANES), idx1, idx2, n_rows * PACK_WORDS)
        ys = _experts(tile_expert, tile_valid, xsorted.reshape(n_rows // 8, PACK_WORDS, 8, LANES),
                      w_gate[0], w_up[0], w_down[0], TILE_EXPERT)
        g = _sc_gather2(ys.reshape(n_rows * PACK_WORDS, LANES), idx1, idx2)
        return _final(h1, g.reshape(2, n // 8, PACK_WORDS, 8, LANES), rt, l2g, l2b, TILE)

    return moe(0, geom.n_p).reshape(x_prompt.shape), moe(geom.n_p, geom.n_s).reshape(x_sample.shape)
```

```python
import functools
import math
from typing import NamedTuple

import jax
import jax.numpy as jnp
from jax import lax
from jax.experimental import pallas as pl
from jax.experimental.pallas import tpu as pltpu
from jax.experimental.pallas import tpu_sc as plsc

D_MODEL = 1024
N_META = 16
BLOCK = 128
WINDOW = 128
N_Q_HEADS = 8
N_KV_HEADS = 2
HEAD_DIM = 64
ATT_WIDTH = N_Q_HEADS * HEAD_DIM
KV_WIDTH = N_KV_HEADS * HEAD_DIM
CONV_WIDTH = D_MODEL // 2
CONV_K = 31
N_BUCKETS = 32
MAX_DISTANCE = 128
N_GROUPS = 4
EXPERTS_PER_GROUP = 8
N_EXPERTS = N_GROUPS * EXPERTS_PER_GROUP
D_EXPERT = 256
LN_EPS = 1e-5
DEPTH = 1
ALPHA = (2 * DEPTH) ** 0.25
NEG = -1e30
Q_END = ATT_WIDTH
K_END = Q_END + KV_WIDTH
V_END = K_END + KV_WIDTH
GLU_END = V_END + 2 * CONV_WIDTH
GA_END = GLU_END + D_MODEL

N_PAIRS = N_Q_HEADS // 2
LANES = 128
CONV_HALO = 16
ROUTER_LANES = 128

TILE = 512
CONV_ROWS = 128
LN_ROWS = 64
SHIFT_ROWS = 128
TILE_PROJ = 1024
TILE_OUT = 1024
TILE_ATTN = 4096
TILE_EXPERT = 512
PACK_WORDS = 4
SC_WINDOW = 128
ROUTE_FIELDS = 8
VMEM_LIMIT = 56 * 1024 * 1024

BF16 = jnp.bfloat16
F32 = jnp.float32


def _layer_norm(x, g, b):
    mu = jnp.mean(x, axis=-1, keepdims=True)
    xc = x - mu
    var = jnp.mean(xc * xc, axis=-1, keepdims=True)
    return xc * lax.rsqrt(var + LN_EPS) * g + b


def _dot(a, b):
    return jnp.dot(a, b, preferred_element_type=F32)


def _dot_nt(a, b):
    return lax.dot_general(a, b, (((1,), (1,)), ((), ())), preferred_element_type=F32)


class _Geom(NamedTuple):
    n_p: int
    n_s: int
    tp: int
    ts: int


def _geom(x_prompt, x_sample, tile):
    (bp, sp, _), (bs, ss, _) = x_prompt.shape, x_sample.shape
    return _Geom(bp * sp // tile, bs * ss // tile, sp // tile, ss // tile)


def _seq_pos(t, g):
    is_p = t < g.n_p
    local = jnp.where(is_p, lax.rem(t, g.tp), lax.rem(jnp.maximum(t - g.n_p, 0), g.ts))
    return is_p, local == 0, local == jnp.where(is_p, g.tp - 1, g.ts - 1)


def _x_specs(g, tile, width):
    return [pl.BlockSpec((tile, width), lambda t: (jnp.minimum(t, g.n_p - 1), 0)),
            pl.BlockSpec((tile, width), lambda t: (jnp.maximum(t - g.n_p, 0), 0))]


def _proj_kernel(xp_ref, xs_ref, g_ref, b_ref, wq_ref, wkv_ref, wglu_ref, h0_ref, q_ref, kv_ref, zc_ref, *, geom):
    x = jnp.where(pl.program_id(0) < geom.n_p, xp_ref[...], xs_ref[...])
    h0 = _layer_norm(x, g_ref[...], b_ref[...])
    h0_ref[...] = h0
    h = h0.astype(BF16)
    q_ref[...] = _dot(h, wq_ref[...]).astype(BF16)
    kv_ref[...] = _dot(h, wkv_ref[...]).astype(BF16)
    u = _dot(h, wglu_ref[...])
    zc_ref[...] = (u[:, :CONV_WIDTH] * jax.nn.sigmoid(u[:, CONV_WIDTH:])).astype(BF16)


def _proj(xp, xs, geom, ln_g, ln_b, wq, wkv, wglu, tile):
    T = (geom.n_p + geom.n_s) * tile
    row = lambda w: pl.BlockSpec((tile, w), lambda t: (t, 0))
    full = lambda a: pl.BlockSpec(a.shape, lambda t: (0,) * a.ndim)
    return pl.pallas_call(
        functools.partial(_proj_kernel, geom=geom),
        grid=(geom.n_p + geom.n_s,),
        in_specs=_x_specs(geom, tile, D_MODEL) + [full(ln_g), full(ln_b), full(wq), full(wkv), full(wglu)],
        out_specs=[row(D_MODEL), row(ATT_WIDTH), row(2 * KV_WIDTH), row(CONV_WIDTH)],
        out_shape=[
            jax.ShapeDtypeStruct((T, D_MODEL), F32),
            jax.ShapeDtypeStruct((T, ATT_WIDTH), BF16),
            jax.ShapeDtypeStruct((T, 2 * KV_WIDTH), BF16),
            jax.ShapeDtypeStruct((T, CONV_WIDTH), BF16),
        ],
        compiler_params=pltpu.CompilerParams(dimension_semantics=("parallel",), vmem_limit_bytes=VMEM_LIMIT),
        name="proj",
    )(xp, xs, ln_g, ln_b, wq, wkv, wglu)


def _attn_kernel(q_ref, kvp_ref, kvc_ref, kvn_ref, kvm_ref, bias_ref, mbias_ref, sink_ref, o_ref, *, tile, geom):
    _, seq_first, seq_last = _seq_pos(pl.program_id(0), geom)
    blocks = tile // BLOCK
    scale = HEAD_DIM ** -0.5

    lane = lax.broadcasted_iota(jnp.int32, (1, LANES), 1)
    lo = lane < HEAD_DIM

    def split_heads(t):
        z = jnp.zeros_like(t)
        return jnp.where(lo, t, z), jnp.where(lo, z, t)

    kv_ext = jnp.concatenate([kvp_ref[...], kvc_ref[...], kvn_ref[...]], axis=0)
    k_ext = kv_ext[:, :KV_WIDTH] * jnp.asarray(scale, BF16)
    v_ext = kv_ext[:, KV_WIDTH:]
    ka, kb = split_heads(k_ext)
    va, vb = split_heads(v_ext)
    kma, kmb = split_heads(kvm_ref[:, :KV_WIDTH] * jnp.asarray(scale, BF16))
    vma, vmb = split_heads(kvm_ref[:, KV_WIDTH:])
    km_cat = jnp.concatenate([kma, kmb], axis=0)
    vm_cat = jnp.concatenate([vma, vmb], axis=0)

    mlane = lax.broadcasted_iota(jnp.int32, (1, 2 * N_META), 1)
    m_first = mlane < N_META
    sink = sink_ref[...]
    lane_o = lax.broadcasted_iota(jnp.int32, (1, LANES), 1) < HEAD_DIM

    for j in range(blocks):
        first = jnp.logical_and(seq_first, j == 0)
        last = jnp.logical_and(seq_last, j == blocks - 1)
        variant = jnp.where(first, 0, jnp.where(last, 2, 1))
        mvariant = jnp.where(first, 0, 1)

        r0 = j * BLOCK
        qb = q_ref[r0:r0 + BLOCK, :]
        q4 = jnp.concatenate([qb[:, p * LANES:(p + 1) * LANES] for p in range(N_PAIRS)], axis=0)
        k_cat = jnp.concatenate([ka[r0:r0 + 3 * BLOCK], kb[r0:r0 + 3 * BLOCK]], axis=0)
        v_cat = jnp.concatenate([va[r0:r0 + 3 * BLOCK], vb[r0:r0 + 3 * BLOCK]], axis=0)

        s = _dot_nt(q4, k_cat) + bias_ref[variant]
        sm = _dot_nt(q4, km_cat) + mbias_ref[mvariant]

        s_a, s_b = s[:, :3 * BLOCK], s[:, 3 * BLOCK:]
        sm_a = jnp.where(m_first, sm, NEG)
        sm_b = jnp.where(m_first, NEG, sm)
        m_a = jnp.maximum(jnp.maximum(jnp.max(s_a, axis=1, keepdims=True),
                                      jnp.max(sm_a, axis=1, keepdims=True)), sink[:, 0:1])
        m_b = jnp.maximum(jnp.maximum(jnp.max(s_b, axis=1, keepdims=True),
                                      jnp.max(sm_b, axis=1, keepdims=True)), sink[:, 1:2])
        p_a = jnp.exp(s_a - m_a)
        p_b = jnp.exp(s_b - m_b)
        pm = jnp.exp(jnp.where(m_first, sm - m_a, sm - m_b))
        l_a = (jnp.sum(p_a, axis=1, keepdims=True) + jnp.sum(jnp.where(m_first, pm, 0.0), axis=1, keepdims=True)
               + jnp.exp(sink[:, 0:1] - m_a))
        l_b = (jnp.sum(p_b, axis=1, keepdims=True) + jnp.sum(jnp.where(m_first, 0.0, pm), axis=1, keepdims=True)
               + jnp.exp(sink[:, 1:2] - m_b))
        p = jnp.concatenate([p_a, p_b], axis=1).astype(BF16)
        o = _dot(p, v_cat) + _dot(pm.astype(BF16), vm_cat)
        o = o * jnp.where(lane_o, 1.0 / l_a, 1.0 / l_b)
        for pr in range(N_PAIRS):
            o_ref[r0:r0 + BLOCK, pr * LANES:(pr + 1) * LANES] = o[pr * BLOCK:(pr + 1) * BLOCK].astype(BF16)


def _attn(q, kv, geom, kv_meta, bias, mbias, sink_tab, tile):
    T = q.shape[0]
    bpt = tile // BLOCK
    n_blocks = T // BLOCK
    full = lambda a: pl.BlockSpec(a.shape, lambda t: (0,) * a.ndim)
    return pl.pallas_call(
        functools.partial(_attn_kernel, tile=tile, geom=geom),
        grid=(T // tile,),
        in_specs=[
            pl.BlockSpec((tile, ATT_WIDTH), lambda t: (t, 0)),
            pl.BlockSpec((BLOCK, 2 * KV_WIDTH), lambda t: (jnp.maximum(t * bpt - 1, 0), 0)),
            pl.BlockSpec((tile, 2 * KV_WIDTH), lambda t: (t, 0)),
            pl.BlockSpec((BLOCK, 2 * KV_WIDTH), lambda t: (jnp.minimum((t + 1) * bpt, n_blocks - 1), 0)),
            full(kv_meta), full(bias), full(mbias), full(sink_tab),
        ],
        out_specs=pl.BlockSpec((tile, ATT_WIDTH), lambda t: (t, 0)),
        out_shape=jax.ShapeDtypeStruct((T, ATT_WIDTH), BF16),
        compiler_params=pltpu.CompilerParams(dimension_semantics=("parallel",), vmem_limit_bytes=VMEM_LIMIT),
        name="attn",
    )(q, kv, kv, kv, kv_meta, bias, mbias, sink_tab)


def _conv_kernel(zp_ref, zc_ref, zn_ref, zm_ref, w_ref, cb_ref, g_ref, b_ref, o_ref, ext_ref, sh_ref, y_ref,
                 *, tile, geom):
    i = pl.program_id(0)
    _, seq_first, seq_last = _seq_pos(i, geom)
    ext_ref[0:CONV_HALO, :] = jnp.where(seq_first, zm_ref[...], zp_ref[...]).astype(F32)
    ext_ref[CONV_HALO:CONV_HALO + tile, :] = zc_ref[...].astype(F32)
    ext_ref[CONV_HALO + tile:, :] = jnp.where(seq_last, 0.0, zn_ref[...].astype(F32))
    off = CONV_HALO - CONV_K // 2
    reach = (off + CONV_K - 1) // 8 * 8
    for p in range(1, 8):
        for r0 in range(0, tile + reach, SHIFT_ROWS):
            n = min(SHIFT_ROWS, tile + reach - r0)
            sh_ref[p - 1, r0:r0 + n, :] = ext_ref[r0 + p:r0 + p + n, :]

    def taps(r0, cs):
        acc = jnp.zeros((CONV_ROWS, LANES), F32)
        for k in range(CONV_K):
            p, a = (off + k) % 8, (off + k) // 8 * 8
            rows = slice(r0 + a, r0 + a + CONV_ROWS)
            win = ext_ref[rows, cs] if p == 0 else sh_ref[p - 1, rows, cs]
            acc = acc + win * w_ref[k:k + 1, cs]
        y_ref[r0:r0 + CONV_ROWS, cs] = acc

    for c in range(CONV_WIDTH // LANES):
        for r in range(tile // CONV_ROWS):
            pl.when(i >= 0)(functools.partial(taps, r * CONV_ROWS, slice(c * LANES, (c + 1) * LANES)))
    cb, g, b = cb_ref[...], g_ref[...], b_ref[...]
    for r in range(tile // LN_ROWS):
        r0 = r * LN_ROWS
        y = _layer_norm(y_ref[r0:r0 + LN_ROWS, :] + cb, g, b)
        o_ref[r0:r0 + LN_ROWS, :] = (y * jax.nn.sigmoid(y)).astype(BF16)


def _conv(zc, geom, z_meta, conv_w, conv_b, ln_g, ln_b, tile):
    T, C = zc.shape
    hpt = tile // CONV_HALO
    n_halo = T // CONV_HALO
    full = lambda a: pl.BlockSpec(a.shape, lambda t: (0,) * a.ndim)
    return pl.pallas_call(
        functools.partial(_conv_kernel, tile=tile, geom=geom),
        grid=(T // tile,),
        in_specs=[
            pl.BlockSpec((CONV_HALO, C), lambda t: (jnp.maximum(t * hpt - 1, 0), 0)),
            pl.BlockSpec((tile, C), lambda t: (t, 0)),
            pl.BlockSpec((CONV_HALO, C), lambda t: (jnp.minimum((t + 1) * hpt, n_halo - 1), 0)),
            full(z_meta), full(conv_w), full(conv_b), full(ln_g), full(ln_b),
        ],
        out_specs=pl.BlockSpec((tile, C), lambda t: (t, 0)),
        out_shape=jax.ShapeDtypeStruct((T, C), BF16),
        scratch_shapes=[pltpu.VMEM((tile + 2 * CONV_HALO, C), F32),
                        pltpu.VMEM((7, tile + 2 * CONV_HALO - 8, C), F32),
                        pltpu.VMEM((tile, C), F32)],
        compiler_params=pltpu.CompilerParams(dimension_semantics=("parallel",), vmem_limit_bytes=VMEM_LIMIT),
        name="conv",
    )(zc, zc, zc, z_meta, conv_w, conv_b, ln_g, ln_b)


def _route(r):
    lane_i = lax.broadcasted_iota(jnp.int32, r.shape, 1)
    lane = lane_i.astype(F32)
    big = float(1 << 20)
    is_g = jnp.logical_and(lane_i >= N_EXPERTS, lane_i < N_EXPERTS + N_GROUPS)
    lg = jnp.where(is_g, r, -jnp.inf)
    mg = jnp.max(lg, axis=1, keepdims=True)
    g_w = 1.0 / jnp.sum(jnp.exp(lg - mg), axis=1, keepdims=True)
    g_idx = jnp.min(jnp.where(lg == mg, lane - N_EXPERTS, big), axis=1, keepdims=True)
    lane_group = jnp.right_shift(lane_i, 3).astype(F32)
    in_group = jnp.logical_and(lane_i < N_EXPERTS, lane_group == g_idx)
    le = jnp.where(in_group, r, -jnp.inf)
    m1 = jnp.max(le, axis=1, keepdims=True)
    den = jnp.sum(jnp.exp(le - m1), axis=1, keepdims=True)
    i1 = jnp.min(jnp.where(le == m1, lane, big), axis=1, keepdims=True)
    le2 = jnp.where(lane == i1, -jnp.inf, le)
    m2 = jnp.max(le2, axis=1, keepdims=True)
    i2 = jnp.min(jnp.where(le2 == m2, lane, big), axis=1, keepdims=True)
    p1 = 1.0 / den
    p2 = jnp.exp(m2 - m1) / den
    tot = p1 + p2
    return i1, i2, g_w * (p1 / tot), g_w * (p2 / tot)


def _pack_bf16_pairs(x):
    half = x.shape[1] // 2
    words = []
    for j in range(half // LANES):
        lo = pltpu.bitcast(x[:, j * LANES:(j + 1) * LANES].astype(BF16).astype(F32), jnp.uint32)
        hi = pltpu.bitcast(x[:, half + j * LANES:half + (j + 1) * LANES].astype(BF16).astype(F32), jnp.uint32)
        words.append(hi | (lo >> 16))
    return words


def _unpack_bf16_pairs(words):
    lo = [pltpu.bitcast(w << 16, F32) for w in words]
    hi = [pltpu.bitcast(w & jnp.uint32(0xFFFF0000), F32) for w in words]
    return jnp.concatenate(lo + hi, axis=1)


def _out_kernel(h0_ref, att_ref, cz_ref, wg_ref, watt_ref, wco_ref, wout_ref,
                l1g_ref, l1b_ref, wr_ref, br_ref, before_ref, h1_ref, h1p_ref, rt_ref, fld_ref, cnt_ref, *, tile):
    @pl.when(pl.program_id(0) == 0)
    def _():
        cnt_ref[...] = jnp.zeros_like(cnt_ref)

    h0 = h0_ref[...]
    hb = h0.astype(BF16)
    g_att = jax.nn.sigmoid(_dot(hb, wg_ref[:, :D_MODEL]))
    mix = g_att * _dot(att_ref[...], watt_ref[...])
    g_conv = jax.nn.sigmoid(_dot(hb, wg_ref[:, D_MODEL:]))
    mix = mix + g_conv * _dot(cz_ref[...], wco_ref[...])
    m = _dot(mix.astype(BF16), wout_ref[...])
    h1 = _layer_norm(ALPHA * h0 + m, l1g_ref[...], l1b_ref[...])
    h1_ref[...] = h1
    for j, w in enumerate(_pack_bf16_pairs(h1)):
        h1p_ref[:, j, :, :] = w.reshape(tile // 8, 8, LANES)

    r = _dot(h1.astype(BF16), wr_ref[...]) + br_ref[...]
    i1, i2, w1, w2 = _route(r)
    lane = lax.broadcasted_iota(jnp.int32, (tile, ROUTER_LANES), 1)
    lane_f = lane.astype(F32)
    hit1, hit2 = lane_f == i1, lane_f == i2
    onehot = jnp.where(jnp.logical_or(hit1, hit2), 1.0, 0.0)
    seen = _dot(before_ref[...], onehot.astype(BF16)) + cnt_ref[0:1, :]
    rank1 = jnp.sum(jnp.where(hit1, seen, 0.0), axis=1, keepdims=True)
    rank2 = jnp.sum(jnp.where(hit2, seen, 0.0), axis=1, keepdims=True)
    cnt_ref[...] = cnt_ref[...] + jnp.sum(onehot, axis=0, keepdims=True)
    fields = (i1, i2, w1, w2, rank1, rank2)
    rt = jnp.zeros((tile, ROUTER_LANES), F32)
    for k, v in enumerate(fields):
        rt = jnp.where(lane == k, v, rt)
    rt_ref[...] = rt
    fld_ref[...] = jnp.transpose(rt)[0:ROUTE_FIELDS, :]


def _out(h0, att, cz, wg, watt, wco, wout, l1g, l1b, wr, br, tile, first_tile, n_tiles):
    T, D = n_tiles * tile, D_MODEL
    own = lambda w: pl.BlockSpec((tile, w), lambda i: (i, 0))
    flat = lambda w: pl.BlockSpec((tile, w), lambda i: (first_tile + i, 0))
    full = lambda a: pl.BlockSpec(a.shape, lambda i: (0,) * a.ndim, pipeline_mode=pl.Buffered(1))
    before = (jnp.arange(tile)[None, :] < jnp.arange(tile)[:, None]).astype(BF16)
    return pl.pallas_call(
        functools.partial(_out_kernel, tile=tile),
        grid=(T // tile,),
        in_specs=[flat(D), flat(ATT_WIDTH), flat(CONV_WIDTH), full(wg),
                  full(watt), full(wco), full(wout), full(l1g), full(l1b), full(wr), full(br), full(before)],
        out_specs=[own(D),
                   pl.BlockSpec((tile // 8, PACK_WORDS, 8, LANES), lambda i: (i, 0, 0, 0)),
                   own(ROUTER_LANES),
                   pl.BlockSpec((ROUTE_FIELDS, tile), lambda i: (0, i)),
                   pl.BlockSpec((8, ROUTER_LANES), lambda i: (0, 0))],
        out_shape=[jax.ShapeDtypeStruct((T, D), F32),
                   jax.ShapeDtypeStruct((T // 8, PACK_WORDS, 8, LANES), jnp.uint32),
                   jax.ShapeDtypeStruct((T, ROUTER_LANES), F32),
                   jax.ShapeDtypeStruct((ROUTE_FIELDS, T), F32),
                   jax.ShapeDtypeStruct((8, ROUTER_LANES), F32)],
        compiler_params=pltpu.CompilerParams(dimension_semantics=("arbitrary",), vmem_limit_bytes=VMEM_LIMIT),
        name="out",
    )(h0, att, cz, wg, watt, wco, wout, l1g, l1b, wr, br, before)


def _sc_scatter2(src, idx_a, idx_b, n_out):
    m = src.shape[0]
    mesh = plsc.VectorSubcoreMesh(core_axis_name="c", subcore_axis_name="s")

    @functools.partial(pl.kernel, out_type=jax.ShapeDtypeStruct((n_out, LANES), src.dtype), mesh=mesh)
    def k(x_hbm, ia_hbm, ib_hbm, o_hbm):
        def body(x_vmem, ia_vmem, ib_vmem):
            pltpu.sync_copy(x_vmem, o_hbm.at[ia_vmem.at[0]])
            pltpu.sync_copy(x_vmem, o_hbm.at[ib_vmem.at[0]])

        pltpu.emit_pipeline(
            body, grid=(m // SC_WINDOW,),
            in_specs=[pl.BlockSpec((SC_WINDOW, LANES), index_map=lambda i: (i, 0)),
                      pl.BlockSpec((1, SC_WINDOW), index_map=lambda i: (i, 0)),
                      pl.BlockSpec((1, SC_WINDOW), index_map=lambda i: (i, 0))],
            out_specs=[],
            core_axis_name=("c", "s"), dimension_semantics=(pltpu.PARALLEL,),
        )(x_hbm, ia_hbm, ib_hbm)

    return k(src, idx_a, idx_b)


def _sc_gather2(table, idx_a, idx_b):
    windows = idx_a.shape[0]
    m = windows * SC_WINDOW
    mesh = plsc.VectorSubcoreMesh(core_axis_name="c", subcore_axis_name="s")

    @functools.partial(pl.kernel, out_type=jax.ShapeDtypeStruct((2 * m, LANES), table.dtype), mesh=mesh)
    def k(x_hbm, ia_hbm, ib_hbm, o_hbm):
        def body(i_vmem, o_vmem):
            pltpu.sync_copy(x_hbm.at[i_vmem.at[0]], o_vmem)

        for half, i_hbm in enumerate((ia_hbm, ib_hbm)):
            pltpu.emit_pipeline(
                body, grid=(windows,),
                in_specs=[pl.BlockSpec((1, SC_WINDOW), index_map=lambda i: (i, 0))],
                out_specs=[pl.BlockSpec((SC_WINDOW, LANES), index_map=lambda i, half=half: (half * windows + i, 0))],
                core_axis_name=("c", "s"), dimension_semantics=(pltpu.PARALLEL,),
            )(i_hbm, o_hbm)

    return k(table, idx_a, idx_b).reshape(2, m, LANES)


def _expert_kernel(te_ref, tv_ref, xs_ref, wg_ref, wu_ref, wd_ref, ys_ref, wgb_ref, wub_ref, wdb_ref, *, tile):
    n = pl.program_id(0)
    valid = tv_ref[n]

    @pl.when(jnp.logical_or(n == 0, te_ref[n] != te_ref[jnp.maximum(n - 1, 0)]))
    def _():
        wgb_ref[...] = wg_ref[...].astype(BF16)
        wub_ref[...] = wu_ref[...].astype(BF16)
        wdb_ref[...] = wd_ref[...].astype(BF16)

    @pl.when(valid > 0)
    def _():
        x = _unpack_bf16_pairs([xs_ref[:, j, :, :].reshape(tile, LANES) for j in range(PACK_WORDS)])
        rows = lax.broadcasted_iota(jnp.int32, (tile, 1), 0)
        x = jnp.where(rows < valid, x, 0.0).astype(BF16)
        gt = _dot(x, wgb_ref[...])
        up = _dot(x, wub_ref[...])
        hid = (gt * jax.nn.sigmoid(gt)) * up
        y = _dot(hid.astype(BF16), wdb_ref[...])
        for j, w in enumerate(_pack_bf16_pairs(y)):
            ys_ref[:, j, :, :] = w.reshape(tile // 8, 8, LANES)

    @pl.when(valid <= 0)
    def _():
        ys_ref[...] = jnp.zeros_like(ys_ref)


def _experts(tile_expert, tile_valid, xs, w_gate, w_up, w_down, tile):
    n_tiles = xs.shape[0] * 8 // tile
    blk = pl.BlockSpec((tile // 8, PACK_WORDS, 8, LANES), lambda n, te, tv: (n, 0, 0, 0))
    return pl.pallas_call(
        functools.partial(_expert_kernel, tile=tile),
        grid_spec=pltpu.PrefetchScalarGridSpec(
            num_scalar_prefetch=2,
            grid=(n_tiles,),
            in_specs=[blk,
                      pl.BlockSpec((None, D_MODEL, D_EXPERT), lambda n, te, tv: (te[n], 0, 0)),
                      pl.BlockSpec((None, D_MODEL, D_EXPERT), lambda n, te, tv: (te[n], 0, 0)),
                      pl.BlockSpec((None, D_EXPERT, D_MODEL), lambda n, te, tv: (te[n], 0, 0))],
            out_specs=blk,
            scratch_shapes=[pltpu.VMEM((D_MODEL, D_EXPERT), BF16), pltpu.VMEM((D_MODEL, D_EXPERT), BF16),
                            pltpu.VMEM((D_EXPERT, D_MODEL), BF16)],
        ),
        out_shape=jax.ShapeDtypeStruct(xs.shape, jnp.uint32),
        compiler_params=pltpu.CompilerParams(
            dimension_semantics=("arbitrary",), vmem_limit_bytes=VMEM_LIMIT),
        name="experts",
    )(tile_expert, tile_valid, xs, w_gate, w_up, w_down)


def _final_kernel(h_ref, g_ref, rt_ref, l2g_ref, l2b_ref, o_ref, *, tile):
    rt = rt_ref[...]
    lane = lax.broadcasted_iota(jnp.int32, rt.shape, 1)
    w1 = jnp.sum(jnp.where(lane == 2, rt, 0.0), axis=1, keepdims=True)
    w2 = jnp.sum(jnp.where(lane == 3, rt, 0.0), axis=1, keepdims=True)
    y1 = _unpack_bf16_pairs([g_ref[0, :, j, :, :].reshape(tile, LANES) for j in range(PACK_WORDS)])
    y2 = _unpack_bf16_pairs([g_ref[1, :, j, :, :].reshape(tile, LANES) for j in range(PACK_WORDS)])
    f = w1 * y1 + w2 * y2
    o_ref[...] = _layer_norm(ALPHA * h_ref[...] + f, l2g_ref[...], l2b_ref[...])


def _final(h1, g, rt, l2g, l2b, tile):
    T, D = h1.shape
    row = lambda w: pl.BlockSpec((tile, w), lambda i: (i, 0))
    full = lambda a: pl.BlockSpec(a.shape, lambda i: (0,) * a.ndim)
    return pl.pallas_call(
        functools.partial(_final_kernel, tile=tile),
        grid=(T // tile,),
        in_specs=[row(D),
                  pl.BlockSpec((2, tile // 8, PACK_WORDS, 8, LANES), lambda i: (0, i, 0, 0, 0)),
                  row(ROUTER_LANES), full(l2g), full(l2b)],
        out_specs=row(D),
        out_shape=jax.ShapeDtypeStruct((T, D), F32),
        compiler_params=pltpu.CompilerParams(dimension_semantics=("parallel",), vmem_limit_bytes=VMEM_LIMIT),
        name="final",
    )(h1, g, rt, l2g, l2b)


def _dispatch_plan(fields, counts, n_tokens, tile):
    cnt = counts[0, :N_EXPERTS].astype(jnp.int32)
    padded = (cnt + tile - 1) // tile * tile
    base = jnp.cumsum(padded) - padded
    e_ids = jnp.arange(N_EXPERTS, dtype=jnp.int32)

    def dest(row_e, row_r):
        e = fields[row_e].astype(jnp.int32)
        seg = jnp.sum(jnp.where(e[None, :] == e_ids[:, None], base[:, None], 0), axis=0)
        pos = seg + fields[row_r].astype(jnp.int32)
        p = pos.reshape(n_tokens // 8, 1, 8)
        j = jnp.arange(PACK_WORDS, dtype=jnp.int32).reshape(1, PACK_WORDS, 1)
        return ((p // 8) * (8 * PACK_WORDS) + j * 8 + p % 8).reshape(n_tokens * PACK_WORDS // SC_WINDOW, SC_WINDOW)

    n_tiles = (2 * n_tokens) // tile + N_EXPERTS
    start = jnp.arange(n_tiles, dtype=jnp.int32) * tile
    seg_end = base + padded
    te = jnp.minimum(jnp.sum((start[:, None] >= seg_end[None, :]).astype(jnp.int32), axis=1), N_EXPERTS - 1)
    te_base = jnp.sum(jnp.where(te[:, None] == e_ids[None, :], base[None, :], 0), axis=1)
    te_cnt = jnp.sum(jnp.where(te[:, None] == e_ids[None, :], cnt[None, :], 0), axis=1)
    tv = jnp.clip(te_cnt - (start - te_base), 0, tile)
    return dest(0, 4), dest(1, 5), te, tv, n_tiles


def _t5_bucket(rel):
    half = N_BUCKETS // 2
    max_exact = half // 2
    ret = jnp.where(rel > 0, half, 0)
    n = jnp.abs(rel)
    nf = jnp.maximum(n, 1).astype(F32)
    large = max_exact + (jnp.log(nf / max_exact) / math.log(MAX_DISTANCE / max_exact)
                         * (half - max_exact)).astype(jnp.int32)
    large = jnp.minimum(large, half - 1)
    return ret + jnp.where(n < max_exact, n, large)


def _bucket_bias(rel_bias, bucket):
    rb = rel_bias.astype(F32)
    out = jnp.zeros((N_Q_HEADS,) + bucket.shape, F32)
    for b in range(N_BUCKETS):
        out = out + jnp.where(bucket[None] == b, rb[b][:, None, None], 0.0)
    return out


def _pair_rows(t):
    return jnp.concatenate([t[:N_PAIRS], t[N_PAIRS:]], axis=-1).reshape(N_PAIRS * BLOCK, -1)


def _bias_tables(rel_bias, sink):
    qi = jnp.arange(BLOCK)
    kj = jnp.arange(3 * BLOCK) - BLOCK
    rel = kj[None, :] - qi[:, None]
    band = _bucket_bias(rel_bias, _t5_bucket(rel))
    in_win = (jnp.abs(rel) <= WINDOW)[None]
    not_prev = (kj >= 0)[None, None, :]
    not_next = (kj < BLOCK)[None, None, :]
    variants = [jnp.where(in_win & not_prev, band, NEG),
                jnp.where(in_win, band, NEG),
                jnp.where(in_win & not_next, band, NEG)]
    bias = jnp.stack([_pair_rows(v) for v in variants])

    off = BLOCK - N_META
    mvars = []
    for blk in (1, 2):
        qpos = blk * BLOCK + qi - off
        meta_rel = jnp.arange(N_META)[None, :] - qpos[:, None]
        mvars.append(_pair_rows(_bucket_bias(rel_bias, _t5_bucket(meta_rel))))
    mbias = jnp.stack(mvars)
    s = sink.astype(F32)
    sink_tab = jnp.repeat(jnp.stack([s[:N_PAIRS], s[N_PAIRS:]], axis=-1), BLOCK, axis=0)
    return bias, mbias, sink_tab


def kernel(x_prompt, x_sample, meta, ln_in_g, ln_in_b, rel_bias, w_in, w_att_branch, sink, conv_w, conv_b,
           conv_ln_g, conv_ln_b, w_conv_out, w_out, ln1_g, ln1_b, w_group, b_group, w_router, b_router,
           w_gate, w_up, w_down, ln2_g, ln2_b):
    row = lambda v: v.reshape(1, -1).astype(F32)
    w = w_in[0]
    wq = (w[:, :Q_END].reshape(D_MODEL, 2, N_PAIRS, HEAD_DIM).transpose(0, 2, 1, 3)
          .reshape(D_MODEL, ATT_WIDTH).astype(BF16))
    watt = (w_att_branch[0].reshape(2, N_PAIRS, HEAD_DIM, D_MODEL).transpose(1, 0, 2, 3)
            .reshape(ATT_WIDTH, D_MODEL).astype(BF16))
    wkv = w[:, Q_END:V_END].astype(BF16)
    wglu = w[:, V_END:GLU_END].astype(BF16)
    wg = w[:, GLU_END:].astype(BF16)
    wco = w_conv_out[0].astype(BF16)
    wout = w_out[0].astype(BF16)
    wr = jnp.zeros((D_MODEL, ROUTER_LANES), F32)
    wr = wr.at[:, :N_EXPERTS].set(w_router[0]).at[:, N_EXPERTS:N_EXPERTS + N_GROUPS].set(w_group[0]).astype(BF16)
    br = jnp.zeros((1, ROUTER_LANES), F32)
    br = br.at[0, :N_EXPERTS].set(b_router[0]).at[0, N_EXPERTS:N_EXPERTS + N_GROUPS].set(b_group[0])
    ln_g, ln_b = row(ln_in_g), row(ln_in_b)
    bias, mbias, sink_tab = _bias_tables(rel_bias, sink[0])

    xm = jnp.concatenate([jnp.zeros((BLOCK - N_META, D_MODEL), F32), meta.astype(F32)], axis=0)
    _, _, kv_m, zc_m = _proj(xm, xm, _Geom(1, 1, 1, 1), ln_g, ln_b, wq, wkv, wglu, BLOCK)
    kv_meta = kv_m[BLOCK - N_META:BLOCK]
    z_meta = zc_m[BLOCK - N_META:BLOCK]

    (bp, sp, _), (bs, ss, _) = x_prompt.shape, x_sample.shape
    xp, xs = x_prompt.reshape(bp * sp, D_MODEL), x_sample.reshape(bs * ss, D_MODEL)
    geom = _geom(x_prompt, x_sample, TILE)
    h0, q, kv, zc = _proj(xp, xs, _geom(x_prompt, x_sample, TILE_PROJ), ln_g, ln_b, wq, wkv, wglu, TILE_PROJ)
    att = _attn(q, kv, _geom(x_prompt, x_sample, TILE_ATTN), kv_meta, bias, mbias, sink_tab, TILE_ATTN)
    cz = _conv(zc, geom, z_meta, conv_w[0], row(conv_b[0]), row(conv_ln_g[0]), row(conv_ln_b[0]), TILE)
    l1g, l1b, l2g, l2b = row(ln1_g[0]), row(ln1_b[0]), row(ln2_g[0]), row(ln2_b[0])

    def moe(first_tile, n_tiles):
        n = n_tiles * TILE
        h1, h1p, rt, fields, counts = _out(h0, att, cz, wg, watt, wco, wout, l1g, l1b, wr, br,
                                           TILE_OUT, first_tile * TILE // TILE_OUT, n_tiles * TILE // TILE_OUT)
        idx1, idx2, tile_expert, tile_valid, n_tiles = _dispatch_plan(fields, counts, n, TILE_EXPERT)
        n_rows = n_tiles * TILE_EXPERT
        xsorted = _sc_scatter2(h1p.reshape(n * PACK_WORDS, LANES), idx1, idx2, n_rows * PACK_WORDS)
        ys = _experts(tile_expert, tile_valid, xsorted.reshape(n_rows // 8, PACK_WORDS, 8, LANES),
                      w_gate[0], w_up[0], w_down[0], TILE_EXPERT)
        g = _sc_gather2(ys.reshape(n_rows * PACK_WORDS, LANES), idx1, idx2)
        return _final(h1, g.reshape(2, n // 8, PACK_WORDS, 8, LANES), rt, l2g, l2b, TILE)

    return moe(0, geom.n_p).reshape(x_prompt.shape), moe(geom.n_p, geom.n_s).reshape(x_sample.shape)
```

```python
import functools
import math
from typing import NamedTuple

import jax
import jax.numpy as jnp
from jax import lax
from jax.experimental import pallas as pl
from jax.experimental.pallas import tpu as pltpu
from jax.experimental.pallas import tpu_sc as plsc

D_MODEL = 1024
N_META = 16
BLOCK = 128
WINDOW = 128
N_Q_HEADS = 8
N_KV_HEADS = 2
HEAD_DIM = 64
ATT_WIDTH = N_Q_HEADS * HEAD_DIM
KV_WIDTH = N_KV_HEADS * HEAD_DIM
CONV_WIDTH = D_MODEL // 2
CONV_K = 31
N_BUCKETS = 32
MAX_DISTANCE = 128
N_GROUPS = 4
EXPERTS_PER_GROUP = 8
N_EXPERTS = N_GROUPS * EXPERTS_PER_GROUP
D_EXPERT = 256
LN_EPS = 1e-5
DEPTH = 1
ALPHA = (2 * DEPTH) ** 0.25
NEG = -1e30
Q_END = ATT_WIDTH
K_END = Q_END + KV_WIDTH
V_END = K_END + KV_WIDTH
GLU_END = V_END + 2 * CONV_WIDTH
GA_END = GLU_END + D_MODEL

N_PAIRS = N_Q_HEADS // 2
LANES = 128
CONV_HALO = 16
ROUTER_LANES = 128

TILE = 512
CONV_ROWS = 128
LN_ROWS = 64
SHIFT_ROWS = 128
TILE_PROJ = 1024
TILE_OUT = 1024
TILE_FINAL = 1024
TILE_ATTN = 2048
TILE_EXPERT = 512
PACK_WORDS = 4
SC_WINDOW = 128
ROUTE_FIELDS = 8
VMEM_LIMIT = 56 * 1024 * 1024

BF16 = jnp.bfloat16
F32 = jnp.float32


def _layer_norm(x, g, b):
    mu = jnp.mean(x, axis=-1, keepdims=True)
    xc = x - mu
    var = jnp.mean(xc * xc, axis=-1, keepdims=True)
    return xc * lax.rsqrt(var + LN_EPS) * g + b


def _dot(a, b):
    return jnp.dot(a, b, preferred_element_type=F32)


def _dot_nt(a, b):
    return lax.dot_general(a, b, (((1,), (1,)), ((), ())), preferred_element_type=F32)


class _Geom(NamedTuple):
    n_p: int
    n_s: int
    tp: int
    ts: int


def _geom(x_prompt, x_sample, tile):
    (bp, sp, _), (bs, ss, _) = x_prompt.shape, x_sample.shape
    return _Geom(bp * sp // tile, bs * ss // tile, sp // tile, ss // tile)


def _seq_pos(t, g):
    is_p = t < g.n_p
    local = jnp.where(is_p, lax.rem(t, g.tp), lax.rem(jnp.maximum(t - g.n_p, 0), g.ts))
    return is_p, local == 0, local == jnp.where(is_p, g.tp - 1, g.ts - 1)


def _x_specs(g, tile, width):
    return [pl.BlockSpec((tile, width), lambda t: (jnp.minimum(t, g.n_p - 1), 0)),
            pl.BlockSpec((tile, width), lambda t: (jnp.maximum(t - g.n_p, 0), 0))]


def _proj_kernel(xp_ref, xs_ref, g_ref, b_ref, wq_ref, wkv_ref, wglu_ref, h0_ref, q_ref, kv_ref, zc_ref, *, geom):
    x = jnp.where(pl.program_id(0) < geom.n_p, xp_ref[...], xs_ref[...])
    h0 = _layer_norm(x, g_ref[...], b_ref[...])
    h0_ref[...] = h0
    h = h0.astype(BF16)
    q_ref[...] = _dot(h, wq_ref[...]).astype(BF16)
    kv_ref[...] = _dot(h, wkv_ref[...]).astype(BF16)
    u = _dot(h, wglu_ref[...])
    zc_ref[...] = (u[:, :CONV_WIDTH] * jax.nn.sigmoid(u[:, CONV_WIDTH:])).astype(BF16)


def _proj(xp, xs, geom, ln_g, ln_b, wq, wkv, wglu, tile):
    T = (geom.n_p + geom.n_s) * tile
    row = lambda w: pl.BlockSpec((tile, w), lambda t: (t, 0))
    full = lambda a: pl.BlockSpec(a.shape, lambda t: (0,) * a.ndim)
    return pl.pallas_call(
        functools.partial(_proj_kernel, geom=geom),
        grid=(geom.n_p + geom.n_s,),
        in_specs=_x_specs(geom, tile, D_MODEL) + [full(ln_g), full(ln_b), full(wq), full(wkv), full(wglu)],
        out_specs=[row(D_MODEL), row(ATT_WIDTH), row(2 * KV_WIDTH), row(CONV_WIDTH)],
        out_shape=[
            jax.ShapeDtypeStruct((T, D_MODEL), F32),
            jax.ShapeDtypeStruct((T, ATT_WIDTH), BF16),
            jax.ShapeDtypeStruct((T, 2 * KV_WIDTH), BF16),
            jax.ShapeDtypeStruct((T, CONV_WIDTH), BF16),
        ],
        compiler_params=pltpu.CompilerParams(dimension_semantics=("parallel",), vmem_limit_bytes=VMEM_LIMIT),
        name="proj",
    )(xp, xs, ln_g, ln_b, wq, wkv, wglu)


def _attn_kernel(q_ref, kvp_ref, kvc_ref, kvn_ref, kvm_ref, bias_ref, mbias_ref, sink_ref, o_ref, *, tile, geom):
    _, seq_first, seq_last = _seq_pos(pl.program_id(0), geom)
    blocks = tile // BLOCK
    scale = HEAD_DIM ** -0.5

    lane = lax.broadcasted_iota(jnp.int32, (1, LANES), 1)
    lo = lane < HEAD_DIM

    def split_heads(t):
        z = jnp.zeros_like(t)
        return jnp.where(lo, t, z), jnp.where(lo, z, t)

    kv_ext = jnp.concatenate([kvp_ref[...], kvc_ref[...], kvn_ref[...]], axis=0)
    k_ext = kv_ext[:, :KV_WIDTH] * jnp.asarray(scale, BF16)
    v_ext = kv_ext[:, KV_WIDTH:]
    ka, kb = split_heads(k_ext)
    va, vb = split_heads(v_ext)
    kma, kmb = split_heads(kvm_ref[:, :KV_WIDTH] * jnp.asarray(scale, BF16))
    vma, vmb = split_heads(kvm_ref[:, KV_WIDTH:])
    km_cat = jnp.concatenate([kma, kmb], axis=0)
    vm_cat = jnp.concatenate([vma, vmb], axis=0)

    mlane = lax.broadcasted_iota(jnp.int32, (1, 2 * N_META), 1)
    m_first = mlane < N_META
    sink = sink_ref[...]
    lane_o = lax.broadcasted_iota(jnp.int32, (1, LANES), 1) < HEAD_DIM

    for j in range(blocks):
        first = jnp.logical_and(seq_first, j == 0)
        last = jnp.logical_and(seq_last, j == blocks - 1)
        variant = jnp.where(first, 0, jnp.where(last, 2, 1))
        mvariant = jnp.where(first, 0, 1)

        r0 = j * BLOCK
        qb = q_ref[r0:r0 + BLOCK, :]
        q4 = jnp.concatenate([qb[:, p * LANES:(p + 1) * LANES] for p in range(N_PAIRS)], axis=0)
        k_cat = jnp.concatenate([ka[r0:r0 + 3 * BLOCK], kb[r0:r0 + 3 * BLOCK]], axis=0)
        v_cat = jnp.concatenate([va[r0:r0 + 3 * BLOCK], vb[r0:r0 + 3 * BLOCK]], axis=0)

        s = _dot_nt(q4, k_cat) + bias_ref[variant]
        sm = _dot_nt(q4, km_cat) + mbias_ref[mvariant]

        s_a, s_b = s[:, :3 * BLOCK], s[:, 3 * BLOCK:]
        sm_a = jnp.where(m_first, sm, NEG)
        sm_b = jnp.where(m_first, NEG, sm)
        m_a = jnp.maximum(jnp.maximum(jnp.max(s_a, axis=1, keepdims=True),
                                      jnp.max(sm_a, axis=1, keepdims=True)), sink[:, 0:1])
        m_b = jnp.maximum(jnp.maximum(jnp.max(s_b, axis=1, keepdims=True),
                                      jnp.max(sm_b, axis=1, keepdims=True)), sink[:, 1:2])
        p_a = jnp.exp(s_a - m_a)
        p_b = jnp.exp(s_b - m_b)
        pm = jnp.exp(jnp.where(m_first, sm - m_a, sm - m_b))
        l_a = (jnp.sum(p_a, axis=1, keepdims=True) + jnp.sum(jnp.where(m_first, pm, 0.0), axis=1, keepdims=True)
               + jnp.exp(sink[:, 0:1] - m_a))
        l_b = (jnp.sum(p_b, axis=1, keepdims=True) + jnp.sum(jnp.where(m_first, 0.0, pm), axis=1, keepdims=True)
               + jnp.exp(sink[:, 1:2] - m_b))
        p = jnp.concatenate([p_a, p_b], axis=1).astype(BF16)
        o = _dot(p, v_cat) + _dot(pm.astype(BF16), vm_cat)
        o = o * jnp.where(lane_o, 1.0 / l_a, 1.0 / l_b)
        for pr in range(N_PAIRS):
            o_ref[r0:r0 + BLOCK, pr * LANES:(pr + 1) * LANES] = o[pr * BLOCK:(pr + 1) * BLOCK].astype(BF16)


def _attn(q, kv, geom, kv_meta, bias, mbias, sink_tab, tile):
    T = q.shape[0]
    bpt = tile // BLOCK
    n_blocks = T // BLOCK
    full = lambda a: pl.BlockSpec(a.shape, lambda t: (0,) * a.ndim)
    return pl.pallas_call(
        functools.partial(_attn_kernel, tile=tile, geom=geom),
        grid=(T // tile,),
        in_specs=[
            pl.BlockSpec((tile, ATT_WIDTH), lambda t: (t, 0)),
            pl.BlockSpec((BLOCK, 2 * KV_WIDTH), lambda t: (jnp.maximum(t * bpt - 1, 0), 0)),
            pl.BlockSpec((tile, 2 * KV_WIDTH), lambda t: (t, 0)),
            pl.BlockSpec((BLOCK, 2 * KV_WIDTH), lambda t: (jnp.minimum((t + 1) * bpt, n_blocks - 1), 0)),
            full(kv_meta), full(bias), full(mbias), full(sink_tab),
        ],
        out_specs=pl.BlockSpec((tile, ATT_WIDTH), lambda t: (t, 0)),
        out_shape=jax.ShapeDtypeStruct((T, ATT_WIDTH), BF16),
        compiler_params=pltpu.CompilerParams(dimension_semantics=("parallel",), vmem_limit_bytes=VMEM_LIMIT),
        name="attn",
    )(q, kv, kv, kv, kv_meta, bias, mbias, sink_tab)


def _conv_kernel(zp_ref, zc_ref, zn_ref, zm_ref, w_ref, cb_ref, g_ref, b_ref, o_ref, ext_ref, sh_ref, y_ref,
                 *, tile, geom):
    i = pl.program_id(0)
    _, seq_first, seq_last = _seq_pos(i, geom)
    ext_ref[0:CONV_HALO, :] = jnp.where(seq_first, zm_ref[...], zp_ref[...]).astype(F32)
    ext_ref[CONV_HALO:CONV_HALO + tile, :] = zc_ref[...].astype(F32)
    ext_ref[CONV_HALO + tile:, :] = jnp.where(seq_last, 0.0, zn_ref[...].astype(F32))
    off = CONV_HALO - CONV_K // 2
    reach = (off + CONV_K - 1) // 8 * 8
    for p in range(1, 8):
        for r0 in range(0, tile + reach, SHIFT_ROWS):
            n = min(SHIFT_ROWS, tile + reach - r0)
            sh_ref[p - 1, r0:r0 + n, :] = ext_ref[r0 + p:r0 + p + n, :]

    def taps(r0, cs):
        acc = jnp.zeros((CONV_ROWS, LANES), F32)
        for k in range(CONV_K):
            p, a = (off + k) % 8, (off + k) // 8 * 8
            rows = slice(r0 + a, r0 + a + CONV_ROWS)
            win = ext_ref[rows, cs] if p == 0 else sh_ref[p - 1, rows, cs]
            acc = acc + win * w_ref[k:k + 1, cs]
        y_ref[r0:r0 + CONV_ROWS, cs] = acc

    for c in range(CONV_WIDTH // LANES):
        for r in range(tile // CONV_ROWS):
            pl.when(i >= 0)(functools.partial(taps, r * CONV_ROWS, slice(c * LANES, (c + 1) * LANES)))
    cb, g, b = cb_ref[...], g_ref[...], b_ref[...]
    for r in range(tile // LN_ROWS):
        r0 = r * LN_ROWS
        y = _layer_norm(y_ref[r0:r0 + LN_ROWS, :] + cb, g, b)
        o_ref[r0:r0 + LN_ROWS, :] = (y * jax.nn.sigmoid(y)).astype(BF16)


def _conv(zc, geom, z_meta, conv_w, conv_b, ln_g, ln_b, tile):
    T, C = zc.shape
    hpt = tile // CONV_HALO
    n_halo = T // CONV_HALO
    full = lambda a: pl.BlockSpec(a.shape, lambda t: (0,) * a.ndim)
    return pl.pallas_call(
        functools.partial(_conv_kernel, tile=tile, geom=geom),
        grid=(T // tile,),
        in_specs=[
            pl.BlockSpec((CONV_HALO, C), lambda t: (jnp.maximum(t * hpt - 1, 0), 0)),
            pl.BlockSpec((tile, C), lambda t: (t, 0)),
            pl.BlockSpec((CONV_HALO, C), lambda t: (jnp.minimum((t + 1) * hpt, n_halo - 1), 0)),
            full(z_meta), full(conv_w), full(conv_b), full(ln_g), full(ln_b),
        ],
        out_specs=pl.BlockSpec((tile, C), lambda t: (t, 0)),
        out_shape=jax.ShapeDtypeStruct((T, C), BF16),
        scratch_shapes=[pltpu.VMEM((tile + 2 * CONV_HALO, C), F32),
                        pltpu.VMEM((7, tile + 2 * CONV_HALO - 8, C), F32),
                        pltpu.VMEM((tile, C), F32)],
        compiler_params=pltpu.CompilerParams(dimension_semantics=("parallel",), vmem_limit_bytes=VMEM_LIMIT),
        name="conv",
    )(zc, zc, zc, z_meta, conv_w, conv_b, ln_g, ln_b)


def _route(r):
    lane_i = lax.broadcasted_iota(jnp.int32, r.shape, 1)
    lane = lane_i.astype(F32)
    big = float(1 << 20)
    is_g = jnp.logical_and(lane_i >= N_EXPERTS, lane_i < N_EXPERTS + N_GROUPS)
    lg = jnp.where(is_g, r, -jnp.inf)
    mg = jnp.max(lg, axis=1, keepdims=True)
    g_w = 1.0 / jnp.sum(jnp.exp(lg - mg), axis=1, keepdims=True)
    g_idx = jnp.min(jnp.where(lg == mg, lane - N_EXPERTS, big), axis=1, keepdims=True)
    lane_group = jnp.right_shift(lane_i, 3).astype(F32)
    in_group = jnp.logical_and(lane_i < N_EXPERTS, lane_group == g_idx)
    le = jnp.where(in_group, r, -jnp.inf)
    m1 = jnp.max(le, axis=1, keepdims=True)
    den = jnp.sum(jnp.exp(le - m1), axis=1, keepdims=True)
    i1 = jnp.min(jnp.where(le == m1, lane, big), axis=1, keepdims=True)
    le2 = jnp.where(lane == i1, -jnp.inf, le)
    m2 = jnp.max(le2, axis=1, keepdims=True)
    i2 = jnp.min(jnp.where(le2 == m2, lane, big), axis=1, keepdims=True)
    p1 = 1.0 / den
    p2 = jnp.exp(m2 - m1) / den
    tot = p1 + p2
    return i1, i2, g_w * (p1 / tot), g_w * (p2 / tot)


def _pack_bf16_pairs(x):
    half = x.shape[1] // 2
    words = []
    for j in range(half // LANES):
        lo = pltpu.bitcast(x[:, j * LANES:(j + 1) * LANES].astype(BF16).astype(F32), jnp.uint32)
        hi = pltpu.bitcast(x[:, half + j * LANES:half + (j + 1) * LANES].astype(BF16).astype(F32), jnp.uint32)
        words.append(hi | (lo >> 16))
    return words


def _unpack_bf16_pairs(words):
    lo = [pltpu.bitcast(w << 16, F32) for w in words]
    hi = [pltpu.bitcast(w & jnp.uint32(0xFFFF0000), F32) for w in words]
    return jnp.concatenate(lo + hi, axis=1)


def _out_kernel(h0_ref, att_ref, cz_ref, wg_ref, watt_ref, wco_ref, wout_ref,
                l1g_ref, l1b_ref, wr_ref, br_ref, before_ref, h1_ref, h1p_ref, rt_ref, fld_ref, cnt_ref, *, tile):
    @pl.when(pl.program_id(0) == 0)
    def _():
        cnt_ref[...] = jnp.zeros_like(cnt_ref)

    h0 = h0_ref[...]
    hb = h0.astype(BF16)
    g_att = jax.nn.sigmoid(_dot(hb, wg_ref[:, :D_MODEL]))
    mix = g_att * _dot(att_ref[...], watt_ref[...])
    g_conv = jax.nn.sigmoid(_dot(hb, wg_ref[:, D_MODEL:]))
    mix = mix + g_conv * _dot(cz_ref[...], wco_ref[...])
    m = _dot(mix.astype(BF16), wout_ref[...])
    h1 = _layer_norm(ALPHA * h0 + m, l1g_ref[...], l1b_ref[...])
    h1_ref[...] = h1
    for j, w in enumerate(_pack_bf16_pairs(h1)):
        h1p_ref[:, j, :, :] = w.reshape(tile // 8, 8, LANES)

    r = _dot(h1.astype(BF16), wr_ref[...]) + br_ref[...]
    i1, i2, w1, w2 = _route(r)
    lane = lax.broadcasted_iota(jnp.int32, (tile, ROUTER_LANES), 1)
    lane_f = lane.astype(F32)
    hit1, hit2 = lane_f == i1, lane_f == i2
    onehot = jnp.where(jnp.logical_or(hit1, hit2), 1.0, 0.0)
    seen = _dot(before_ref[...], onehot.astype(BF16)) + cnt_ref[0:1, :]
    rank1 = jnp.sum(jnp.where(hit1, seen, 0.0), axis=1, keepdims=True)
    rank2 = jnp.sum(jnp.where(hit2, seen, 0.0), axis=1, keepdims=True)
    cnt_ref[...] = cnt_ref[...] + jnp.sum(onehot, axis=0, keepdims=True)
    fields = (i1, i2, w1, w2, rank1, rank2)
    rt = jnp.zeros((tile, ROUTER_LANES), F32)
    for k, v in enumerate(fields):
        rt = jnp.where(lane == k, v, rt)
    rt_ref[...] = rt
    fld_ref[...] = jnp.transpose(rt)[0:ROUTE_FIELDS, :]


def _out(h0, att, cz, wg, watt, wco, wout, l1g, l1b, wr, br, tile, first_tile, n_tiles):
    T, D = n_tiles * tile, D_MODEL
    own = lambda w: pl.BlockSpec((tile, w), lambda i: (i, 0))
    flat = lambda w: pl.BlockSpec((tile, w), lambda i: (first_tile + i, 0))
    full = lambda a: pl.BlockSpec(a.shape, lambda i: (0,) * a.ndim, pipeline_mode=pl.Buffered(1))
    before = (jnp.arange(tile)[None, :] < jnp.arange(tile)[:, None]).astype(BF16)
    return pl.pallas_call(
        functools.partial(_out_kernel, tile=tile),
        grid=(T // tile,),
        in_specs=[flat(D), flat(ATT_WIDTH), flat(CONV_WIDTH), full(wg),
                  full(watt), full(wco), full(wout), full(l1g), full(l1b), full(wr), full(br), full(before)],
        out_specs=[own(D),
                   pl.BlockSpec((tile // 8, PACK_WORDS, 8, LANES), lambda i: (i, 0, 0, 0)),
                   own(ROUTER_LANES),
                   pl.BlockSpec((ROUTE_FIELDS, tile), lambda i: (0, i)),
                   pl.BlockSpec((8, ROUTER_LANES), lambda i: (0, 0))],
        out_shape=[jax.ShapeDtypeStruct((T, D), F32),
                   jax.ShapeDtypeStruct((T // 8, PACK_WORDS, 8, LANES), jnp.uint32),
                   jax.ShapeDtypeStruct((T, ROUTER_LANES), F32),
                   jax.ShapeDtypeStruct((ROUTE_FIELDS, T), F32),
                   jax.ShapeDtypeStruct((8, ROUTER_LANES), F32)],
        compiler_params=pltpu.CompilerParams(dimension_semantics=("arbitrary",), vmem_limit_bytes=VMEM_LIMIT),
        name="out",
    )(h0, att, cz, wg, watt, wco, wout, l1g, l1b, wr, br, before)


def _sc_scatter2(src, idx_a, idx_b, n_out):
    m = src.shape[0]
    mesh = plsc.VectorSubcoreMesh(core_axis_name="c", subcore_axis_name="s")

    @functools.partial(pl.kernel, out_type=jax.ShapeDtypeStruct((n_out, LANES), src.dtype), mesh=mesh)
    def k(x_hbm, ia_hbm, ib_hbm, o_hbm):
        def body(x_vmem, ia_vmem, ib_vmem):
            pltpu.sync_copy(x_vmem, o_hbm.at[ia_vmem.at[0]])
            pltpu.sync_copy(x_vmem, o_hbm.at[ib_vmem.at[0]])

        pltpu.emit_pipeline(
            body, grid=(m // SC_WINDOW,),
            in_specs=[pl.BlockSpec((SC_WINDOW, LANES), index_map=lambda i: (i, 0)),
                      pl.BlockSpec((1, SC_WINDOW), index_map=lambda i: (i, 0)),
                      pl.BlockSpec((1, SC_WINDOW), index_map=lambda i: (i, 0))],
            out_specs=[],
            core_axis_name=("c", "s"), dimension_semantics=(pltpu.PARALLEL,),
        )(x_hbm, ia_hbm, ib_hbm)

    return k(src, idx_a, idx_b)


def _sc_gather2(table, idx_a, idx_b):
    windows = idx_a.shape[0]
    m = windows * SC_WINDOW
    mesh = plsc.VectorSubcoreMesh(core_axis_name="c", subcore_axis_name="s")

    @functools.partial(pl.kernel, out_type=jax.ShapeDtypeStruct((2 * m, LANES), table.dtype), mesh=mesh)
    def k(x_hbm, ia_hbm, ib_hbm, o_hbm):
        def body(i_vmem, o_vmem):
            pltpu.sync_copy(x_hbm.at[i_vmem.at[0]], o_vmem)

        for half, i_hbm in enumerate((ia_hbm, ib_hbm)):
            pltpu.emit_pipeline(
                body, grid=(windows,),
                in_specs=[pl.BlockSpec((1, SC_WINDOW), index_map=lambda i: (i, 0))],
                out_specs=[pl.BlockSpec((SC_WINDOW, LANES), index_map=lambda i, half=half: (half * windows + i, 0))],
                core_axis_name=("c", "s"), dimension_semantics=(pltpu.PARALLEL,),
            )(i_hbm, o_hbm)

    return k(table, idx_a, idx_b).reshape(2, m, LANES)


def _expert_kernel(te_ref, tv_ref, used_ref, xs_ref, wg_ref, wu_ref, wd_ref, ys_ref, wgb_ref, wub_ref, wdb_ref,
                   *, tile):
    n = pl.program_id(0)
    valid = tv_ref[n]

    @pl.when(jnp.logical_or(n == 0, te_ref[n] != te_ref[jnp.maximum(n - 1, 0)]))
    def _():
        wgb_ref[...] = wg_ref[...].astype(BF16)
        wub_ref[...] = wu_ref[...].astype(BF16)
        wdb_ref[...] = wd_ref[...].astype(BF16)

    @pl.when(n < used_ref[0])
    def _():
        x = _unpack_bf16_pairs([xs_ref[:, j, :, :].reshape(tile, LANES) for j in range(PACK_WORDS)])
        rows = lax.broadcasted_iota(jnp.int32, (tile, 1), 0)
        x = jnp.where(rows < valid, x, 0.0).astype(BF16)
        gt = _dot(x, wgb_ref[...])
        up = _dot(x, wub_ref[...])
        hid = (gt * jax.nn.sigmoid(gt)) * up
        y = _dot(hid.astype(BF16), wdb_ref[...])
        for j, w in enumerate(_pack_bf16_pairs(y)):
            ys_ref[:, j, :, :] = w.reshape(tile // 8, 8, LANES)


def _experts(tile_expert, tile_valid, tiles_used, xs, w_gate, w_up, w_down, tile):
    n_tiles = xs.shape[0] * 8 // tile
    blk = pl.BlockSpec((tile // 8, PACK_WORDS, 8, LANES),
                       lambda n, te, tv, used: (jnp.minimum(n, jnp.maximum(used[0] - 1, 0)), 0, 0, 0))
    return pl.pallas_call(
        functools.partial(_expert_kernel, tile=tile),
        grid_spec=pltpu.PrefetchScalarGridSpec(
            num_scalar_prefetch=3,
            grid=(n_tiles,),
            in_specs=[blk,
                      pl.BlockSpec((None, D_MODEL, D_EXPERT), lambda n, te, tv, used: (te[n], 0, 0)),
                      pl.BlockSpec((None, D_MODEL, D_EXPERT), lambda n, te, tv, used: (te[n], 0, 0)),
                      pl.BlockSpec((None, D_EXPERT, D_MODEL), lambda n, te, tv, used: (te[n], 0, 0))],
            out_specs=blk,
            scratch_shapes=[pltpu.VMEM((D_MODEL, D_EXPERT), BF16), pltpu.VMEM((D_MODEL, D_EXPERT), BF16),
                            pltpu.VMEM((D_EXPERT, D_MODEL), BF16)],
        ),
        out_shape=jax.ShapeDtypeStruct(xs.shape, jnp.uint32),
        compiler_params=pltpu.CompilerParams(
            dimension_semantics=("arbitrary",), vmem_limit_bytes=VMEM_LIMIT),
        name="experts",
    )(tile_expert, tile_valid, tiles_used, xs, w_gate, w_up, w_down)


def _final_kernel(h_ref, g_ref, rt_ref, l2g_ref, l2b_ref, o_ref, *, tile):
    rt = rt_ref[...]
    lane = lax.broadcasted_iota(jnp.int32, rt.shape, 1)
    w1 = jnp.sum(jnp.where(lane == 2, rt, 0.0), axis=1, keepdims=True)
    w2 = jnp.sum(jnp.where(lane == 3, rt, 0.0), axis=1, keepdims=True)
    y1 = _unpack_bf16_pairs([g_ref[0, :, j, :, :].reshape(tile, LANES) for j in range(PACK_WORDS)])
    y2 = _unpack_bf16_pairs([g_ref[1, :, j, :, :].reshape(tile, LANES) for j in range(PACK_WORDS)])
    f = w1 * y1 + w2 * y2
    o_ref[...] = _layer_norm(ALPHA * h_ref[...] + f, l2g_ref[...], l2b_ref[...])


def _final(h1, g, rt, l2g, l2b, tile):
    T, D = h1.shape
    row = lambda w: pl.BlockSpec((tile, w), lambda i: (i, 0))
    full = lambda a: pl.BlockSpec(a.shape, lambda i: (0,) * a.ndim)
    return pl.pallas_call(
        functools.partial(_final_kernel, tile=tile),
        grid=(T // tile,),
        in_specs=[row(D),
                  pl.BlockSpec((2, tile // 8, PACK_WORDS, 8, LANES), lambda i: (0, i, 0, 0, 0)),
                  row(ROUTER_LANES), full(l2g), full(l2b)],
        out_specs=row(D),
        out_shape=jax.ShapeDtypeStruct((T, D), F32),
        compiler_params=pltpu.CompilerParams(dimension_semantics=("parallel",), vmem_limit_bytes=VMEM_LIMIT),
        name="final",
    )(h1, g, rt, l2g, l2b)


def _dispatch_plan(fields, counts, n_tokens, tile):
    cnt = counts[0, :N_EXPERTS].astype(jnp.int32)
    padded = (cnt + tile - 1) // tile * tile
    base = jnp.cumsum(padded) - padded
    e_ids = jnp.arange(N_EXPERTS, dtype=jnp.int32)

    def dest(row_e, row_r):
        e = fields[row_e].astype(jnp.int32)
        seg = jnp.sum(jnp.where(e[None, :] == e_ids[:, None], base[:, None], 0), axis=0)
        pos = seg + fields[row_r].astype(jnp.int32)
        p = pos.reshape(n_tokens // 8, 1, 8)
        j = jnp.arange(PACK_WORDS, dtype=jnp.int32).reshape(1, PACK_WORDS, 1)
        return ((p // 8) * (8 * PACK_WORDS) + j * 8 + p % 8).reshape(n_tokens * PACK_WORDS // SC_WINDOW, SC_WINDOW)

    n_tiles = (2 * n_tokens) // tile + N_EXPERTS
    start = jnp.arange(n_tiles, dtype=jnp.int32) * tile
    seg_end = base + padded
    te = jnp.minimum(jnp.sum((start[:, None] >= seg_end[None, :]).astype(jnp.int32), axis=1), N_EXPERTS - 1)
    te_base = jnp.sum(jnp.where(te[:, None] == e_ids[None, :], base[None, :], 0), axis=1)
    te_cnt = jnp.sum(jnp.where(te[:, None] == e_ids[None, :], cnt[None, :], 0), axis=1)
    tv = jnp.clip(te_cnt - (start - te_base), 0, tile)
    tiles_used = (jnp.sum(padded) // tile).reshape(1)
    return dest(0, 4), dest(1, 5), te, tv, tiles_used, n_tiles


def _t5_bucket(rel):
    half = N_BUCKETS // 2
    max_exact = half // 2
    ret = jnp.where(rel > 0, half, 0)
    n = jnp.abs(rel)
    nf = jnp.maximum(n, 1).astype(F32)
    large = max_exact + (jnp.log(nf / max_exact) / math.log(MAX_DISTANCE / max_exact)
                         * (half - max_exact)).astype(jnp.int32)
    large = jnp.minimum(large, half - 1)
    return ret + jnp.where(n < max_exact, n, large)


def _bucket_bias(rel_bias, bucket):
    rb = rel_bias.astype(F32)
    out = jnp.zeros((N_Q_HEADS,) + bucket.shape, F32)
    for b in range(N_BUCKETS):
        out = out + jnp.where(bucket[None] == b, rb[b][:, None, None], 0.0)
    return out


def _pair_rows(t):
    return jnp.concatenate([t[:N_PAIRS], t[N_PAIRS:]], axis=-1).reshape(N_PAIRS * BLOCK, -1)


def _bias_tables(rel_bias, sink):
    qi = jnp.arange(BLOCK)
    kj = jnp.arange(3 * BLOCK) - BLOCK
    rel = kj[None, :] - qi[:, None]
    band = _bucket_bias(rel_bias, _t5_bucket(rel))
    in_win = (jnp.abs(rel) <= WINDOW)[None]
    not_prev = (kj >= 0)[None, None, :]
    not_next = (kj < BLOCK)[None, None, :]
    variants = [jnp.where(in_win & not_prev, band, NEG),
                jnp.where(in_win, band, NEG),
                jnp.where(in_win & not_next, band, NEG)]
    bias = jnp.stack([_pair_rows(v) for v in variants])

    off = BLOCK - N_META
    mvars = []
    for blk in (1, 2):
        qpos = blk * BLOCK + qi - off
        meta_rel = jnp.arange(N_META)[None, :] - qpos[:, None]
        mvars.append(_pair_rows(_bucket_bias(rel_bias, _t5_bucket(meta_rel))))
    mbias = jnp.stack(mvars)
    s = sink.astype(F32)
    sink_tab = jnp.repeat(jnp.stack([s[:N_PAIRS], s[N_PAIRS:]], axis=-1), BLOCK, axis=0)
    return bias, mbias, sink_tab


def kernel(x_prompt, x_sample, meta, ln_in_g, ln_in_b, rel_bias, w_in, w_att_branch, sink, conv_w, conv_b,
           conv_ln_g, conv_ln_b, w_conv_out, w_out, ln1_g, ln1_b, w_group, b_group, w_router, b_router,
           w_gate, w_up, w_down, ln2_g, ln2_b):
    row = lambda v: v.reshape(1, -1).astype(F32)
    w = w_in[0]
    wq = (w[:, :Q_END].reshape(D_MODEL, 2, N_PAIRS, HEAD_DIM).transpose(0, 2, 1, 3)
          .reshape(D_MODEL, ATT_WIDTH).astype(BF16))
    watt = (w_att_branch[0].reshape(2, N_PAIRS, HEAD_DIM, D_MODEL).transpose(1, 0, 2, 3)
            .reshape(ATT_WIDTH, D_MODEL).astype(BF16))
    wkv = w[:, Q_END:V_END].astype(BF16)
    wglu = w[:, V_END:GLU_END].astype(BF16)
    wg = w[:, GLU_END:].astype(BF16)
    wco = w_conv_out[0].astype(BF16)
    wout = w_out[0].astype(BF16)
    wr = jnp.zeros((D_MODEL, ROUTER_LANES), F32)
    wr = wr.at[:, :N_EXPERTS].set(w_router[0]).at[:, N_EXPERTS:N_EXPERTS + N_GROUPS].set(w_group[0]).astype(BF16)
    br = jnp.zeros((1, ROUTER_LANES), F32)
    br = br.at[0, :N_EXPERTS].set(b_router[0]).at[0, N_EXPERTS:N_EXPERTS + N_GROUPS].set(b_group[0])
    ln_g, ln_b = row(ln_in_g), row(ln_in_b)
    bias, mbias, sink_tab = _bias_tables(rel_bias, sink[0])

    xm = jnp.concatenate([jnp.zeros((BLOCK - N_META, D_MODEL), F32), meta.astype(F32)], axis=0)
    _, _, kv_m, zc_m = _proj(xm, xm, _Geom(1, 1, 1, 1), ln_g, ln_b, wq, wkv, wglu, BLOCK)
    kv_meta = kv_m[BLOCK - N_META:BLOCK]
    z_meta = zc_m[BLOCK - N_META:BLOCK]

    (bp, sp, _), (bs, ss, _) = x_prompt.shape, x_sample.shape
    xp, xs = x_prompt.reshape(bp * sp, D_MODEL), x_sample.reshape(bs * ss, D_MODEL)
    geom = _geom(x_prompt, x_sample, TILE)
    h0, q, kv, zc = _proj(xp, xs, _geom(x_prompt, x_sample, TILE_PROJ), ln_g, ln_b, wq, wkv, wglu, TILE_PROJ)
    att = _attn(q, kv, _geom(x_prompt, x_sample, TILE_ATTN), kv_meta, bias, mbias, sink_tab, TILE_ATTN)
    cz = _conv(zc, geom, z_meta, conv_w[0], row(conv_b[0]), row(conv_ln_g[0]), row(conv_ln_b[0]), TILE)
    l1g, l1b, l2g, l2b = row(ln1_g[0]), row(ln1_b[0]), row(ln2_g[0]), row(ln2_b[0])

    def moe(first_tile, n_tiles):
        n = n_tiles * TILE
        h1, h1p, rt, fields, counts = _out(h0, att, cz, wg, watt, wco, wout, l1g, l1b, wr, br,
                                           TILE_OUT, first_tile * TILE // TILE_OUT, n_tiles * TILE // TILE_OUT)
        idx1, idx2, tile_expert, tile_valid, tiles_used, n_tiles = _dispatch_plan(fields, counts, n, TILE_EXPERT)
        n_rows = n_tiles * TILE_EXPERT
        xsorted = _sc_scatter2(h1p.reshape(n * PACK_WORDS, LANES), idx1, idx2, n_rows * PACK_WORDS)
        ys = _experts(tile_expert, tile_valid, tiles_used, xsorted.reshape(n_rows // 8, PACK_WORDS, 8, LANES),
                      w_gate[0], w_up[0], w_down[0], TILE_EXPERT)
        g = _sc_gather2(ys.reshape(n_rows * PACK_WORDS, LANES), idx1, idx2)
        return _final(h1, g.reshape(2, n // 8, PACK_WORDS, 8, LANES), rt, l2g, l2b, TILE_FINAL)

    return moe(0, geom.n_p).reshape(x_prompt.shape), moe(geom.n_p, geom.n_s).reshape(x_sample.shape)
```

```python
import functools
import math
from typing import NamedTuple

import jax
import jax.numpy as jnp
from jax import lax
from jax.experimental import pallas as pl
from jax.experimental.pallas import tpu as pltpu
from jax.experimental.pallas import tpu_sc as plsc

D_MODEL = 1024
N_META = 16
BLOCK = 128
WINDOW = 128
N_Q_HEADS = 8
N_KV_HEADS = 2
HEAD_DIM = 64
ATT_WIDTH = N_Q_HEADS * HEAD_DIM
KV_WIDTH = N_KV_HEADS * HEAD_DIM
CONV_WIDTH = D_MODEL // 2
CONV_K = 31
N_BUCKETS = 32
MAX_DISTANCE = 128
N_GROUPS = 4
EXPERTS_PER_GROUP = 8
N_EXPERTS = N_GROUPS * EXPERTS_PER_GROUP
D_EXPERT = 256
LN_EPS = 1e-5
DEPTH = 1
ALPHA = (2 * DEPTH) ** 0.25
NEG = -1e30
Q_END = ATT_WIDTH
K_END = Q_END + KV_WIDTH
V_END = K_END + KV_WIDTH
GLU_END = V_END + 2 * CONV_WIDTH
GA_END = GLU_END + D_MODEL

N_PAIRS = N_Q_HEADS // 2
LANES = 128
CONV_HALO = 16
ROUTER_LANES = 128

TILE = 512
CONV_ROWS = 128
LN_ROWS = 64
SHIFT_ROWS = 128
TILE_PROJ = 1024
TILE_OUT = 1024
XS_SLOTS = 3
TILE_FINAL = 1024
TILE_ATTN = 2048
TILE_EXPERT = 512
PACK_WORDS = 4
SC_WINDOW = 128
ROUTE_FIELDS = 8
VMEM_LIMIT = 56 * 1024 * 1024

BF16 = jnp.bfloat16
F32 = jnp.float32


def _layer_norm(x, g, b):
    mu = jnp.mean(x, axis=-1, keepdims=True)
    xc = x - mu
    var = jnp.mean(xc * xc, axis=-1, keepdims=True)
    return xc * lax.rsqrt(var + LN_EPS) * g + b


def _dot(a, b):
    return jnp.dot(a, b, preferred_element_type=F32)


def _dot_nt(a, b):
    return lax.dot_general(a, b, (((1,), (1,)), ((), ())), preferred_element_type=F32)


class _Geom(NamedTuple):
    n_p: int
    n_s: int
    tp: int
    ts: int


def _geom(x_prompt, x_sample, tile):
    (bp, sp, _), (bs, ss, _) = x_prompt.shape, x_sample.shape
    return _Geom(bp * sp // tile, bs * ss // tile, sp // tile, ss // tile)


def _seq_pos(t, g):
    is_p = t < g.n_p
    local = jnp.where(is_p, lax.rem(t, g.tp), lax.rem(jnp.maximum(t - g.n_p, 0), g.ts))
    return is_p, local == 0, local == jnp.where(is_p, g.tp - 1, g.ts - 1)


def _x_specs(g, tile, width):
    return [pl.BlockSpec((tile, width), lambda t: (jnp.minimum(t, g.n_p - 1), 0)),
            pl.BlockSpec((tile, width), lambda t: (jnp.maximum(t - g.n_p, 0), 0))]


def _proj_kernel(xp_ref, xs_ref, g_ref, b_ref, wq_ref, wkv_ref, wglu_ref, h0_ref, q_ref, kv_ref, zc_ref, *, geom):
    x = jnp.where(pl.program_id(0) < geom.n_p, xp_ref[...], xs_ref[...])
    h0 = _layer_norm(x, g_ref[...], b_ref[...])
    h0_ref[...] = h0
    h = h0.astype(BF16)
    q_ref[...] = _dot(h, wq_ref[...]).astype(BF16)
    kv_ref[...] = _dot(h, wkv_ref[...]).astype(BF16)
    u = _dot(h, wglu_ref[...])
    zc_ref[...] = (u[:, :CONV_WIDTH] * jax.nn.sigmoid(u[:, CONV_WIDTH:])).astype(BF16)


def _proj(xp, xs, geom, ln_g, ln_b, wq, wkv, wglu, tile):
    T = (geom.n_p + geom.n_s) * tile
    row = lambda w: pl.BlockSpec((tile, w), lambda t: (t, 0))
    full = lambda a: pl.BlockSpec(a.shape, lambda t: (0,) * a.ndim)
    return pl.pallas_call(
        functools.partial(_proj_kernel, geom=geom),
        grid=(geom.n_p + geom.n_s,),
        in_specs=_x_specs(geom, tile, D_MODEL) + [full(ln_g), full(ln_b), full(wq), full(wkv), full(wglu)],
        out_specs=[row(D_MODEL), row(ATT_WIDTH), row(2 * KV_WIDTH), row(CONV_WIDTH)],
        out_shape=[
            jax.ShapeDtypeStruct((T, D_MODEL), F32),
            jax.ShapeDtypeStruct((T, ATT_WIDTH), BF16),
            jax.ShapeDtypeStruct((T, 2 * KV_WIDTH), BF16),
            jax.ShapeDtypeStruct((T, CONV_WIDTH), BF16),
        ],
        compiler_params=pltpu.CompilerParams(dimension_semantics=("parallel",), vmem_limit_bytes=VMEM_LIMIT),
        name="proj",
    )(xp, xs, ln_g, ln_b, wq, wkv, wglu)


def _attn_kernel(q_ref, kvp_ref, kvc_ref, kvn_ref, kvm_ref, bias_ref, mbias_ref, sink_ref, o_ref, *, tile, geom):
    _, seq_first, seq_last = _seq_pos(pl.program_id(0), geom)
    blocks = tile // BLOCK
    scale = HEAD_DIM ** -0.5

    lane = lax.broadcasted_iota(jnp.int32, (1, LANES), 1)
    lo = lane < HEAD_DIM

    def split_heads(t):
        z = jnp.zeros_like(t)
        return jnp.where(lo, t, z), jnp.where(lo, z, t)

    kv_ext = jnp.concatenate([kvp_ref[...], kvc_ref[...], kvn_ref[...]], axis=0)
    k_ext = kv_ext[:, :KV_WIDTH] * jnp.asarray(scale, BF16)
    v_ext = kv_ext[:, KV_WIDTH:]
    ka, kb = split_heads(k_ext)
    va, vb = split_heads(v_ext)
    kma, kmb = split_heads(kvm_ref[:, :KV_WIDTH] * jnp.asarray(scale, BF16))
    vma, vmb = split_heads(kvm_ref[:, KV_WIDTH:])
    km_cat = jnp.concatenate([kma, kmb], axis=0)
    vm_cat = jnp.concatenate([vma, vmb], axis=0)

    mlane = lax.broadcasted_iota(jnp.int32, (1, 2 * N_META), 1)
    m_first = mlane < N_META
    sink = sink_ref[...]
    lane_o = lax.broadcasted_iota(jnp.int32, (1, LANES), 1) < HEAD_DIM

    for j in range(blocks):
        first = jnp.logical_and(seq_first, j == 0)
        last = jnp.logical_and(seq_last, j == blocks - 1)
        variant = jnp.where(first, 0, jnp.where(last, 2, 1))
        mvariant = jnp.where(first, 0, 1)

        r0 = j * BLOCK
        qb = q_ref[r0:r0 + BLOCK, :]
        q4 = jnp.concatenate([qb[:, p * LANES:(p + 1) * LANES] for p in range(N_PAIRS)], axis=0)
        k_cat = jnp.concatenate([ka[r0:r0 + 3 * BLOCK], kb[r0:r0 + 3 * BLOCK]], axis=0)
        v_cat = jnp.concatenate([va[r0:r0 + 3 * BLOCK], vb[r0:r0 + 3 * BLOCK]], axis=0)

        s = _dot_nt(q4, k_cat) + bias_ref[variant]
        sm = _dot_nt(q4, km_cat) + mbias_ref[mvariant]

        s_a, s_b = s[:, :3 * BLOCK], s[:, 3 * BLOCK:]
        sm_a = jnp.where(m_first, sm, NEG)
        sm_b = jnp.where(m_first, NEG, sm)
        m_a = jnp.maximum(jnp.maximum(jnp.max(s_a, axis=1, keepdims=True),
                                      jnp.max(sm_a, axis=1, keepdims=True)), sink[:, 0:1])
        m_b = jnp.maximum(jnp.maximum(jnp.max(s_b, axis=1, keepdims=True),
                                      jnp.max(sm_b, axis=1, keepdims=True)), sink[:, 1:2])
        p_a = jnp.exp(s_a - m_a)
        p_b = jnp.exp(s_b - m_b)
        pm = jnp.exp(jnp.where(m_first, sm - m_a, sm - m_b))
        l_a = (jnp.sum(p_a, axis=1, keepdims=True) + jnp.sum(jnp.where(m_first, pm, 0.0), axis=1, keepdims=True)
               + jnp.exp(sink[:, 0:1] - m_a))
        l_b = (jnp.sum(p_b, axis=1, keepdims=True) + jnp.sum(jnp.where(m_first, 0.0, pm), axis=1, keepdims=True)
               + jnp.exp(sink[:, 1:2] - m_b))
        p = jnp.concatenate([p_a, p_b], axis=1).astype(BF16)
        o = _dot(p, v_cat) + _dot(pm.astype(BF16), vm_cat)
        o = o * jnp.where(lane_o, 1.0 / l_a, 1.0 / l_b)
        for pr in range(N_PAIRS):
            o_ref[r0:r0 + BLOCK, pr * LANES:(pr + 1) * LANES] = o[pr * BLOCK:(pr + 1) * BLOCK].astype(BF16)


def _attn(q, kv, geom, kv_meta, bias, mbias, sink_tab, tile):
    T = q.shape[0]
    bpt = tile // BLOCK
    n_blocks = T // BLOCK
    full = lambda a: pl.BlockSpec(a.shape, lambda t: (0,) * a.ndim)
    return pl.pallas_call(
        functools.partial(_attn_kernel, tile=tile, geom=geom),
        grid=(T // tile,),
        in_specs=[
            pl.BlockSpec((tile, ATT_WIDTH), lambda t: (t, 0)),
            pl.BlockSpec((BLOCK, 2 * KV_WIDTH), lambda t: (jnp.maximum(t * bpt - 1, 0), 0)),
            pl.BlockSpec((tile, 2 * KV_WIDTH), lambda t: (t, 0)),
            pl.BlockSpec((BLOCK, 2 * KV_WIDTH), lambda t: (jnp.minimum((t + 1) * bpt, n_blocks - 1), 0)),
            full(kv_meta), full(bias), full(mbias), full(sink_tab),
        ],
        out_specs=pl.BlockSpec((tile, ATT_WIDTH), lambda t: (t, 0)),
        out_shape=jax.ShapeDtypeStruct((T, ATT_WIDTH), BF16),
        compiler_params=pltpu.CompilerParams(dimension_semantics=("parallel",), vmem_limit_bytes=VMEM_LIMIT),
        name="attn",
    )(q, kv, kv, kv, kv_meta, bias, mbias, sink_tab)


def _conv_kernel(zp_ref, zc_ref, zn_ref, zm_ref, w_ref, cb_ref, g_ref, b_ref, o_ref, ext_ref, sh_ref, y_ref,
                 *, tile, geom):
    i = pl.program_id(0)
    _, seq_first, seq_last = _seq_pos(i, geom)
    ext_ref[0:CONV_HALO, :] = jnp.where(seq_first, zm_ref[...], zp_ref[...]).astype(F32)
    ext_ref[CONV_HALO:CONV_HALO + tile, :] = zc_ref[...].astype(F32)
    ext_ref[CONV_HALO + tile:, :] = jnp.where(seq_last, 0.0, zn_ref[...].astype(F32))
    off = CONV_HALO - CONV_K // 2
    reach = (off + CONV_K - 1) // 8 * 8
    for p in range(1, 8):
        for r0 in range(0, tile + reach, SHIFT_ROWS):
            n = min(SHIFT_ROWS, tile + reach - r0)
            sh_ref[p - 1, r0:r0 + n, :] = ext_ref[r0 + p:r0 + p + n, :]

    def taps(r0, cs):
        acc = jnp.zeros((CONV_ROWS, LANES), F32)
        for k in range(CONV_K):
            p, a = (off + k) % 8, (off + k) // 8 * 8
            rows = slice(r0 + a, r0 + a + CONV_ROWS)
            win = ext_ref[rows, cs] if p == 0 else sh_ref[p - 1, rows, cs]
            acc = acc + win * w_ref[k:k + 1, cs]
        y_ref[r0:r0 + CONV_ROWS, cs] = acc

    for c in range(CONV_WIDTH // LANES):
        for r in range(tile // CONV_ROWS):
            pl.when(i >= 0)(functools.partial(taps, r * CONV_ROWS, slice(c * LANES, (c + 1) * LANES)))
    cb, g, b = cb_ref[...], g_ref[...], b_ref[...]
    for r in range(tile // LN_ROWS):
        r0 = r * LN_ROWS
        y = _layer_norm(y_ref[r0:r0 + LN_ROWS, :] + cb, g, b)
        o_ref[r0:r0 + LN_ROWS, :] = (y * jax.nn.sigmoid(y)).astype(BF16)


def _conv(zc, geom, z_meta, conv_w, conv_b, ln_g, ln_b, tile):
    T, C = zc.shape
    hpt = tile // CONV_HALO
    n_halo = T // CONV_HALO
    full = lambda a: pl.BlockSpec(a.shape, lambda t: (0,) * a.ndim)
    return pl.pallas_call(
        functools.partial(_conv_kernel, tile=tile, geom=geom),
        grid=(T // tile,),
        in_specs=[
            pl.BlockSpec((CONV_HALO, C), lambda t: (jnp.maximum(t * hpt - 1, 0), 0)),
            pl.BlockSpec((tile, C), lambda t: (t, 0)),
            pl.BlockSpec((CONV_HALO, C), lambda t: (jnp.minimum((t + 1) * hpt, n_halo - 1), 0)),
            full(z_meta), full(conv_w), full(conv_b), full(ln_g), full(ln_b),
        ],
        out_specs=pl.BlockSpec((tile, C), lambda t: (t, 0)),
        out_shape=jax.ShapeDtypeStruct((T, C), BF16),
        scratch_shapes=[pltpu.VMEM((tile + 2 * CONV_HALO, C), F32),
                        pltpu.VMEM((7, tile + 2 * CONV_HALO - 8, C), F32),
                        pltpu.VMEM((tile, C), F32)],
        compiler_params=pltpu.CompilerParams(dimension_semantics=("parallel",), vmem_limit_bytes=VMEM_LIMIT),
        name="conv",
    )(zc, zc, zc, z_meta, conv_w, conv_b, ln_g, ln_b)


def _route(r):
    lane_i = lax.broadcasted_iota(jnp.int32, r.shape, 1)
    lane = lane_i.astype(F32)
    big = float(1 << 20)
    is_g = jnp.logical_and(lane_i >= N_EXPERTS, lane_i < N_EXPERTS + N_GROUPS)
    lg = jnp.where(is_g, r, -jnp.inf)
    mg = jnp.max(lg, axis=1, keepdims=True)
    g_w = 1.0 / jnp.sum(jnp.exp(lg - mg), axis=1, keepdims=True)
    g_idx = jnp.min(jnp.where(lg == mg, lane - N_EXPERTS, big), axis=1, keepdims=True)
    lane_group = jnp.right_shift(lane_i, 3).astype(F32)
    in_group = jnp.logical_and(lane_i < N_EXPERTS, lane_group == g_idx)
    le = jnp.where(in_group, r, -jnp.inf)
    m1 = jnp.max(le, axis=1, keepdims=True)
    den = jnp.sum(jnp.exp(le - m1), axis=1, keepdims=True)
    i1 = jnp.min(jnp.where(le == m1, lane, big), axis=1, keepdims=True)
    le2 = jnp.where(lane == i1, -jnp.inf, le)
    m2 = jnp.max(le2, axis=1, keepdims=True)
    i2 = jnp.min(jnp.where(le2 == m2, lane, big), axis=1, keepdims=True)
    p1 = 1.0 / den
    p2 = jnp.exp(m2 - m1) / den
    tot = p1 + p2
    return i1, i2, g_w * (p1 / tot), g_w * (p2 / tot)


def _pack_bf16_pairs(x):
    half = x.shape[1] // 2
    words = []
    for j in range(half // LANES):
        lo = pltpu.bitcast(x[:, j * LANES:(j + 1) * LANES].astype(BF16).astype(F32), jnp.uint32)
        hi = pltpu.bitcast(x[:, half + j * LANES:half + (j + 1) * LANES].astype(BF16).astype(F32), jnp.uint32)
        words.append(hi | (lo >> 16))
    return words


def _unpack_bf16_pairs(words):
    lo = [pltpu.bitcast(w << 16, F32) for w in words]
    hi = [pltpu.bitcast(w & jnp.uint32(0xFFFF0000), F32) for w in words]
    return jnp.concatenate(lo + hi, axis=1)


def _out_kernel(h0_ref, att_ref, cz_ref, wg_ref, watt_ref, wco_ref, wout_ref,
                l1g_ref, l1b_ref, wr_ref, br_ref, before_ref, h1_ref, h1p_ref, rt_ref, fld_ref, cnt_ref, *, tile):
    @pl.when(pl.program_id(0) == 0)
    def _():
        cnt_ref[...] = jnp.zeros_like(cnt_ref)

    h0 = h0_ref[...]
    hb = h0.astype(BF16)
    g_att = jax.nn.sigmoid(_dot(hb, wg_ref[:, :D_MODEL]))
    mix = g_att * _dot(att_ref[...], watt_ref[...])
    g_conv = jax.nn.sigmoid(_dot(hb, wg_ref[:, D_MODEL:]))
    mix = mix + g_conv * _dot(cz_ref[...], wco_ref[...])
    m = _dot(mix.astype(BF16), wout_ref[...])
    h1 = _layer_norm(ALPHA * h0 + m, l1g_ref[...], l1b_ref[...])
    h1_ref[...] = h1
    for j, w in enumerate(_pack_bf16_pairs(h1)):
        h1p_ref[:, j, :, :] = w.reshape(tile // 8, 8, LANES)

    r = _dot(h1.astype(BF16), wr_ref[...]) + br_ref[...]
    i1, i2, w1, w2 = _route(r)
    lane = lax.broadcasted_iota(jnp.int32, (tile, ROUTER_LANES), 1)
    lane_f = lane.astype(F32)
    hit1, hit2 = lane_f == i1, lane_f == i2
    onehot = jnp.where(jnp.logical_or(hit1, hit2), 1.0, 0.0)
    seen = _dot(before_ref[...], onehot.astype(BF16)) + cnt_ref[0:1, :]
    rank1 = jnp.sum(jnp.where(hit1, seen, 0.0), axis=1, keepdims=True)
    rank2 = jnp.sum(jnp.where(hit2, seen, 0.0), axis=1, keepdims=True)
    cnt_ref[...] = cnt_ref[...] + jnp.sum(onehot, axis=0, keepdims=True)
    fields = (i1, i2, w1, w2, rank1, rank2)
    rt = jnp.zeros((tile, ROUTER_LANES), F32)
    for k, v in enumerate(fields):
        rt = jnp.where(lane == k, v, rt)
    rt_ref[...] = rt
    fld_ref[...] = jnp.transpose(rt)[0:ROUTE_FIELDS, :]


def _out(h0, att, cz, wg, watt, wco, wout, l1g, l1b, wr, br, tile, first_tile, n_tiles):
    T, D = n_tiles * tile, D_MODEL
    own = lambda w: pl.BlockSpec((tile, w), lambda i: (i, 0))
    flat = lambda w: pl.BlockSpec((tile, w), lambda i: (first_tile + i, 0))
    full = lambda a: pl.BlockSpec(a.shape, lambda i: (0,) * a.ndim, pipeline_mode=pl.Buffered(1))
    before = (jnp.arange(tile)[None, :] < jnp.arange(tile)[:, None]).astype(BF16)
    return pl.pallas_call(
        functools.partial(_out_kernel, tile=tile),
        grid=(T // tile,),
        in_specs=[flat(D), flat(ATT_WIDTH), flat(CONV_WIDTH), full(wg),
                  full(watt), full(wco), full(wout), full(l1g), full(l1b), full(wr), full(br), full(before)],
        out_specs=[own(D),
                   pl.BlockSpec((tile // 8, PACK_WORDS, 8, LANES), lambda i: (i, 0, 0, 0)),
                   own(ROUTER_LANES),
                   pl.BlockSpec((ROUTE_FIELDS, tile), lambda i: (0, i)),
                   pl.BlockSpec((8, ROUTER_LANES), lambda i: (0, 0))],
        out_shape=[jax.ShapeDtypeStruct((T, D), F32),
                   jax.ShapeDtypeStruct((T // 8, PACK_WORDS, 8, LANES), jnp.uint32),
                   jax.ShapeDtypeStruct((T, ROUTER_LANES), F32),
                   jax.ShapeDtypeStruct((ROUTE_FIELDS, T), F32),
                   jax.ShapeDtypeStruct((8, ROUTER_LANES), F32)],
        compiler_params=pltpu.CompilerParams(dimension_semantics=("arbitrary",), vmem_limit_bytes=VMEM_LIMIT),
        name="out",
    )(h0, att, cz, wg, watt, wco, wout, l1g, l1b, wr, br, before)


def _sc_scatter2(src, idx_a, idx_b, n_out):
    m = src.shape[0]
    mesh = plsc.VectorSubcoreMesh(core_axis_name="c", subcore_axis_name="s")

    @functools.partial(pl.kernel, out_type=jax.ShapeDtypeStruct((n_out, LANES), src.dtype), mesh=mesh)
    def k(x_hbm, ia_hbm, ib_hbm, o_hbm):
        def body(x_vmem, ia_vmem, ib_vmem):
            pltpu.sync_copy(x_vmem, o_hbm.at[ia_vmem.at[0]])
            pltpu.sync_copy(x_vmem, o_hbm.at[ib_vmem.at[0]])

        pltpu.emit_pipeline(
            body, grid=(m // SC_WINDOW,),
            in_specs=[pl.BlockSpec((SC_WINDOW, LANES), index_map=lambda i: (i, 0)),
                      pl.BlockSpec((1, SC_WINDOW), index_map=lambda i: (i, 0)),
                      pl.BlockSpec((1, SC_WINDOW), index_map=lambda i: (i, 0))],
            out_specs=[],
            core_axis_name=("c", "s"), dimension_semantics=(pltpu.PARALLEL,),
        )(x_hbm, ia_hbm, ib_hbm)

    return k(src, idx_a, idx_b)


def _sc_gather2(table, idx_a, idx_b):
    windows = idx_a.shape[0]
    m = windows * SC_WINDOW
    mesh = plsc.VectorSubcoreMesh(core_axis_name="c", subcore_axis_name="s")

    @functools.partial(pl.kernel, out_type=jax.ShapeDtypeStruct((2 * m, LANES), table.dtype), mesh=mesh)
    def k(x_hbm, ia_hbm, ib_hbm, o_hbm):
        def body(i_vmem, o_vmem):
            pltpu.sync_copy(x_hbm.at[i_vmem.at[0]], o_vmem)

        for half, i_hbm in enumerate((ia_hbm, ib_hbm)):
            pltpu.emit_pipeline(
                body, grid=(windows,),
                in_specs=[pl.BlockSpec((1, SC_WINDOW), index_map=lambda i: (i, 0))],
                out_specs=[pl.BlockSpec((SC_WINDOW, LANES), index_map=lambda i, half=half: (half * windows + i, 0))],
                core_axis_name=("c", "s"), dimension_semantics=(pltpu.PARALLEL,),
            )(i_hbm, o_hbm)

    return k(table, idx_a, idx_b).reshape(2, m, LANES)


def _expert_kernel(te_ref, tv_ref, used_ref, xs_hbm, wg_ref, wu_ref, wd_ref, ys_ref, wgb_ref, wub_ref, wdb_ref,
                   xbuf_ref, sem_ref, *, tile):
    n = pl.program_id(0)
    valid = tv_ref[n]
    used = used_ref[0]

    def row_tile_copy(step):
        slot = lax.rem(step, XS_SLOTS)
        return pltpu.make_async_copy(xs_hbm.at[pl.ds(step * (tile // 8), tile // 8)], xbuf_ref.at[slot], sem_ref.at[slot])

    @pl.when(n == 0)
    def _():
        row_tile_copy(0).start()

        @pl.when(used > 1)
        def _():
            row_tile_copy(1).start()

    @pl.when(n + 2 < used)
    def _():
        row_tile_copy(n + 2).start()

    @pl.when(jnp.logical_or(n == 0, te_ref[n] != te_ref[jnp.maximum(n - 1, 0)]))
    def _():
        wgb_ref[...] = wg_ref[...].astype(BF16)
        wub_ref[...] = wu_ref[...].astype(BF16)
        wdb_ref[...] = wd_ref[...].astype(BF16)

    @pl.when(n < used)
    def _():
        row_tile_copy(n).wait()
        slot = lax.rem(n, XS_SLOTS)
        x = _unpack_bf16_pairs([xbuf_ref[slot, :, j, :, :].reshape(tile, LANES) for j in range(PACK_WORDS)])
        rows = lax.broadcasted_iota(jnp.int32, (tile, 1), 0)
        x = jnp.where(rows < valid, x, 0.0).astype(BF16)
        gt = _dot(x, wgb_ref[...])
        up = _dot(x, wub_ref[...])
        hid = (gt * jax.nn.sigmoid(gt)) * up
        y = _dot(hid.astype(BF16), wdb_ref[...])
        for j, w in enumerate(_pack_bf16_pairs(y)):
            ys_ref[:, j, :, :] = w.reshape(tile // 8, 8, LANES)


def _experts(tile_expert, tile_valid, tiles_used, xs, w_gate, w_up, w_down, tile):
    n_tiles = xs.shape[0] * 8 // tile
    blk = pl.BlockSpec((tile // 8, PACK_WORDS, 8, LANES),
                       lambda n, te, tv, used: (jnp.minimum(n, jnp.maximum(used[0] - 1, 0)), 0, 0, 0))
    return pl.pallas_call(
        functools.partial(_expert_kernel, tile=tile),
        grid_spec=pltpu.PrefetchScalarGridSpec(
            num_scalar_prefetch=3,
            grid=(n_tiles,),
            in_specs=[pl.BlockSpec(memory_space=pl.ANY),
                      pl.BlockSpec((None, D_MODEL, D_EXPERT), lambda n, te, tv, used: (te[n], 0, 0)),
                      pl.BlockSpec((None, D_MODEL, D_EXPERT), lambda n, te, tv, used: (te[n], 0, 0)),
                      pl.BlockSpec((None, D_EXPERT, D_MODEL), lambda n, te, tv, used: (te[n], 0, 0))],
            out_specs=blk,
            scratch_shapes=[pltpu.VMEM((D_MODEL, D_EXPERT), BF16), pltpu.VMEM((D_MODEL, D_EXPERT), BF16),
                            pltpu.VMEM((D_EXPERT, D_MODEL), BF16),
                            pltpu.VMEM((XS_SLOTS, tile // 8, PACK_WORDS, 8, LANES), jnp.uint32),
                            pltpu.SemaphoreType.DMA((XS_SLOTS,))],
        ),
        out_shape=jax.ShapeDtypeStruct(xs.shape, jnp.uint32),
        compiler_params=pltpu.CompilerParams(
            dimension_semantics=("arbitrary",), vmem_limit_bytes=VMEM_LIMIT),
        name="experts",
    )(tile_expert, tile_valid, tiles_used, xs, w_gate, w_up, w_down)


def _final_kernel(h_ref, g_ref, rt_ref, l2g_ref, l2b_ref, o_ref, *, tile):
    rt = rt_ref[...]
    lane = lax.broadcasted_iota(jnp.int32, rt.shape, 1)
    w1 = jnp.sum(jnp.where(lane == 2, rt, 0.0), axis=1, keepdims=True)
    w2 = jnp.sum(jnp.where(lane == 3, rt, 0.0), axis=1, keepdims=True)
    y1 = _unpack_bf16_pairs([g_ref[0, :, j, :, :].reshape(tile, LANES) for j in range(PACK_WORDS)])
    y2 = _unpack_bf16_pairs([g_ref[1, :, j, :, :].reshape(tile, LANES) for j in range(PACK_WORDS)])
    f = w1 * y1 + w2 * y2
    o_ref[...] = _layer_norm(ALPHA * h_ref[...] + f, l2g_ref[...], l2b_ref[...])


def _final(h1, g, rt, l2g, l2b, tile):
    T, D = h1.shape
    row = lambda w: pl.BlockSpec((tile, w), lambda i: (i, 0))
    full = lambda a: pl.BlockSpec(a.shape, lambda i: (0,) * a.ndim)
    return pl.pallas_call(
        functools.partial(_final_kernel, tile=tile),
        grid=(T // tile,),
        in_specs=[row(D),
                  pl.BlockSpec((2, tile // 8, PACK_WORDS, 8, LANES), lambda i: (0, i, 0, 0, 0)),
                  row(ROUTER_LANES), full(l2g), full(l2b)],
        out_specs=row(D),
        out_shape=jax.ShapeDtypeStruct((T, D), F32),
        compiler_params=pltpu.CompilerParams(dimension_semantics=("parallel",), vmem_limit_bytes=VMEM_LIMIT),
        name="final",
    )(h1, g, rt, l2g, l2b)


def _dispatch_plan(fields, counts, n_tokens, tile):
    cnt = counts[0, :N_EXPERTS].astype(jnp.int32)
    padded = (cnt + tile - 1) // tile * tile
    base = jnp.cumsum(padded) - padded
    e_ids = jnp.arange(N_EXPERTS, dtype=jnp.int32)

    def dest(row_e, row_r):
        e = fields[row_e].astype(jnp.int32)
        seg = jnp.sum(jnp.where(e[None, :] == e_ids[:, None], base[:, None], 0), axis=0)
        pos = seg + fields[row_r].astype(jnp.int32)
        p = pos.reshape(n_tokens // 8, 1, 8)
        j = jnp.arange(PACK_WORDS, dtype=jnp.int32).reshape(1, PACK_WORDS, 1)
        return ((p // 8) * (8 * PACK_WORDS) + j * 8 + p % 8).reshape(n_tokens * PACK_WORDS // SC_WINDOW, SC_WINDOW)

    n_tiles = (2 * n_tokens) // tile + N_EXPERTS
    start = jnp.arange(n_tiles, dtype=jnp.int32) * tile
    seg_end = base + padded
    te = jnp.minimum(jnp.sum((start[:, None] >= seg_end[None, :]).astype(jnp.int32), axis=1), N_EXPERTS - 1)
    te_base = jnp.sum(jnp.where(te[:, None] == e_ids[None, :], base[None, :], 0), axis=1)
    te_cnt = jnp.sum(jnp.where(te[:, None] == e_ids[None, :], cnt[None, :], 0), axis=1)
    tv = jnp.clip(te_cnt - (start - te_base), 0, tile)
    tiles_used = (jnp.sum(padded) // tile).reshape(1)
    return dest(0, 4), dest(1, 5), te, tv, tiles_used, n_tiles


def _t5_bucket(rel):
    half = N_BUCKETS // 2
    max_exact = half // 2
    ret = jnp.where(rel > 0, half, 0)
    n = jnp.abs(rel)
    nf = jnp.maximum(n, 1).astype(F32)
    large = max_exact + (jnp.log(nf / max_exact) / math.log(MAX_DISTANCE / max_exact)
                         * (half - max_exact)).astype(jnp.int32)
    large = jnp.minimum(large, half - 1)
    return ret + jnp.where(n < max_exact, n, large)


def _bucket_bias(rel_bias, bucket):
    rb = rel_bias.astype(F32)
    out = jnp.zeros((N_Q_HEADS,) + bucket.shape, F32)
    for b in range(N_BUCKETS):
        out = out + jnp.where(bucket[None] == b, rb[b][:, None, None], 0.0)
    return out


def _pair_rows(t):
    return jnp.concatenate([t[:N_PAIRS], t[N_PAIRS:]], axis=-1).reshape(N_PAIRS * BLOCK, -1)


def _bias_tables(rel_bias, sink):
    qi = jnp.arange(BLOCK)
    kj = jnp.arange(3 * BLOCK) - BLOCK
    rel = kj[None, :] - qi[:, None]
    band = _bucket_bias(rel_bias, _t5_bucket(rel))
    in_win = (jnp.abs(rel) <= WINDOW)[None]
    not_prev = (kj >= 0)[None, None, :]
    not_next = (kj < BLOCK)[None, None, :]
    variants = [jnp.where(in_win & not_prev, band, NEG),
                jnp.where(in_win, band, NEG),
                jnp.where(in_win & not_next, band, NEG)]
    bias = jnp.stack([_pair_rows(v) for v in variants])

    off = BLOCK - N_META
    mvars = []
    for blk in (1, 2):
        qpos = blk * BLOCK + qi - off
        meta_rel = jnp.arange(N_META)[None, :] - qpos[:, None]
        mvars.append(_pair_rows(_bucket_bias(rel_bias, _t5_bucket(meta_rel))))
    mbias = jnp.stack(mvars)
    s = sink.astype(F32)
    sink_tab = jnp.repeat(jnp.stack([s[:N_PAIRS], s[N_PAIRS:]], axis=-1), BLOCK, axis=0)
    return bias, mbias, sink_tab


def kernel(x_prompt, x_sample, meta, ln_in_g, ln_in_b, rel_bias, w_in, w_att_branch, sink, conv_w, conv_b,
           conv_ln_g, conv_ln_b, w_conv_out, w_out, ln1_g, ln1_b, w_group, b_group, w_router, b_router,
           w_gate, w_up, w_down, ln2_g, ln2_b):
    row = lambda v: v.reshape(1, -1).astype(F32)
    w = w_in[0]
    wq = (w[:, :Q_END].reshape(D_MODEL, 2, N_PAIRS, HEAD_DIM).transpose(0, 2, 1, 3)
          .reshape(D_MODEL, ATT_WIDTH).astype(BF16))
    watt = (w_att_branch[0].reshape(2, N_PAIRS, HEAD_DIM, D_MODEL).transpose(1, 0, 2, 3)
            .reshape(ATT_WIDTH, D_MODEL).astype(BF16))
    wkv = w[:, Q_END:V_END].astype(BF16)
    wglu = w[:, V_END:GLU_END].astype(BF16)
    wg = w[:, GLU_END:].astype(BF16)
    wco = w_conv_out[0].astype(BF16)
    wout = w_out[0].astype(BF16)
    wr = jnp.zeros((D_MODEL, ROUTER_LANES), F32)
    wr = wr.at[:, :N_EXPERTS].set(w_router[0]).at[:, N_EXPERTS:N_EXPERTS + N_GROUPS].set(w_group[0]).astype(BF16)
    br = jnp.zeros((1, ROUTER_LANES), F32)
    br = br.at[0, :N_EXPERTS].set(b_router[0]).at[0, N_EXPERTS:N_EXPERTS + N_GROUPS].set(b_group[0])
    ln_g, ln_b = row(ln_in_g), row(ln_in_b)
    bias, mbias, sink_tab = _bias_tables(rel_bias, sink[0])

    xm = jnp.concatenate([jnp.zeros((BLOCK - N_META, D_MODEL), F32), meta.astype(F32)], axis=0)
    _, _, kv_m, zc_m = _proj(xm, xm, _Geom(1, 1, 1, 1), ln_g, ln_b, wq, wkv, wglu, BLOCK)
    kv_meta = kv_m[BLOCK - N_META:BLOCK]
    z_meta = zc_m[BLOCK - N_META:BLOCK]

    (bp, sp, _), (bs, ss, _) = x_prompt.shape, x_sample.shape
    xp, xs = x_prompt.reshape(bp * sp, D_MODEL), x_sample.reshape(bs * ss, D_MODEL)
    geom = _geom(x_prompt, x_sample, TILE)
    h0, q, kv, zc = _proj(xp, xs, _geom(x_prompt, x_sample, TILE_PROJ), ln_g, ln_b, wq, wkv, wglu, TILE_PROJ)
    att = _attn(q, kv, _geom(x_prompt, x_sample, TILE_ATTN), kv_meta, bias, mbias, sink_tab, TILE_ATTN)
    cz = _conv(zc, geom, z_meta, conv_w[0], row(conv_b[0]), row(conv_ln_g[0]), row(conv_ln_b[0]), TILE)
    l1g, l1b, l2g, l2b = row(ln1_g[0]), row(ln1_b[0]), row(ln2_g[0]), row(ln2_b[0])

    def moe(first_tile, n_tiles):
        n = n_tiles * TILE
        h1, h1p, rt, fields, counts = _out(h0, att, cz, wg, watt, wco, wout, l1g, l1b, wr, br,
                                           TILE_OUT, first_tile * TILE // TILE_OUT, n_tiles * TILE // TILE_OUT)
        idx1, idx2, tile_expert, tile_valid, tiles_used, n_tiles = _dispatch_plan(fields, counts, n, TILE_EXPERT)
        n_rows = n_tiles * TILE_EXPERT
        xsorted = _sc_scatter2(h1p.reshape(n * PACK_WORDS, LANES), idx1, idx2, n_rows * PACK_WORDS)
        ys = _experts(tile_expert, tile_valid, tiles_used, xsorted.reshape(n_rows // 8, PACK_WORDS, 8, LANES),
                      w_gate[0], w_up[0], w_down[0], TILE_EXPERT)
        g = _sc_gather2(ys.reshape(n_rows * PACK_WORDS, LANES), idx1, idx2)
        return _final(h1, g.reshape(2, n // 8, PACK_WORDS, 8, LANES), rt, l2g, l2b, TILE_FINAL)

    return moe(0, geom.n_p).reshape(x_prompt.shape), moe(geom.n_p, geom.n_s).reshape(x_sample.shape)
```

```python
import functools
import math
from typing import NamedTuple

import jax
import jax.numpy as jnp
from jax import lax
from jax.experimental import pallas as pl
from jax.experimental.pallas import tpu as pltpu
from jax.experimental.pallas import tpu_sc as plsc

D_MODEL = 1024
N_META = 16
BLOCK = 128
WINDOW = 128
N_Q_HEADS = 8
N_KV_HEADS = 2
HEAD_DIM = 64
ATT_WIDTH = N_Q_HEADS * HEAD_DIM
KV_WIDTH = N_KV_HEADS * HEAD_DIM
CONV_WIDTH = D_MODEL // 2
CONV_K = 31
N_BUCKETS = 32
MAX_DISTANCE = 128
N_GROUPS = 4
EXPERTS_PER_GROUP = 8
N_EXPERTS = N_GROUPS * EXPERTS_PER_GROUP
D_EXPERT = 256
LN_EPS = 1e-5
DEPTH = 1
ALPHA = (2 * DEPTH) ** 0.25
NEG = -1e30
Q_END = ATT_WIDTH
K_END = Q_END + KV_WIDTH
V_END = K_END + KV_WIDTH
GLU_END = V_END + 2 * CONV_WIDTH
GA_END = GLU_END + D_MODEL

N_PAIRS = N_Q_HEADS // 2
LANES = 128
CONV_HALO = 16
ROUTER_LANES = 128

TILE = 512
CONV_ROWS = 128
LN_ROWS = 64
SHIFT_ROWS = 128
TILE_PROJ = 1024
TILE_OUT = 1024
TILE_CONV = 1024
XS_SLOTS = 3
TILE_FINAL = 1024
TILE_ATTN = 2048
TILE_EXPERT = 512
PACK_WORDS = 4
SC_WINDOW = 128
ROUTE_FIELDS = 8
VMEM_LIMIT = 56 * 1024 * 1024

BF16 = jnp.bfloat16
F32 = jnp.float32


def _layer_norm(x, g, b):
    mu = jnp.mean(x, axis=-1, keepdims=True)
    xc = x - mu
    var = jnp.mean(xc * xc, axis=-1, keepdims=True)
    return xc * lax.rsqrt(var + LN_EPS) * g + b


def _dot(a, b):
    return jnp.dot(a, b, preferred_element_type=F32)


def _dot_nt(a, b):
    return lax.dot_general(a, b, (((1,), (1,)), ((), ())), preferred_element_type=F32)


class _Geom(NamedTuple):
    n_p: int
    n_s: int
    tp: int
    ts: int


def _geom(x_prompt, x_sample, tile):
    (bp, sp, _), (bs, ss, _) = x_prompt.shape, x_sample.shape
    return _Geom(bp * sp // tile, bs * ss // tile, sp // tile, ss // tile)


def _seq_pos(t, g):
    is_p = t < g.n_p
    local = jnp.where(is_p, lax.rem(t, g.tp), lax.rem(jnp.maximum(t - g.n_p, 0), g.ts))
    return is_p, local == 0, local == jnp.where(is_p, g.tp - 1, g.ts - 1)


def _x_specs(g, tile, width):
    return [pl.BlockSpec((tile, width), lambda t: (jnp.minimum(t, g.n_p - 1), 0)),
            pl.BlockSpec((tile, width), lambda t: (jnp.maximum(t - g.n_p, 0), 0))]


def _proj_kernel(xp_ref, xs_ref, g_ref, b_ref, wq_ref, wkv_ref, wglu_ref, h0_ref, q_ref, kv_ref, zc_ref, *, geom):
    x = jnp.where(pl.program_id(0) < geom.n_p, xp_ref[...], xs_ref[...])
    h0 = _layer_norm(x, g_ref[...], b_ref[...])
    h0_ref[...] = h0
    h = h0.astype(BF16)
    q_ref[...] = _dot(h, wq_ref[...]).astype(BF16)
    kv_ref[...] = _dot(h, wkv_ref[...]).astype(BF16)
    u = _dot(h, wglu_ref[...])
    zc_ref[...] = (u[:, :CONV_WIDTH] * jax.nn.sigmoid(u[:, CONV_WIDTH:])).astype(BF16)


def _proj(xp, xs, geom, ln_g, ln_b, wq, wkv, wglu, tile):
    T = (geom.n_p + geom.n_s) * tile
    row = lambda w: pl.BlockSpec((tile, w), lambda t: (t, 0))
    full = lambda a: pl.BlockSpec(a.shape, lambda t: (0,) * a.ndim)
    return pl.pallas_call(
        functools.partial(_proj_kernel, geom=geom),
        grid=(geom.n_p + geom.n_s,),
        in_specs=_x_specs(geom, tile, D_MODEL) + [full(ln_g), full(ln_b), full(wq), full(wkv), full(wglu)],
        out_specs=[row(D_MODEL), row(ATT_WIDTH), row(2 * KV_WIDTH), row(CONV_WIDTH)],
        out_shape=[
            jax.ShapeDtypeStruct((T, D_MODEL), F32),
            jax.ShapeDtypeStruct((T, ATT_WIDTH), BF16),
            jax.ShapeDtypeStruct((T, 2 * KV_WIDTH), BF16),
            jax.ShapeDtypeStruct((T, CONV_WIDTH), BF16),
        ],
        compiler_params=pltpu.CompilerParams(dimension_semantics=("parallel",), vmem_limit_bytes=VMEM_LIMIT),
        name="proj",
    )(xp, xs, ln_g, ln_b, wq, wkv, wglu)


def _attn_kernel(q_ref, kvp_ref, kvc_ref, kvn_ref, kvm_ref, bias_ref, mbias_ref, sink_ref, o_ref, *, tile, geom):
    _, seq_first, seq_last = _seq_pos(pl.program_id(0), geom)
    blocks = tile // BLOCK
    scale = HEAD_DIM ** -0.5

    lane = lax.broadcasted_iota(jnp.int32, (1, LANES), 1)
    lo = lane < HEAD_DIM

    def split_heads(t):
        z = jnp.zeros_like(t)
        return jnp.where(lo, t, z), jnp.where(lo, z, t)

    kv_ext = jnp.concatenate([kvp_ref[...], kvc_ref[...], kvn_ref[...]], axis=0)
    k_ext = kv_ext[:, :KV_WIDTH] * jnp.asarray(scale, BF16)
    v_ext = kv_ext[:, KV_WIDTH:]
    ka, kb = split_heads(k_ext)
    va, vb = split_heads(v_ext)
    kma, kmb = split_heads(kvm_ref[:, :KV_WIDTH] * jnp.asarray(scale, BF16))
    vma, vmb = split_heads(kvm_ref[:, KV_WIDTH:])
    km_cat = jnp.concatenate([kma, kmb], axis=0)
    vm_cat = jnp.concatenate([vma, vmb], axis=0)

    mlane = lax.broadcasted_iota(jnp.int32, (1, 2 * N_META), 1)
    m_first = mlane < N_META
    sink = sink_ref[...]
    lane_o = lax.broadcasted_iota(jnp.int32, (1, LANES), 1) < HEAD_DIM

    for j in range(blocks):
        first = jnp.logical_and(seq_first, j == 0)
        last = jnp.logical_and(seq_last, j == blocks - 1)
        variant = jnp.where(first, 0, jnp.where(last, 2, 1))
        mvariant = jnp.where(first, 0, 1)

        r0 = j * BLOCK
        qb = q_ref[r0:r0 + BLOCK, :]
        q4 = jnp.concatenate([qb[:, p * LANES:(p + 1) * LANES] for p in range(N_PAIRS)], axis=0)
        k_cat = jnp.concatenate([ka[r0:r0 + 3 * BLOCK], kb[r0:r0 + 3 * BLOCK]], axis=0)
        v_cat = jnp.concatenate([va[r0:r0 + 3 * BLOCK], vb[r0:r0 + 3 * BLOCK]], axis=0)

        s = _dot_nt(q4, k_cat) + bias_ref[variant]
        sm = _dot_nt(q4, km_cat) + mbias_ref[mvariant]

        s_a, s_b = s[:, :3 * BLOCK], s[:, 3 * BLOCK:]
        sm_a = jnp.where(m_first, sm, NEG)
        sm_b = jnp.where(m_first, NEG, sm)
        m_a = jnp.maximum(jnp.maximum(jnp.max(s_a, axis=1, keepdims=True),
                                      jnp.max(sm_a, axis=1, keepdims=True)), sink[:, 0:1])
        m_b = jnp.maximum(jnp.maximum(jnp.max(s_b, axis=1, keepdims=True),
                                      jnp.max(sm_b, axis=1, keepdims=True)), sink[:, 1:2])
        p_a = jnp.exp(s_a - m_a)
        p_b = jnp.exp(s_b - m_b)
        pm = jnp.exp(jnp.where(m_first, sm - m_a, sm - m_b))
        l_a = (jnp.sum(p_a, axis=1, keepdims=True) + jnp.sum(jnp.where(m_first, pm, 0.0), axis=1, keepdims=True)
               + jnp.exp(sink[:, 0:1] - m_a))
        l_b = (jnp.sum(p_b, axis=1, keepdims=True) + jnp.sum(jnp.where(m_first, 0.0, pm), axis=1, keepdims=True)
               + jnp.exp(sink[:, 1:2] - m_b))
        p = jnp.concatenate([p_a, p_b], axis=1).astype(BF16)
        o = _dot(p, v_cat) + _dot(pm.astype(BF16), vm_cat)
        o = o * jnp.where(lane_o, 1.0 / l_a, 1.0 / l_b)
        for pr in range(N_PAIRS):
            o_ref[r0:r0 + BLOCK, pr * LANES:(pr + 1) * LANES] = o[pr * BLOCK:(pr + 1) * BLOCK].astype(BF16)


def _attn(q, kv, geom, kv_meta, bias, mbias, sink_tab, tile):
    T = q.shape[0]
    bpt = tile // BLOCK
    n_blocks = T // BLOCK
    full = lambda a: pl.BlockSpec(a.shape, lambda t: (0,) * a.ndim)
    return pl.pallas_call(
        functools.partial(_attn_kernel, tile=tile, geom=geom),
        grid=(T // tile,),
        in_specs=[
            pl.BlockSpec((tile, ATT_WIDTH), lambda t: (t, 0)),
            pl.BlockSpec((BLOCK, 2 * KV_WIDTH), lambda t: (jnp.maximum(t * bpt - 1, 0), 0)),
            pl.BlockSpec((tile, 2 * KV_WIDTH), lambda t: (t, 0)),
            pl.BlockSpec((BLOCK, 2 * KV_WIDTH), lambda t: (jnp.minimum((t + 1) * bpt, n_blocks - 1), 0)),
            full(kv_meta), full(bias), full(mbias), full(sink_tab),
        ],
        out_specs=pl.BlockSpec((tile, ATT_WIDTH), lambda t: (t, 0)),
        out_shape=jax.ShapeDtypeStruct((T, ATT_WIDTH), BF16),
        compiler_params=pltpu.CompilerParams(dimension_semantics=("parallel",), vmem_limit_bytes=VMEM_LIMIT),
        name="attn",
    )(q, kv, kv, kv, kv_meta, bias, mbias, sink_tab)


def _conv_kernel(zp_ref, zc_ref, zn_ref, zm_ref, w_ref, cb_ref, g_ref, b_ref, o_ref, ext_ref, sh_ref, y_ref,
                 *, tile, geom):
    i = pl.program_id(0)
    _, seq_first, seq_last = _seq_pos(i, geom)
    ext_ref[0:CONV_HALO, :] = jnp.where(seq_first, zm_ref[...], zp_ref[...]).astype(F32)
    ext_ref[CONV_HALO:CONV_HALO + tile, :] = zc_ref[...].astype(F32)
    ext_ref[CONV_HALO + tile:, :] = jnp.where(seq_last, 0.0, zn_ref[...].astype(F32))
    off = CONV_HALO - CONV_K // 2
    reach = (off + CONV_K - 1) // 8 * 8
    for p in range(1, 8):
        for r0 in range(0, tile + reach, SHIFT_ROWS):
            n = min(SHIFT_ROWS, tile + reach - r0)
            sh_ref[p - 1, r0:r0 + n, :] = ext_ref[r0 + p:r0 + p + n, :]

    def taps(r0, cs):
        acc = jnp.zeros((CONV_ROWS, LANES), F32)
        for k in range(CONV_K):
            p, a = (off + k) % 8, (off + k) // 8 * 8
            rows = slice(r0 + a, r0 + a + CONV_ROWS)
            win = ext_ref[rows, cs] if p == 0 else sh_ref[p - 1, rows, cs]
            acc = acc + win * w_ref[k:k + 1, cs]
        y_ref[r0:r0 + CONV_ROWS, cs] = acc

    for c in range(CONV_WIDTH // LANES):
        for r in range(tile // CONV_ROWS):
            pl.when(i >= 0)(functools.partial(taps, r * CONV_ROWS, slice(c * LANES, (c + 1) * LANES)))
    cb, g, b = cb_ref[...], g_ref[...], b_ref[...]
    for r in range(tile // LN_ROWS):
        r0 = r * LN_ROWS
        y = _layer_norm(y_ref[r0:r0 + LN_ROWS, :] + cb, g, b)
        o_ref[r0:r0 + LN_ROWS, :] = (y * jax.nn.sigmoid(y)).astype(BF16)


def _conv(zc, geom, z_meta, conv_w, conv_b, ln_g, ln_b, tile):
    T, C = zc.shape
    hpt = tile // CONV_HALO
    n_halo = T // CONV_HALO
    full = lambda a: pl.BlockSpec(a.shape, lambda t: (0,) * a.ndim)
    return pl.pallas_call(
        functools.partial(_conv_kernel, tile=tile, geom=geom),
        grid=(T // tile,),
        in_specs=[
            pl.BlockSpec((CONV_HALO, C), lambda t: (jnp.maximum(t * hpt - 1, 0), 0)),
            pl.BlockSpec((tile, C), lambda t: (t, 0)),
            pl.BlockSpec((CONV_HALO, C), lambda t: (jnp.minimum((t + 1) * hpt, n_halo - 1), 0)),
            full(z_meta), full(conv_w), full(conv_b), full(ln_g), full(ln_b),
        ],
        out_specs=pl.BlockSpec((tile, C), lambda t: (t, 0)),
        out_shape=jax.ShapeDtypeStruct((T, C), BF16),
        scratch_shapes=[pltpu.VMEM((tile + 2 * CONV_HALO, C), F32),
                        pltpu.VMEM((7, tile + 2 * CONV_HALO - 8, C), F32),
                        pltpu.VMEM((tile, C), F32)],
        compiler_params=pltpu.CompilerParams(dimension_semantics=("parallel",), vmem_limit_bytes=VMEM_LIMIT),
        name="conv",
    )(zc, zc, zc, z_meta, conv_w, conv_b, ln_g, ln_b)


def _route(r):
    lane_i = lax.broadcasted_iota(jnp.int32, r.shape, 1)
    lane = lane_i.astype(F32)
    big = float(1 << 20)
    is_g = jnp.logical_and(lane_i >= N_EXPERTS, lane_i < N_EXPERTS + N_GROUPS)
    lg = jnp.where(is_g, r, -jnp.inf)
    mg = jnp.max(lg, axis=1, keepdims=True)
    g_w = 1.0 / jnp.sum(jnp.exp(lg - mg), axis=1, keepdims=True)
    g_idx = jnp.min(jnp.where(lg == mg, lane - N_EXPERTS, big), axis=1, keepdims=True)
    lane_group = jnp.right_shift(lane_i, 3).astype(F32)
    in_group = jnp.logical_and(lane_i < N_EXPERTS, lane_group == g_idx)
    le = jnp.where(in_group, r, -jnp.inf)
    m1 = jnp.max(le, axis=1, keepdims=True)
    den = jnp.sum(jnp.exp(le - m1), axis=1, keepdims=True)
    i1 = jnp.min(jnp.where(le == m1, lane, big), axis=1, keepdims=True)
    le2 = jnp.where(lane == i1, -jnp.inf, le)
    m2 = jnp.max(le2, axis=1, keepdims=True)
    i2 = jnp.min(jnp.where(le2 == m2, lane, big), axis=1, keepdims=True)
    p1 = 1.0 / den
    p2 = jnp.exp(m2 - m1) / den
    tot = p1 + p2
    return i1, i2, g_w * (p1 / tot), g_w * (p2 / tot)


def _pack_bf16_pairs(x):
    half = x.shape[1] // 2
    words = []
    for j in range(half // LANES):
        lo = pltpu.bitcast(x[:, j * LANES:(j + 1) * LANES].astype(BF16).astype(F32), jnp.uint32)
        hi = pltpu.bitcast(x[:, half + j * LANES:half + (j + 1) * LANES].astype(BF16).astype(F32), jnp.uint32)
        words.append(hi | (lo >> 16))
    return words


def _unpack_bf16_pairs(words):
    lo = [pltpu.bitcast(w << 16, F32) for w in words]
    hi = [pltpu.bitcast(w & jnp.uint32(0xFFFF0000), F32) for w in words]
    return jnp.concatenate(lo + hi, axis=1)


def _out_kernel(h0_ref, att_ref, cz_ref, wg_ref, watt_ref, wco_ref, wout_ref,
                l1g_ref, l1b_ref, wr_ref, br_ref, before_ref, h1_ref, h1p_ref, rt_ref, fld_ref, cnt_ref, *, tile):
    @pl.when(pl.program_id(0) == 0)
    def _():
        cnt_ref[...] = jnp.zeros_like(cnt_ref)

    h0 = h0_ref[...]
    hb = h0.astype(BF16)
    g_att = jax.nn.sigmoid(_dot(hb, wg_ref[:, :D_MODEL]))
    mix = g_att * _dot(att_ref[...], watt_ref[...])
    g_conv = jax.nn.sigmoid(_dot(hb, wg_ref[:, D_MODEL:]))
    mix = mix + g_conv * _dot(cz_ref[...], wco_ref[...])
    m = _dot(mix.astype(BF16), wout_ref[...])
    h1 = _layer_norm(ALPHA * h0 + m, l1g_ref[...], l1b_ref[...])
    h1_ref[...] = h1
    for j, w in enumerate(_pack_bf16_pairs(h1)):
        h1p_ref[:, j, :, :] = w.reshape(tile // 8, 8, LANES)

    r = _dot(h1.astype(BF16), wr_ref[...]) + br_ref[...]
    i1, i2, w1, w2 = _route(r)
    lane = lax.broadcasted_iota(jnp.int32, (tile, ROUTER_LANES), 1)
    lane_f = lane.astype(F32)
    hit1, hit2 = lane_f == i1, lane_f == i2
    onehot = jnp.where(jnp.logical_or(hit1, hit2), 1.0, 0.0)
    seen = _dot(before_ref[...], onehot.astype(BF16)) + cnt_ref[0:1, :]
    rank1 = jnp.sum(jnp.where(hit1, seen, 0.0), axis=1, keepdims=True)
    rank2 = jnp.sum(jnp.where(hit2, seen, 0.0), axis=1, keepdims=True)
    cnt_ref[...] = cnt_ref[...] + jnp.sum(onehot, axis=0, keepdims=True)
    fields = (i1, i2, w1, w2, rank1, rank2)
    rt = jnp.zeros((tile, ROUTER_LANES), F32)
    for k, v in enumerate(fields):
        rt = jnp.where(lane == k, v, rt)
    rt_ref[...] = rt
    fld_ref[...] = jnp.transpose(rt)[0:ROUTE_FIELDS, :]


def _out(h0, att, cz, wg, watt, wco, wout, l1g, l1b, wr, br, tile, first_tile, n_tiles):
    T, D = n_tiles * tile, D_MODEL
    own = lambda w: pl.BlockSpec((tile, w), lambda i: (i, 0))
    flat = lambda w: pl.BlockSpec((tile, w), lambda i: (first_tile + i, 0))
    full = lambda a: pl.BlockSpec(a.shape, lambda i: (0,) * a.ndim, pipeline_mode=pl.Buffered(1))
    before = (jnp.arange(tile)[None, :] < jnp.arange(tile)[:, None]).astype(BF16)
    return pl.pallas_call(
        functools.partial(_out_kernel, tile=tile),
        grid=(T // tile,),
        in_specs=[flat(D), flat(ATT_WIDTH), flat(CONV_WIDTH), full(wg),
                  full(watt), full(wco), full(wout), full(l1g), full(l1b), full(wr), full(br), full(before)],
        out_specs=[own(D),
                   pl.BlockSpec((tile // 8, PACK_WORDS, 8, LANES), lambda i: (i, 0, 0, 0)),
                   own(ROUTER_LANES),
                   pl.BlockSpec((ROUTE_FIELDS, tile), lambda i: (0, i)),
                   pl.BlockSpec((8, ROUTER_LANES), lambda i: (0, 0))],
        out_shape=[jax.ShapeDtypeStruct((T, D), F32),
                   jax.ShapeDtypeStruct((T // 8, PACK_WORDS, 8, LANES), jnp.uint32),
                   jax.ShapeDtypeStruct((T, ROUTER_LANES), F32),
                   jax.ShapeDtypeStruct((ROUTE_FIELDS, T), F32),
                   jax.ShapeDtypeStruct((8, ROUTER_LANES), F32)],
        compiler_params=pltpu.CompilerParams(dimension_semantics=("arbitrary",), vmem_limit_bytes=VMEM_LIMIT),
        name="out",
    )(h0, att, cz, wg, watt, wco, wout, l1g, l1b, wr, br, before)


def _sc_scatter2(src, idx_a, idx_b, n_out):
    m = src.shape[0]
    mesh = plsc.VectorSubcoreMesh(core_axis_name="c", subcore_axis_name="s")

    @functools.partial(pl.kernel, out_type=jax.ShapeDtypeStruct((n_out, LANES), src.dtype), mesh=mesh)
    def k(x_hbm, ia_hbm, ib_hbm, o_hbm):
        def body(x_vmem, ia_vmem, ib_vmem):
            pltpu.sync_copy(x_vmem, o_hbm.at[ia_vmem.at[0]])
            pltpu.sync_copy(x_vmem, o_hbm.at[ib_vmem.at[0]])

        pltpu.emit_pipeline(
            body, grid=(m // SC_WINDOW,),
            in_specs=[pl.BlockSpec((SC_WINDOW, LANES), index_map=lambda i: (i, 0)),
                      pl.BlockSpec((1, SC_WINDOW), index_map=lambda i: (i, 0)),
                      pl.BlockSpec((1, SC_WINDOW), index_map=lambda i: (i, 0))],
            out_specs=[],
            core_axis_name=("c", "s"), dimension_semantics=(pltpu.PARALLEL,),
        )(x_hbm, ia_hbm, ib_hbm)

    return k(src, idx_a, idx_b)


def _sc_gather2(table, idx_a, idx_b):
    windows = idx_a.shape[0]
    m = windows * SC_WINDOW
    mesh = plsc.VectorSubcoreMesh(core_axis_name="c", subcore_axis_name="s")

    @functools.partial(pl.kernel, out_type=jax.ShapeDtypeStruct((2 * m, LANES), table.dtype), mesh=mesh)
    def k(x_hbm, ia_hbm, ib_hbm, o_hbm):
        def body(i_vmem, o_vmem):
            pltpu.sync_copy(x_hbm.at[i_vmem.at[0]], o_vmem)

        for half, i_hbm in enumerate((ia_hbm, ib_hbm)):
            pltpu.emit_pipeline(
                body, grid=(windows,),
                in_specs=[pl.BlockSpec((1, SC_WINDOW), index_map=lambda i: (i, 0))],
                out_specs=[pl.BlockSpec((SC_WINDOW, LANES), index_map=lambda i, half=half: (half * windows + i, 0))],
                core_axis_name=("c", "s"), dimension_semantics=(pltpu.PARALLEL,),
            )(i_hbm, o_hbm)

    return k(table, idx_a, idx_b).reshape(2, m, LANES)


def _expert_kernel(te_ref, tv_ref, used_ref, xs_hbm, wg_ref, wu_ref, wd_ref, ys_ref, wgb_ref, wub_ref, wdb_ref,
                   xbuf_ref, sem_ref, *, tile):
    n = pl.program_id(0)
    valid = tv_ref[n]
    used = used_ref[0]

    def row_tile_copy(step):
        slot = lax.rem(step, XS_SLOTS)
        return pltpu.make_async_copy(xs_hbm.at[pl.ds(step * (tile // 8), tile // 8)], xbuf_ref.at[slot], sem_ref.at[slot])

    @pl.when(n == 0)
    def _():
        row_tile_copy(0).start()

        @pl.when(used > 1)
        def _():
            row_tile_copy(1).start()

    @pl.when(n + 2 < used)
    def _():
        row_tile_copy(n + 2).start()

    @pl.when(jnp.logical_or(n == 0, te_ref[n] != te_ref[jnp.maximum(n - 1, 0)]))
    def _():
        wgb_ref[...] = wg_ref[...].astype(BF16)
        wub_ref[...] = wu_ref[...].astype(BF16)
        wdb_ref[...] = wd_ref[...].astype(BF16)

    @pl.when(n < used)
    def _():
        row_tile_copy(n).wait()
        slot = lax.rem(n, XS_SLOTS)
        x = _unpack_bf16_pairs([xbuf_ref[slot, :, j, :, :].reshape(tile, LANES) for j in range(PACK_WORDS)])
        rows = lax.broadcasted_iota(jnp.int32, (tile, 1), 0)
        x = jnp.where(rows < valid, x, 0.0).astype(BF16)
        gt = _dot(x, wgb_ref[...])
        up = _dot(x, wub_ref[...])
        hid = (gt * jax.nn.sigmoid(gt)) * up
        y = _dot(hid.astype(BF16), wdb_ref[...])
        for j, w in enumerate(_pack_bf16_pairs(y)):
            ys_ref[:, j, :, :] = w.reshape(tile // 8, 8, LANES)


def _experts(tile_expert, tile_valid, tiles_used, xs, w_gate, w_up, w_down, tile):
    n_tiles = xs.shape[0] * 8 // tile
    blk = pl.BlockSpec((tile // 8, PACK_WORDS, 8, LANES),
                       lambda n, te, tv, used: (jnp.minimum(n, jnp.maximum(used[0] - 1, 0)), 0, 0, 0))
    return pl.pallas_call(
        functools.partial(_expert_kernel, tile=tile),
        grid_spec=pltpu.PrefetchScalarGridSpec(
            num_scalar_prefetch=3,
            grid=(n_tiles,),
            in_specs=[pl.BlockSpec(memory_space=pl.ANY),
                      pl.BlockSpec((None, D_MODEL, D_EXPERT), lambda n, te, tv, used: (te[n], 0, 0)),
                      pl.BlockSpec((None, D_MODEL, D_EXPERT), lambda n, te, tv, used: (te[n], 0, 0)),
                      pl.BlockSpec((None, D_EXPERT, D_MODEL), lambda n, te, tv, used: (te[n], 0, 0))],
            out_specs=blk,
            scratch_shapes=[pltpu.VMEM((D_MODEL, D_EXPERT), BF16), pltpu.VMEM((D_MODEL, D_EXPERT), BF16),
                            pltpu.VMEM((D_EXPERT, D_MODEL), BF16),
                            pltpu.VMEM((XS_SLOTS, tile // 8, PACK_WORDS, 8, LANES), jnp.uint32),
                            pltpu.SemaphoreType.DMA((XS_SLOTS,))],
        ),
        out_shape=jax.ShapeDtypeStruct(xs.shape, jnp.uint32),
        compiler_params=pltpu.CompilerParams(
            dimension_semantics=("arbitrary",), vmem_limit_bytes=VMEM_LIMIT),
        name="experts",
    )(tile_expert, tile_valid, tiles_used, xs, w_gate, w_up, w_down)


def _final_kernel(h_ref, g_ref, rt_ref, l2g_ref, l2b_ref, o_ref, *, tile):
    rt = rt_ref[...]
    lane = lax.broadcasted_iota(jnp.int32, rt.shape, 1)
    w1 = jnp.sum(jnp.where(lane == 2, rt, 0.0), axis=1, keepdims=True)
    w2 = jnp.sum(jnp.where(lane == 3, rt, 0.0), axis=1, keepdims=True)
    y1 = _unpack_bf16_pairs([g_ref[0, :, j, :, :].reshape(tile, LANES) for j in range(PACK_WORDS)])
    y2 = _unpack_bf16_pairs([g_ref[1, :, j, :, :].reshape(tile, LANES) for j in range(PACK_WORDS)])
    f = w1 * y1 + w2 * y2
    o_ref[...] = _layer_norm(ALPHA * h_ref[...] + f, l2g_ref[...], l2b_ref[...])


def _final(h1, g, rt, l2g, l2b, tile):
    T, D = h1.shape
    row = lambda w: pl.BlockSpec((tile, w), lambda i: (i, 0))
    full = lambda a: pl.BlockSpec(a.shape, lambda i: (0,) * a.ndim)
    return pl.pallas_call(
        functools.partial(_final_kernel, tile=tile),
        grid=(T // tile,),
        in_specs=[row(D),
                  pl.BlockSpec((2, tile // 8, PACK_WORDS, 8, LANES), lambda i: (0, i, 0, 0, 0)),
                  row(ROUTER_LANES), full(l2g), full(l2b)],
        out_specs=row(D),
        out_shape=jax.ShapeDtypeStruct((T, D), F32),
        compiler_params=pltpu.CompilerParams(dimension_semantics=("parallel",), vmem_limit_bytes=VMEM_LIMIT),
        name="final",
    )(h1, g, rt, l2g, l2b)


def _dispatch_plan(fields, counts, n_tokens, tile):
    cnt = counts[0, :N_EXPERTS].astype(jnp.int32)
    padded = (cnt + tile - 1) // tile * tile
    base = jnp.cumsum(padded) - padded
    e_ids = jnp.arange(N_EXPERTS, dtype=jnp.int32)

    def dest(row_e, row_r):
        e = fields[row_e].astype(jnp.int32)
        seg = jnp.sum(jnp.where(e[None, :] == e_ids[:, None], base[:, None], 0), axis=0)
        pos = seg + fields[row_r].astype(jnp.int32)
        p = pos.reshape(n_tokens // 8, 1, 8)
        j = jnp.arange(PACK_WORDS, dtype=jnp.int32).reshape(1, PACK_WORDS, 1)
        return ((p // 8) * (8 * PACK_WORDS) + j * 8 + p % 8).reshape(n_tokens * PACK_WORDS // SC_WINDOW, SC_WINDOW)

    n_tiles = (2 * n_tokens) // tile + N_EXPERTS
    start = jnp.arange(n_tiles, dtype=jnp.int32) * tile
    seg_end = base + padded
    te = jnp.minimum(jnp.sum((start[:, None] >= seg_end[None, :]).astype(jnp.int32), axis=1), N_EXPERTS - 1)
    te_base = jnp.sum(jnp.where(te[:, None] == e_ids[None, :], base[None, :], 0), axis=1)
    te_cnt = jnp.sum(jnp.where(te[:, None] == e_ids[None, :], cnt[None, :], 0), axis=1)
    tv = jnp.clip(te_cnt - (start - te_base), 0, tile)
    tiles_used = (jnp.sum(padded) // tile).reshape(1)
    return dest(0, 4), dest(1, 5), te, tv, tiles_used, n_tiles


def _t5_bucket(rel):
    half = N_BUCKETS // 2
    max_exact = half // 2
    ret = jnp.where(rel > 0, half, 0)
    n = jnp.abs(rel)
    nf = jnp.maximum(n, 1).astype(F32)
    large = max_exact + (jnp.log(nf / max_exact) / math.log(MAX_DISTANCE / max_exact)
                         * (half - max_exact)).astype(jnp.int32)
    large = jnp.minimum(large, half - 1)
    return ret + jnp.where(n < max_exact, n, large)


def _bucket_bias(rel_bias, bucket):
    rb = rel_bias.astype(F32)
    out = jnp.zeros((N_Q_HEADS,) + bucket.shape, F32)
    for b in range(N_BUCKETS):
        out = out + jnp.where(bucket[None] == b, rb[b][:, None, None], 0.0)
    return out


def _pair_rows(t):
    return jnp.concatenate([t[:N_PAIRS], t[N_PAIRS:]], axis=-1).reshape(N_PAIRS * BLOCK, -1)


def _bias_tables(rel_bias, sink):
    qi = jnp.arange(BLOCK)
    kj = jnp.arange(3 * BLOCK) - BLOCK
    rel = kj[None, :] - qi[:, None]
    band = _bucket_bias(rel_bias, _t5_bucket(rel))
    in_win = (jnp.abs(rel) <= WINDOW)[None]
    not_prev = (kj >= 0)[None, None, :]
    not_next = (kj < BLOCK)[None, None, :]
    variants = [jnp.where(in_win & not_prev, band, NEG),
                jnp.where(in_win, band, NEG),
                jnp.where(in_win & not_next, band, NEG)]
    bias = jnp.stack([_pair_rows(v) for v in variants])

    off = BLOCK - N_META
    mvars = []
    for blk in (1, 2):
        qpos = blk * BLOCK + qi - off
        meta_rel = jnp.arange(N_META)[None, :] - qpos[:, None]
        mvars.append(_pair_rows(_bucket_bias(rel_bias, _t5_bucket(meta_rel))))
    mbias = jnp.stack(mvars)
    s = sink.astype(F32)
    sink_tab = jnp.repeat(jnp.stack([s[:N_PAIRS], s[N_PAIRS:]], axis=-1), BLOCK, axis=0)
    return bias, mbias, sink_tab


def kernel(x_prompt, x_sample, meta, ln_in_g, ln_in_b, rel_bias, w_in, w_att_branch, sink, conv_w, conv_b,
           conv_ln_g, conv_ln_b, w_conv_out, w_out, ln1_g, ln1_b, w_group, b_group, w_router, b_router,
           w_gate, w_up, w_down, ln2_g, ln2_b):
    row = lambda v: v.reshape(1, -1).astype(F32)
    w = w_in[0]
    wq = (w[:, :Q_END].reshape(D_MODEL, 2, N_PAIRS, HEAD_DIM).transpose(0, 2, 1, 3)
          .reshape(D_MODEL, ATT_WIDTH).astype(BF16))
    watt = (w_att_branch[0].reshape(2, N_PAIRS, HEAD_DIM, D_MODEL).transpose(1, 0, 2, 3)
            .reshape(ATT_WIDTH, D_MODEL).astype(BF16))
    wkv = w[:, Q_END:V_END].astype(BF16)
    wglu = w[:, V_END:GLU_END].astype(BF16)
    wg = w[:, GLU_END:].astype(BF16)
    wco = w_conv_out[0].astype(BF16)
    wout = w_out[0].astype(BF16)
    wr = jnp.zeros((D_MODEL, ROUTER_LANES), F32)
    wr = wr.at[:, :N_EXPERTS].set(w_router[0]).at[:, N_EXPERTS:N_EXPERTS + N_GROUPS].set(w_group[0]).astype(BF16)
    br = jnp.zeros((1, ROUTER_LANES), F32)
    br = br.at[0, :N_EXPERTS].set(b_router[0]).at[0, N_EXPERTS:N_EXPERTS + N_GROUPS].set(b_group[0])
    ln_g, ln_b = row(ln_in_g), row(ln_in_b)
    bias, mbias, sink_tab = _bias_tables(rel_bias, sink[0])

    xm = jnp.concatenate([jnp.zeros((BLOCK - N_META, D_MODEL), F32), meta.astype(F32)], axis=0)
    _, _, kv_m, zc_m = _proj(xm, xm, _Geom(1, 1, 1, 1), ln_g, ln_b, wq, wkv, wglu, BLOCK)
    kv_meta = kv_m[BLOCK - N_META:BLOCK]
    z_meta = zc_m[BLOCK - N_META:BLOCK]

    (bp, sp, _), (bs, ss, _) = x_prompt.shape, x_sample.shape
    xp, xs = x_prompt.reshape(bp * sp, D_MODEL), x_sample.reshape(bs * ss, D_MODEL)
    geom = _geom(x_prompt, x_sample, TILE)
    h0, q, kv, zc = _proj(xp, xs, _geom(x_prompt, x_sample, TILE_PROJ), ln_g, ln_b, wq, wkv, wglu, TILE_PROJ)
    att = _attn(q, kv, _geom(x_prompt, x_sample, TILE_ATTN), kv_meta, bias, mbias, sink_tab, TILE_ATTN)
    cz = _conv(zc, _geom(x_prompt, x_sample, TILE_CONV), z_meta, conv_w[0], row(conv_b[0]), row(conv_ln_g[0]),
               row(conv_ln_b[0]), TILE_CONV)
    l1g, l1b, l2g, l2b = row(ln1_g[0]), row(ln1_b[0]), row(ln2_g[0]), row(ln2_b[0])

    def moe(first_tile, n_tiles):
        n = n_tiles * TILE
        h1, h1p, rt, fields, counts = _out(h0, att, cz, wg, watt, wco, wout, l1g, l1b, wr, br,
                                           TILE_OUT, first_tile * TILE // TILE_OUT, n_tiles * TILE // TILE_OUT)
        idx1, idx2, tile_expert, tile_valid, tiles_used, n_tiles = _dispatch_plan(fields, counts, n, TILE_EXPERT)
        n_rows = n_tiles * TILE_EXPERT
        xsorted = _sc_scatter2(h1p.reshape(n * PACK_WORDS, LANES), idx1, idx2, n_rows * PACK_WORDS)
        ys = _experts(tile_expert, tile_valid, tiles_used, xsorted.reshape(n_rows // 8, PACK_WORDS, 8, LANES),
                      w_gate[0], w_up[0], w_down[0], TILE_EXPERT)
        g = _sc_gather2(ys.reshape(n_rows * PACK_WORDS, LANES), idx1, idx2)
        return _final(h1, g.reshape(2, n // 8, PACK_WORDS, 8, LANES), rt, l2g, l2b, TILE_FINAL)

    return moe(0, geom.n_p).reshape(x_prompt.shape), moe(geom.n_p, geom.n_s).reshape(x_sample.shape)
```

```python
import functools
import math
from typing import NamedTuple

import jax
import jax.numpy as jnp
from jax import lax
from jax.experimental import pallas as pl
from jax.experimental.pallas import tpu as pltpu
from jax.experimental.pallas import tpu_sc as plsc

D_MODEL = 1024
N_META = 16
BLOCK = 128
WINDOW = 128
N_Q_HEADS = 8
N_KV_HEADS = 2
HEAD_DIM = 64
ATT_WIDTH = N_Q_HEADS * HEAD_DIM
KV_WIDTH = N_KV_HEADS * HEAD_DIM
CONV_WIDTH = D_MODEL // 2
CONV_K = 31
N_BUCKETS = 32
MAX_DISTANCE = 128
N_GROUPS = 4
EXPERTS_PER_GROUP = 8
N_EXPERTS = N_GROUPS * EXPERTS_PER_GROUP
D_EXPERT = 256
LN_EPS = 1e-5
DEPTH = 1
ALPHA = (2 * DEPTH) ** 0.25
NEG = -1e30
Q_END = ATT_WIDTH
K_END = Q_END + KV_WIDTH
V_END = K_END + KV_WIDTH
GLU_END = V_END + 2 * CONV_WIDTH

N_PAIRS = N_Q_HEADS // 2
LANES = 128
SUBLANES = 8
CONV_HALO = 16
ROUTER_LANES = 128

TILE = 512
CONV_ROWS = 128
LN_ROWS = 64
SHIFT_ROWS = 128
TILE_PROJ = 1024
TILE_OUT = 1024
XS_SLOTS = 3
TILE_FINAL = 1024
TILE_ATTN = 2048
TILE_EXPERT = 512
PACK_WORDS = 4
SC_WINDOW = 128
ROUTE_FIELDS = 8
VMEM_LIMIT = 56 * 1024 * 1024

BF16 = jnp.bfloat16
F32 = jnp.float32


def _layer_norm(x, g, b):
    mu = jnp.mean(x, axis=-1, keepdims=True)
    xc = x - mu
    var = jnp.mean(xc * xc, axis=-1, keepdims=True)
    return xc * lax.rsqrt(var + LN_EPS) * g + b


def _dot(a, b):
    return jnp.dot(a, b, preferred_element_type=F32)


def _dot_nt(a, b):
    return lax.dot_general(a, b, (((1,), (1,)), ((), ())), preferred_element_type=F32)


class _Geom(NamedTuple):
    n_p: int
    n_s: int
    tp: int
    ts: int


def _geom(x_prompt, x_sample, tile):
    (bp, sp, _), (bs, ss, _) = x_prompt.shape, x_sample.shape
    return _Geom(bp * sp // tile, bs * ss // tile, sp // tile, ss // tile)


def _seq_pos(t, g):
    is_p = t < g.n_p
    local = jnp.where(is_p, lax.rem(t, g.tp), lax.rem(jnp.maximum(t - g.n_p, 0), g.ts))
    return is_p, local == 0, local == jnp.where(is_p, g.tp - 1, g.ts - 1)


def _x_specs(g, tile, width):
    return [pl.BlockSpec((tile, width), lambda t: (jnp.minimum(t, g.n_p - 1), 0)),
            pl.BlockSpec((tile, width), lambda t: (jnp.maximum(t - g.n_p, 0), 0))]


def _proj_kernel(xp_ref, xs_ref, g_ref, b_ref, wq_ref, wkv_ref, wglu_ref, h0_ref, q_ref, kv_ref, zc_ref, *, geom):
    x = jnp.where(pl.program_id(0) < geom.n_p, xp_ref[...], xs_ref[...])
    h0 = _layer_norm(x, g_ref[...], b_ref[...])
    h0_ref[...] = h0
    h = h0.astype(BF16)
    q_ref[...] = _dot(h, wq_ref[...]).astype(BF16)
    kv_ref[...] = _dot(h, wkv_ref[...]).astype(BF16)
    u = _dot(h, wglu_ref[...])
    zc_ref[...] = (u[:, :CONV_WIDTH] * jax.nn.sigmoid(u[:, CONV_WIDTH:])).astype(BF16)


def _proj(xp, xs, geom, ln_g, ln_b, wq, wkv, wglu, tile):
    T = (geom.n_p + geom.n_s) * tile
    row = lambda w: pl.BlockSpec((tile, w), lambda t: (t, 0))
    full = lambda a: pl.BlockSpec(a.shape, lambda t: (0,) * a.ndim)
    return pl.pallas_call(
        functools.partial(_proj_kernel, geom=geom),
        grid=(geom.n_p + geom.n_s,),
        in_specs=_x_specs(geom, tile, D_MODEL) + [full(ln_g), full(ln_b), full(wq), full(wkv), full(wglu)],
        out_specs=[row(D_MODEL), row(ATT_WIDTH), row(2 * KV_WIDTH), row(CONV_WIDTH)],
        out_shape=[
            jax.ShapeDtypeStruct((T, D_MODEL), F32),
            jax.ShapeDtypeStruct((T, ATT_WIDTH), BF16),
            jax.ShapeDtypeStruct((T, 2 * KV_WIDTH), BF16),
            jax.ShapeDtypeStruct((T, CONV_WIDTH), BF16),
        ],
        compiler_params=pltpu.CompilerParams(dimension_semantics=("parallel",), vmem_limit_bytes=VMEM_LIMIT),
        name="proj",
    )(xp, xs, ln_g, ln_b, wq, wkv, wglu)


def _attn_kernel(q_ref, kvp_ref, kvc_ref, kvn_ref, kvm_ref, bias_ref, mbias_ref, sink_ref, o_ref, *, tile, geom):
    _, seq_first, seq_last = _seq_pos(pl.program_id(0), geom)
    blocks = tile // BLOCK
    scale = HEAD_DIM ** -0.5

    lane = lax.broadcasted_iota(jnp.int32, (1, LANES), 1)
    lo = lane < HEAD_DIM

    def split_heads(t):
        z = jnp.zeros_like(t)
        return jnp.where(lo, t, z), jnp.where(lo, z, t)

    kv_ext = jnp.concatenate([kvp_ref[...], kvc_ref[...], kvn_ref[...]], axis=0)
    k_ext = kv_ext[:, :KV_WIDTH] * jnp.asarray(scale, BF16)
    v_ext = kv_ext[:, KV_WIDTH:]
    ka, kb = split_heads(k_ext)
    va, vb = split_heads(v_ext)
    kma, kmb = split_heads(kvm_ref[:, :KV_WIDTH] * jnp.asarray(scale, BF16))
    vma, vmb = split_heads(kvm_ref[:, KV_WIDTH:])
    km_cat = jnp.concatenate([kma, kmb], axis=0)
    vm_cat = jnp.concatenate([vma, vmb], axis=0)

    mlane = lax.broadcasted_iota(jnp.int32, (1, 2 * N_META), 1)
    m_first = mlane < N_META
    sink = sink_ref[...]
    lane_o = lax.broadcasted_iota(jnp.int32, (1, LANES), 1) < HEAD_DIM

    for j in range(blocks):
        first = jnp.logical_and(seq_first, j == 0)
        last = jnp.logical_and(seq_last, j == blocks - 1)
        variant = jnp.where(first, 0, jnp.where(last, 2, 1))
        mvariant = jnp.where(first, 0, 1)

        r0 = j * BLOCK
        qb = q_ref[r0:r0 + BLOCK, :]
        q4 = jnp.concatenate([qb[:, p * LANES:(p + 1) * LANES] for p in range(N_PAIRS)], axis=0)
        k_cat = jnp.concatenate([ka[r0:r0 + 3 * BLOCK], kb[r0:r0 + 3 * BLOCK]], axis=0)
        v_cat = jnp.concatenate([va[r0:r0 + 3 * BLOCK], vb[r0:r0 + 3 * BLOCK]], axis=0)

        s = _dot_nt(q4, k_cat) + bias_ref[variant]
        sm = _dot_nt(q4, km_cat) + mbias_ref[mvariant]

        s_a, s_b = s[:, :3 * BLOCK], s[:, 3 * BLOCK:]
        sm_a = jnp.where(m_first, sm, NEG)
        sm_b = jnp.where(m_first, NEG, sm)
        m_a = jnp.maximum(jnp.maximum(jnp.max(s_a, axis=1, keepdims=True),
                                      jnp.max(sm_a, axis=1, keepdims=True)), sink[:, 0:1])
        m_b = jnp.maximum(jnp.maximum(jnp.max(s_b, axis=1, keepdims=True),
                                      jnp.max(sm_b, axis=1, keepdims=True)), sink[:, 1:2])
        p_a = jnp.exp(s_a - m_a)
        p_b = jnp.exp(s_b - m_b)
        pm = jnp.exp(jnp.where(m_first, sm - m_a, sm - m_b))
        l_a = (jnp.sum(p_a, axis=1, keepdims=True) + jnp.sum(jnp.where(m_first, pm, 0.0), axis=1, keepdims=True)
               + jnp.exp(sink[:, 0:1] - m_a))
        l_b = (jnp.sum(p_b, axis=1, keepdims=True) + jnp.sum(jnp.where(m_first, 0.0, pm), axis=1, keepdims=True)
               + jnp.exp(sink[:, 1:2] - m_b))
        p = jnp.concatenate([p_a, p_b], axis=1).astype(BF16)
        o = _dot(p, v_cat) + _dot(pm.astype(BF16), vm_cat)
        o = o * jnp.where(lane_o, 1.0 / l_a, 1.0 / l_b)
        for pr in range(N_PAIRS):
            o_ref[r0:r0 + BLOCK, pr * LANES:(pr + 1) * LANES] = o[pr * BLOCK:(pr + 1) * BLOCK].astype(BF16)


def _attn(q, kv, geom, kv_meta, bias, mbias, sink_tab, tile):
    T = q.shape[0]
    bpt = tile // BLOCK
    n_blocks = T // BLOCK
    full = lambda a: pl.BlockSpec(a.shape, lambda t: (0,) * a.ndim)
    return pl.pallas_call(
        functools.partial(_attn_kernel, tile=tile, geom=geom),
        grid=(T // tile,),
        in_specs=[
            pl.BlockSpec((tile, ATT_WIDTH), lambda t: (t, 0)),
            pl.BlockSpec((BLOCK, 2 * KV_WIDTH), lambda t: (jnp.maximum(t * bpt - 1, 0), 0)),
            pl.BlockSpec((tile, 2 * KV_WIDTH), lambda t: (t, 0)),
            pl.BlockSpec((BLOCK, 2 * KV_WIDTH), lambda t: (jnp.minimum((t + 1) * bpt, n_blocks - 1), 0)),
            full(kv_meta), full(bias), full(mbias), full(sink_tab),
        ],
        out_specs=pl.BlockSpec((tile, ATT_WIDTH), lambda t: (t, 0)),
        out_shape=jax.ShapeDtypeStruct((T, ATT_WIDTH), BF16),
        compiler_params=pltpu.CompilerParams(dimension_semantics=("parallel",), vmem_limit_bytes=VMEM_LIMIT),
        name="attn",
    )(q, kv, kv, kv, kv_meta, bias, mbias, sink_tab)


def _conv_kernel(zp_ref, zc_ref, zn_ref, zm_ref, w_ref, cb_ref, g_ref, b_ref, o_ref, ext_ref, sh_ref, y_ref,
                 *, tile, geom):
    i = pl.program_id(0)
    _, seq_first, seq_last = _seq_pos(i, geom)
    ext_ref[0:CONV_HALO, :] = jnp.where(seq_first, zm_ref[...], zp_ref[...]).astype(F32)
    ext_ref[CONV_HALO:CONV_HALO + tile, :] = zc_ref[...].astype(F32)
    ext_ref[CONV_HALO + tile:, :] = jnp.where(seq_last, 0.0, zn_ref[...].astype(F32))
    off = CONV_HALO - CONV_K // 2
    reach = (off + CONV_K - 1) // SUBLANES * SUBLANES
    for p in range(1, SUBLANES):
        for r0 in range(0, tile + reach, SHIFT_ROWS):
            n = min(SHIFT_ROWS, tile + reach - r0)
            sh_ref[p - 1, r0:r0 + n, :] = ext_ref[r0 + p:r0 + p + n, :]

    def taps(r0, cs):
        acc = jnp.zeros((CONV_ROWS, LANES), F32)
        for k in range(CONV_K):
            p, a = (off + k) % SUBLANES, (off + k) // SUBLANES * SUBLANES
            rows = slice(r0 + a, r0 + a + CONV_ROWS)
            win = ext_ref[rows, cs] if p == 0 else sh_ref[p - 1, rows, cs]
            acc = acc + win * w_ref[k:k + 1, cs]
        y_ref[r0:r0 + CONV_ROWS, cs] = acc

    for c in range(CONV_WIDTH // LANES):
        for r in range(tile // CONV_ROWS):
            pl.when(i >= 0)(functools.partial(taps, r * CONV_ROWS, slice(c * LANES, (c + 1) * LANES)))
    cb, g, b = cb_ref[...], g_ref[...], b_ref[...]
    for r in range(tile // LN_ROWS):
        r0 = r * LN_ROWS
        y = _layer_norm(y_ref[r0:r0 + LN_ROWS, :] + cb, g, b)
        o_ref[r0:r0 + LN_ROWS, :] = (y * jax.nn.sigmoid(y)).astype(BF16)


def _conv(zc, geom, z_meta, conv_w, conv_b, ln_g, ln_b, tile):
    T, C = zc.shape
    hpt = tile // CONV_HALO
    n_halo = T // CONV_HALO
    full = lambda a: pl.BlockSpec(a.shape, lambda t: (0,) * a.ndim)
    return pl.pallas_call(
        functools.partial(_conv_kernel, tile=tile, geom=geom),
        grid=(T // tile,),
        in_specs=[
            pl.BlockSpec((CONV_HALO, C), lambda t: (jnp.maximum(t * hpt - 1, 0), 0)),
            pl.BlockSpec((tile, C), lambda t: (t, 0)),
            pl.BlockSpec((CONV_HALO, C), lambda t: (jnp.minimum((t + 1) * hpt, n_halo - 1), 0)),
            full(z_meta), full(conv_w), full(conv_b), full(ln_g), full(ln_b),
        ],
        out_specs=pl.BlockSpec((tile, C), lambda t: (t, 0)),
        out_shape=jax.ShapeDtypeStruct((T, C), BF16),
        scratch_shapes=[pltpu.VMEM((tile + 2 * CONV_HALO, C), F32),
                        pltpu.VMEM((SUBLANES - 1, tile + 2 * CONV_HALO - SUBLANES, C), F32),
                        pltpu.VMEM((tile, C), F32)],
        compiler_params=pltpu.CompilerParams(dimension_semantics=("parallel",), vmem_limit_bytes=VMEM_LIMIT),
        name="conv",
    )(zc, zc, zc, z_meta, conv_w, conv_b, ln_g, ln_b)


def _route(r):
    lane_i = lax.broadcasted_iota(jnp.int32, r.shape, 1)
    lane = lane_i.astype(F32)
    big = float(1 << 20)
    is_g = jnp.logical_and(lane_i >= N_EXPERTS, lane_i < N_EXPERTS + N_GROUPS)
    lg = jnp.where(is_g, r, -jnp.inf)
    mg = jnp.max(lg, axis=1, keepdims=True)
    g_w = 1.0 / jnp.sum(jnp.exp(lg - mg), axis=1, keepdims=True)
    g_idx = jnp.min(jnp.where(lg == mg, lane - N_EXPERTS, big), axis=1, keepdims=True)
    lane_group = jnp.right_shift(lane_i, EXPERTS_PER_GROUP.bit_length() - 1).astype(F32)
    in_group = jnp.logical_and(lane_i < N_EXPERTS, lane_group == g_idx)
    le = jnp.where(in_group, r, -jnp.inf)
    m1 = jnp.max(le, axis=1, keepdims=True)
    den = jnp.sum(jnp.exp(le - m1), axis=1, keepdims=True)
    i1 = jnp.min(jnp.where(le == m1, lane, big), axis=1, keepdims=True)
    le2 = jnp.where(lane == i1, -jnp.inf, le)
    m2 = jnp.max(le2, axis=1, keepdims=True)
    i2 = jnp.min(jnp.where(le2 == m2, lane, big), axis=1, keepdims=True)
    p1 = 1.0 / den
    p2 = jnp.exp(m2 - m1) / den
    tot = p1 + p2
    return i1, i2, g_w * (p1 / tot), g_w * (p2 / tot)


def _pack_bf16_pairs(x):
    half = x.shape[1] // 2
    words = []
    for j in range(half // LANES):
        lo = pltpu.bitcast(x[:, j * LANES:(j + 1) * LANES].astype(BF16).astype(F32), jnp.uint32)
        hi = pltpu.bitcast(x[:, half + j * LANES:half + (j + 1) * LANES].astype(BF16).astype(F32), jnp.uint32)
        words.append(hi | (lo >> 16))
    return words


def _unpack_bf16_pairs(words):
    lo = [pltpu.bitcast(w << 16, F32) for w in words]
    hi = [pltpu.bitcast(w & jnp.uint32(0xFFFF0000), F32) for w in words]
    return jnp.concatenate(lo + hi, axis=1)


def _out_kernel(h0_ref, att_ref, cz_ref, wg_ref, watt_ref, wco_ref, wout_ref,
                l1g_ref, l1b_ref, wr_ref, br_ref, before_ref, h1_ref, h1p_ref, rt_ref, fld_ref, cnt_ref, *, tile):
    @pl.when(pl.program_id(0) == 0)
    def _():
        cnt_ref[...] = jnp.zeros_like(cnt_ref)

    h0 = h0_ref[...]
    hb = h0.astype(BF16)
    g_att = jax.nn.sigmoid(_dot(hb, wg_ref[:, :D_MODEL]))
    mix = g_att * _dot(att_ref[...], watt_ref[...])
    g_conv = jax.nn.sigmoid(_dot(hb, wg_ref[:, D_MODEL:]))
    mix = mix + g_conv * _dot(cz_ref[...], wco_ref[...])
    m = _dot(mix.astype(BF16), wout_ref[...])
    h1 = _layer_norm(ALPHA * h0 + m, l1g_ref[...], l1b_ref[...])
    h1_ref[...] = h1
    for j, w in enumerate(_pack_bf16_pairs(h1)):
        h1p_ref[:, j, :, :] = w.reshape(tile // SUBLANES, SUBLANES, LANES)

    r = _dot(h1.astype(BF16), wr_ref[...]) + br_ref[...]
    i1, i2, w1, w2 = _route(r)
    lane = lax.broadcasted_iota(jnp.int32, (tile, ROUTER_LANES), 1)
    lane_f = lane.astype(F32)
    hit1, hit2 = lane_f == i1, lane_f == i2
    onehot = jnp.where(jnp.logical_or(hit1, hit2), 1.0, 0.0)
    seen = _dot(before_ref[...], onehot.astype(BF16)) + cnt_ref[0:1, :]
    rank1 = jnp.sum(jnp.where(hit1, seen, 0.0), axis=1, keepdims=True)
    rank2 = jnp.sum(jnp.where(hit2, seen, 0.0), axis=1, keepdims=True)
    cnt_ref[...] = cnt_ref[...] + jnp.sum(onehot, axis=0, keepdims=True)
    fields = (i1, i2, w1, w2, rank1, rank2)
    rt = jnp.zeros((tile, ROUTER_LANES), F32)
    for k, v in enumerate(fields):
        rt = jnp.where(lane == k, v, rt)
    rt_ref[...] = rt
    fld_ref[...] = jnp.transpose(rt)[0:ROUTE_FIELDS, :]


def _out(h0, att, cz, wg, watt, wco, wout, l1g, l1b, wr, br, tile, first_tile, n_tiles):
    T, D = n_tiles * tile, D_MODEL
    own = lambda w: pl.BlockSpec((tile, w), lambda i: (i, 0))
    flat = lambda w: pl.BlockSpec((tile, w), lambda i: (first_tile + i, 0))
    full = lambda a: pl.BlockSpec(a.shape, lambda i: (0,) * a.ndim, pipeline_mode=pl.Buffered(1))
    before = (jnp.arange(tile)[None, :] < jnp.arange(tile)[:, None]).astype(BF16)
    return pl.pallas_call(
        functools.partial(_out_kernel, tile=tile),
        grid=(T // tile,),
        in_specs=[flat(D), flat(ATT_WIDTH), flat(CONV_WIDTH), full(wg),
                  full(watt), full(wco), full(wout), full(l1g), full(l1b), full(wr), full(br), full(before)],
        out_specs=[own(D),
                   pl.BlockSpec((tile // SUBLANES, PACK_WORDS, SUBLANES, LANES), lambda i: (i, 0, 0, 0)),
                   own(ROUTER_LANES),
                   pl.BlockSpec((ROUTE_FIELDS, tile), lambda i: (0, i)),
                   pl.BlockSpec((SUBLANES, ROUTER_LANES), lambda i: (0, 0))],
        out_shape=[jax.ShapeDtypeStruct((T, D), F32),
                   jax.ShapeDtypeStruct((T // SUBLANES, PACK_WORDS, SUBLANES, LANES), jnp.uint32),
                   jax.ShapeDtypeStruct((T, ROUTER_LANES), F32),
                   jax.ShapeDtypeStruct((ROUTE_FIELDS, T), F32),
                   jax.ShapeDtypeStruct((SUBLANES, ROUTER_LANES), F32)],
        compiler_params=pltpu.CompilerParams(dimension_semantics=("arbitrary",), vmem_limit_bytes=VMEM_LIMIT),
        name="out",
    )(h0, att, cz, wg, watt, wco, wout, l1g, l1b, wr, br, before)


def _sc_scatter2(src, idx_a, idx_b, n_out):
    m = src.shape[0]
    mesh = plsc.VectorSubcoreMesh(core_axis_name="c", subcore_axis_name="s")

    @functools.partial(pl.kernel, out_type=jax.ShapeDtypeStruct((n_out, LANES), src.dtype), mesh=mesh)
    def k(x_hbm, ia_hbm, ib_hbm, o_hbm):
        def body(x_vmem, ia_vmem, ib_vmem):
            pltpu.sync_copy(x_vmem, o_hbm.at[ia_vmem.at[0]])
            pltpu.sync_copy(x_vmem, o_hbm.at[ib_vmem.at[0]])

        pltpu.emit_pipeline(
            body, grid=(m // SC_WINDOW,),
            in_specs=[pl.BlockSpec((SC_WINDOW, LANES), index_map=lambda i: (i, 0)),
                      pl.BlockSpec((1, SC_WINDOW), index_map=lambda i: (i, 0)),
                      pl.BlockSpec((1, SC_WINDOW), index_map=lambda i: (i, 0))],
            out_specs=[],
            core_axis_name=("c", "s"), dimension_semantics=(pltpu.PARALLEL,),
        )(x_hbm, ia_hbm, ib_hbm)

    return k(src, idx_a, idx_b)


def _sc_gather2(table, idx_a, idx_b):
    windows = idx_a.shape[0]
    m = windows * SC_WINDOW
    mesh = plsc.VectorSubcoreMesh(core_axis_name="c", subcore_axis_name="s")

    @functools.partial(pl.kernel, out_type=jax.ShapeDtypeStruct((2 * m, LANES), table.dtype), mesh=mesh)
    def k(x_hbm, ia_hbm, ib_hbm, o_hbm):
        def body(i_vmem, o_vmem):
            pltpu.sync_copy(x_hbm.at[i_vmem.at[0]], o_vmem)

        for half, i_hbm in enumerate((ia_hbm, ib_hbm)):
            pltpu.emit_pipeline(
                body, grid=(windows,),
                in_specs=[pl.BlockSpec((1, SC_WINDOW), index_map=lambda i: (i, 0))],
                out_specs=[pl.BlockSpec((SC_WINDOW, LANES), index_map=lambda i, half=half: (half * windows + i, 0))],
                core_axis_name=("c", "s"), dimension_semantics=(pltpu.PARALLEL,),
            )(i_hbm, o_hbm)

    return k(table, idx_a, idx_b).reshape(2, m, LANES)


def _expert_kernel(te_ref, tv_ref, used_ref, xs_hbm, wg_ref, wu_ref, wd_ref, ys_ref, wgb_ref, wub_ref, wdb_ref,
                   xbuf_ref, sem_ref, *, tile):
    n = pl.program_id(0)
    valid = tv_ref[n]
    used = used_ref[0]

    def row_tile_copy(step):
        slot = lax.rem(step, XS_SLOTS)
        return pltpu.make_async_copy(xs_hbm.at[pl.ds(step * (tile // SUBLANES), tile // SUBLANES)], xbuf_ref.at[slot], sem_ref.at[slot])

    @pl.when(n == 0)
    def _():
        row_tile_copy(0).start()

        @pl.when(used > 1)
        def _():
            row_tile_copy(1).start()

    @pl.when(n + 2 < used)
    def _():
        row_tile_copy(n + 2).start()

    @pl.when(jnp.logical_or(n == 0, te_ref[n] != te_ref[jnp.maximum(n - 1, 0)]))
    def _():
        wgb_ref[...] = wg_ref[...].astype(BF16)
        wub_ref[...] = wu_ref[...].astype(BF16)
        wdb_ref[...] = wd_ref[...].astype(BF16)

    @pl.when(n < used)
    def _():
        row_tile_copy(n).wait()
        slot = lax.rem(n, XS_SLOTS)
        x = _unpack_bf16_pairs([xbuf_ref[slot, :, j, :, :].reshape(tile, LANES) for j in range(PACK_WORDS)])
        rows = lax.broadcasted_iota(jnp.int32, (tile, 1), 0)
        x = jnp.where(rows < valid, x, 0.0).astype(BF16)
        gt = _dot(x, wgb_ref[...])
        up = _dot(x, wub_ref[...])
        hid = (gt * jax.nn.sigmoid(gt)) * up
        y = _dot(hid.astype(BF16), wdb_ref[...])
        for j, w in enumerate(_pack_bf16_pairs(y)):
            ys_ref[:, j, :, :] = w.reshape(tile // SUBLANES, SUBLANES, LANES)


def _experts(tile_expert, tile_valid, tiles_used, xs, w_gate, w_up, w_down, tile):
    n_tiles = xs.shape[0] * SUBLANES // tile
    blk = pl.BlockSpec((tile // SUBLANES, PACK_WORDS, SUBLANES, LANES),
                       lambda n, te, tv, used: (jnp.minimum(n, jnp.maximum(used[0] - 1, 0)), 0, 0, 0))
    return pl.pallas_call(
        functools.partial(_expert_kernel, tile=tile),
        grid_spec=pltpu.PrefetchScalarGridSpec(
            num_scalar_prefetch=3,
            grid=(n_tiles,),
            in_specs=[pl.BlockSpec(memory_space=pl.ANY),
                      pl.BlockSpec((None, D_MODEL, D_EXPERT), lambda n, te, tv, used: (te[n], 0, 0)),
                      pl.BlockSpec((None, D_MODEL, D_EXPERT), lambda n, te, tv, used: (te[n], 0, 0)),
                      pl.BlockSpec((None, D_EXPERT, D_MODEL), lambda n, te, tv, used: (te[n], 0, 0))],
            out_specs=blk,
            scratch_shapes=[pltpu.VMEM((D_MODEL, D_EXPERT), BF16), pltpu.VMEM((D_MODEL, D_EXPERT), BF16),
                            pltpu.VMEM((D_EXPERT, D_MODEL), BF16),
                            pltpu.VMEM((XS_SLOTS, tile // SUBLANES, PACK_WORDS, SUBLANES, LANES), jnp.uint32),
                            pltpu.SemaphoreType.DMA((XS_SLOTS,))],
        ),
        out_shape=jax.ShapeDtypeStruct(xs.shape, jnp.uint32),
        compiler_params=pltpu.CompilerParams(
            dimension_semantics=("arbitrary",), vmem_limit_bytes=VMEM_LIMIT),
        name="experts",
    )(tile_expert, tile_valid, tiles_used, xs, w_gate, w_up, w_down)


def _final_kernel(h_ref, g_ref, rt_ref, l2g_ref, l2b_ref, o_ref, *, tile):
    rt = rt_ref[...]
    lane = lax.broadcasted_iota(jnp.int32, rt.shape, 1)
    w1 = jnp.sum(jnp.where(lane == 2, rt, 0.0), axis=1, keepdims=True)
    w2 = jnp.sum(jnp.where(lane == 3, rt, 0.0), axis=1, keepdims=True)
    y1 = _unpack_bf16_pairs([g_ref[0, :, j, :, :].reshape(tile, LANES) for j in range(PACK_WORDS)])
    y2 = _unpack_bf16_pairs([g_ref[1, :, j, :, :].reshape(tile, LANES) for j in range(PACK_WORDS)])
    f = w1 * y1 + w2 * y2
    o_ref[...] = _layer_norm(ALPHA * h_ref[...] + f, l2g_ref[...], l2b_ref[...])


def _final(h1, g, rt, l2g, l2b, tile):
    T, D = h1.shape
    row = lambda w: pl.BlockSpec((tile, w), lambda i: (i, 0))
    full = lambda a: pl.BlockSpec(a.shape, lambda i: (0,) * a.ndim)
    return pl.pallas_call(
        functools.partial(_final_kernel, tile=tile),
        grid=(T // tile,),
        in_specs=[row(D),
                  pl.BlockSpec((2, tile // SUBLANES, PACK_WORDS, SUBLANES, LANES), lambda i: (0, i, 0, 0, 0)),
                  row(ROUTER_LANES), full(l2g), full(l2b)],
        out_specs=row(D),
        out_shape=jax.ShapeDtypeStruct((T, D), F32),
        compiler_params=pltpu.CompilerParams(dimension_semantics=("parallel",), vmem_limit_bytes=VMEM_LIMIT),
        name="final",
    )(h1, g, rt, l2g, l2b)


def _dispatch_plan(fields, counts, n_tokens, tile):
    cnt = counts[0, :N_EXPERTS].astype(jnp.int32)
    padded = (cnt + tile - 1) // tile * tile
    base = jnp.cumsum(padded) - padded
    e_ids = jnp.arange(N_EXPERTS, dtype=jnp.int32)

    def dest(row_e, row_r):
        e = fields[row_e].astype(jnp.int32)
        seg = jnp.sum(jnp.where(e[None, :] == e_ids[:, None], base[:, None], 0), axis=0)
        pos = seg + fields[row_r].astype(jnp.int32)
        first = (pos // SUBLANES) * (SUBLANES * PACK_WORDS) + pos % SUBLANES
        per_window = SC_WINDOW // PACK_WORDS
        first = first.reshape(n_tokens // LANES, LANES)
        windows = []
        for c in range(LANES // per_window):
            pieces = [first[:, c * per_window + SUBLANES * a:c * per_window + SUBLANES * (a + 1)] + SUBLANES * j
                      for a in range(per_window // SUBLANES) for j in range(PACK_WORDS)]
            windows.append(jnp.concatenate(pieces, axis=1))
        return jnp.stack(windows, axis=1).reshape(n_tokens * PACK_WORDS // SC_WINDOW, SC_WINDOW)

    n_tiles = (2 * n_tokens) // tile + N_EXPERTS
    start = jnp.arange(n_tiles, dtype=jnp.int32) * tile
    seg_end = base + padded
    te = jnp.minimum(jnp.sum((start[:, None] >= seg_end[None, :]).astype(jnp.int32), axis=1), N_EXPERTS - 1)
    te_base = jnp.sum(jnp.where(te[:, None] == e_ids[None, :], base[None, :], 0), axis=1)
    te_cnt = jnp.sum(jnp.where(te[:, None] == e_ids[None, :], cnt[None, :], 0), axis=1)
    tv = jnp.clip(te_cnt - (start - te_base), 0, tile)
    tiles_used = (jnp.sum(padded) // tile).reshape(1)
    return dest(0, 4), dest(1, 5), te, tv, tiles_used, n_tiles


def _t5_bucket(rel):
    half = N_BUCKETS // 2
    max_exact = half // 2
    ret = jnp.where(rel > 0, half, 0)
    n = jnp.abs(rel)
    nf = jnp.maximum(n, 1).astype(F32)
    large = max_exact + (jnp.log(nf / max_exact) / math.log(MAX_DISTANCE / max_exact)
                         * (half - max_exact)).astype(jnp.int32)
    large = jnp.minimum(large, half - 1)
    return ret + jnp.where(n < max_exact, n, large)


def _bucket_bias(rel_bias, bucket):
    rb = rel_bias.astype(F32)
    out = jnp.zeros((N_Q_HEADS,) + bucket.shape, F32)
    for b in range(N_BUCKETS):
        out = out + jnp.where(bucket[None] == b, rb[b][:, None, None], 0.0)
    return out


def _pair_rows(t):
    return jnp.concatenate([t[:N_PAIRS], t[N_PAIRS:]], axis=-1).reshape(N_PAIRS * BLOCK, -1)


def _bias_tables(rel_bias, sink):
    qi = jnp.arange(BLOCK)
    kj = jnp.arange(3 * BLOCK) - BLOCK
    rel = kj[None, :] - qi[:, None]
    band = _bucket_bias(rel_bias, _t5_bucket(rel))
    in_win = (jnp.abs(rel) <= WINDOW)[None]
    not_prev = (kj >= 0)[None, None, :]
    not_next = (kj < BLOCK)[None, None, :]
    variants = [jnp.where(in_win & not_prev, band, NEG),
                jnp.where(in_win, band, NEG),
                jnp.where(in_win & not_next, band, NEG)]
    bias = jnp.stack([_pair_rows(v) for v in variants])

    off = BLOCK - N_META
    mvars = []
    for blk in (1, 2):
        qpos = blk * BLOCK + qi - off
        meta_rel = jnp.arange(N_META)[None, :] - qpos[:, None]
        mvars.append(_pair_rows(_bucket_bias(rel_bias, _t5_bucket(meta_rel))))
    mbias = jnp.stack(mvars)
    s = sink.astype(F32)
    sink_tab = jnp.repeat(jnp.stack([s[:N_PAIRS], s[N_PAIRS:]], axis=-1), BLOCK, axis=0)
    return bias, mbias, sink_tab


def kernel(x_prompt, x_sample, meta, ln_in_g, ln_in_b, rel_bias, w_in, w_att_branch, sink, conv_w, conv_b,
           conv_ln_g, conv_ln_b, w_conv_out, w_out, ln1_g, ln1_b, w_group, b_group, w_router, b_router,
           w_gate, w_up, w_down, ln2_g, ln2_b):
    row = lambda v: v.reshape(1, -1).astype(F32)
    w = w_in[0]
    wq = (w[:, :Q_END].reshape(D_MODEL, 2, N_PAIRS, HEAD_DIM).transpose(0, 2, 1, 3)
          .reshape(D_MODEL, ATT_WIDTH).astype(BF16))
    watt = (w_att_branch[0].reshape(2, N_PAIRS, HEAD_DIM, D_MODEL).transpose(1, 0, 2, 3)
            .reshape(ATT_WIDTH, D_MODEL).astype(BF16))
    wkv = w[:, Q_END:V_END].astype(BF16)
    wglu = w[:, V_END:GLU_END].astype(BF16)
    wg = w[:, GLU_END:].astype(BF16)
    wco = w_conv_out[0].astype(BF16)
    wout = w_out[0].astype(BF16)
    wr = jnp.zeros((D_MODEL, ROUTER_LANES), F32)
    wr = wr.at[:, :N_EXPERTS].set(w_router[0]).at[:, N_EXPERTS:N_EXPERTS + N_GROUPS].set(w_group[0]).astype(BF16)
    br = jnp.zeros((1, ROUTER_LANES), F32)
    br = br.at[0, :N_EXPERTS].set(b_router[0]).at[0, N_EXPERTS:N_EXPERTS + N_GROUPS].set(b_group[0])
    ln_g, ln_b = row(ln_in_g), row(ln_in_b)
    bias, mbias, sink_tab = _bias_tables(rel_bias, sink[0])

    xm = jnp.concatenate([jnp.zeros((BLOCK - N_META, D_MODEL), F32), meta.astype(F32)], axis=0)
    _, _, kv_m, zc_m = _proj(xm, xm, _Geom(1, 1, 1, 1), ln_g, ln_b, wq, wkv, wglu, BLOCK)
    kv_meta = kv_m[BLOCK - N_META:BLOCK]
    z_meta = zc_m[BLOCK - N_META:BLOCK]

    (bp, sp, _), (bs, ss, _) = x_prompt.shape, x_sample.shape
    xp, xs = x_prompt.reshape(bp * sp, D_MODEL), x_sample.reshape(bs * ss, D_MODEL)
    geom = _geom(x_prompt, x_sample, TILE)
    h0, q, kv, zc = _proj(xp, xs, _geom(x_prompt, x_sample, TILE_PROJ), ln_g, ln_b, wq, wkv, wglu, TILE_PROJ)
    att = _attn(q, kv, _geom(x_prompt, x_sample, TILE_ATTN), kv_meta, bias, mbias, sink_tab, TILE_ATTN)
    cz = _conv(zc, geom, z_meta, conv_w[0], row(conv_b[0]), row(conv_ln_g[0]), row(conv_ln_b[0]), TILE)
    l1g, l1b, l2g, l2b = row(ln1_g[0]), row(ln1_b[0]), row(ln2_g[0]), row(ln2_b[0])

    def moe(first_tile, n_tiles):
        n = n_tiles * TILE
        h1, h1p, rt, fields, counts = _out(h0, att, cz, wg, watt, wco, wout, l1g, l1b, wr, br,
                                           TILE_OUT, first_tile * TILE // TILE_OUT, n_tiles * TILE // TILE_OUT)
        idx1, idx2, tile_expert, tile_valid, tiles_used, n_tiles = _dispatch_plan(fields, counts, n, TILE_EXPERT)
        n_rows = n_tiles * TILE_EXPERT
        xsorted = _sc_scatter2(h1p.reshape(n * PACK_WORDS, LANES), idx1, idx2, n_rows * PACK_WORDS)
        ys = _experts(tile_expert, tile_valid, tiles_used,
                      xsorted.reshape(n_rows // SUBLANES, PACK_WORDS, SUBLANES, LANES),
                      w_gate[0], w_up[0], w_down[0], TILE_EXPERT)
        g = _sc_gather2(ys.reshape(n_rows * PACK_WORDS, LANES), idx1, idx2)
        return _final(h1, g.reshape(2, n // SUBLANES, PACK_WORDS, SUBLANES, LANES), rt, l2g, l2b, TILE_FINAL)

    return moe(0, geom.n_p).reshape(x_prompt.shape), moe(geom.n_p, geom.n_s).reshape(x_sample.shape)
```

```python
import functools
import math
from typing import NamedTuple

import jax
import jax.numpy as jnp
from jax import lax
from jax.experimental import pallas as pl
from jax.experimental.pallas import tpu as pltpu
from jax.experimental.pallas import tpu_sc as plsc

D_MODEL = 1024
N_META = 16
BLOCK = 128
WINDOW = 128
N_Q_HEADS = 8
N_KV_HEADS = 2
HEAD_DIM = 64
ATT_WIDTH = N_Q_HEADS * HEAD_DIM
KV_WIDTH = N_KV_HEADS * HEAD_DIM
CONV_WIDTH = D_MODEL // 2
CONV_K = 31
N_BUCKETS = 32
MAX_DISTANCE = 128
N_GROUPS = 4
EXPERTS_PER_GROUP = 8
N_EXPERTS = N_GROUPS * EXPERTS_PER_GROUP
D_EXPERT = 256
LN_EPS = 1e-5
DEPTH = 1
ALPHA = (2 * DEPTH) ** 0.25
NEG = -1e30
Q_END = ATT_WIDTH
K_END = Q_END + KV_WIDTH
V_END = K_END + KV_WIDTH
GLU_END = V_END + 2 * CONV_WIDTH

N_PAIRS = N_Q_HEADS // 2
LANES = 128
SUBLANES = 8
CONV_HALO = 16
ROUTER_LANES = 128

TILE = 512
CONV_ROWS = 128
LN_ROWS = 64
SHIFT_ROWS = 128
TILE_PROJ = 1024
TILE_OUT = 1024
XS_SLOTS = 3
TILE_FINAL = 1024
TILE_ATTN = 2048
TILE_EXPERT = 512
PACK_WORDS = 4
SC_WINDOW = 128
ROUTE_FIELDS = 8
VMEM_LIMIT = 56 * 1024 * 1024

BF16 = jnp.bfloat16
F32 = jnp.float32


def _layer_norm(x, g, b):
    mu = jnp.mean(x, axis=-1, keepdims=True)
    xc = x - mu
    var = jnp.mean(xc * xc, axis=-1, keepdims=True)
    return xc * lax.rsqrt(var + LN_EPS) * g + b


def _dot(a, b):
    return jnp.dot(a, b, preferred_element_type=F32)


def _dot_nt(a, b):
    return lax.dot_general(a, b, (((1,), (1,)), ((), ())), preferred_element_type=F32)


class _Geom(NamedTuple):
    n_p: int
    n_s: int
    tp: int
    ts: int


def _geom(x_prompt, x_sample, tile):
    (bp, sp, _), (bs, ss, _) = x_prompt.shape, x_sample.shape
    return _Geom(bp * sp // tile, bs * ss // tile, sp // tile, ss // tile)


def _seq_pos(t, g):
    is_p = t < g.n_p
    local = jnp.where(is_p, lax.rem(t, g.tp), lax.rem(jnp.maximum(t - g.n_p, 0), g.ts))
    return is_p, local == 0, local == jnp.where(is_p, g.tp - 1, g.ts - 1)


def _x_specs(g, tile, width):
    return [pl.BlockSpec((tile, width), lambda t: (jnp.minimum(t, g.n_p - 1), 0)),
            pl.BlockSpec((tile, width), lambda t: (jnp.maximum(t - g.n_p, 0), 0))]


def _proj_kernel(xp_ref, xs_ref, g_ref, b_ref, wq_ref, wkv_ref, wglu_ref, h0_ref, q_ref, kv_ref, zc_ref, *, geom):
    x = jnp.where(pl.program_id(0) < geom.n_p, xp_ref[...], xs_ref[...])
    h0 = _layer_norm(x, g_ref[...], b_ref[...])
    h0_ref[...] = h0
    h = h0.astype(BF16)
    q_ref[...] = _dot(h, wq_ref[...]).astype(BF16)
    kv_ref[...] = _dot(h, wkv_ref[...]).astype(BF16)
    u = _dot(h, wglu_ref[...])
    zc_ref[...] = (u[:, :CONV_WIDTH] * jax.nn.sigmoid(u[:, CONV_WIDTH:])).astype(BF16)


def _proj(xp, xs, geom, ln_g, ln_b, wq, wkv, wglu, tile):
    T = (geom.n_p + geom.n_s) * tile
    row = lambda w: pl.BlockSpec((tile, w), lambda t: (t, 0))
    full = lambda a: pl.BlockSpec(a.shape, lambda t: (0,) * a.ndim)
    return pl.pallas_call(
        functools.partial(_proj_kernel, geom=geom),
        grid=(geom.n_p + geom.n_s,),
        in_specs=_x_specs(geom, tile, D_MODEL) + [full(ln_g), full(ln_b), full(wq), full(wkv), full(wglu)],
        out_specs=[row(D_MODEL), row(ATT_WIDTH), row(2 * KV_WIDTH), row(CONV_WIDTH)],
        out_shape=[
            jax.ShapeDtypeStruct((T, D_MODEL), F32),
            jax.ShapeDtypeStruct((T, ATT_WIDTH), BF16),
            jax.ShapeDtypeStruct((T, 2 * KV_WIDTH), BF16),
            jax.ShapeDtypeStruct((T, CONV_WIDTH), BF16),
        ],
        compiler_params=pltpu.CompilerParams(dimension_semantics=("parallel",), vmem_limit_bytes=VMEM_LIMIT),
        name="proj",
    )(xp, xs, ln_g, ln_b, wq, wkv, wglu)


def _attn_kernel(q_ref, kvp_ref, kvc_ref, kvn_ref, kvm_ref, bias_ref, mbias_ref, sink_ref, o_ref, *, tile, geom):
    _, seq_first, seq_last = _seq_pos(pl.program_id(0), geom)
    blocks = tile // BLOCK
    scale = HEAD_DIM ** -0.5

    lane = lax.broadcasted_iota(jnp.int32, (1, LANES), 1)
    lo = lane < HEAD_DIM

    def split_heads(t):
        z = jnp.zeros_like(t)
        return jnp.where(lo, t, z), jnp.where(lo, z, t)

    kv_ext = jnp.concatenate([kvp_ref[...], kvc_ref[...], kvn_ref[...]], axis=0)
    k_ext = kv_ext[:, :KV_WIDTH] * jnp.asarray(scale, BF16)
    v_ext = kv_ext[:, KV_WIDTH:]
    ka, kb = split_heads(k_ext)
    va, vb = split_heads(v_ext)
    kma, kmb = split_heads(kvm_ref[:, :KV_WIDTH] * jnp.asarray(scale, BF16))
    vma, vmb = split_heads(kvm_ref[:, KV_WIDTH:])
    km_cat = jnp.concatenate([kma, kmb], axis=0)
    vm_cat = jnp.concatenate([vma, vmb], axis=0)

    mlane = lax.broadcasted_iota(jnp.int32, (1, 2 * N_META), 1)
    m_first = mlane < N_META
    sink = sink_ref[...]
    lane_o = lax.broadcasted_iota(jnp.int32, (1, LANES), 1) < HEAD_DIM

    for j in range(blocks):
        first = jnp.logical_and(seq_first, j == 0)
        last = jnp.logical_and(seq_last, j == blocks - 1)
        variant = jnp.where(first, 0, jnp.where(last, 2, 1))
        mvariant = jnp.where(first, 0, 1)

        r0 = j * BLOCK
        qb = q_ref[r0:r0 + BLOCK, :]
        q4 = jnp.concatenate([qb[:, p * LANES:(p + 1) * LANES] for p in range(N_PAIRS)], axis=0)
        k_cat = jnp.concatenate([ka[r0:r0 + 3 * BLOCK], kb[r0:r0 + 3 * BLOCK]], axis=0)
        v_cat = jnp.concatenate([va[r0:r0 + 3 * BLOCK], vb[r0:r0 + 3 * BLOCK]], axis=0)

        s = _dot_nt(q4, k_cat) + bias_ref[variant]
        sm = _dot_nt(q4, km_cat) + mbias_ref[mvariant]

        s_a, s_b = s[:, :3 * BLOCK], s[:, 3 * BLOCK:]
        sm_a = jnp.where(m_first, sm, NEG)
        sm_b = jnp.where(m_first, NEG, sm)
        m_a = jnp.maximum(jnp.maximum(jnp.max(s_a, axis=1, keepdims=True),
                                      jnp.max(sm_a, axis=1, keepdims=True)), sink[:, 0:1])
        m_b = jnp.maximum(jnp.maximum(jnp.max(s_b, axis=1, keepdims=True),
                                      jnp.max(sm_b, axis=1, keepdims=True)), sink[:, 1:2])
        p_a = jnp.exp(s_a - m_a)
        p_b = jnp.exp(s_b - m_b)
        pm = jnp.exp(jnp.where(m_first, sm - m_a, sm - m_b))
        l_a = (jnp.sum(p_a, axis=1, keepdims=True) + jnp.sum(jnp.where(m_first, pm, 0.0), axis=1, keepdims=True)
               + jnp.exp(sink[:, 0:1] - m_a))
        l_b = (jnp.sum(p_b, axis=1, keepdims=True) + jnp.sum(jnp.where(m_first, 0.0, pm), axis=1, keepdims=True)
               + jnp.exp(sink[:, 1:2] - m_b))
        p = jnp.concatenate([p_a, p_b], axis=1).astype(BF16)
        o = _dot(p, v_cat) + _dot(pm.astype(BF16), vm_cat)
        o = o * jnp.where(lane_o, 1.0 / l_a, 1.0 / l_b)
        for pr in range(N_PAIRS):
            o_ref[r0:r0 + BLOCK, pr * LANES:(pr + 1) * LANES] = o[pr * BLOCK:(pr + 1) * BLOCK].astype(BF16)


def _attn(q, kv, geom, kv_meta, bias, mbias, sink_tab, tile):
    T = q.shape[0]
    bpt = tile // BLOCK
    n_blocks = T // BLOCK
    full = lambda a: pl.BlockSpec(a.shape, lambda t: (0,) * a.ndim)
    return pl.pallas_call(
        functools.partial(_attn_kernel, tile=tile, geom=geom),
        grid=(T // tile,),
        in_specs=[
            pl.BlockSpec((tile, ATT_WIDTH), lambda t: (t, 0)),
            pl.BlockSpec((BLOCK, 2 * KV_WIDTH), lambda t: (jnp.maximum(t * bpt - 1, 0), 0)),
            pl.BlockSpec((tile, 2 * KV_WIDTH), lambda t: (t, 0)),
            pl.BlockSpec((BLOCK, 2 * KV_WIDTH), lambda t: (jnp.minimum((t + 1) * bpt, n_blocks - 1), 0)),
            full(kv_meta), full(bias), full(mbias), full(sink_tab),
        ],
        out_specs=pl.BlockSpec((tile, ATT_WIDTH), lambda t: (t, 0)),
        out_shape=jax.ShapeDtypeStruct((T, ATT_WIDTH), BF16),
        compiler_params=pltpu.CompilerParams(dimension_semantics=("parallel",), vmem_limit_bytes=VMEM_LIMIT),
        name="attn",
    )(q, kv, kv, kv, kv_meta, bias, mbias, sink_tab)


def _conv_kernel(zp_ref, zc_ref, zn_ref, zm_ref, w_ref, cb_ref, g_ref, b_ref, o_ref, ext_ref, sh_ref, y_ref,
                 *, tile, geom):
    i = pl.program_id(0)
    _, seq_first, seq_last = _seq_pos(i, geom)
    ext_ref[0:CONV_HALO, :] = jnp.where(seq_first, zm_ref[...], zp_ref[...]).astype(F32)
    ext_ref[CONV_HALO:CONV_HALO + tile, :] = zc_ref[...].astype(F32)
    ext_ref[CONV_HALO + tile:, :] = jnp.where(seq_last, 0.0, zn_ref[...].astype(F32))
    off = CONV_HALO - CONV_K // 2
    reach = (off + CONV_K - 1) // SUBLANES * SUBLANES
    for p in range(1, SUBLANES):
        for r0 in range(0, tile + reach, SHIFT_ROWS):
            n = min(SHIFT_ROWS, tile + reach - r0)
            sh_ref[p - 1, r0:r0 + n, :] = ext_ref[r0 + p:r0 + p + n, :]

    def taps(r0, cs):
        acc = jnp.zeros((CONV_ROWS, LANES), F32)
        for k in range(CONV_K):
            p, a = (off + k) % SUBLANES, (off + k) // SUBLANES * SUBLANES
            rows = slice(r0 + a, r0 + a + CONV_ROWS)
            win = ext_ref[rows, cs] if p == 0 else sh_ref[p - 1, rows, cs]
            acc = acc + win * w_ref[k:k + 1, cs]
        y_ref[r0:r0 + CONV_ROWS, cs] = acc

    for c in range(CONV_WIDTH // LANES):
        for r in range(tile // CONV_ROWS):
            pl.when(i >= 0)(functools.partial(taps, r * CONV_ROWS, slice(c * LANES, (c + 1) * LANES)))
    cb, g, b = cb_ref[...], g_ref[...], b_ref[...]
    for r in range(tile // LN_ROWS):
        r0 = r * LN_ROWS
        y = _layer_norm(y_ref[r0:r0 + LN_ROWS, :] + cb, g, b)
        o_ref[r0:r0 + LN_ROWS, :] = (y * jax.nn.sigmoid(y)).astype(BF16)


def _conv(zc, geom, z_meta, conv_w, conv_b, ln_g, ln_b, tile):
    T, C = zc.shape
    hpt = tile // CONV_HALO
    n_halo = T // CONV_HALO
    full = lambda a: pl.BlockSpec(a.shape, lambda t: (0,) * a.ndim)
    return pl.pallas_call(
        functools.partial(_conv_kernel, tile=tile, geom=geom),
        grid=(T // tile,),
        in_specs=[
            pl.BlockSpec((CONV_HALO, C), lambda t: (jnp.maximum(t * hpt - 1, 0), 0)),
            pl.BlockSpec((tile, C), lambda t: (t, 0)),
            pl.BlockSpec((CONV_HALO, C), lambda t: (jnp.minimum((t + 1) * hpt, n_halo - 1), 0)),
            full(z_meta), full(conv_w), full(conv_b), full(ln_g), full(ln_b),
        ],
        out_specs=pl.BlockSpec((tile, C), lambda t: (t, 0)),
        out_shape=jax.ShapeDtypeStruct((T, C), BF16),
        scratch_shapes=[pltpu.VMEM((tile + 2 * CONV_HALO, C), F32),
                        pltpu.VMEM((SUBLANES - 1, tile + 2 * CONV_HALO - SUBLANES, C), F32),
                        pltpu.VMEM((tile, C), F32)],
        compiler_params=pltpu.CompilerParams(dimension_semantics=("parallel",), vmem_limit_bytes=VMEM_LIMIT),
        name="conv",
    )(zc, zc, zc, z_meta, conv_w, conv_b, ln_g, ln_b)


def _route(r):
    lane_i = lax.broadcasted_iota(jnp.int32, r.shape, 1)
    lane = lane_i.astype(F32)
    big = float(1 << 20)
    is_g = jnp.logical_and(lane_i >= N_EXPERTS, lane_i < N_EXPERTS + N_GROUPS)
    lg = jnp.where(is_g, r, -jnp.inf)
    mg = jnp.max(lg, axis=1, keepdims=True)
    g_w = 1.0 / jnp.sum(jnp.exp(lg - mg), axis=1, keepdims=True)
    g_idx = jnp.min(jnp.where(lg == mg, lane - N_EXPERTS, big), axis=1, keepdims=True)
    lane_group = jnp.right_shift(lane_i, EXPERTS_PER_GROUP.bit_length() - 1).astype(F32)
    in_group = jnp.logical_and(lane_i < N_EXPERTS, lane_group == g_idx)
    le = jnp.where(in_group, r, -jnp.inf)
    m1 = jnp.max(le, axis=1, keepdims=True)
    den = jnp.sum(jnp.exp(le - m1), axis=1, keepdims=True)
    i1 = jnp.min(jnp.where(le == m1, lane, big), axis=1, keepdims=True)
    le2 = jnp.where(lane == i1, -jnp.inf, le)
    m2 = jnp.max(le2, axis=1, keepdims=True)
    i2 = jnp.min(jnp.where(le2 == m2, lane, big), axis=1, keepdims=True)
    p1 = 1.0 / den
    p2 = jnp.exp(m2 - m1) / den
    tot = p1 + p2
    return i1, i2, g_w * (p1 / tot), g_w * (p2 / tot)


def _pack_bf16_pairs(x):
    half = x.shape[1] // 2
    words = []
    for j in range(half // LANES):
        lo = pltpu.bitcast(x[:, j * LANES:(j + 1) * LANES].astype(BF16).astype(F32), jnp.uint32)
        hi = pltpu.bitcast(x[:, half + j * LANES:half + (j + 1) * LANES].astype(BF16).astype(F32), jnp.uint32)
        words.append(hi | (lo >> 16))
    return words


def _unpack_bf16_pairs(words):
    lo = [pltpu.bitcast(w << 16, F32) for w in words]
    hi = [pltpu.bitcast(w & jnp.uint32(0xFFFF0000), F32) for w in words]
    return jnp.concatenate(lo + hi, axis=1)


def _out_kernel(h0_ref, att_ref, cz_ref, wg_ref, watt_ref, wco_ref, wout_ref,
                l1g_ref, l1b_ref, wr_ref, br_ref, before_ref, h1_ref, h1p_ref, rt_ref, fld_ref, cnt_ref, *, tile):
    @pl.when(pl.program_id(0) == 0)
    def _():
        cnt_ref[...] = jnp.zeros_like(cnt_ref)

    h0 = h0_ref[...]
    hb = h0.astype(BF16)
    g_att = jax.nn.sigmoid(_dot(hb, wg_ref[:, :D_MODEL]))
    mix = g_att * _dot(att_ref[...], watt_ref[...])
    g_conv = jax.nn.sigmoid(_dot(hb, wg_ref[:, D_MODEL:]))
    mix = mix + g_conv * _dot(cz_ref[...], wco_ref[...])
    m = _dot(mix.astype(BF16), wout_ref[...])
    h1 = _layer_norm(ALPHA * h0 + m, l1g_ref[...], l1b_ref[...])
    h1_ref[...] = h1
    for j, w in enumerate(_pack_bf16_pairs(h1)):
        h1p_ref[:, j, :, :] = w.reshape(tile // SUBLANES, SUBLANES, LANES)

    r = _dot(h1.astype(BF16), wr_ref[...]) + br_ref[...]
    i1, i2, w1, w2 = _route(r)
    lane = lax.broadcasted_iota(jnp.int32, (tile, ROUTER_LANES), 1)
    lane_f = lane.astype(F32)
    hit1, hit2 = lane_f == i1, lane_f == i2
    onehot = jnp.where(jnp.logical_or(hit1, hit2), 1.0, 0.0)
    seen = _dot(before_ref[...], onehot.astype(BF16)) + cnt_ref[0:1, :]
    rank1 = jnp.sum(jnp.where(hit1, seen, 0.0), axis=1, keepdims=True)
    rank2 = jnp.sum(jnp.where(hit2, seen, 0.0), axis=1, keepdims=True)
    cnt_ref[...] = cnt_ref[...] + jnp.sum(onehot, axis=0, keepdims=True)
    fields = (i1, i2, w1, w2, rank1, rank2)
    rt = jnp.zeros((tile, ROUTER_LANES), F32)
    for k, v in enumerate(fields):
        rt = jnp.where(lane == k, v, rt)
    rt_ref[...] = rt
    fld_ref[...] = jnp.transpose(rt)[0:ROUTE_FIELDS, :]


def _out(h0, att, cz, wg, watt, wco, wout, l1g, l1b, wr, br, tile, first_tile, n_tiles):
    T, D = n_tiles * tile, D_MODEL
    own = lambda w: pl.BlockSpec((tile, w), lambda i: (i, 0))
    flat = lambda w: pl.BlockSpec((tile, w), lambda i: (first_tile + i, 0))
    full = lambda a: pl.BlockSpec(a.shape, lambda i: (0,) * a.ndim, pipeline_mode=pl.Buffered(1))
    before = (jnp.arange(tile)[None, :] < jnp.arange(tile)[:, None]).astype(BF16)
    return pl.pallas_call(
        functools.partial(_out_kernel, tile=tile),
        grid=(T // tile,),
        in_specs=[flat(D), flat(ATT_WIDTH), flat(CONV_WIDTH), full(wg),
                  full(watt), full(wco), full(wout), full(l1g), full(l1b), full(wr), full(br), full(before)],
        out_specs=[own(D),
                   pl.BlockSpec((tile // SUBLANES, PACK_WORDS, SUBLANES, LANES), lambda i: (i, 0, 0, 0)),
                   own(ROUTER_LANES),
                   pl.BlockSpec((ROUTE_FIELDS, tile), lambda i: (0, i)),
                   pl.BlockSpec((SUBLANES, ROUTER_LANES), lambda i: (0, 0))],
        out_shape=[jax.ShapeDtypeStruct((T, D), F32),
                   jax.ShapeDtypeStruct((T // SUBLANES, PACK_WORDS, SUBLANES, LANES), jnp.uint32),
                   jax.ShapeDtypeStruct((T, ROUTER_LANES), F32),
                   jax.ShapeDtypeStruct((ROUTE_FIELDS, T), F32),
                   jax.ShapeDtypeStruct((SUBLANES, ROUTER_LANES), F32)],
        compiler_params=pltpu.CompilerParams(dimension_semantics=("arbitrary",), vmem_limit_bytes=VMEM_LIMIT),
        name="out",
    )(h0, att, cz, wg, watt, wco, wout, l1g, l1b, wr, br, before)


def _sc_scatter2(src, idx_a, idx_b, n_out):
    m = src.shape[0]
    mesh = plsc.VectorSubcoreMesh(core_axis_name="c", subcore_axis_name="s")

    @functools.partial(pl.kernel, out_type=jax.ShapeDtypeStruct((n_out, LANES), src.dtype), mesh=mesh)
    def k(x_hbm, ia_hbm, ib_hbm, o_hbm):
        def body(x_vmem, ia_vmem, ib_vmem):
            pltpu.sync_copy(x_vmem, o_hbm.at[ia_vmem.at[0]])
            pltpu.sync_copy(x_vmem, o_hbm.at[ib_vmem.at[0]])

        pltpu.emit_pipeline(
            body, grid=(m // SC_WINDOW,),
            in_specs=[pl.BlockSpec((SC_WINDOW, LANES), index_map=lambda i: (i, 0)),
                      pl.BlockSpec((1, SC_WINDOW), index_map=lambda i: (i, 0)),
                      pl.BlockSpec((1, SC_WINDOW), index_map=lambda i: (i, 0))],
            out_specs=[],
            core_axis_name=("c", "s"), dimension_semantics=(pltpu.PARALLEL,),
        )(x_hbm, ia_hbm, ib_hbm)

    return k(src, idx_a, idx_b)


def _sc_gather2(table, idx_a, idx_b):
    windows = idx_a.shape[0]
    m = windows * SC_WINDOW
    mesh = plsc.VectorSubcoreMesh(core_axis_name="c", subcore_axis_name="s")

    @functools.partial(pl.kernel, out_type=jax.ShapeDtypeStruct((2 * m, LANES), table.dtype), mesh=mesh)
    def k(x_hbm, ia_hbm, ib_hbm, o_hbm):
        def body(i_vmem, o_vmem):
            pltpu.sync_copy(x_hbm.at[i_vmem.at[0]], o_vmem)

        for half, i_hbm in enumerate((ia_hbm, ib_hbm)):
            pltpu.emit_pipeline(
                body, grid=(windows,),
                in_specs=[pl.BlockSpec((1, SC_WINDOW), index_map=lambda i: (i, 0))],
                out_specs=[pl.BlockSpec((SC_WINDOW, LANES), index_map=lambda i, half=half: (half * windows + i, 0))],
                core_axis_name=("c", "s"), dimension_semantics=(pltpu.PARALLEL,),
            )(i_hbm, o_hbm)

    return k(table, idx_a, idx_b).reshape(2, m, LANES)


def _expert_kernel(te_ref, tv_ref, used_ref, xs_hbm, wg_ref, wu_ref, wd_ref, ys_ref, wgb_ref, wub_ref, wdb_ref,
                   xbuf_ref, sem_ref, *, tile):
    n = pl.program_id(0)
    valid = tv_ref[n]
    used = used_ref[0]

    def row_tile_copy(step):
        slot = lax.rem(step, XS_SLOTS)
        return pltpu.make_async_copy(xs_hbm.at[pl.ds(step * (tile // SUBLANES), tile // SUBLANES)], xbuf_ref.at[slot], sem_ref.at[slot])

    @pl.when(n == 0)
    def _():
        row_tile_copy(0).start()

        @pl.when(used > 1)
        def _():
            row_tile_copy(1).start()

    @pl.when(n + 2 < used)
    def _():
        row_tile_copy(n + 2).start()

    @pl.when(jnp.logical_or(n == 0, te_ref[n] != te_ref[jnp.maximum(n - 1, 0)]))
    def _():
        wgb_ref[...] = wg_ref[...].astype(BF16)
        wub_ref[...] = wu_ref[...].astype(BF16)
        wdb_ref[...] = wd_ref[...].astype(BF16)

    @pl.when(n < used)
    def _():
        row_tile_copy(n).wait()
        slot = lax.rem(n, XS_SLOTS)
        x = _unpack_bf16_pairs([xbuf_ref[slot, :, j, :, :].reshape(tile, LANES) for j in range(PACK_WORDS)])
        rows = lax.broadcasted_iota(jnp.int32, (tile, 1), 0)
        x = jnp.where(rows < valid, x, 0.0).astype(BF16)
        gt = _dot(x, wgb_ref[...])
        up = _dot(x, wub_ref[...])
        hid = (gt * jax.nn.sigmoid(gt)) * up
        y = _dot(hid.astype(BF16), wdb_ref[...])
        for j, w in enumerate(_pack_bf16_pairs(y)):
            ys_ref[:, j, :, :] = w.reshape(tile // SUBLANES, SUBLANES, LANES)


def _experts(tile_expert, tile_valid, tiles_used, xs, w_gate, w_up, w_down, tile):
    n_tiles = xs.shape[0] * SUBLANES // tile
    blk = pl.BlockSpec((tile // SUBLANES, PACK_WORDS, SUBLANES, LANES),
                       lambda n, te, tv, used: (jnp.minimum(n, jnp.maximum(used[0] - 1, 0)), 0, 0, 0))
    return pl.pallas_call(
        functools.partial(_expert_kernel, tile=tile),
        grid_spec=pltpu.PrefetchScalarGridSpec(
            num_scalar_prefetch=3,
            grid=(n_tiles,),
            in_specs=[pl.BlockSpec(memory_space=pl.ANY),
                      pl.BlockSpec((None, D_MODEL, D_EXPERT), lambda n, te, tv, used: (te[n], 0, 0)),
                      pl.BlockSpec((None, D_MODEL, D_EXPERT), lambda n, te, tv, used: (te[n], 0, 0)),
                      pl.BlockSpec((None, D_EXPERT, D_MODEL), lambda n, te, tv, used: (te[n], 0, 0))],
            out_specs=blk,
            scratch_shapes=[pltpu.VMEM((D_MODEL, D_EXPERT), BF16), pltpu.VMEM((D_MODEL, D_EXPERT), BF16),
                            pltpu.VMEM((D_EXPERT, D_MODEL), BF16),
                            pltpu.VMEM((XS_SLOTS, tile // SUBLANES, PACK_WORDS, SUBLANES, LANES), jnp.uint32),
                            pltpu.SemaphoreType.DMA((XS_SLOTS,))],
        ),
        out_shape=jax.ShapeDtypeStruct(xs.shape, jnp.uint32),
        compiler_params=pltpu.CompilerParams(
            dimension_semantics=("arbitrary",), vmem_limit_bytes=VMEM_LIMIT),
        name="experts",
    )(tile_expert, tile_valid, tiles_used, xs, w_gate, w_up, w_down)


def _final_kernel(h_ref, g_ref, rt_ref, l2g_ref, l2b_ref, o_ref, *, tile):
    rt = rt_ref[...]
    lane = lax.broadcasted_iota(jnp.int32, rt.shape, 1)
    w1 = jnp.sum(jnp.where(lane == 2, rt, 0.0), axis=1, keepdims=True)
    w2 = jnp.sum(jnp.where(lane == 3, rt, 0.0), axis=1, keepdims=True)
    y1 = _unpack_bf16_pairs([g_ref[0, :, j, :, :].reshape(tile, LANES) for j in range(PACK_WORDS)])
    y2 = _unpack_bf16_pairs([g_ref[1, :, j, :, :].reshape(tile, LANES) for j in range(PACK_WORDS)])
    f = w1 * y1 + w2 * y2
    o_ref[...] = _layer_norm(ALPHA * h_ref[...] + f, l2g_ref[...], l2b_ref[...])


def _final(h1, g, rt, l2g, l2b, tile):
    T, D = h1.shape
    row = lambda w: pl.BlockSpec((tile, w), lambda i: (i, 0))
    full = lambda a: pl.BlockSpec(a.shape, lambda i: (0,) * a.ndim)
    return pl.pallas_call(
        functools.partial(_final_kernel, tile=tile),
        grid=(T // tile,),
        in_specs=[row(D),
                  pl.BlockSpec((2, tile // SUBLANES, PACK_WORDS, SUBLANES, LANES), lambda i: (0, i, 0, 0, 0)),
                  row(ROUTER_LANES), full(l2g), full(l2b)],
        out_specs=row(D),
        out_shape=jax.ShapeDtypeStruct((T, D), F32),
        compiler_params=pltpu.CompilerParams(dimension_semantics=("parallel",), vmem_limit_bytes=VMEM_LIMIT),
        name="final",
    )(h1, g, rt, l2g, l2b)


def _dispatch_plan(fields, counts, n_tokens, tile):
    cnt = counts[0, :N_EXPERTS].astype(jnp.int32)
    padded = (cnt + tile - 1) // tile * tile
    base = jnp.cumsum(padded) - padded
    e_ids = jnp.arange(N_EXPERTS, dtype=jnp.int32)

    def dest(row_e, row_r):
        e = fields[row_e].astype(jnp.int32)
        seg = jnp.sum(jnp.where(e[None, :] == e_ids[:, None], base[:, None], 0), axis=0)
        pos = seg + fields[row_r].astype(jnp.int32)
        first = (pos // SUBLANES) * (SUBLANES * PACK_WORDS) + pos % SUBLANES
        per_window = SC_WINDOW // PACK_WORDS
        first = first.reshape(n_tokens // LANES, LANES)
        windows = []
        for c in range(LANES // per_window):
            pieces = [first[:, c * per_window + SUBLANES * a:c * per_window + SUBLANES * (a + 1)] + SUBLANES * j
                      for a in range(per_window // SUBLANES) for j in range(PACK_WORDS)]
            windows.append(jnp.concatenate(pieces, axis=1))
        return jnp.stack(windows, axis=1).reshape(n_tokens * PACK_WORDS // SC_WINDOW, SC_WINDOW)

    n_tiles = (2 * n_tokens) // tile + N_EXPERTS
    start = jnp.arange(n_tiles, dtype=jnp.int32) * tile
    seg_end = base + padded
    te = jnp.minimum(jnp.sum((start[:, None] >= seg_end[None, :]).astype(jnp.int32), axis=1), N_EXPERTS - 1)
    te_base = jnp.sum(jnp.where(te[:, None] == e_ids[None, :], base[None, :], 0), axis=1)
    te_cnt = jnp.sum(jnp.where(te[:, None] == e_ids[None, :], cnt[None, :], 0), axis=1)
    tv = jnp.clip(te_cnt - (start - te_base), 0, tile)
    tiles_used = (jnp.sum(padded) // tile).reshape(1)
    return dest(0, 4), dest(1, 5), te, tv, tiles_used, n_tiles


def _t5_bucket(rel):
    half = N_BUCKETS // 2
    max_exact = half // 2
    ret = jnp.where(rel > 0, half, 0)
    n = jnp.abs(rel)
    nf = jnp.maximum(n, 1).astype(F32)
    large = max_exact + (jnp.log(nf / max_exact) / math.log(MAX_DISTANCE / max_exact)
                         * (half - max_exact)).astype(jnp.int32)
    large = jnp.minimum(large, half - 1)
    return ret + jnp.where(n < max_exact, n, large)


def _bucket_bias(rel_bias, bucket):
    rb = rel_bias.astype(F32)
    out = jnp.zeros((N_Q_HEADS,) + bucket.shape, F32)
    for b in range(N_BUCKETS):
        out = out + jnp.where(bucket[None] == b, rb[b][:, None, None], 0.0)
    return out


def _pair_rows(t):
    return jnp.concatenate([t[:N_PAIRS], t[N_PAIRS:]], axis=-1).reshape(N_PAIRS * BLOCK, -1)


def _bias_tables(rel_bias, sink):
    qi = jnp.arange(BLOCK)
    kj = jnp.arange(3 * BLOCK) - BLOCK
    rel = kj[None, :] - qi[:, None]
    band = _bucket_bias(rel_bias, _t5_bucket(rel))
    in_win = (jnp.abs(rel) <= WINDOW)[None]
    not_prev = (kj >= 0)[None, None, :]
    not_next = (kj < BLOCK)[None, None, :]
    variants = [jnp.where(in_win & not_prev, band, NEG),
                jnp.where(in_win, band, NEG),
                jnp.where(in_win & not_next, band, NEG)]
    bias = jnp.stack([_pair_rows(v) for v in variants])

    off = BLOCK - N_META
    mvars = []
    for blk in (1, 2):
        qpos = blk * BLOCK + qi - off
        meta_rel = jnp.arange(N_META)[None, :] - qpos[:, None]
        mvars.append(_pair_rows(_bucket_bias(rel_bias, _t5_bucket(meta_rel))))
    mbias = jnp.stack(mvars)
    s = sink.astype(F32)
    sink_tab = jnp.repeat(jnp.stack([s[:N_PAIRS], s[N_PAIRS:]], axis=-1), BLOCK, axis=0)
    return bias, mbias, sink_tab


def kernel(x_prompt, x_sample, meta, ln_in_g, ln_in_b, rel_bias, w_in, w_att_branch, sink, conv_w, conv_b,
           conv_ln_g, conv_ln_b, w_conv_out, w_out, ln1_g, ln1_b, w_group, b_group, w_router, b_router,
           w_gate, w_up, w_down, ln2_g, ln2_b):
    row = lambda v: v.reshape(1, -1).astype(F32)
    w = w_in[0]
    wq = (w[:, :Q_END].reshape(D_MODEL, 2, N_PAIRS, HEAD_DIM).transpose(0, 2, 1, 3)
          .reshape(D_MODEL, ATT_WIDTH).astype(BF16))
    watt = (w_att_branch[0].reshape(2, N_PAIRS, HEAD_DIM, D_MODEL).transpose(1, 0, 2, 3)
            .reshape(ATT_WIDTH, D_MODEL).astype(BF16))
    wkv = w[:, Q_END:V_END].astype(BF16)
    wglu = w[:, V_END:GLU_END].astype(BF16)
    wg = w[:, GLU_END:].astype(BF16)
    wco = w_conv_out[0].astype(BF16)
    wout = w_out[0].astype(BF16)
    wr = jnp.zeros((D_MODEL, ROUTER_LANES), F32)
    wr = wr.at[:, :N_EXPERTS].set(w_router[0]).at[:, N_EXPERTS:N_EXPERTS + N_GROUPS].set(w_group[0]).astype(BF16)
    br = jnp.zeros((1, ROUTER_LANES), F32)
    br = br.at[0, :N_EXPERTS].set(b_router[0]).at[0, N_EXPERTS:N_EXPERTS + N_GROUPS].set(b_group[0])
    ln_g, ln_b = row(ln_in_g), row(ln_in_b)
    bias, mbias, sink_tab = _bias_tables(rel_bias, sink[0])

    xm = jnp.concatenate([jnp.zeros((BLOCK - N_META, D_MODEL), F32), meta.astype(F32)], axis=0)
    _, _, kv_m, zc_m = _proj(xm, xm, _Geom(1, 1, 1, 1), ln_g, ln_b, wq, wkv, wglu, BLOCK)
    kv_meta = kv_m[BLOCK - N_META:BLOCK]
    z_meta = zc_m[BLOCK - N_META:BLOCK]

    (bp, sp, _), (bs, ss, _) = x_prompt.shape, x_sample.shape
    xp, xs = x_prompt.reshape(bp * sp, D_MODEL), x_sample.reshape(bs * ss, D_MODEL)
    geom = _geom(x_prompt, x_sample, TILE)
    h0, q, kv, zc = _proj(xp, xs, _geom(x_prompt, x_sample, TILE_PROJ), ln_g, ln_b, wq, wkv, wglu, TILE_PROJ)
    att = _attn(q, kv, _geom(x_prompt, x_sample, TILE_ATTN), kv_meta, bias, mbias, sink_tab, TILE_ATTN)
    cz = _conv(zc, geom, z_meta, conv_w[0], row(conv_b[0]), row(conv_ln_g[0]), row(conv_ln_b[0]), TILE)
    l1g, l1b, l2g, l2b = row(ln1_g[0]), row(ln1_b[0]), row(ln2_g[0]), row(ln2_b[0])

    def moe(first_tile, n_tiles):
        n = n_tiles * TILE
        h1, h1p, rt, fields, counts = _out(h0, att, cz, wg, watt, wco, wout, l1g, l1b, wr, br,
                                           TILE_OUT, first_tile * TILE // TILE_OUT, n_tiles * TILE // TILE_OUT)
        idx1, idx2, tile_expert, tile_valid, tiles_used, n_tiles = _dispatch_plan(fields, counts, n, TILE_EXPERT)
        n_rows = n_tiles * TILE_EXPERT
        xsorted = _sc_scatter2(h1p.reshape(n * PACK_WORDS, LANES), idx1, idx2, n_rows * PACK_WORDS)
        ys = _experts(tile_expert, tile_valid, tiles_used,
                      xsorted.reshape(n_rows // SUBLANES, PACK_WORDS, SUBLANES, LANES),
                      w_gate[0], w_up[0], w_down[0], TILE_EXPERT)
        g = _sc_gather2(ys.reshape(n_rows * PACK_WORDS, LANES), idx1, idx2)
        return _final(h1, g.reshape(2, n // SUBLANES, PACK_WORDS, SUBLANES, LANES), rt, l2g, l2b, TILE_FINAL)

    y_sample = moe(geom.n_p, geom.n_s).reshape(x_sample.shape)
    y_prompt = moe(0, geom.n_p).reshape(x_prompt.shape)
    return y_prompt, y_sample
```

```python
import functools
import math
from typing import NamedTuple

import jax
import jax.numpy as jnp
from jax import lax
from jax.experimental import pallas as pl
from jax.experimental.pallas import tpu as pltpu
from jax.experimental.pallas import tpu_sc as plsc

D_MODEL = 1024
N_META = 16
BLOCK = 128
WINDOW = 128
N_Q_HEADS = 8
N_KV_HEADS = 2
HEAD_DIM = 64
ATT_WIDTH = N_Q_HEADS * HEAD_DIM
KV_WIDTH = N_KV_HEADS * HEAD_DIM
CONV_WIDTH = D_MODEL // 2
CONV_K = 31
N_BUCKETS = 32
MAX_DISTANCE = 128
N_GROUPS = 4
EXPERTS_PER_GROUP = 8
N_EXPERTS = N_GROUPS * EXPERTS_PER_GROUP
D_EXPERT = 256
LN_EPS = 1e-5
DEPTH = 1
ALPHA = (2 * DEPTH) ** 0.25
NEG = -1e30
Q_END = ATT_WIDTH
K_END = Q_END + KV_WIDTH
V_END = K_END + KV_WIDTH
GLU_END = V_END + 2 * CONV_WIDTH

N_PAIRS = N_Q_HEADS // 2
LANES = 128
SUBLANES = 8
CONV_HALO = 16
ROUTER_LANES = 128

TILE = 512
CONV_ROWS = 128
LN_ROWS = 64
SHIFT_ROWS = 128
TILE_PROJ = 1024
TILE_OUT = 1024
XS_SLOTS = 3
TILE_FINAL = 1024
TILE_ATTN = 2048
TILE_EXPERT = 512
PACK_WORDS = 4
SC_WINDOW = 128
ROUTE_FIELDS = 8
VMEM_LIMIT = 56 * 1024 * 1024

BF16 = jnp.bfloat16
F32 = jnp.float32


def _layer_norm(x, g, b):
    mu = jnp.mean(x, axis=-1, keepdims=True)
    xc = x - mu
    var = jnp.mean(xc * xc, axis=-1, keepdims=True)
    return xc * lax.rsqrt(var + LN_EPS) * g + b


def _dot(a, b):
    return jnp.dot(a, b, preferred_element_type=F32)


def _dot_nt(a, b):
    return lax.dot_general(a, b, (((1,), (1,)), ((), ())), preferred_element_type=F32)


class _Geom(NamedTuple):
    n_p: int
    n_s: int
    tp: int
    ts: int


def _geom(x_prompt, x_sample, tile):
    (bp, sp, _), (bs, ss, _) = x_prompt.shape, x_sample.shape
    return _Geom(bp * sp // tile, bs * ss // tile, sp // tile, ss // tile)


def _seq_pos(t, g):
    is_p = t < g.n_p
    local = jnp.where(is_p, lax.rem(t, g.tp), lax.rem(jnp.maximum(t - g.n_p, 0), g.ts))
    return is_p, local == 0, local == jnp.where(is_p, g.tp - 1, g.ts - 1)


def _x_specs(g, tile, width):
    return [pl.BlockSpec((tile, width), lambda t: (jnp.minimum(t, g.n_p - 1), 0)),
            pl.BlockSpec((tile, width), lambda t: (jnp.maximum(t - g.n_p, 0), 0))]


def _proj_kernel(xp_ref, xs_ref, g_ref, b_ref, wq_ref, wkv_ref, wglu_ref, h0_ref, q_ref, kv_ref, zc_ref, *, geom):
    x = jnp.where(pl.program_id(0) < geom.n_p, xp_ref[...], xs_ref[...])
    h0 = _layer_norm(x, g_ref[...], b_ref[...])
    h0_ref[...] = h0
    h = h0.astype(BF16)
    q_ref[...] = _dot(h, wq_ref[...]).astype(BF16)
    kv_ref[...] = _dot(h, wkv_ref[...]).astype(BF16)
    u = _dot(h, wglu_ref[...])
    zc_ref[...] = (u[:, :CONV_WIDTH] * jax.nn.sigmoid(u[:, CONV_WIDTH:])).astype(BF16)


def _proj(xp, xs, geom, ln_g, ln_b, wq, wkv, wglu, tile):
    T = (geom.n_p + geom.n_s) * tile
    row = lambda w: pl.BlockSpec((tile, w), lambda t: (t, 0))
    full = lambda a: pl.BlockSpec(a.shape, lambda t: (0,) * a.ndim)
    return pl.pallas_call(
        functools.partial(_proj_kernel, geom=geom),
        grid=(geom.n_p + geom.n_s,),
        in_specs=_x_specs(geom, tile, D_MODEL) + [full(ln_g), full(ln_b), full(wq), full(wkv), full(wglu)],
        out_specs=[row(D_MODEL), row(ATT_WIDTH), row(2 * KV_WIDTH), row(CONV_WIDTH)],
        out_shape=[
            jax.ShapeDtypeStruct((T, D_MODEL), F32),
            jax.ShapeDtypeStruct((T, ATT_WIDTH), BF16),
            jax.ShapeDtypeStruct((T, 2 * KV_WIDTH), BF16),
            jax.ShapeDtypeStruct((T, CONV_WIDTH), BF16),
        ],
        compiler_params=pltpu.CompilerParams(dimension_semantics=("parallel",), vmem_limit_bytes=VMEM_LIMIT),
        name="proj",
    )(xp, xs, ln_g, ln_b, wq, wkv, wglu)


def _attn_kernel(q_ref, kvp_ref, kvc_ref, kvn_ref, kvm_ref, bias_ref, mbias_ref, sink_ref, o_ref, *, tile, geom):
    _, seq_first, seq_last = _seq_pos(pl.program_id(0), geom)
    blocks = tile // BLOCK
    scale = HEAD_DIM ** -0.5

    lane = lax.broadcasted_iota(jnp.int32, (1, LANES), 1)
    lo = lane < HEAD_DIM

    def split_heads(t):
        z = jnp.zeros_like(t)
        return jnp.where(lo, t, z), jnp.where(lo, z, t)

    kv_ext = jnp.concatenate([kvp_ref[...], kvc_ref[...], kvn_ref[...]], axis=0)
    k_ext = kv_ext[:, :KV_WIDTH] * jnp.asarray(scale, BF16)
    v_ext = kv_ext[:, KV_WIDTH:]
    ka, kb = split_heads(k_ext)
    va, vb = split_heads(v_ext)
    kma, kmb = split_heads(kvm_ref[:, :KV_WIDTH] * jnp.asarray(scale, BF16))
    vma, vmb = split_heads(kvm_ref[:, KV_WIDTH:])
    km_cat = jnp.concatenate([kma, kmb], axis=0)
    vm_cat = jnp.concatenate([vma, vmb], axis=0)

    mlane = lax.broadcasted_iota(jnp.int32, (1, 2 * N_META), 1)
    m_first = mlane < N_META
    sink = sink_ref[...]
    lane_o = lax.broadcasted_iota(jnp.int32, (1, LANES), 1) < HEAD_DIM

    for j in range(blocks):
        first = jnp.logical_and(seq_first, j == 0)
        last = jnp.logical_and(seq_last, j == blocks - 1)
        variant = jnp.where(first, 0, jnp.where(last, 2, 1))
        mvariant = jnp.where(first, 0, 1)

        r0 = j * BLOCK
        qb = q_ref[r0:r0 + BLOCK, :]
        q4 = jnp.concatenate([qb[:, p * LANES:(p + 1) * LANES] for p in range(N_PAIRS)], axis=0)
        k_cat = jnp.concatenate([ka[r0:r0 + 3 * BLOCK], kb[r0:r0 + 3 * BLOCK]], axis=0)
        v_cat = jnp.concatenate([va[r0:r0 + 3 * BLOCK], vb[r0:r0 + 3 * BLOCK]], axis=0)

        s = _dot_nt(q4, k_cat) + bias_ref[variant]
        sm = _dot_nt(q4, km_cat) + mbias_ref[mvariant]

        s_a, s_b = s[:, :3 * BLOCK], s[:, 3 * BLOCK:]
        sm_a = jnp.where(m_first, sm, NEG)
        sm_b = jnp.where(m_first, NEG, sm)
        m_a = jnp.maximum(jnp.maximum(jnp.max(s_a, axis=1, keepdims=True),
                                      jnp.max(sm_a, axis=1, keepdims=True)), sink[:, 0:1])
        m_b = jnp.maximum(jnp.maximum(jnp.max(s_b, axis=1, keepdims=True),
                                      jnp.max(sm_b, axis=1, keepdims=True)), sink[:, 1:2])
        p_a = jnp.exp(s_a - m_a)
        p_b = jnp.exp(s_b - m_b)
        pm = jnp.exp(jnp.where(m_first, sm - m_a, sm - m_b))
        l_a = (jnp.sum(p_a, axis=1, keepdims=True) + jnp.sum(jnp.where(m_first, pm, 0.0), axis=1, keepdims=True)
               + jnp.exp(sink[:, 0:1] - m_a))
        l_b = (jnp.sum(p_b, axis=1, keepdims=True) + jnp.sum(jnp.where(m_first, 0.0, pm), axis=1, keepdims=True)
               + jnp.exp(sink[:, 1:2] - m_b))
        p = jnp.concatenate([p_a, p_b], axis=1).astype(BF16)
        o = _dot(p, v_cat) + _dot(pm.astype(BF16), vm_cat)
        o = o * jnp.where(lane_o, 1.0 / l_a, 1.0 / l_b)
        for pr in range(N_PAIRS):
            o_ref[r0:r0 + BLOCK, pr * LANES:(pr + 1) * LANES] = o[pr * BLOCK:(pr + 1) * BLOCK].astype(BF16)


def _attn(q, kv, geom, kv_meta, bias, mbias, sink_tab, tile):
    T = q.shape[0]
    bpt = tile // BLOCK
    n_blocks = T // BLOCK
    full = lambda a: pl.BlockSpec(a.shape, lambda t: (0,) * a.ndim)
    return pl.pallas_call(
        functools.partial(_attn_kernel, tile=tile, geom=geom),
        grid=(T // tile,),
        in_specs=[
            pl.BlockSpec((tile, ATT_WIDTH), lambda t: (t, 0)),
            pl.BlockSpec((BLOCK, 2 * KV_WIDTH), lambda t: (jnp.maximum(t * bpt - 1, 0), 0)),
            pl.BlockSpec((tile, 2 * KV_WIDTH), lambda t: (t, 0)),
            pl.BlockSpec((BLOCK, 2 * KV_WIDTH), lambda t: (jnp.minimum((t + 1) * bpt, n_blocks - 1), 0)),
            full(kv_meta), full(bias), full(mbias), full(sink_tab),
        ],
        out_specs=pl.BlockSpec((tile, ATT_WIDTH), lambda t: (t, 0)),
        out_shape=jax.ShapeDtypeStruct((T, ATT_WIDTH), BF16),
        compiler_params=pltpu.CompilerParams(dimension_semantics=("parallel",), vmem_limit_bytes=VMEM_LIMIT),
        name="attn",
    )(q, kv, kv, kv, kv_meta, bias, mbias, sink_tab)


def _conv_kernel(zp_ref, zc_ref, zn_ref, zm_ref, w_ref, cb_ref, g_ref, b_ref, o_ref, ext_ref, sh_ref, y_ref,
                 *, tile, geom):
    i = pl.program_id(0)
    _, seq_first, seq_last = _seq_pos(i, geom)
    ext_ref[0:CONV_HALO, :] = jnp.where(seq_first, zm_ref[...], zp_ref[...]).astype(F32)
    ext_ref[CONV_HALO:CONV_HALO + tile, :] = zc_ref[...].astype(F32)
    ext_ref[CONV_HALO + tile:, :] = jnp.where(seq_last, 0.0, zn_ref[...].astype(F32))
    off = CONV_HALO - CONV_K // 2
    reach = (off + CONV_K - 1) // SUBLANES * SUBLANES
    for p in range(1, SUBLANES):
        for r0 in range(0, tile + reach, SHIFT_ROWS):
            n = min(SHIFT_ROWS, tile + reach - r0)
            sh_ref[p - 1, r0:r0 + n, :] = ext_ref[r0 + p:r0 + p + n, :]

    def taps(r0, cs):
        acc = jnp.zeros((CONV_ROWS, LANES), F32)
        for k in range(CONV_K):
            p, a = (off + k) % SUBLANES, (off + k) // SUBLANES * SUBLANES
            rows = slice(r0 + a, r0 + a + CONV_ROWS)
            win = ext_ref[rows, cs] if p == 0 else sh_ref[p - 1, rows, cs]
            acc = acc + win * w_ref[k:k + 1, cs]
        y_ref[r0:r0 + CONV_ROWS, cs] = acc

    for c in range(CONV_WIDTH // LANES):
        for r in range(tile // CONV_ROWS):
            pl.when(i >= 0)(functools.partial(taps, r * CONV_ROWS, slice(c * LANES, (c + 1) * LANES)))
    cb, g, b = cb_ref[...], g_ref[...], b_ref[...]
    for r in range(tile // LN_ROWS):
        r0 = r * LN_ROWS
        y = _layer_norm(y_ref[r0:r0 + LN_ROWS, :] + cb, g, b)
        o_ref[r0:r0 + LN_ROWS, :] = (y * jax.nn.sigmoid(y)).astype(BF16)


def _conv(zc, geom, z_meta, conv_w, conv_b, ln_g, ln_b, tile):
    T, C = zc.shape
    hpt = tile // CONV_HALO
    n_halo = T // CONV_HALO
    full = lambda a: pl.BlockSpec(a.shape, lambda t: (0,) * a.ndim)
    return pl.pallas_call(
        functools.partial(_conv_kernel, tile=tile, geom=geom),
        grid=(T // tile,),
        in_specs=[
            pl.BlockSpec((CONV_HALO, C), lambda t: (jnp.maximum(t * hpt - 1, 0), 0)),
            pl.BlockSpec((tile, C), lambda t: (t, 0)),
            pl.BlockSpec((CONV_HALO, C), lambda t: (jnp.minimum((t + 1) * hpt, n_halo - 1), 0)),
            full(z_meta), full(conv_w), full(conv_b), full(ln_g), full(ln_b),
        ],
        out_specs=pl.BlockSpec((tile, C), lambda t: (t, 0)),
        out_shape=jax.ShapeDtypeStruct((T, C), BF16),
        scratch_shapes=[pltpu.VMEM((tile + 2 * CONV_HALO, C), F32),
                        pltpu.VMEM((SUBLANES - 1, tile + 2 * CONV_HALO - SUBLANES, C), F32),
                        pltpu.VMEM((tile, C), F32)],
        compiler_params=pltpu.CompilerParams(dimension_semantics=("parallel",), vmem_limit_bytes=VMEM_LIMIT),
        name="conv",
    )(zc, zc, zc, z_meta, conv_w, conv_b, ln_g, ln_b)


def _route(r):
    lane_i = lax.broadcasted_iota(jnp.int32, r.shape, 1)
    lane = lane_i.astype(F32)
    big = float(1 << 20)
    is_g = jnp.logical_and(lane_i >= N_EXPERTS, lane_i < N_EXPERTS + N_GROUPS)
    lg = jnp.where(is_g, r, -jnp.inf)
    mg = jnp.max(lg, axis=1, keepdims=True)
    g_w = 1.0 / jnp.sum(jnp.exp(lg - mg), axis=1, keepdims=True)
    g_idx = jnp.min(jnp.where(lg == mg, lane - N_EXPERTS, big), axis=1, keepdims=True)
    lane_group = jnp.right_shift(lane_i, EXPERTS_PER_GROUP.bit_length() - 1).astype(F32)
    in_group = jnp.logical_and(lane_i < N_EXPERTS, lane_group == g_idx)
    le = jnp.where(in_group, r, -jnp.inf)
    m1 = jnp.max(le, axis=1, keepdims=True)
    den = jnp.sum(jnp.exp(le - m1), axis=1, keepdims=True)
    i1 = jnp.min(jnp.where(le == m1, lane, big), axis=1, keepdims=True)
    le2 = jnp.where(lane == i1, -jnp.inf, le)
    m2 = jnp.max(le2, axis=1, keepdims=True)
    i2 = jnp.min(jnp.where(le2 == m2, lane, big), axis=1, keepdims=True)
    p1 = 1.0 / den
    p2 = jnp.exp(m2 - m1) / den
    tot = p1 + p2
    return i1, i2, g_w * (p1 / tot), g_w * (p2 / tot)


def _pack_bf16_pairs(x):
    half = x.shape[1] // 2
    words = []
    for j in range(half // LANES):
        lo = pltpu.bitcast(x[:, j * LANES:(j + 1) * LANES].astype(BF16).astype(F32), jnp.uint32)
        hi = pltpu.bitcast(x[:, half + j * LANES:half + (j + 1) * LANES].astype(BF16).astype(F32), jnp.uint32)
        words.append(hi | (lo >> 16))
    return words


def _unpack_bf16_pairs(words):
    lo = [pltpu.bitcast(w << 16, F32) for w in words]
    hi = [pltpu.bitcast(w & jnp.uint32(0xFFFF0000), F32) for w in words]
    return jnp.concatenate(lo + hi, axis=1)


def _out_kernel(h0_ref, att_ref, cz_ref, wg_ref, watt_ref, wco_ref, wout_ref,
                l1g_ref, l1b_ref, wr_ref, br_ref, before_ref, h1_ref, h1p_ref, rt_ref, fld_ref, cnt_ref, *, tile):
    @pl.when(pl.program_id(0) == 0)
    def _():
        cnt_ref[...] = jnp.zeros_like(cnt_ref)

    h0 = h0_ref[...]
    hb = h0.astype(BF16)
    g_att = jax.nn.sigmoid(_dot(hb, wg_ref[:, :D_MODEL]))
    mix = g_att * _dot(att_ref[...], watt_ref[...])
    g_conv = jax.nn.sigmoid(_dot(hb, wg_ref[:, D_MODEL:]))
    mix = mix + g_conv * _dot(cz_ref[...], wco_ref[...])
    m = _dot(mix.astype(BF16), wout_ref[...])
    h1 = _layer_norm(ALPHA * h0 + m, l1g_ref[...], l1b_ref[...])
    h1_ref[...] = h1
    for j, w in enumerate(_pack_bf16_pairs(h1)):
        h1p_ref[:, j, :, :] = w.reshape(tile // SUBLANES, SUBLANES, LANES)

    r = _dot(h1.astype(BF16), wr_ref[...]) + br_ref[...]
    i1, i2, w1, w2 = _route(r)
    lane = lax.broadcasted_iota(jnp.int32, (tile, ROUTER_LANES), 1)
    lane_f = lane.astype(F32)
    hit1, hit2 = lane_f == i1, lane_f == i2
    onehot = jnp.where(jnp.logical_or(hit1, hit2), 1.0, 0.0)
    seen = _dot(before_ref[...], onehot.astype(BF16)) + cnt_ref[0:1, :]
    rank1 = jnp.sum(jnp.where(hit1, seen, 0.0), axis=1, keepdims=True)
    rank2 = jnp.sum(jnp.where(hit2, seen, 0.0), axis=1, keepdims=True)
    cnt_ref[...] = cnt_ref[...] + jnp.sum(onehot, axis=0, keepdims=True)
    fields = (i1, i2, w1, w2, rank1, rank2)
    rt = jnp.zeros((tile, ROUTER_LANES), F32)
    for k, v in enumerate(fields):
        rt = jnp.where(lane == k, v, rt)
    rt_ref[...] = rt
    fld_ref[...] = jnp.transpose(rt)[0:ROUTE_FIELDS, :]


def _out(h0, att, cz, wg, watt, wco, wout, l1g, l1b, wr, br, tile, first_tile, n_tiles):
    T, D = n_tiles * tile, D_MODEL
    own = lambda w: pl.BlockSpec((tile, w), lambda i: (i, 0))
    flat = lambda w: pl.BlockSpec((tile, w), lambda i: (first_tile + i, 0))
    full = lambda a: pl.BlockSpec(a.shape, lambda i: (0,) * a.ndim, pipeline_mode=pl.Buffered(1))
    before = (jnp.arange(tile)[None, :] < jnp.arange(tile)[:, None]).astype(BF16)
    return pl.pallas_call(
        functools.partial(_out_kernel, tile=tile),
        grid=(T // tile,),
        in_specs=[flat(D), flat(ATT_WIDTH), flat(CONV_WIDTH), full(wg),
                  full(watt), full(wco), full(wout), full(l1g), full(l1b), full(wr), full(br), full(before)],
        out_specs=[own(D),
                   pl.BlockSpec((tile // SUBLANES, PACK_WORDS, SUBLANES, LANES), lambda i: (i, 0, 0, 0)),
                   own(ROUTER_LANES),
                   pl.BlockSpec((ROUTE_FIELDS, tile), lambda i: (0, i)),
                   pl.BlockSpec((SUBLANES, ROUTER_LANES), lambda i: (0, 0))],
        out_shape=[jax.ShapeDtypeStruct((T, D), F32),
                   jax.ShapeDtypeStruct((T // SUBLANES, PACK_WORDS, SUBLANES, LANES), jnp.uint32),
                   jax.ShapeDtypeStruct((T, ROUTER_LANES), F32),
                   jax.ShapeDtypeStruct((ROUTE_FIELDS, T), F32),
                   jax.ShapeDtypeStruct((SUBLANES, ROUTER_LANES), F32)],
        compiler_params=pltpu.CompilerParams(dimension_semantics=("arbitrary",), vmem_limit_bytes=VMEM_LIMIT),
        name="out",
    )(h0, att, cz, wg, watt, wco, wout, l1g, l1b, wr, br, before)


def _sc_scatter2(src, idx_a, idx_b, n_out):
    m = src.shape[0]
    mesh = plsc.VectorSubcoreMesh(core_axis_name="c", subcore_axis_name="s")

    @functools.partial(pl.kernel, out_type=jax.ShapeDtypeStruct((n_out, LANES), src.dtype), mesh=mesh)
    def k(x_hbm, ia_hbm, ib_hbm, o_hbm):
        def body(x_vmem, ia_vmem, ib_vmem):
            pltpu.sync_copy(x_vmem, o_hbm.at[ia_vmem.at[0]])
            pltpu.sync_copy(x_vmem, o_hbm.at[ib_vmem.at[0]])

        pltpu.emit_pipeline(
            body, grid=(m // SC_WINDOW,),
            in_specs=[pl.BlockSpec((SC_WINDOW, LANES), index_map=lambda i: (i, 0)),
                      pl.BlockSpec((1, SC_WINDOW), index_map=lambda i: (i, 0)),
                      pl.BlockSpec((1, SC_WINDOW), index_map=lambda i: (i, 0))],
            out_specs=[],
            core_axis_name=("c", "s"), dimension_semantics=(pltpu.PARALLEL,),
        )(x_hbm, ia_hbm, ib_hbm)

    return k(src, idx_a, idx_b)


def _sc_gather2(table, idx_a, idx_b):
    windows = idx_a.shape[0]
    m = windows * SC_WINDOW
    mesh = plsc.VectorSubcoreMesh(core_axis_name="c", subcore_axis_name="s")

    @functools.partial(pl.kernel, out_type=jax.ShapeDtypeStruct((2 * m, LANES), table.dtype), mesh=mesh)
    def k(x_hbm, ia_hbm, ib_hbm, o_hbm):
        def body(i_vmem, o_vmem):
            pltpu.sync_copy(x_hbm.at[i_vmem.at[0]], o_vmem)

        for half, i_hbm in enumerate((ia_hbm, ib_hbm)):
            pltpu.emit_pipeline(
                body, grid=(windows,),
                in_specs=[pl.BlockSpec((1, SC_WINDOW), index_map=lambda i: (i, 0))],
                out_specs=[pl.BlockSpec((SC_WINDOW, LANES), index_map=lambda i, half=half: (half * windows + i, 0))],
                core_axis_name=("c", "s"), dimension_semantics=(pltpu.PARALLEL,),
            )(i_hbm, o_hbm)

    return k(table, idx_a, idx_b).reshape(2, m, LANES)


def _expert_kernel(te_ref, tv_ref, used_ref, seg_ref, nxt_ref, xs_hbm, wg_hbm, wu_hbm, wd_hbm, ys_ref,
                   wgb_ref, wub_ref, wdb_ref, xbuf_ref, xsem_ref, wg_buf, wu_buf, wd_buf, wsem_ref, *, tile):
    n = pl.program_id(0)
    valid = tv_ref[n]
    used = used_ref[0]

    def row_tile_copy(step):
        slot = lax.rem(step, XS_SLOTS)
        return pltpu.make_async_copy(xs_hbm.at[pl.ds(step * (tile // SUBLANES), tile // SUBLANES)], xbuf_ref.at[slot],
                                     xsem_ref.at[slot])

    def weight_copies(expert, slot):
        return [pltpu.make_async_copy(hbm.at[expert], buf.at[slot], wsem_ref.at[slot, i])
                for i, (hbm, buf) in enumerate(((wg_hbm, wg_buf), (wu_hbm, wu_buf), (wd_hbm, wd_buf)))]

    @pl.when(n == 0)
    def _():
        row_tile_copy(0).start()
        for c in weight_copies(te_ref[0], 0):
            c.start()

        @pl.when(used > 1)
        def _():
            row_tile_copy(1).start()

    @pl.when(n + 2 < used)
    def _():
        row_tile_copy(n + 2).start()

    seg = seg_ref[n]
    first_of_segment = jnp.logical_and(seg >= 0, jnp.logical_or(n == 0, seg != seg_ref[jnp.maximum(n - 1, 0)]))

    @pl.when(first_of_segment)
    def _():
        slot = lax.rem(seg, 2)
        for c in weight_copies(te_ref[n], slot):
            c.wait()
        wgb_ref[...] = wg_buf[slot].astype(BF16)
        wub_ref[...] = wu_buf[slot].astype(BF16)
        wdb_ref[...] = wd_buf[slot].astype(BF16)

        @pl.when(nxt_ref[n] >= 0)
        def _():
            for c in weight_copies(nxt_ref[n], 1 - slot):
                c.start()

    @pl.when(n < used)
    def _():
        row_tile_copy(n).wait()
        slot = lax.rem(n, XS_SLOTS)
        x = _unpack_bf16_pairs([xbuf_ref[slot, :, j, :, :].reshape(tile, LANES) for j in range(PACK_WORDS)])
        rows = lax.broadcasted_iota(jnp.int32, (tile, 1), 0)
        x = jnp.where(rows < valid, x, 0.0).astype(BF16)
        gt = _dot(x, wgb_ref[...])
        up = _dot(x, wub_ref[...])
        hid = (gt * jax.nn.sigmoid(gt)) * up
        y = _dot(hid.astype(BF16), wdb_ref[...])
        for j, w in enumerate(_pack_bf16_pairs(y)):
            ys_ref[:, j, :, :] = w.reshape(tile // SUBLANES, SUBLANES, LANES)


def _experts(tile_expert, tile_valid, tiles_used, tile_segment, next_expert, xs, w_gate, w_up, w_down, tile):
    n_tiles = xs.shape[0] * SUBLANES // tile
    blk = pl.BlockSpec((tile // SUBLANES, PACK_WORDS, SUBLANES, LANES),
                       lambda n, te, tv, used, seg, nxt: (jnp.minimum(n, jnp.maximum(used[0] - 1, 0)), 0, 0, 0))
    any_space = pl.BlockSpec(memory_space=pl.ANY)
    return pl.pallas_call(
        functools.partial(_expert_kernel, tile=tile),
        grid_spec=pltpu.PrefetchScalarGridSpec(
            num_scalar_prefetch=5,
            grid=(n_tiles,),
            in_specs=[any_space, any_space, any_space, any_space],
            out_specs=blk,
            scratch_shapes=[pltpu.VMEM((D_MODEL, D_EXPERT), BF16), pltpu.VMEM((D_MODEL, D_EXPERT), BF16),
                            pltpu.VMEM((D_EXPERT, D_MODEL), BF16),
                            pltpu.VMEM((XS_SLOTS, tile // SUBLANES, PACK_WORDS, SUBLANES, LANES), jnp.uint32),
                            pltpu.SemaphoreType.DMA((XS_SLOTS,)),
                            pltpu.VMEM((2, D_MODEL, D_EXPERT), F32), pltpu.VMEM((2, D_MODEL, D_EXPERT), F32),
                            pltpu.VMEM((2, D_EXPERT, D_MODEL), F32),
                            pltpu.SemaphoreType.DMA((2, 3))],
        ),
        out_shape=jax.ShapeDtypeStruct(xs.shape, jnp.uint32),
        compiler_params=pltpu.CompilerParams(
            dimension_semantics=("arbitrary",), vmem_limit_bytes=VMEM_LIMIT),
        name="experts",
    )(tile_expert, tile_valid, tiles_used, tile_segment, next_expert, xs, w_gate, w_up, w_down)


def _final_kernel(h_ref, g_ref, rt_ref, l2g_ref, l2b_ref, o_ref, *, tile):
    rt = rt_ref[...]
    lane = lax.broadcasted_iota(jnp.int32, rt.shape, 1)
    w1 = jnp.sum(jnp.where(lane == 2, rt, 0.0), axis=1, keepdims=True)
    w2 = jnp.sum(jnp.where(lane == 3, rt, 0.0), axis=1, keepdims=True)
    y1 = _unpack_bf16_pairs([g_ref[0, :, j, :, :].reshape(tile, LANES) for j in range(PACK_WORDS)])
    y2 = _unpack_bf16_pairs([g_ref[1, :, j, :, :].reshape(tile, LANES) for j in range(PACK_WORDS)])
    f = w1 * y1 + w2 * y2
    o_ref[...] = _layer_norm(ALPHA * h_ref[...] + f, l2g_ref[...], l2b_ref[...])


def _final(h1, g, rt, l2g, l2b, tile):
    T, D = h1.shape
    row = lambda w: pl.BlockSpec((tile, w), lambda i: (i, 0))
    full = lambda a: pl.BlockSpec(a.shape, lambda i: (0,) * a.ndim)
    return pl.pallas_call(
        functools.partial(_final_kernel, tile=tile),
        grid=(T // tile,),
        in_specs=[row(D),
                  pl.BlockSpec((2, tile // SUBLANES, PACK_WORDS, SUBLANES, LANES), lambda i: (0, i, 0, 0, 0)),
                  row(ROUTER_LANES), full(l2g), full(l2b)],
        out_specs=row(D),
        out_shape=jax.ShapeDtypeStruct((T, D), F32),
        compiler_params=pltpu.CompilerParams(dimension_semantics=("parallel",), vmem_limit_bytes=VMEM_LIMIT),
        name="final",
    )(h1, g, rt, l2g, l2b)


def _dispatch_plan(fields, counts, n_tokens, tile):
    cnt = counts[0, :N_EXPERTS].astype(jnp.int32)
    padded = (cnt + tile - 1) // tile * tile
    base = jnp.cumsum(padded) - padded
    e_ids = jnp.arange(N_EXPERTS, dtype=jnp.int32)

    def dest(row_e, row_r):
        e = fields[row_e].astype(jnp.int32)
        seg = jnp.sum(jnp.where(e[None, :] == e_ids[:, None], base[:, None], 0), axis=0)
        pos = seg + fields[row_r].astype(jnp.int32)
        first = (pos // SUBLANES) * (SUBLANES * PACK_WORDS) + pos % SUBLANES
        per_window = SC_WINDOW // PACK_WORDS
        first = first.reshape(n_tokens // LANES, LANES)
        windows = []
        for c in range(LANES // per_window):
            pieces = [first[:, c * per_window + SUBLANES * a:c * per_window + SUBLANES * (a + 1)] + SUBLANES * j
                      for a in range(per_window // SUBLANES) for j in range(PACK_WORDS)]
            windows.append(jnp.concatenate(pieces, axis=1))
        return jnp.stack(windows, axis=1).reshape(n_tokens * PACK_WORDS // SC_WINDOW, SC_WINDOW)

    n_tiles = (2 * n_tokens) // tile + N_EXPERTS
    start = jnp.arange(n_tiles, dtype=jnp.int32) * tile
    seg_end = base + padded
    te = jnp.minimum(jnp.sum((start[:, None] >= seg_end[None, :]).astype(jnp.int32), axis=1), N_EXPERTS - 1)
    te_base = jnp.sum(jnp.where(te[:, None] == e_ids[None, :], base[None, :], 0), axis=1)
    te_cnt = jnp.sum(jnp.where(te[:, None] == e_ids[None, :], cnt[None, :], 0), axis=1)
    tv = jnp.clip(te_cnt - (start - te_base), 0, tile)
    tiles_used = (jnp.sum(padded) // tile).reshape(1)
    in_use = start < tiles_used[0] * tile
    starts_segment = in_use & jnp.concatenate([jnp.ones((1,), bool), te[1:] != te[:-1]])
    seg = jnp.where(in_use, jnp.cumsum(starts_segment.astype(jnp.int32)) - 1, -1)
    follows = starts_segment[None, :] & (seg[None, :] == seg[:, None] + 1)
    nxt = jnp.where(jnp.any(follows, axis=1), jnp.sum(jnp.where(follows, te[None, :], 0), axis=1), -1)
    nxt = jnp.where(in_use, nxt, -1)
    return dest(0, 4), dest(1, 5), te, tv, tiles_used, seg, nxt, n_tiles


def _t5_bucket(rel):
    half = N_BUCKETS // 2
    max_exact = half // 2
    ret = jnp.where(rel > 0, half, 0)
    n = jnp.abs(rel)
    nf = jnp.maximum(n, 1).astype(F32)
    large = max_exact + (jnp.log(nf / max_exact) / math.log(MAX_DISTANCE / max_exact)
                         * (half - max_exact)).astype(jnp.int32)
    large = jnp.minimum(large, half - 1)
    return ret + jnp.where(n < max_exact, n, large)


def _bucket_bias(rel_bias, bucket):
    rb = rel_bias.astype(F32)
    out = jnp.zeros((N_Q_HEADS,) + bucket.shape, F32)
    for b in range(N_BUCKETS):
        out = out + jnp.where(bucket[None] == b, rb[b][:, None, None], 0.0)
    return out


def _pair_rows(t):
    return jnp.concatenate([t[:N_PAIRS], t[N_PAIRS:]], axis=-1).reshape(N_PAIRS * BLOCK, -1)


def _bias_tables(rel_bias, sink):
    qi = jnp.arange(BLOCK)
    kj = jnp.arange(3 * BLOCK) - BLOCK
    rel = kj[None, :] - qi[:, None]
    band = _bucket_bias(rel_bias, _t5_bucket(rel))
    in_win = (jnp.abs(rel) <= WINDOW)[None]
    not_prev = (kj >= 0)[None, None, :]
    not_next = (kj < BLOCK)[None, None, :]
    variants = [jnp.where(in_win & not_prev, band, NEG),
                jnp.where(in_win, band, NEG),
                jnp.where(in_win & not_next, band, NEG)]
    bias = jnp.stack([_pair_rows(v) for v in variants])

    off = BLOCK - N_META
    mvars = []
    for blk in (1, 2):
        qpos = blk * BLOCK + qi - off
        meta_rel = jnp.arange(N_META)[None, :] - qpos[:, None]
        mvars.append(_pair_rows(_bucket_bias(rel_bias, _t5_bucket(meta_rel))))
    mbias = jnp.stack(mvars)
    s = sink.astype(F32)
    sink_tab = jnp.repeat(jnp.stack([s[:N_PAIRS], s[N_PAIRS:]], axis=-1), BLOCK, axis=0)
    return bias, mbias, sink_tab


def kernel(x_prompt, x_sample, meta, ln_in_g, ln_in_b, rel_bias, w_in, w_att_branch, sink, conv_w, conv_b,
           conv_ln_g, conv_ln_b, w_conv_out, w_out, ln1_g, ln1_b, w_group, b_group, w_router, b_router,
           w_gate, w_up, w_down, ln2_g, ln2_b):
    row = lambda v: v.reshape(1, -1).astype(F32)
    w = w_in[0]
    wq = (w[:, :Q_END].reshape(D_MODEL, 2, N_PAIRS, HEAD_DIM).transpose(0, 2, 1, 3)
          .reshape(D_MODEL, ATT_WIDTH).astype(BF16))
    watt = (w_att_branch[0].reshape(2, N_PAIRS, HEAD_DIM, D_MODEL).transpose(1, 0, 2, 3)
            .reshape(ATT_WIDTH, D_MODEL).astype(BF16))
    wkv = w[:, Q_END:V_END].astype(BF16)
    wglu = w[:, V_END:GLU_END].astype(BF16)
    wg = w[:, GLU_END:].astype(BF16)
    wco = w_conv_out[0].astype(BF16)
    wout = w_out[0].astype(BF16)
    wr = jnp.zeros((D_MODEL, ROUTER_LANES), F32)
    wr = wr.at[:, :N_EXPERTS].set(w_router[0]).at[:, N_EXPERTS:N_EXPERTS + N_GROUPS].set(w_group[0]).astype(BF16)
    br = jnp.zeros((1, ROUTER_LANES), F32)
    br = br.at[0, :N_EXPERTS].set(b_router[0]).at[0, N_EXPERTS:N_EXPERTS + N_GROUPS].set(b_group[0])
    ln_g, ln_b = row(ln_in_g), row(ln_in_b)
    bias, mbias, sink_tab = _bias_tables(rel_bias, sink[0])

    xm = jnp.concatenate([jnp.zeros((BLOCK - N_META, D_MODEL), F32), meta.astype(F32)], axis=0)
    _, _, kv_m, zc_m = _proj(xm, xm, _Geom(1, 1, 1, 1), ln_g, ln_b, wq, wkv, wglu, BLOCK)
    kv_meta = kv_m[BLOCK - N_META:BLOCK]
    z_meta = zc_m[BLOCK - N_META:BLOCK]

    (bp, sp, _), (bs, ss, _) = x_prompt.shape, x_sample.shape
    xp, xs = x_prompt.reshape(bp * sp, D_MODEL), x_sample.reshape(bs * ss, D_MODEL)
    geom = _geom(x_prompt, x_sample, TILE)
    h0, q, kv, zc = _proj(xp, xs, _geom(x_prompt, x_sample, TILE_PROJ), ln_g, ln_b, wq, wkv, wglu, TILE_PROJ)
    att = _attn(q, kv, _geom(x_prompt, x_sample, TILE_ATTN), kv_meta, bias, mbias, sink_tab, TILE_ATTN)
    cz = _conv(zc, geom, z_meta, conv_w[0], row(conv_b[0]), row(conv_ln_g[0]), row(conv_ln_b[0]), TILE)
    l1g, l1b, l2g, l2b = row(ln1_g[0]), row(ln1_b[0]), row(ln2_g[0]), row(ln2_b[0])

    def moe(first_tile, n_tiles):
        n = n_tiles * TILE
        h1, h1p, rt, fields, counts = _out(h0, att, cz, wg, watt, wco, wout, l1g, l1b, wr, br,
                                           TILE_OUT, first_tile * TILE // TILE_OUT, n_tiles * TILE // TILE_OUT)
        idx1, idx2, tile_expert, tile_valid, tiles_used, tile_segment, next_expert, n_tiles = _dispatch_plan(
            fields, counts, n, TILE_EXPERT)
        n_rows = n_tiles * TILE_EXPERT
        xsorted = _sc_scatter2(h1p.reshape(n * PACK_WORDS, LANES), idx1, idx2, n_rows * PACK_WORDS)
        ys = _experts(tile_expert, tile_valid, tiles_used, tile_segment, next_expert,
                      xsorted.reshape(n_rows // SUBLANES, PACK_WORDS, SUBLANES, LANES),
                      w_gate[0], w_up[0], w_down[0], TILE_EXPERT)
        g = _sc_gather2(ys.reshape(n_rows * PACK_WORDS, LANES), idx1, idx2)
        return _final(h1, g.reshape(2, n // SUBLANES, PACK_WORDS, SUBLANES, LANES), rt, l2g, l2b, TILE_FINAL)

    return moe(0, geom.n_p).reshape(x_prompt.shape), moe(geom.n_p, geom.n_s).reshape(x_sample.shape)
```

```python
import functools
import math
from typing import NamedTuple

import jax
import jax.numpy as jnp
from jax import lax
from jax.experimental import pallas as pl
from jax.experimental.pallas import tpu as pltpu
from jax.experimental.pallas import tpu_sc as plsc

D_MODEL = 1024
N_META = 16
BLOCK = 128
WINDOW = 128
N_Q_HEADS = 8
N_KV_HEADS = 2
HEAD_DIM = 64
ATT_WIDTH = N_Q_HEADS * HEAD_DIM
KV_WIDTH = N_KV_HEADS * HEAD_DIM
CONV_WIDTH = D_MODEL // 2
CONV_K = 31
N_BUCKETS = 32
MAX_DISTANCE = 128
N_GROUPS = 4
EXPERTS_PER_GROUP = 8
N_EXPERTS = N_GROUPS * EXPERTS_PER_GROUP
D_EXPERT = 256
LN_EPS = 1e-5
DEPTH = 1
ALPHA = (2 * DEPTH) ** 0.25
NEG = -1e30
Q_END = ATT_WIDTH
K_END = Q_END + KV_WIDTH
V_END = K_END + KV_WIDTH
GLU_END = V_END + 2 * CONV_WIDTH

N_PAIRS = N_Q_HEADS // 2
LANES = 128
SUBLANES = 8
CONV_HALO = 16
ROUTER_LANES = 128

TILE = 512
CONV_ROWS = 128
LN_ROWS = 64
SHIFT_ROWS = 128
TILE_PROJ = 1024
TILE_OUT = 1024
XS_SLOTS = 3
TILE_FINAL = 1024
TILE_ATTN = 2048
TILE_EXPERT = 512
PACK_WORDS = 4
SC_WINDOW = 128
ROUTE_FIELDS = 8
VMEM_LIMIT = 56 * 1024 * 1024

BF16 = jnp.bfloat16
F32 = jnp.float32


def _layer_norm(x, g, b):
    mu = jnp.mean(x, axis=-1, keepdims=True)
    xc = x - mu
    var = jnp.mean(xc * xc, axis=-1, keepdims=True)
    return xc * lax.rsqrt(var + LN_EPS) * g + b


def _dot(a, b):
    return jnp.dot(a, b, preferred_element_type=F32)


def _dot_nt(a, b):
    return lax.dot_general(a, b, (((1,), (1,)), ((), ())), preferred_element_type=F32)


class _Geom(NamedTuple):
    n_p: int
    n_s: int
    tp: int
    ts: int


def _geom(x_prompt, x_sample, tile):
    (bp, sp, _), (bs, ss, _) = x_prompt.shape, x_sample.shape
    return _Geom(bp * sp // tile, bs * ss // tile, sp // tile, ss // tile)


def _seq_pos(t, g):
    is_p = t < g.n_p
    local = jnp.where(is_p, lax.rem(t, g.tp), lax.rem(jnp.maximum(t - g.n_p, 0), g.ts))
    return is_p, local == 0, local == jnp.where(is_p, g.tp - 1, g.ts - 1)


def _x_specs(g, tile, width):
    return [pl.BlockSpec((tile, width), lambda t: (jnp.minimum(t, g.n_p - 1), 0)),
            pl.BlockSpec((tile, width), lambda t: (jnp.maximum(t - g.n_p, 0), 0))]


def _proj_kernel(xp_ref, xs_ref, g_ref, b_ref, wq_ref, wkv_ref, wglu_ref, h0_ref, q_ref, kv_ref, zc_ref, *, geom):
    x = jnp.where(pl.program_id(0) < geom.n_p, xp_ref[...], xs_ref[...])
    h0 = _layer_norm(x, g_ref[...], b_ref[...])
    h0_ref[...] = h0
    h = h0.astype(BF16)
    q_ref[...] = _dot(h, wq_ref[...]).astype(BF16)
    kv_ref[...] = _dot(h, wkv_ref[...]).astype(BF16)
    u = _dot(h, wglu_ref[...])
    zc_ref[...] = (u[:, :CONV_WIDTH] * jax.nn.sigmoid(u[:, CONV_WIDTH:])).astype(BF16)


def _proj(xp, xs, geom, ln_g, ln_b, wq, wkv, wglu, tile):
    T = (geom.n_p + geom.n_s) * tile
    row = lambda w: pl.BlockSpec((tile, w), lambda t: (t, 0))
    full = lambda a: pl.BlockSpec(a.shape, lambda t: (0,) * a.ndim)
    return pl.pallas_call(
        functools.partial(_proj_kernel, geom=geom),
        grid=(geom.n_p + geom.n_s,),
        in_specs=_x_specs(geom, tile, D_MODEL) + [full(ln_g), full(ln_b), full(wq), full(wkv), full(wglu)],
        out_specs=[row(D_MODEL), row(ATT_WIDTH), row(2 * KV_WIDTH), row(CONV_WIDTH)],
        out_shape=[
            jax.ShapeDtypeStruct((T, D_MODEL), F32),
            jax.ShapeDtypeStruct((T, ATT_WIDTH), BF16),
            jax.ShapeDtypeStruct((T, 2 * KV_WIDTH), BF16),
            jax.ShapeDtypeStruct((T, CONV_WIDTH), BF16),
        ],
        compiler_params=pltpu.CompilerParams(dimension_semantics=("parallel",), vmem_limit_bytes=VMEM_LIMIT),
        name="proj",
    )(xp, xs, ln_g, ln_b, wq, wkv, wglu)


def _attn_kernel(q_ref, kvp_ref, kvc_ref, kvn_ref, kvm_ref, bias_ref, mbias_ref, sink_ref, o_ref, *, tile, geom):
    _, seq_first, seq_last = _seq_pos(pl.program_id(0), geom)
    blocks = tile // BLOCK
    scale = HEAD_DIM ** -0.5

    lane = lax.broadcasted_iota(jnp.int32, (1, LANES), 1)
    lo = lane < HEAD_DIM

    def split_heads(t):
        z = jnp.zeros_like(t)
        return jnp.where(lo, t, z), jnp.where(lo, z, t)

    kv_ext = jnp.concatenate([kvp_ref[...], kvc_ref[...], kvn_ref[...]], axis=0)
    k_ext = kv_ext[:, :KV_WIDTH] * jnp.asarray(scale, BF16)
    v_ext = kv_ext[:, KV_WIDTH:]
    ka, kb = split_heads(k_ext)
    va, vb = split_heads(v_ext)
    kma, kmb = split_heads(kvm_ref[:, :KV_WIDTH] * jnp.asarray(scale, BF16))
    vma, vmb = split_heads(kvm_ref[:, KV_WIDTH:])
    km_cat = jnp.concatenate([kma, kmb], axis=0)
    vm_cat = jnp.concatenate([vma, vmb], axis=0)

    mlane = lax.broadcasted_iota(jnp.int32, (1, 2 * N_META), 1)
    m_first = mlane < N_META
    sink = sink_ref[...]
    lane_o = lax.broadcasted_iota(jnp.int32, (1, LANES), 1) < HEAD_DIM

    for j in range(blocks):
        first = jnp.logical_and(seq_first, j == 0)
        last = jnp.logical_and(seq_last, j == blocks - 1)
        variant = jnp.where(first, 0, jnp.where(last, 2, 1))
        mvariant = jnp.where(first, 0, 1)

        r0 = j * BLOCK
        qb = q_ref[r0:r0 + BLOCK, :]
        q4 = jnp.concatenate([qb[:, p * LANES:(p + 1) * LANES] for p in range(N_PAIRS)], axis=0)
        k_cat = jnp.concatenate([ka[r0:r0 + 3 * BLOCK], kb[r0:r0 + 3 * BLOCK]], axis=0)
        v_cat = jnp.concatenate([va[r0:r0 + 3 * BLOCK], vb[r0:r0 + 3 * BLOCK]], axis=0)

        s = _dot_nt(q4, k_cat) + bias_ref[variant]
        sm = _dot_nt(q4, km_cat) + mbias_ref[mvariant]

        s_a, s_b = s[:, :3 * BLOCK], s[:, 3 * BLOCK:]
        sm_a = jnp.where(m_first, sm, NEG)
        sm_b = jnp.where(m_first, NEG, sm)
        m_a = jnp.maximum(jnp.maximum(jnp.max(s_a, axis=1, keepdims=True),
                                      jnp.max(sm_a, axis=1, keepdims=True)), sink[:, 0:1])
        m_b = jnp.maximum(jnp.maximum(jnp.max(s_b, axis=1, keepdims=True),
                                      jnp.max(sm_b, axis=1, keepdims=True)), sink[:, 1:2])
        p_a = jnp.exp(s_a - m_a)
        p_b = jnp.exp(s_b - m_b)
        pm = jnp.exp(jnp.where(m_first, sm - m_a, sm - m_b))
        l_a = (jnp.sum(p_a, axis=1, keepdims=True) + jnp.sum(jnp.where(m_first, pm, 0.0), axis=1, keepdims=True)
               + jnp.exp(sink[:, 0:1] - m_a))
        l_b = (jnp.sum(p_b, axis=1, keepdims=True) + jnp.sum(jnp.where(m_first, 0.0, pm), axis=1, keepdims=True)
               + jnp.exp(sink[:, 1:2] - m_b))
        p = jnp.concatenate([p_a, p_b], axis=1).astype(BF16)
        o = _dot(p, v_cat) + _dot(pm.astype(BF16), vm_cat)
        o = o * jnp.where(lane_o, 1.0 / l_a, 1.0 / l_b)
        for pr in range(N_PAIRS):
            o_ref[r0:r0 + BLOCK, pr * LANES:(pr + 1) * LANES] = o[pr * BLOCK:(pr + 1) * BLOCK].astype(BF16)


def _attn(q, kv, geom, kv_meta, bias, mbias, sink_tab, tile):
    T = q.shape[0]
    bpt = tile // BLOCK
    n_blocks = T // BLOCK
    full = lambda a: pl.BlockSpec(a.shape, lambda t: (0,) * a.ndim)
    return pl.pallas_call(
        functools.partial(_attn_kernel, tile=tile, geom=geom),
        grid=(T // tile,),
        in_specs=[
            pl.BlockSpec((tile, ATT_WIDTH), lambda t: (t, 0)),
            pl.BlockSpec((BLOCK, 2 * KV_WIDTH), lambda t: (jnp.maximum(t * bpt - 1, 0), 0)),
            pl.BlockSpec((tile, 2 * KV_WIDTH), lambda t: (t, 0)),
            pl.BlockSpec((BLOCK, 2 * KV_WIDTH), lambda t: (jnp.minimum((t + 1) * bpt, n_blocks - 1), 0)),
            full(kv_meta), full(bias), full(mbias), full(sink_tab),
        ],
        out_specs=pl.BlockSpec((tile, ATT_WIDTH), lambda t: (t, 0)),
        out_shape=jax.ShapeDtypeStruct((T, ATT_WIDTH), BF16),
        compiler_params=pltpu.CompilerParams(dimension_semantics=("parallel",), vmem_limit_bytes=VMEM_LIMIT),
        name="attn",
    )(q, kv, kv, kv, kv_meta, bias, mbias, sink_tab)


def _conv_kernel(zp_ref, zc_ref, zn_ref, zm_ref, w_ref, cb_ref, g_ref, b_ref, o_ref, ext_ref, sh_ref, y_ref,
                 *, tile, geom):
    i = pl.program_id(0)
    _, seq_first, seq_last = _seq_pos(i, geom)
    ext_ref[0:CONV_HALO, :] = jnp.where(seq_first, zm_ref[...], zp_ref[...]).astype(F32)
    ext_ref[CONV_HALO:CONV_HALO + tile, :] = zc_ref[...].astype(F32)
    ext_ref[CONV_HALO + tile:, :] = jnp.where(seq_last, 0.0, zn_ref[...].astype(F32))
    off = CONV_HALO - CONV_K // 2
    reach = (off + CONV_K - 1) // SUBLANES * SUBLANES
    for p in range(1, SUBLANES):
        for r0 in range(0, tile + reach, SHIFT_ROWS):
            n = min(SHIFT_ROWS, tile + reach - r0)
            sh_ref[p - 1, r0:r0 + n, :] = ext_ref[r0 + p:r0 + p + n, :]

    def taps(r0, cs):
        acc = jnp.zeros((CONV_ROWS, LANES), F32)
        for k in range(CONV_K):
            p, a = (off + k) % SUBLANES, (off + k) // SUBLANES * SUBLANES
            rows = slice(r0 + a, r0 + a + CONV_ROWS)
            win = ext_ref[rows, cs] if p == 0 else sh_ref[p - 1, rows, cs]
            acc = acc + win * w_ref[k:k + 1, cs]
        y_ref[r0:r0 + CONV_ROWS, cs] = acc

    for c in range(CONV_WIDTH // LANES):
        for r in range(tile // CONV_ROWS):
            pl.when(i >= 0)(functools.partial(taps, r * CONV_ROWS, slice(c * LANES, (c + 1) * LANES)))
    cb, g, b = cb_ref[...], g_ref[...], b_ref[...]
    for r in range(tile // LN_ROWS):
        r0 = r * LN_ROWS
        y = _layer_norm(y_ref[r0:r0 + LN_ROWS, :] + cb, g, b)
        o_ref[r0:r0 + LN_ROWS, :] = (y * jax.nn.sigmoid(y)).astype(BF16)


def _conv(zc, geom, z_meta, conv_w, conv_b, ln_g, ln_b, tile):
    T, C = zc.shape
    hpt = tile // CONV_HALO
    n_halo = T // CONV_HALO
    full = lambda a: pl.BlockSpec(a.shape, lambda t: (0,) * a.ndim)
    return pl.pallas_call(
        functools.partial(_conv_kernel, tile=tile, geom=geom),
        grid=(T // tile,),
        in_specs=[
            pl.BlockSpec((CONV_HALO, C), lambda t: (jnp.maximum(t * hpt - 1, 0), 0)),
            pl.BlockSpec((tile, C), lambda t: (t, 0)),
            pl.BlockSpec((CONV_HALO, C), lambda t: (jnp.minimum((t + 1) * hpt, n_halo - 1), 0)),
            full(z_meta), full(conv_w), full(conv_b), full(ln_g), full(ln_b),
        ],
        out_specs=pl.BlockSpec((tile, C), lambda t: (t, 0)),
        out_shape=jax.ShapeDtypeStruct((T, C), BF16),
        scratch_shapes=[pltpu.VMEM((tile + 2 * CONV_HALO, C), F32),
                        pltpu.VMEM((SUBLANES - 1, tile + 2 * CONV_HALO - SUBLANES, C), F32),
                        pltpu.VMEM((tile, C), F32)],
        compiler_params=pltpu.CompilerParams(dimension_semantics=("parallel",), vmem_limit_bytes=VMEM_LIMIT),
        name="conv",
    )(zc, zc, zc, z_meta, conv_w, conv_b, ln_g, ln_b)


def _route(r):
    lane_i = lax.broadcasted_iota(jnp.int32, r.shape, 1)
    lane = lane_i.astype(F32)
    big = float(1 << 20)
    is_g = jnp.logical_and(lane_i >= N_EXPERTS, lane_i < N_EXPERTS + N_GROUPS)
    lg = jnp.where(is_g, r, -jnp.inf)
    mg = jnp.max(lg, axis=1, keepdims=True)
    g_w = 1.0 / jnp.sum(jnp.exp(lg - mg), axis=1, keepdims=True)
    g_idx = jnp.min(jnp.where(lg == mg, lane - N_EXPERTS, big), axis=1, keepdims=True)
    lane_group = jnp.right_shift(lane_i, EXPERTS_PER_GROUP.bit_length() - 1).astype(F32)
    in_group = jnp.logical_and(lane_i < N_EXPERTS, lane_group == g_idx)
    le = jnp.where(in_group, r, -jnp.inf)
    m1 = jnp.max(le, axis=1, keepdims=True)
    den = jnp.sum(jnp.exp(le - m1), axis=1, keepdims=True)
    i1 = jnp.min(jnp.where(le == m1, lane, big), axis=1, keepdims=True)
    le2 = jnp.where(lane == i1, -jnp.inf, le)
    m2 = jnp.max(le2, axis=1, keepdims=True)
    i2 = jnp.min(jnp.where(le2 == m2, lane, big), axis=1, keepdims=True)
    p1 = 1.0 / den
    p2 = jnp.exp(m2 - m1) / den
    tot = p1 + p2
    return i1, i2, g_w * (p1 / tot), g_w * (p2 / tot)


def _pack_bf16_pairs(x):
    half = x.shape[1] // 2
    words = []
    for j in range(half // LANES):
        lo = pltpu.bitcast(x[:, j * LANES:(j + 1) * LANES].astype(BF16).astype(F32), jnp.uint32)
        hi = pltpu.bitcast(x[:, half + j * LANES:half + (j + 1) * LANES].astype(BF16).astype(F32), jnp.uint32)
        words.append(hi | (lo >> 16))
    return words


def _unpack_bf16_pairs(words):
    lo = [pltpu.bitcast(w << 16, F32) for w in words]
    hi = [pltpu.bitcast(w & jnp.uint32(0xFFFF0000), F32) for w in words]
    return jnp.concatenate(lo + hi, axis=1)


def _out_kernel(h0_ref, att_ref, cz_ref, wg_ref, watt_ref, wco_ref, wout_ref,
                l1g_ref, l1b_ref, wr_ref, br_ref, before_ref, h1_ref, h1p_ref, rt_ref, fld_ref, cnt_ref, *, tile):
    @pl.when(pl.program_id(0) == 0)
    def _():
        cnt_ref[...] = jnp.zeros_like(cnt_ref)

    h0 = h0_ref[...]
    hb = h0.astype(BF16)
    g_att = jax.nn.sigmoid(_dot(hb, wg_ref[:, :D_MODEL]))
    mix = g_att * _dot(att_ref[...], watt_ref[...])
    g_conv = jax.nn.sigmoid(_dot(hb, wg_ref[:, D_MODEL:]))
    mix = mix + g_conv * _dot(cz_ref[...], wco_ref[...])
    m = _dot(mix.astype(BF16), wout_ref[...])
    h1 = _layer_norm(ALPHA * h0 + m, l1g_ref[...], l1b_ref[...])
    h1_ref[...] = h1
    for j, w in enumerate(_pack_bf16_pairs(h1)):
        h1p_ref[:, j, :, :] = w.reshape(tile // SUBLANES, SUBLANES, LANES)

    r = _dot(h1.astype(BF16), wr_ref[...]) + br_ref[...]
    i1, i2, w1, w2 = _route(r)
    lane = lax.broadcasted_iota(jnp.int32, (tile, ROUTER_LANES), 1)
    lane_f = lane.astype(F32)
    hit1, hit2 = lane_f == i1, lane_f == i2
    onehot = jnp.where(jnp.logical_or(hit1, hit2), 1.0, 0.0)
    seen = _dot(before_ref[...], onehot.astype(BF16)) + cnt_ref[0:1, :]
    rank1 = jnp.sum(jnp.where(hit1, seen, 0.0), axis=1, keepdims=True)
    rank2 = jnp.sum(jnp.where(hit2, seen, 0.0), axis=1, keepdims=True)
    cnt_ref[...] = cnt_ref[...] + jnp.sum(onehot, axis=0, keepdims=True)
    fields = (i1, i2, w1, w2, rank1, rank2)
    rt = jnp.zeros((tile, ROUTER_LANES), F32)
    for k, v in enumerate(fields):
        rt = jnp.where(lane == k, v, rt)
    rt_ref[...] = rt
    fld_ref[...] = jnp.transpose(rt)[0:ROUTE_FIELDS, :]


def _out(h0, att, cz, wg, watt, wco, wout, l1g, l1b, wr, br, tile, first_tile, n_tiles):
    T, D = n_tiles * tile, D_MODEL
    own = lambda w: pl.BlockSpec((tile, w), lambda i: (i, 0))
    flat = lambda w: pl.BlockSpec((tile, w), lambda i: (first_tile + i, 0))
    full = lambda a: pl.BlockSpec(a.shape, lambda i: (0,) * a.ndim, pipeline_mode=pl.Buffered(1))
    before = (jnp.arange(tile)[None, :] < jnp.arange(tile)[:, None]).astype(BF16)
    return pl.pallas_call(
        functools.partial(_out_kernel, tile=tile),
        grid=(T // tile,),
        in_specs=[flat(D), flat(ATT_WIDTH), flat(CONV_WIDTH), full(wg),
                  full(watt), full(wco), full(wout), full(l1g), full(l1b), full(wr), full(br), full(before)],
        out_specs=[own(D),
                   pl.BlockSpec((tile // SUBLANES, PACK_WORDS, SUBLANES, LANES), lambda i: (i, 0, 0, 0)),
                   own(ROUTER_LANES),
                   pl.BlockSpec((ROUTE_FIELDS, tile), lambda i: (0, i)),
                   pl.BlockSpec((SUBLANES, ROUTER_LANES), lambda i: (0, 0))],
        out_shape=[jax.ShapeDtypeStruct((T, D), F32),
                   jax.ShapeDtypeStruct((T // SUBLANES, PACK_WORDS, SUBLANES, LANES), jnp.uint32),
                   jax.ShapeDtypeStruct((T, ROUTER_LANES), F32),
                   jax.ShapeDtypeStruct((ROUTE_FIELDS, T), F32),
                   jax.ShapeDtypeStruct((SUBLANES, ROUTER_LANES), F32)],
        compiler_params=pltpu.CompilerParams(dimension_semantics=("arbitrary",), vmem_limit_bytes=VMEM_LIMIT),
        name="out",
    )(h0, att, cz, wg, watt, wco, wout, l1g, l1b, wr, br, before)


def _sc_scatter2(src, idx_a, idx_b, n_out):
    m = src.shape[0]
    mesh = plsc.VectorSubcoreMesh(core_axis_name="c", subcore_axis_name="s")

    @functools.partial(pl.kernel, out_type=jax.ShapeDtypeStruct((n_out, LANES), src.dtype), mesh=mesh)
    def k(x_hbm, ia_hbm, ib_hbm, o_hbm):
        def body(x_vmem, ia_vmem, ib_vmem):
            pltpu.sync_copy(x_vmem, o_hbm.at[ia_vmem.at[0]])
            pltpu.sync_copy(x_vmem, o_hbm.at[ib_vmem.at[0]])

        pltpu.emit_pipeline(
            body, grid=(m // SC_WINDOW,),
            in_specs=[pl.BlockSpec((SC_WINDOW, LANES), index_map=lambda i: (i, 0)),
                      pl.BlockSpec((1, SC_WINDOW), index_map=lambda i: (i, 0)),
                      pl.BlockSpec((1, SC_WINDOW), index_map=lambda i: (i, 0))],
            out_specs=[],
            core_axis_name=("c", "s"), dimension_semantics=(pltpu.PARALLEL,),
        )(x_hbm, ia_hbm, ib_hbm)

    return k(src, idx_a, idx_b)


def _sc_gather2(table, idx_a, idx_b):
    windows = idx_a.shape[0]
    m = windows * SC_WINDOW
    mesh = plsc.VectorSubcoreMesh(core_axis_name="c", subcore_axis_name="s")

    @functools.partial(pl.kernel, out_type=jax.ShapeDtypeStruct((2 * m, LANES), table.dtype), mesh=mesh)
    def k(x_hbm, ia_hbm, ib_hbm, o_hbm):
        def body(i_vmem, o_vmem):
            pltpu.sync_copy(x_hbm.at[i_vmem.at[0]], o_vmem)

        for half, i_hbm in enumerate((ia_hbm, ib_hbm)):
            pltpu.emit_pipeline(
                body, grid=(windows,),
                in_specs=[pl.BlockSpec((1, SC_WINDOW), index_map=lambda i: (i, 0))],
                out_specs=[pl.BlockSpec((SC_WINDOW, LANES), index_map=lambda i, half=half: (half * windows + i, 0))],
                core_axis_name=("c", "s"), dimension_semantics=(pltpu.PARALLEL,),
            )(i_hbm, o_hbm)

    return k(table, idx_a, idx_b).reshape(2, m, LANES)


def _expert_kernel(te_ref, tv_ref, used_ref, seg_ref, nxt_ref, xs_hbm, wg_hbm, wu_hbm, wd_hbm, ys_ref,
                   wgb_ref, wub_ref, wdb_ref, xbuf_ref, xsem_ref, wg_buf, wu_buf, wd_buf, wsem_ref, *, tile):
    n = pl.program_id(0)
    valid = tv_ref[n]
    used = used_ref[0]

    def row_tile_copy(step):
        slot = lax.rem(step, XS_SLOTS)
        return pltpu.make_async_copy(xs_hbm.at[pl.ds(step * (tile // SUBLANES), tile // SUBLANES)], xbuf_ref.at[slot],
                                     xsem_ref.at[slot])

    def weight_copies(expert, slot):
        return [pltpu.make_async_copy(hbm.at[expert], buf.at[slot], wsem_ref.at[slot, i])
                for i, (hbm, buf) in enumerate(((wg_hbm, wg_buf), (wu_hbm, wu_buf), (wd_hbm, wd_buf)))]

    @pl.when(n == 0)
    def _():
        row_tile_copy(0).start()
        for c in weight_copies(te_ref[0], 0):
            c.start()

        @pl.when(used > 1)
        def _():
            row_tile_copy(1).start()

    @pl.when(n + 2 < used)
    def _():
        row_tile_copy(n + 2).start()

    seg = seg_ref[n]
    first_of_segment = jnp.logical_and(seg >= 0, jnp.logical_or(n == 0, seg != seg_ref[jnp.maximum(n - 1, 0)]))

    @pl.when(first_of_segment)
    def _():
        slot = lax.rem(seg, 2)
        for c in weight_copies(te_ref[n], slot):
            c.wait()
        wgb_ref[...] = wg_buf[slot].astype(BF16)
        wub_ref[...] = wu_buf[slot].astype(BF16)
        wdb_ref[...] = wd_buf[slot].astype(BF16)

        @pl.when(nxt_ref[n] >= 0)
        def _():
            for c in weight_copies(nxt_ref[n], 1 - slot):
                c.start()

    @pl.when(n < used)
    def _():
        row_tile_copy(n).wait()
        slot = lax.rem(n, XS_SLOTS)
        x = _unpack_bf16_pairs([xbuf_ref[slot, :, j, :, :].reshape(tile, LANES) for j in range(PACK_WORDS)])
        rows = lax.broadcasted_iota(jnp.int32, (tile, 1), 0)
        x = jnp.where(rows < valid, x, 0.0).astype(BF16)
        gt = _dot(x, wgb_ref[...])
        up = _dot(x, wub_ref[...])
        hid = (gt * jax.nn.sigmoid(gt)) * up
        y = _dot(hid.astype(BF16), wdb_ref[...])
        for j, w in enumerate(_pack_bf16_pairs(y)):
            ys_ref[:, j, :, :] = w.reshape(tile // SUBLANES, SUBLANES, LANES)


def _experts(tile_expert, tile_valid, tiles_used, tile_segment, next_expert, xs, w_gate, w_up, w_down, tile):
    n_tiles = xs.shape[0] * SUBLANES // tile
    blk = pl.BlockSpec((tile // SUBLANES, PACK_WORDS, SUBLANES, LANES),
                       lambda n, te, tv, used, seg, nxt: (jnp.minimum(n, jnp.maximum(used[0] - 1, 0)), 0, 0, 0))
    any_space = pl.BlockSpec(memory_space=pl.ANY)
    return pl.pallas_call(
        functools.partial(_expert_kernel, tile=tile),
        grid_spec=pltpu.PrefetchScalarGridSpec(
            num_scalar_prefetch=5,
            grid=(n_tiles,),
            in_specs=[any_space, any_space, any_space, any_space],
            out_specs=blk,
            scratch_shapes=[pltpu.VMEM((D_MODEL, D_EXPERT), BF16), pltpu.VMEM((D_MODEL, D_EXPERT), BF16),
                            pltpu.VMEM((D_EXPERT, D_MODEL), BF16),
                            pltpu.VMEM((XS_SLOTS, tile // SUBLANES, PACK_WORDS, SUBLANES, LANES), jnp.uint32),
                            pltpu.SemaphoreType.DMA((XS_SLOTS,)),
                            pltpu.VMEM((2, D_MODEL, D_EXPERT), F32), pltpu.VMEM((2, D_MODEL, D_EXPERT), F32),
                            pltpu.VMEM((2, D_EXPERT, D_MODEL), F32),
                            pltpu.SemaphoreType.DMA((2, 3))],
        ),
        out_shape=jax.ShapeDtypeStruct(xs.shape, jnp.uint32),
        compiler_params=pltpu.CompilerParams(
            dimension_semantics=("arbitrary",), vmem_limit_bytes=VMEM_LIMIT),
        name="experts",
    )(tile_expert, tile_valid, tiles_used, tile_segment, next_expert, xs, w_gate, w_up, w_down)


def _final_kernel(h_ref, g_ref, rt_ref, l2g_ref, l2b_ref, *rest, tile):
    o_ref = rest[-1]
    rt = rt_ref[...]
    lane = lax.broadcasted_iota(jnp.int32, rt.shape, 1)
    w1 = jnp.sum(jnp.where(lane == 2, rt, 0.0), axis=1, keepdims=True)
    w2 = jnp.sum(jnp.where(lane == 3, rt, 0.0), axis=1, keepdims=True)
    y1 = _unpack_bf16_pairs([g_ref[0, :, j, :, :].reshape(tile, LANES) for j in range(PACK_WORDS)])
    y2 = _unpack_bf16_pairs([g_ref[1, :, j, :, :].reshape(tile, LANES) for j in range(PACK_WORDS)])
    f = w1 * y1 + w2 * y2
    o_ref[...] = _layer_norm(ALPHA * h_ref[...] + f, l2g_ref[...], l2b_ref[...])


def _final(h1, g, rt, l2g, l2b, tile, first_tile, y_so_far=None):
    T, D = h1.shape
    n_tiles = g.shape[1] * SUBLANES // tile
    row = lambda w: pl.BlockSpec((tile, w), lambda i: (first_tile + i, 0))
    full = lambda a: pl.BlockSpec(a.shape, lambda i: (0,) * a.ndim)
    in_specs = [row(D),
                pl.BlockSpec((2, tile // SUBLANES, PACK_WORDS, SUBLANES, LANES), lambda i: (0, i, 0, 0, 0)),
                row(ROUTER_LANES), full(l2g), full(l2b)]
    args = [h1, g, rt, l2g, l2b]
    aliases = {}
    if y_so_far is not None:
        in_specs.append(pl.BlockSpec(memory_space=pl.ANY))
        args.append(y_so_far)
        aliases = {len(args) - 1: 0}
    return pl.pallas_call(
        functools.partial(_final_kernel, tile=tile),
        grid=(n_tiles,),
        in_specs=in_specs,
        out_specs=row(D),
        out_shape=jax.ShapeDtypeStruct((T, D), F32),
        input_output_aliases=aliases,
        compiler_params=pltpu.CompilerParams(dimension_semantics=("parallel",), vmem_limit_bytes=VMEM_LIMIT),
        name="final",
    )(*args)


def _dispatch_plan(fields, counts, n_tokens, tile):
    cnt = counts[0, :N_EXPERTS].astype(jnp.int32)
    padded = (cnt + tile - 1) // tile * tile
    base = jnp.cumsum(padded) - padded
    e_ids = jnp.arange(N_EXPERTS, dtype=jnp.int32)

    def dest(row_e, row_r):
        e = fields[row_e].astype(jnp.int32)
        seg = jnp.sum(jnp.where(e[None, :] == e_ids[:, None], base[:, None], 0), axis=0)
        pos = seg + fields[row_r].astype(jnp.int32)
        first = (pos // SUBLANES) * (SUBLANES * PACK_WORDS) + pos % SUBLANES
        per_window = SC_WINDOW // PACK_WORDS
        first = first.reshape(n_tokens // LANES, LANES)
        windows = []
        for c in range(LANES // per_window):
            pieces = [first[:, c * per_window + SUBLANES * a:c * per_window + SUBLANES * (a + 1)] + SUBLANES * j
                      for a in range(per_window // SUBLANES) for j in range(PACK_WORDS)]
            windows.append(jnp.concatenate(pieces, axis=1))
        return jnp.stack(windows, axis=1).reshape(n_tokens * PACK_WORDS // SC_WINDOW, SC_WINDOW)

    n_tiles = (2 * n_tokens) // tile + N_EXPERTS
    start = jnp.arange(n_tiles, dtype=jnp.int32) * tile
    seg_end = base + padded
    te = jnp.minimum(jnp.sum((start[:, None] >= seg_end[None, :]).astype(jnp.int32), axis=1), N_EXPERTS - 1)
    te_base = jnp.sum(jnp.where(te[:, None] == e_ids[None, :], base[None, :], 0), axis=1)
    te_cnt = jnp.sum(jnp.where(te[:, None] == e_ids[None, :], cnt[None, :], 0), axis=1)
    tv = jnp.clip(te_cnt - (start - te_base), 0, tile)
    tiles_used = (jnp.sum(padded) // tile).reshape(1)
    in_use = start < tiles_used[0] * tile
    starts_segment = in_use & jnp.concatenate([jnp.ones((1,), bool), te[1:] != te[:-1]])
    seg = jnp.where(in_use, jnp.cumsum(starts_segment.astype(jnp.int32)) - 1, -1)
    follows = starts_segment[None, :] & (seg[None, :] == seg[:, None] + 1)
    nxt = jnp.where(jnp.any(follows, axis=1), jnp.sum(jnp.where(follows, te[None, :], 0), axis=1), -1)
    nxt = jnp.where(in_use, nxt, -1)
    return dest(0, 4), dest(1, 5), te, tv, tiles_used, seg, nxt, n_tiles


def _t5_bucket(rel):
    half = N_BUCKETS // 2
    max_exact = half // 2
    ret = jnp.where(rel > 0, half, 0)
    n = jnp.abs(rel)
    nf = jnp.maximum(n, 1).astype(F32)
    large = max_exact + (jnp.log(nf / max_exact) / math.log(MAX_DISTANCE / max_exact)
                         * (half - max_exact)).astype(jnp.int32)
    large = jnp.minimum(large, half - 1)
    return ret + jnp.where(n < max_exact, n, large)


def _bucket_bias(rel_bias, bucket):
    rb = rel_bias.astype(F32)
    out = jnp.zeros((N_Q_HEADS,) + bucket.shape, F32)
    for b in range(N_BUCKETS):
        out = out + jnp.where(bucket[None] == b, rb[b][:, None, None], 0.0)
    return out


def _pair_rows(t):
    return jnp.concatenate([t[:N_PAIRS], t[N_PAIRS:]], axis=-1).reshape(N_PAIRS * BLOCK, -1)


def _bias_tables(rel_bias, sink):
    qi = jnp.arange(BLOCK)
    kj = jnp.arange(3 * BLOCK) - BLOCK
    rel = kj[None, :] - qi[:, None]
    band = _bucket_bias(rel_bias, _t5_bucket(rel))
    in_win = (jnp.abs(rel) <= WINDOW)[None]
    not_prev = (kj >= 0)[None, None, :]
    not_next = (kj < BLOCK)[None, None, :]
    variants = [jnp.where(in_win & not_prev, band, NEG),
                jnp.where(in_win, band, NEG),
                jnp.where(in_win & not_next, band, NEG)]
    bias = jnp.stack([_pair_rows(v) for v in variants])

    off = BLOCK - N_META
    mvars = []
    for blk in (1, 2):
        qpos = blk * BLOCK + qi - off
        meta_rel = jnp.arange(N_META)[None, :] - qpos[:, None]
        mvars.append(_pair_rows(_bucket_bias(rel_bias, _t5_bucket(meta_rel))))
    mbias = jnp.stack(mvars)
    s = sink.astype(F32)
    sink_tab = jnp.repeat(jnp.stack([s[:N_PAIRS], s[N_PAIRS:]], axis=-1), BLOCK, axis=0)
    return bias, mbias, sink_tab


def kernel(x_prompt, x_sample, meta, ln_in_g, ln_in_b, rel_bias, w_in, w_att_branch, sink, conv_w, conv_b,
           conv_ln_g, conv_ln_b, w_conv_out, w_out, ln1_g, ln1_b, w_group, b_group, w_router, b_router,
           w_gate, w_up, w_down, ln2_g, ln2_b):
    row = lambda v: v.reshape(1, -1).astype(F32)
    w = w_in[0]
    wq = (w[:, :Q_END].reshape(D_MODEL, 2, N_PAIRS, HEAD_DIM).transpose(0, 2, 1, 3)
          .reshape(D_MODEL, ATT_WIDTH).astype(BF16))
    watt = (w_att_branch[0].reshape(2, N_PAIRS, HEAD_DIM, D_MODEL).transpose(1, 0, 2, 3)
            .reshape(ATT_WIDTH, D_MODEL).astype(BF16))
    wkv = w[:, Q_END:V_END].astype(BF16)
    wglu = w[:, V_END:GLU_END].astype(BF16)
    wg = w[:, GLU_END:].astype(BF16)
    wco = w_conv_out[0].astype(BF16)
    wout = w_out[0].astype(BF16)
    wr = jnp.zeros((D_MODEL, ROUTER_LANES), F32)
    wr = wr.at[:, :N_EXPERTS].set(w_router[0]).at[:, N_EXPERTS:N_EXPERTS + N_GROUPS].set(w_group[0]).astype(BF16)
    br = jnp.zeros((1, ROUTER_LANES), F32)
    br = br.at[0, :N_EXPERTS].set(b_router[0]).at[0, N_EXPERTS:N_EXPERTS + N_GROUPS].set(b_group[0])
    ln_g, ln_b = row(ln_in_g), row(ln_in_b)
    bias, mbias, sink_tab = _bias_tables(rel_bias, sink[0])

    xm = jnp.concatenate([jnp.zeros((BLOCK - N_META, D_MODEL), F32), meta.astype(F32)], axis=0)
    _, _, kv_m, zc_m = _proj(xm, xm, _Geom(1, 1, 1, 1), ln_g, ln_b, wq, wkv, wglu, BLOCK)
    kv_meta = kv_m[BLOCK - N_META:BLOCK]
    z_meta = zc_m[BLOCK - N_META:BLOCK]

    (bp, sp, _), (bs, ss, _) = x_prompt.shape, x_sample.shape
    xp, xs = x_prompt.reshape(bp * sp, D_MODEL), x_sample.reshape(bs * ss, D_MODEL)
    geom = _geom(x_prompt, x_sample, TILE)
    h0, q, kv, zc = _proj(xp, xs, _geom(x_prompt, x_sample, TILE_PROJ), ln_g, ln_b, wq, wkv, wglu, TILE_PROJ)
    att = _attn(q, kv, _geom(x_prompt, x_sample, TILE_ATTN), kv_meta, bias, mbias, sink_tab, TILE_ATTN)
    cz = _conv(zc, geom, z_meta, conv_w[0], row(conv_b[0]), row(conv_ln_g[0]), row(conv_ln_b[0]), TILE)
    l1g, l1b, l2g, l2b = row(ln1_g[0]), row(ln1_b[0]), row(ln2_g[0]), row(ln2_b[0])

    def moe(first_tile, n_tiles):
        n = n_tiles * TILE
        h1, h1p, rt, fields, counts = _out(h0, att, cz, wg, watt, wco, wout, l1g, l1b, wr, br,
                                           TILE_OUT, first_tile * TILE // TILE_OUT, n_tiles * TILE // TILE_OUT)
        idx1, idx2, tile_expert, tile_valid, tiles_used, tile_segment, next_expert, n_tiles = _dispatch_plan(
            fields, counts, n, TILE_EXPERT)
        n_rows = n_tiles * TILE_EXPERT
        xsorted = _sc_scatter2(h1p.reshape(n * PACK_WORDS, LANES), idx1, idx2, n_rows * PACK_WORDS)
        ys = _experts(tile_expert, tile_valid, tiles_used, tile_segment, next_expert,
                      xsorted.reshape(n_rows // SUBLANES, PACK_WORDS, SUBLANES, LANES),
                      w_gate[0], w_up[0], w_down[0], TILE_EXPERT)
        ys = ys.reshape(n_rows * PACK_WORDS, LANES)
        half_windows = idx1.shape[0] // 2
        y = None
        for part in range(2):
            rows = slice(part * half_windows, (part + 1) * half_windows)
            g = _sc_gather2(ys, idx1[rows], idx2[rows])
            y = _final(h1, g.reshape(2, n // 2 // SUBLANES, PACK_WORDS, SUBLANES, LANES), rt, l2g, l2b, TILE_FINAL,
                       part * (n // 2 // TILE_FINAL), y)
        return y

    return moe(0, geom.n_p).reshape(x_prompt.shape), moe(geom.n_p, geom.n_s).reshape(x_sample.shape)
```

```python
import functools
import math
from typing import NamedTuple

import jax
import jax.numpy as jnp
from jax import lax
from jax.experimental import pallas as pl
from jax.experimental.pallas import tpu as pltpu
from jax.experimental.pallas import tpu_sc as plsc

D_MODEL = 1024
N_META = 16
BLOCK = 128
WINDOW = 128
N_Q_HEADS = 8
N_KV_HEADS = 2
HEAD_DIM = 64
ATT_WIDTH = N_Q_HEADS * HEAD_DIM
KV_WIDTH = N_KV_HEADS * HEAD_DIM
CONV_WIDTH = D_MODEL // 2
CONV_K = 31
N_BUCKETS = 32
MAX_DISTANCE = 128
N_GROUPS = 4
EXPERTS_PER_GROUP = 8
N_EXPERTS = N_GROUPS * EXPERTS_PER_GROUP
D_EXPERT = 256
LN_EPS = 1e-5
DEPTH = 1
ALPHA = (2 * DEPTH) ** 0.25
NEG = -1e30
Q_END = ATT_WIDTH
K_END = Q_END + KV_WIDTH
V_END = K_END + KV_WIDTH
GLU_END = V_END + 2 * CONV_WIDTH

N_PAIRS = N_Q_HEADS // 2
LANES = 128
SUBLANES = 8
CONV_HALO = 16
ROUTER_LANES = 128

TILE = 512
CONV_ROWS = 128
LN_ROWS = 64
SHIFT_ROWS = 128
TILE_PROJ = 1024
TILE_OUT = 1024
GATHER_PARTS = 4
XS_SLOTS = 3
TILE_FINAL = 1024
TILE_ATTN = 2048
TILE_EXPERT = 512
PACK_WORDS = 4
SC_WINDOW = 128
ROUTE_FIELDS = 8
VMEM_LIMIT = 56 * 1024 * 1024

BF16 = jnp.bfloat16
F32 = jnp.float32


def _layer_norm(x, g, b):
    mu = jnp.mean(x, axis=-1, keepdims=True)
    xc = x - mu
    var = jnp.mean(xc * xc, axis=-1, keepdims=True)
    return xc * lax.rsqrt(var + LN_EPS) * g + b


def _dot(a, b):
    return jnp.dot(a, b, preferred_element_type=F32)


def _dot_nt(a, b):
    return lax.dot_general(a, b, (((1,), (1,)), ((), ())), preferred_element_type=F32)


class _Geom(NamedTuple):
    n_p: int
    n_s: int
    tp: int
    ts: int


def _geom(x_prompt, x_sample, tile):
    (bp, sp, _), (bs, ss, _) = x_prompt.shape, x_sample.shape
    return _Geom(bp * sp // tile, bs * ss // tile, sp // tile, ss // tile)


def _seq_pos(t, g):
    is_p = t < g.n_p
    local = jnp.where(is_p, lax.rem(t, g.tp), lax.rem(jnp.maximum(t - g.n_p, 0), g.ts))
    return is_p, local == 0, local == jnp.where(is_p, g.tp - 1, g.ts - 1)


def _x_specs(g, tile, width):
    return [pl.BlockSpec((tile, width), lambda t: (jnp.minimum(t, g.n_p - 1), 0)),
            pl.BlockSpec((tile, width), lambda t: (jnp.maximum(t - g.n_p, 0), 0))]


def _proj_kernel(xp_ref, xs_ref, g_ref, b_ref, wq_ref, wkv_ref, wglu_ref, h0_ref, q_ref, kv_ref, zc_ref, *, geom):
    x = jnp.where(pl.program_id(0) < geom.n_p, xp_ref[...], xs_ref[...])
    h0 = _layer_norm(x, g_ref[...], b_ref[...])
    h0_ref[...] = h0
    h = h0.astype(BF16)
    q_ref[...] = _dot(h, wq_ref[...]).astype(BF16)
    kv_ref[...] = _dot(h, wkv_ref[...]).astype(BF16)
    u = _dot(h, wglu_ref[...])
    zc_ref[...] = (u[:, :CONV_WIDTH] * jax.nn.sigmoid(u[:, CONV_WIDTH:])).astype(BF16)


def _proj(xp, xs, geom, ln_g, ln_b, wq, wkv, wglu, tile):
    T = (geom.n_p + geom.n_s) * tile
    row = lambda w: pl.BlockSpec((tile, w), lambda t: (t, 0))
    full = lambda a: pl.BlockSpec(a.shape, lambda t: (0,) * a.ndim)
    return pl.pallas_call(
        functools.partial(_proj_kernel, geom=geom),
        grid=(geom.n_p + geom.n_s,),
        in_specs=_x_specs(geom, tile, D_MODEL) + [full(ln_g), full(ln_b), full(wq), full(wkv), full(wglu)],
        out_specs=[row(D_MODEL), row(ATT_WIDTH), row(2 * KV_WIDTH), row(CONV_WIDTH)],
        out_shape=[
            jax.ShapeDtypeStruct((T, D_MODEL), F32),
            jax.ShapeDtypeStruct((T, ATT_WIDTH), BF16),
            jax.ShapeDtypeStruct((T, 2 * KV_WIDTH), BF16),
            jax.ShapeDtypeStruct((T, CONV_WIDTH), BF16),
        ],
        compiler_params=pltpu.CompilerParams(dimension_semantics=("parallel",), vmem_limit_bytes=VMEM_LIMIT),
        name="proj",
    )(xp, xs, ln_g, ln_b, wq, wkv, wglu)


def _attn_kernel(q_ref, kvp_ref, kvc_ref, kvn_ref, kvm_ref, bias_ref, mbias_ref, sink_ref, o_ref, *, tile, geom):
    _, seq_first, seq_last = _seq_pos(pl.program_id(0), geom)
    blocks = tile // BLOCK
    scale = HEAD_DIM ** -0.5

    lane = lax.broadcasted_iota(jnp.int32, (1, LANES), 1)
    lo = lane < HEAD_DIM

    def split_heads(t):
        z = jnp.zeros_like(t)
        return jnp.where(lo, t, z), jnp.where(lo, z, t)

    kv_ext = jnp.concatenate([kvp_ref[...], kvc_ref[...], kvn_ref[...]], axis=0)
    k_ext = kv_ext[:, :KV_WIDTH] * jnp.asarray(scale, BF16)
    v_ext = kv_ext[:, KV_WIDTH:]
    ka, kb = split_heads(k_ext)
    va, vb = split_heads(v_ext)
    kma, kmb = split_heads(kvm_ref[:, :KV_WIDTH] * jnp.asarray(scale, BF16))
    vma, vmb = split_heads(kvm_ref[:, KV_WIDTH:])
    km_cat = jnp.concatenate([kma, kmb], axis=0)
    vm_cat = jnp.concatenate([vma, vmb], axis=0)

    mlane = lax.broadcasted_iota(jnp.int32, (1, 2 * N_META), 1)
    m_first = mlane < N_META
    sink = sink_ref[...]
    lane_o = lax.broadcasted_iota(jnp.int32, (1, LANES), 1) < HEAD_DIM

    for j in range(blocks):
        first = jnp.logical_and(seq_first, j == 0)
        last = jnp.logical_and(seq_last, j == blocks - 1)
        variant = jnp.where(first, 0, jnp.where(last, 2, 1))
        mvariant = jnp.where(first, 0, 1)

        r0 = j * BLOCK
        qb = q_ref[r0:r0 + BLOCK, :]
        q4 = jnp.concatenate([qb[:, p * LANES:(p + 1) * LANES] for p in range(N_PAIRS)], axis=0)
        k_cat = jnp.concatenate([ka[r0:r0 + 3 * BLOCK], kb[r0:r0 + 3 * BLOCK]], axis=0)
        v_cat = jnp.concatenate([va[r0:r0 + 3 * BLOCK], vb[r0:r0 + 3 * BLOCK]], axis=0)

        s = _dot_nt(q4, k_cat) + bias_ref[variant]
        sm = _dot_nt(q4, km_cat) + mbias_ref[mvariant]

        s_a, s_b = s[:, :3 * BLOCK], s[:, 3 * BLOCK:]
        sm_a = jnp.where(m_first, sm, NEG)
        sm_b = jnp.where(m_first, NEG, sm)
        m_a = jnp.maximum(jnp.maximum(jnp.max(s_a, axis=1, keepdims=True),
                                      jnp.max(sm_a, axis=1, keepdims=True)), sink[:, 0:1])
        m_b = jnp.maximum(jnp.maximum(jnp.max(s_b, axis=1, keepdims=True),
                                      jnp.max(sm_b, axis=1, keepdims=True)), sink[:, 1:2])
        p_a = jnp.exp(s_a - m_a)
        p_b = jnp.exp(s_b - m_b)
        pm = jnp.exp(jnp.where(m_first, sm - m_a, sm - m_b))
        l_a = (jnp.sum(p_a, axis=1, keepdims=True) + jnp.sum(jnp.where(m_first, pm, 0.0), axis=1, keepdims=True)
               + jnp.exp(sink[:, 0:1] - m_a))
        l_b = (jnp.sum(p_b, axis=1, keepdims=True) + jnp.sum(jnp.where(m_first, 0.0, pm), axis=1, keepdims=True)
               + jnp.exp(sink[:, 1:2] - m_b))
        p = jnp.concatenate([p_a, p_b], axis=1).astype(BF16)
        o = _dot(p, v_cat) + _dot(pm.astype(BF16), vm_cat)
        o = o * jnp.where(lane_o, 1.0 / l_a, 1.0 / l_b)
        for pr in range(N_PAIRS):
            o_ref[r0:r0 + BLOCK, pr * LANES:(pr + 1) * LANES] = o[pr * BLOCK:(pr + 1) * BLOCK].astype(BF16)


def _attn(q, kv, geom, kv_meta, bias, mbias, sink_tab, tile):
    T = q.shape[0]
    bpt = tile // BLOCK
    n_blocks = T // BLOCK
    full = lambda a: pl.BlockSpec(a.shape, lambda t: (0,) * a.ndim)
    return pl.pallas_call(
        functools.partial(_attn_kernel, tile=tile, geom=geom),
        grid=(T // tile,),
        in_specs=[
            pl.BlockSpec((tile, ATT_WIDTH), lambda t: (t, 0)),
            pl.BlockSpec((BLOCK, 2 * KV_WIDTH), lambda t: (jnp.maximum(t * bpt - 1, 0), 0)),
            pl.BlockSpec((tile, 2 * KV_WIDTH), lambda t: (t, 0)),
            pl.BlockSpec((BLOCK, 2 * KV_WIDTH), lambda t: (jnp.minimum((t + 1) * bpt, n_blocks - 1), 0)),
            full(kv_meta), full(bias), full(mbias), full(sink_tab),
        ],
        out_specs=pl.BlockSpec((tile, ATT_WIDTH), lambda t: (t, 0)),
        out_shape=jax.ShapeDtypeStruct((T, ATT_WIDTH), BF16),
        compiler_params=pltpu.CompilerParams(dimension_semantics=("parallel",), vmem_limit_bytes=VMEM_LIMIT),
        name="attn",
    )(q, kv, kv, kv, kv_meta, bias, mbias, sink_tab)


def _conv_kernel(zp_ref, zc_ref, zn_ref, zm_ref, w_ref, cb_ref, g_ref, b_ref, o_ref, ext_ref, sh_ref, y_ref,
                 *, tile, geom):
    i = pl.program_id(0)
    _, seq_first, seq_last = _seq_pos(i, geom)
    ext_ref[0:CONV_HALO, :] = jnp.where(seq_first, zm_ref[...], zp_ref[...]).astype(F32)
    ext_ref[CONV_HALO:CONV_HALO + tile, :] = zc_ref[...].astype(F32)
    ext_ref[CONV_HALO + tile:, :] = jnp.where(seq_last, 0.0, zn_ref[...].astype(F32))
    off = CONV_HALO - CONV_K // 2
    reach = (off + CONV_K - 1) // SUBLANES * SUBLANES
    for p in range(1, SUBLANES):
        for r0 in range(0, tile + reach, SHIFT_ROWS):
            n = min(SHIFT_ROWS, tile + reach - r0)
            sh_ref[p - 1, r0:r0 + n, :] = ext_ref[r0 + p:r0 + p + n, :]

    def taps(r0, cs):
        acc = jnp.zeros((CONV_ROWS, LANES), F32)
        for k in range(CONV_K):
            p, a = (off + k) % SUBLANES, (off + k) // SUBLANES * SUBLANES
            rows = slice(r0 + a, r0 + a + CONV_ROWS)
            win = ext_ref[rows, cs] if p == 0 else sh_ref[p - 1, rows, cs]
            acc = acc + win * w_ref[k:k + 1, cs]
        y_ref[r0:r0 + CONV_ROWS, cs] = acc

    for c in range(CONV_WIDTH // LANES):
        for r in range(tile // CONV_ROWS):
            pl.when(i >= 0)(functools.partial(taps, r * CONV_ROWS, slice(c * LANES, (c + 1) * LANES)))
    cb, g, b = cb_ref[...], g_ref[...], b_ref[...]
    for r in range(tile // LN_ROWS):
        r0 = r * LN_ROWS
        y = _layer_norm(y_ref[r0:r0 + LN_ROWS, :] + cb, g, b)
        o_ref[r0:r0 + LN_ROWS, :] = (y * jax.nn.sigmoid(y)).astype(BF16)


def _conv(zc, geom, z_meta, conv_w, conv_b, ln_g, ln_b, tile):
    T, C = zc.shape
    hpt = tile // CONV_HALO
    n_halo = T // CONV_HALO
    full = lambda a: pl.BlockSpec(a.shape, lambda t: (0,) * a.ndim)
    return pl.pallas_call(
        functools.partial(_conv_kernel, tile=tile, geom=geom),
        grid=(T // tile,),
        in_specs=[
            pl.BlockSpec((CONV_HALO, C), lambda t: (jnp.maximum(t * hpt - 1, 0), 0)),
            pl.BlockSpec((tile, C), lambda t: (t, 0)),
            pl.BlockSpec((CONV_HALO, C), lambda t: (jnp.minimum((t + 1) * hpt, n_halo - 1), 0)),
            full(z_meta), full(conv_w), full(conv_b), full(ln_g), full(ln_b),
        ],
        out_specs=pl.BlockSpec((tile, C), lambda t: (t, 0)),
        out_shape=jax.ShapeDtypeStruct((T, C), BF16),
        scratch_shapes=[pltpu.VMEM((tile + 2 * CONV_HALO, C), F32),
                        pltpu.VMEM((SUBLANES - 1, tile + 2 * CONV_HALO - SUBLANES, C), F32),
                        pltpu.VMEM((tile, C), F32)],
        compiler_params=pltpu.CompilerParams(dimension_semantics=("parallel",), vmem_limit_bytes=VMEM_LIMIT),
        name="conv",
    )(zc, zc, zc, z_meta, conv_w, conv_b, ln_g, ln_b)


def _route(r):
    lane_i = lax.broadcasted_iota(jnp.int32, r.shape, 1)
    lane = lane_i.astype(F32)
    big = float(1 << 20)
    is_g = jnp.logical_and(lane_i >= N_EXPERTS, lane_i < N_EXPERTS + N_GROUPS)
    lg = jnp.where(is_g, r, -jnp.inf)
    mg = jnp.max(lg, axis=1, keepdims=True)
    g_w = 1.0 / jnp.sum(jnp.exp(lg - mg), axis=1, keepdims=True)
    g_idx = jnp.min(jnp.where(lg == mg, lane - N_EXPERTS, big), axis=1, keepdims=True)
    lane_group = jnp.right_shift(lane_i, EXPERTS_PER_GROUP.bit_length() - 1).astype(F32)
    in_group = jnp.logical_and(lane_i < N_EXPERTS, lane_group == g_idx)
    le = jnp.where(in_group, r, -jnp.inf)
    m1 = jnp.max(le, axis=1, keepdims=True)
    den = jnp.sum(jnp.exp(le - m1), axis=1, keepdims=True)
    i1 = jnp.min(jnp.where(le == m1, lane, big), axis=1, keepdims=True)
    le2 = jnp.where(lane == i1, -jnp.inf, le)
    m2 = jnp.max(le2, axis=1, keepdims=True)
    i2 = jnp.min(jnp.where(le2 == m2, lane, big), axis=1, keepdims=True)
    p1 = 1.0 / den
    p2 = jnp.exp(m2 - m1) / den
    tot = p1 + p2
    return i1, i2, g_w * (p1 / tot), g_w * (p2 / tot)


def _pack_bf16_pairs(x):
    half = x.shape[1] // 2
    words = []
    for j in range(half // LANES):
        lo = pltpu.bitcast(x[:, j * LANES:(j + 1) * LANES].astype(BF16).astype(F32), jnp.uint32)
        hi = pltpu.bitcast(x[:, half + j * LANES:half + (j + 1) * LANES].astype(BF16).astype(F32), jnp.uint32)
        words.append(hi | (lo >> 16))
    return words


def _unpack_bf16_pairs(words):
    lo = [pltpu.bitcast(w << 16, F32) for w in words]
    hi = [pltpu.bitcast(w & jnp.uint32(0xFFFF0000), F32) for w in words]
    return jnp.concatenate(lo + hi, axis=1)


def _out_kernel(h0_ref, att_ref, cz_ref, wg_ref, watt_ref, wco_ref, wout_ref,
                l1g_ref, l1b_ref, wr_ref, br_ref, before_ref, h1_ref, h1p_ref, rt_ref, fld_ref, cnt_ref, *, tile):
    @pl.when(pl.program_id(0) == 0)
    def _():
        cnt_ref[...] = jnp.zeros_like(cnt_ref)

    h0 = h0_ref[...]
    hb = h0.astype(BF16)
    g_att = jax.nn.sigmoid(_dot(hb, wg_ref[:, :D_MODEL]))
    mix = g_att * _dot(att_ref[...], watt_ref[...])
    g_conv = jax.nn.sigmoid(_dot(hb, wg_ref[:, D_MODEL:]))
    mix = mix + g_conv * _dot(cz_ref[...], wco_ref[...])
    m = _dot(mix.astype(BF16), wout_ref[...])
    h1 = _layer_norm(ALPHA * h0 + m, l1g_ref[...], l1b_ref[...])
    h1_ref[...] = h1
    for j, w in enumerate(_pack_bf16_pairs(h1)):
        h1p_ref[:, j, :, :] = w.reshape(tile // SUBLANES, SUBLANES, LANES)

    r = _dot(h1.astype(BF16), wr_ref[...]) + br_ref[...]
    i1, i2, w1, w2 = _route(r)
    lane = lax.broadcasted_iota(jnp.int32, (tile, ROUTER_LANES), 1)
    lane_f = lane.astype(F32)
    hit1, hit2 = lane_f == i1, lane_f == i2
    onehot = jnp.where(jnp.logical_or(hit1, hit2), 1.0, 0.0)
    seen = _dot(before_ref[...], onehot.astype(BF16)) + cnt_ref[0:1, :]
    rank1 = jnp.sum(jnp.where(hit1, seen, 0.0), axis=1, keepdims=True)
    rank2 = jnp.sum(jnp.where(hit2, seen, 0.0), axis=1, keepdims=True)
    cnt_ref[...] = cnt_ref[...] + jnp.sum(onehot, axis=0, keepdims=True)
    fields = (i1, i2, w1, w2, rank1, rank2)
    rt = jnp.zeros((tile, ROUTER_LANES), F32)
    for k, v in enumerate(fields):
        rt = jnp.where(lane == k, v, rt)
    rt_ref[...] = rt
    fld_ref[...] = jnp.transpose(rt)[0:ROUTE_FIELDS, :]


def _out(h0, att, cz, wg, watt, wco, wout, l1g, l1b, wr, br, tile, first_tile, n_tiles):
    T, D = n_tiles * tile, D_MODEL
    own = lambda w: pl.BlockSpec((tile, w), lambda i: (i, 0))
    flat = lambda w: pl.BlockSpec((tile, w), lambda i: (first_tile + i, 0))
    full = lambda a: pl.BlockSpec(a.shape, lambda i: (0,) * a.ndim, pipeline_mode=pl.Buffered(1))
    before = (jnp.arange(tile)[None, :] < jnp.arange(tile)[:, None]).astype(BF16)
    return pl.pallas_call(
        functools.partial(_out_kernel, tile=tile),
        grid=(T // tile,),
        in_specs=[flat(D), flat(ATT_WIDTH), flat(CONV_WIDTH), full(wg),
                  full(watt), full(wco), full(wout), full(l1g), full(l1b), full(wr), full(br), full(before)],
        out_specs=[own(D),
                   pl.BlockSpec((tile // SUBLANES, PACK_WORDS, SUBLANES, LANES), lambda i: (i, 0, 0, 0)),
                   own(ROUTER_LANES),
                   pl.BlockSpec((ROUTE_FIELDS, tile), lambda i: (0, i)),
                   pl.BlockSpec((SUBLANES, ROUTER_LANES), lambda i: (0, 0))],
        out_shape=[jax.ShapeDtypeStruct((T, D), F32),
                   jax.ShapeDtypeStruct((T // SUBLANES, PACK_WORDS, SUBLANES, LANES), jnp.uint32),
                   jax.ShapeDtypeStruct((T, ROUTER_LANES), F32),
                   jax.ShapeDtypeStruct((ROUTE_FIELDS, T), F32),
                   jax.ShapeDtypeStruct((SUBLANES, ROUTER_LANES), F32)],
        compiler_params=pltpu.CompilerParams(dimension_semantics=("arbitrary",), vmem_limit_bytes=VMEM_LIMIT),
        name="out",
    )(h0, att, cz, wg, watt, wco, wout, l1g, l1b, wr, br, before)


def _sc_scatter2(src, idx_a, idx_b, n_out):
    m = src.shape[0]
    mesh = plsc.VectorSubcoreMesh(core_axis_name="c", subcore_axis_name="s")

    @functools.partial(pl.kernel, out_type=jax.ShapeDtypeStruct((n_out, LANES), src.dtype), mesh=mesh)
    def k(x_hbm, ia_hbm, ib_hbm, o_hbm):
        def body(x_vmem, ia_vmem, ib_vmem):
            pltpu.sync_copy(x_vmem, o_hbm.at[ia_vmem.at[0]])
            pltpu.sync_copy(x_vmem, o_hbm.at[ib_vmem.at[0]])

        pltpu.emit_pipeline(
            body, grid=(m // SC_WINDOW,),
            in_specs=[pl.BlockSpec((SC_WINDOW, LANES), index_map=lambda i: (i, 0)),
                      pl.BlockSpec((1, SC_WINDOW), index_map=lambda i: (i, 0)),
                      pl.BlockSpec((1, SC_WINDOW), index_map=lambda i: (i, 0))],
            out_specs=[],
            core_axis_name=("c", "s"), dimension_semantics=(pltpu.PARALLEL,),
        )(x_hbm, ia_hbm, ib_hbm)

    return k(src, idx_a, idx_b)


def _sc_gather2(table, idx_a, idx_b):
    windows = idx_a.shape[0]
    m = windows * SC_WINDOW
    mesh = plsc.VectorSubcoreMesh(core_axis_name="c", subcore_axis_name="s")

    @functools.partial(pl.kernel, out_type=jax.ShapeDtypeStruct((2 * m, LANES), table.dtype), mesh=mesh)
    def k(x_hbm, ia_hbm, ib_hbm, o_hbm):
        def body(i_vmem, o_vmem):
            pltpu.sync_copy(x_hbm.at[i_vmem.at[0]], o_vmem)

        for half, i_hbm in enumerate((ia_hbm, ib_hbm)):
            pltpu.emit_pipeline(
                body, grid=(windows,),
                in_specs=[pl.BlockSpec((1, SC_WINDOW), index_map=lambda i: (i, 0))],
                out_specs=[pl.BlockSpec((SC_WINDOW, LANES), index_map=lambda i, half=half: (half * windows + i, 0))],
                core_axis_name=("c", "s"), dimension_semantics=(pltpu.PARALLEL,),
            )(i_hbm, o_hbm)

    return k(table, idx_a, idx_b).reshape(2, m, LANES)


def _expert_kernel(te_ref, tv_ref, used_ref, seg_ref, nxt_ref, xs_hbm, wg_hbm, wu_hbm, wd_hbm, ys_ref,
                   wgb_ref, wub_ref, wdb_ref, xbuf_ref, xsem_ref, wg_buf, wu_buf, wd_buf, wsem_ref, *, tile):
    n = pl.program_id(0)
    valid = tv_ref[n]
    used = used_ref[0]

    def row_tile_copy(step):
        slot = lax.rem(step, XS_SLOTS)
        return pltpu.make_async_copy(xs_hbm.at[pl.ds(step * (tile // SUBLANES), tile // SUBLANES)], xbuf_ref.at[slot],
                                     xsem_ref.at[slot])

    def weight_copies(expert, slot):
        return [pltpu.make_async_copy(hbm.at[expert], buf.at[slot], wsem_ref.at[slot, i])
                for i, (hbm, buf) in enumerate(((wg_hbm, wg_buf), (wu_hbm, wu_buf), (wd_hbm, wd_buf)))]

    @pl.when(n == 0)
    def _():
        row_tile_copy(0).start()
        for c in weight_copies(te_ref[0], 0):
            c.start()

        @pl.when(used > 1)
        def _():
            row_tile_copy(1).start()

    @pl.when(n + 2 < used)
    def _():
        row_tile_copy(n + 2).start()

    seg = seg_ref[n]
    first_of_segment = jnp.logical_and(seg >= 0, jnp.logical_or(n == 0, seg != seg_ref[jnp.maximum(n - 1, 0)]))

    @pl.when(first_of_segment)
    def _():
        slot = lax.rem(seg, 2)
        for c in weight_copies(te_ref[n], slot):
            c.wait()
        wgb_ref[...] = wg_buf[slot].astype(BF16)
        wub_ref[...] = wu_buf[slot].astype(BF16)
        wdb_ref[...] = wd_buf[slot].astype(BF16)

        @pl.when(nxt_ref[n] >= 0)
        def _():
            for c in weight_copies(nxt_ref[n], 1 - slot):
                c.start()

    @pl.when(n < used)
    def _():
        row_tile_copy(n).wait()
        slot = lax.rem(n, XS_SLOTS)
        x = _unpack_bf16_pairs([xbuf_ref[slot, :, j, :, :].reshape(tile, LANES) for j in range(PACK_WORDS)])
        rows = lax.broadcasted_iota(jnp.int32, (tile, 1), 0)
        x = jnp.where(rows < valid, x, 0.0).astype(BF16)
        gt = _dot(x, wgb_ref[...])
        up = _dot(x, wub_ref[...])
        hid = (gt * jax.nn.sigmoid(gt)) * up
        y = _dot(hid.astype(BF16), wdb_ref[...])
        for j, w in enumerate(_pack_bf16_pairs(y)):
            ys_ref[:, j, :, :] = w.reshape(tile // SUBLANES, SUBLANES, LANES)


def _experts(tile_expert, tile_valid, tiles_used, tile_segment, next_expert, xs, w_gate, w_up, w_down, tile):
    n_tiles = xs.shape[0] * SUBLANES // tile
    blk = pl.BlockSpec((tile // SUBLANES, PACK_WORDS, SUBLANES, LANES),
                       lambda n, te, tv, used, seg, nxt: (jnp.minimum(n, jnp.maximum(used[0] - 1, 0)), 0, 0, 0))
    any_space = pl.BlockSpec(memory_space=pl.ANY)
    return pl.pallas_call(
        functools.partial(_expert_kernel, tile=tile),
        grid_spec=pltpu.PrefetchScalarGridSpec(
            num_scalar_prefetch=5,
            grid=(n_tiles,),
            in_specs=[any_space, any_space, any_space, any_space],
            out_specs=blk,
            scratch_shapes=[pltpu.VMEM((D_MODEL, D_EXPERT), BF16), pltpu.VMEM((D_MODEL, D_EXPERT), BF16),
                            pltpu.VMEM((D_EXPERT, D_MODEL), BF16),
                            pltpu.VMEM((XS_SLOTS, tile // SUBLANES, PACK_WORDS, SUBLANES, LANES), jnp.uint32),
                            pltpu.SemaphoreType.DMA((XS_SLOTS,)),
                            pltpu.VMEM((2, D_MODEL, D_EXPERT), F32), pltpu.VMEM((2, D_MODEL, D_EXPERT), F32),
                            pltpu.VMEM((2, D_EXPERT, D_MODEL), F32),
                            pltpu.SemaphoreType.DMA((2, 3))],
        ),
        out_shape=jax.ShapeDtypeStruct(xs.shape, jnp.uint32),
        compiler_params=pltpu.CompilerParams(
            dimension_semantics=("arbitrary",), vmem_limit_bytes=VMEM_LIMIT),
        name="experts",
    )(tile_expert, tile_valid, tiles_used, tile_segment, next_expert, xs, w_gate, w_up, w_down)


def _final_kernel(h_ref, g_ref, rt_ref, l2g_ref, l2b_ref, *rest, tile):
    o_ref = rest[-1]
    rt = rt_ref[...]
    lane = lax.broadcasted_iota(jnp.int32, rt.shape, 1)
    w1 = jnp.sum(jnp.where(lane == 2, rt, 0.0), axis=1, keepdims=True)
    w2 = jnp.sum(jnp.where(lane == 3, rt, 0.0), axis=1, keepdims=True)
    y1 = _unpack_bf16_pairs([g_ref[0, :, j, :, :].reshape(tile, LANES) for j in range(PACK_WORDS)])
    y2 = _unpack_bf16_pairs([g_ref[1, :, j, :, :].reshape(tile, LANES) for j in range(PACK_WORDS)])
    f = w1 * y1 + w2 * y2
    o_ref[...] = _layer_norm(ALPHA * h_ref[...] + f, l2g_ref[...], l2b_ref[...])


def _final(h1, g, rt, l2g, l2b, tile, first_tile, y_so_far=None):
    T, D = h1.shape
    n_tiles = g.shape[1] * SUBLANES // tile
    row = lambda w: pl.BlockSpec((tile, w), lambda i: (first_tile + i, 0))
    full = lambda a: pl.BlockSpec(a.shape, lambda i: (0,) * a.ndim)
    in_specs = [row(D),
                pl.BlockSpec((2, tile // SUBLANES, PACK_WORDS, SUBLANES, LANES), lambda i: (0, i, 0, 0, 0)),
                row(ROUTER_LANES), full(l2g), full(l2b)]
    args = [h1, g, rt, l2g, l2b]
    aliases = {}
    if y_so_far is not None:
        in_specs.append(pl.BlockSpec(memory_space=pl.ANY))
        args.append(y_so_far)
        aliases = {len(args) - 1: 0}
    return pl.pallas_call(
        functools.partial(_final_kernel, tile=tile),
        grid=(n_tiles,),
        in_specs=in_specs,
        out_specs=row(D),
        out_shape=jax.ShapeDtypeStruct((T, D), F32),
        input_output_aliases=aliases,
        compiler_params=pltpu.CompilerParams(dimension_semantics=("parallel",), vmem_limit_bytes=VMEM_LIMIT),
        name="final",
    )(*args)


def _dispatch_plan(fields, counts, n_tokens, tile):
    cnt = counts[0, :N_EXPERTS].astype(jnp.int32)
    padded = (cnt + tile - 1) // tile * tile
    base = jnp.cumsum(padded) - padded
    e_ids = jnp.arange(N_EXPERTS, dtype=jnp.int32)

    def dest(row_e, row_r):
        e = fields[row_e].astype(jnp.int32)
        seg = jnp.sum(jnp.where(e[None, :] == e_ids[:, None], base[:, None], 0), axis=0)
        pos = seg + fields[row_r].astype(jnp.int32)
        first = (pos // SUBLANES) * (SUBLANES * PACK_WORDS) + pos % SUBLANES
        per_window = SC_WINDOW // PACK_WORDS
        first = first.reshape(n_tokens // LANES, LANES)
        windows = []
        for c in range(LANES // per_window):
            pieces = [first[:, c * per_window + SUBLANES * a:c * per_window + SUBLANES * (a + 1)] + SUBLANES * j
                      for a in range(per_window // SUBLANES) for j in range(PACK_WORDS)]
            windows.append(jnp.concatenate(pieces, axis=1))
        return jnp.stack(windows, axis=1).reshape(n_tokens * PACK_WORDS // SC_WINDOW, SC_WINDOW)

    n_tiles = (2 * n_tokens) // tile + N_EXPERTS
    start = jnp.arange(n_tiles, dtype=jnp.int32) * tile
    seg_end = base + padded
    te = jnp.minimum(jnp.sum((start[:, None] >= seg_end[None, :]).astype(jnp.int32), axis=1), N_EXPERTS - 1)
    te_base = jnp.sum(jnp.where(te[:, None] == e_ids[None, :], base[None, :], 0), axis=1)
    te_cnt = jnp.sum(jnp.where(te[:, None] == e_ids[None, :], cnt[None, :], 0), axis=1)
    tv = jnp.clip(te_cnt - (start - te_base), 0, tile)
    tiles_used = (jnp.sum(padded) // tile).reshape(1)
    in_use = start < tiles_used[0] * tile
    starts_segment = in_use & jnp.concatenate([jnp.ones((1,), bool), te[1:] != te[:-1]])
    seg = jnp.where(in_use, jnp.cumsum(starts_segment.astype(jnp.int32)) - 1, -1)
    follows = starts_segment[None, :] & (seg[None, :] == seg[:, None] + 1)
    nxt = jnp.where(jnp.any(follows, axis=1), jnp.sum(jnp.where(follows, te[None, :], 0), axis=1), -1)
    nxt = jnp.where(in_use, nxt, -1)
    return dest(0, 4), dest(1, 5), te, tv, tiles_used, seg, nxt, n_tiles


def _t5_bucket(rel):
    half = N_BUCKETS // 2
    max_exact = half // 2
    ret = jnp.where(rel > 0, half, 0)
    n = jnp.abs(rel)
    nf = jnp.maximum(n, 1).astype(F32)
    large = max_exact + (jnp.log(nf / max_exact) / math.log(MAX_DISTANCE / max_exact)
                         * (half - max_exact)).astype(jnp.int32)
    large = jnp.minimum(large, half - 1)
    return ret + jnp.where(n < max_exact, n, large)


def _bucket_bias(rel_bias, bucket):
    rb = rel_bias.astype(F32)
    out = jnp.zeros((N_Q_HEADS,) + bucket.shape, F32)
    for b in range(N_BUCKETS):
        out = out + jnp.where(bucket[None] == b, rb[b][:, None, None], 0.0)
    return out


def _pair_rows(t):
    return jnp.concatenate([t[:N_PAIRS], t[N_PAIRS:]], axis=-1).reshape(N_PAIRS * BLOCK, -1)


def _bias_tables(rel_bias, sink):
    qi = jnp.arange(BLOCK)
    kj = jnp.arange(3 * BLOCK) - BLOCK
    rel = kj[None, :] - qi[:, None]
    band = _bucket_bias(rel_bias, _t5_bucket(rel))
    in_win = (jnp.abs(rel) <= WINDOW)[None]
    not_prev = (kj >= 0)[None, None, :]
    not_next = (kj < BLOCK)[None, None, :]
    variants = [jnp.where(in_win & not_prev, band, NEG),
                jnp.where(in_win, band, NEG),
                jnp.where(in_win & not_next, band, NEG)]
    bias = jnp.stack([_pair_rows(v) for v in variants])

    off = BLOCK - N_META
    mvars = []
    for blk in (1, 2):
        qpos = blk * BLOCK + qi - off
        meta_rel = jnp.arange(N_META)[None, :] - qpos[:, None]
        mvars.append(_pair_rows(_bucket_bias(rel_bias, _t5_bucket(meta_rel))))
    mbias = jnp.stack(mvars)
    s = sink.astype(F32)
    sink_tab = jnp.repeat(jnp.stack([s[:N_PAIRS], s[N_PAIRS:]], axis=-1), BLOCK, axis=0)
    return bias, mbias, sink_tab


def kernel(x_prompt, x_sample, meta, ln_in_g, ln_in_b, rel_bias, w_in, w_att_branch, sink, conv_w, conv_b,
           conv_ln_g, conv_ln_b, w_conv_out, w_out, ln1_g, ln1_b, w_group, b_group, w_router, b_router,
           w_gate, w_up, w_down, ln2_g, ln2_b):
    row = lambda v: v.reshape(1, -1).astype(F32)
    w = w_in[0]
    wq = (w[:, :Q_END].reshape(D_MODEL, 2, N_PAIRS, HEAD_DIM).transpose(0, 2, 1, 3)
          .reshape(D_MODEL, ATT_WIDTH).astype(BF16))
    watt = (w_att_branch[0].reshape(2, N_PAIRS, HEAD_DIM, D_MODEL).transpose(1, 0, 2, 3)
            .reshape(ATT_WIDTH, D_MODEL).astype(BF16))
    wkv = w[:, Q_END:V_END].astype(BF16)
    wglu = w[:, V_END:GLU_END].astype(BF16)
    wg = w[:, GLU_END:].astype(BF16)
    wco = w_conv_out[0].astype(BF16)
    wout = w_out[0].astype(BF16)
    wr = jnp.zeros((D_MODEL, ROUTER_LANES), F32)
    wr = wr.at[:, :N_EXPERTS].set(w_router[0]).at[:, N_EXPERTS:N_EXPERTS + N_GROUPS].set(w_group[0]).astype(BF16)
    br = jnp.zeros((1, ROUTER_LANES), F32)
    br = br.at[0, :N_EXPERTS].set(b_router[0]).at[0, N_EXPERTS:N_EXPERTS + N_GROUPS].set(b_group[0])
    ln_g, ln_b = row(ln_in_g), row(ln_in_b)
    bias, mbias, sink_tab = _bias_tables(rel_bias, sink[0])

    xm = jnp.concatenate([jnp.zeros((BLOCK - N_META, D_MODEL), F32), meta.astype(F32)], axis=0)
    _, _, kv_m, zc_m = _proj(xm, xm, _Geom(1, 1, 1, 1), ln_g, ln_b, wq, wkv, wglu, BLOCK)
    kv_meta = kv_m[BLOCK - N_META:BLOCK]
    z_meta = zc_m[BLOCK - N_META:BLOCK]

    (bp, sp, _), (bs, ss, _) = x_prompt.shape, x_sample.shape
    xp, xs = x_prompt.reshape(bp * sp, D_MODEL), x_sample.reshape(bs * ss, D_MODEL)
    geom = _geom(x_prompt, x_sample, TILE)
    h0, q, kv, zc = _proj(xp, xs, _geom(x_prompt, x_sample, TILE_PROJ), ln_g, ln_b, wq, wkv, wglu, TILE_PROJ)
    att = _attn(q, kv, _geom(x_prompt, x_sample, TILE_ATTN), kv_meta, bias, mbias, sink_tab, TILE_ATTN)
    cz = _conv(zc, geom, z_meta, conv_w[0], row(conv_b[0]), row(conv_ln_g[0]), row(conv_ln_b[0]), TILE)
    l1g, l1b, l2g, l2b = row(ln1_g[0]), row(ln1_b[0]), row(ln2_g[0]), row(ln2_b[0])

    def moe(first_tile, n_tiles):
        n = n_tiles * TILE
        h1, h1p, rt, fields, counts = _out(h0, att, cz, wg, watt, wco, wout, l1g, l1b, wr, br,
                                           TILE_OUT, first_tile * TILE // TILE_OUT, n_tiles * TILE // TILE_OUT)
        idx1, idx2, tile_expert, tile_valid, tiles_used, tile_segment, next_expert, n_tiles = _dispatch_plan(
            fields, counts, n, TILE_EXPERT)
        n_rows = n_tiles * TILE_EXPERT
        xsorted = _sc_scatter2(h1p.reshape(n * PACK_WORDS, LANES), idx1, idx2, n_rows * PACK_WORDS)
        ys = _experts(tile_expert, tile_valid, tiles_used, tile_segment, next_expert,
                      xsorted.reshape(n_rows // SUBLANES, PACK_WORDS, SUBLANES, LANES),
                      w_gate[0], w_up[0], w_down[0], TILE_EXPERT)
        ys = ys.reshape(n_rows * PACK_WORDS, LANES)
        part_windows = idx1.shape[0] // GATHER_PARTS
        part_tokens = n // GATHER_PARTS
        y = None
        for part in range(GATHER_PARTS):
            rows = slice(part * part_windows, (part + 1) * part_windows)
            g = _sc_gather2(ys, idx1[rows], idx2[rows])
            y = _final(h1, g.reshape(2, part_tokens // SUBLANES, PACK_WORDS, SUBLANES, LANES), rt, l2g, l2b, TILE_FINAL,
                       part * (part_tokens // TILE_FINAL), y)
        return y

    return moe(0, geom.n_p).reshape(x_prompt.shape), moe(geom.n_p, geom.n_s).reshape(x_sample.shape)
```
